```python
import math
import jax, jax.numpy as jnp
from jax import lax
import numpy as np


D_MODEL = 2048
BATCH = 2
SEQ = 4096
DEPTH = 4

N_MIXERS = 4
HEAD_DIM = 128
N_SB_HEADS = D_MODEL // HEAD_DIM
N_FOX_HEADS = D_MODEL // HEAD_DIM
SWA_HEAD_DIM = 64
N_SWA_HEADS = D_MODEL // SWA_HEAD_DIM
N_SWA_KV_HEADS = N_SWA_HEADS // 8
WINDOW = 128
MLA_HEADS = D_MODEL // 128
Q_LORA = D_MODEL // 4
KV_LORA = D_MODEL // 8
QK_NOPE = 128
QK_ROPE = 64
V_HEAD = 128
ROPE_THETA = 10000.0
MEM_LEN = 256
MEM_HEADS = 4
MEM_HEAD_DIM = 128
D_FF = ((8 * D_MODEL // 3 + 255) // 256) * 256
CONV_WIDTH = 3
NUM_BUCKETS = 32
MAX_DISTANCE = 128
Q_BLOCK = 128
EPS = 1e-6

SB_W = N_SB_HEADS * HEAD_DIM
FOX_W = N_FOX_HEADS * HEAD_DIM
SWA_W = N_SWA_HEADS * SWA_HEAD_DIM
SWA_KV_W = N_SWA_KV_HEADS * SWA_HEAD_DIM
MLA_W = MLA_HEADS * V_HEAD
MEM_W = MEM_HEADS * MEM_HEAD_DIM

kernel_name = 'hybrid_interleaved_sb_fox_swa_mla'


def _n_uses(m):
    return (DEPTH - m + N_MIXERS - 1) // N_MIXERS


def _rmsnorm(x, g):
    x32 = x.astype(jnp.float32)
    y = x32 * lax.rsqrt(jnp.mean(x32 * x32, axis=-1, keepdims=True) + EPS)
    return y.astype(x.dtype) * g


def _to_blocks(a):
    B, H, S = a.shape[:3]
    a = a.reshape((B, H, S // Q_BLOCK, Q_BLOCK) + a.shape[3:])
    return jnp.moveaxis(a, 2, 0)


def _from_blocks(a):
    nb, B, H, qb = a.shape[:4]
    return jnp.moveaxis(a, 0, 2).reshape((B, H, nb * qb) + a.shape[4:])


def _stick_breaking_attention(q, k, v):
    S = q.shape[2]
    scale = q.shape[-1] ** -0.5
    kpos = jnp.arange(S)

    def block(xs):
        qb, i = xs
        qpos = i * Q_BLOCK + jnp.arange(Q_BLOCK)
        z = jnp.einsum('bhqd,bhkd->bhqk', qb, k).astype(jnp.float32) * scale
        strict = kpos[None, :] < qpos[:, None]
        log_keep = jnp.where(strict, jax.nn.log_sigmoid(-z), 0.0)
        log_after = lax.cumsum(log_keep, axis=3, reverse=True) - log_keep
        a = jnp.where(strict, jnp.exp(jax.nn.log_sigmoid(z) + log_after), 0.0)
        return jnp.einsum('bhqk,bhkd->bhqd', a.astype(v.dtype), v)

    out = lax.map(block, (_to_blocks(q), jnp.arange(S // Q_BLOCK)))
    return _from_blocks(out)


def _causal_softmax_attention(q, k, v, log_fcum=None):
    S = q.shape[2]
    scale = q.shape[-1] ** -0.5
    kpos = jnp.arange(S)
    nb = S // Q_BLOCK

    def block(xs):
        qb, i = xs[0], xs[1]
        qpos = i * Q_BLOCK + jnp.arange(Q_BLOCK)
        logits = jnp.einsum('bhqd,bhkd->bhqk', qb, k).astype(jnp.float32) * scale
        if log_fcum is not None:
            logits = logits + (xs[2][..., :, None] - log_fcum[:, :, None, :])
        logits = jnp.where(kpos[None, :] <= qpos[:, None], logits, -jnp.inf)
        p = jax.nn.softmax(logits, axis=-1)
        return jnp.einsum('bhqk,bhkd->bhqd', p.astype(v.dtype), v)

    if log_fcum is None:
        xs = (_to_blocks(q), jnp.arange(nb))
    else:
        xs = (_to_blocks(q), jnp.arange(nb), _to_blocks(log_fcum))
    return _from_blocks(lax.map(block, xs))


def _memory_attention(q_mem, mem_h, w_mem_kv):
    B, S, _ = q_mem.shape
    q = q_mem.reshape(B, S, MEM_HEADS, MEM_HEAD_DIM)
    kv = (mem_h @ w_mem_kv).reshape(B, mem_h.shape[1], 2, MEM_HEADS, MEM_HEAD_DIM)
    logits = jnp.einsum('bshd,bmhd->bhsm', q, kv[:, :, 0]).astype(jnp.float32) * MEM_HEAD_DIM ** -0.5
    p = jax.nn.softmax(logits, axis=-1)
    o = jnp.einsum('bhsm,bmhd->bshd', p.astype(q.dtype), kv[:, :, 1])
    return o.reshape(B, S, MEM_W)


def _heads(t, n, d):
    B, S, _ = t.shape
    return t.reshape(B, S, n, d).transpose(0, 2, 1, 3)


def _merge_heads(t):
    B, H, S, d = t.shape
    return t.transpose(0, 2, 1, 3).reshape(B, S, H * d)


def _sb_mixer(h, mem_h, w_in, w_mem_kv, w_out):
    q, k, v, q_mem = jnp.split(h @ w_in, [SB_W, 2 * SB_W, 3 * SB_W], axis=-1)
    o = _stick_breaking_attention(_heads(q, N_SB_HEADS, HEAD_DIM), _heads(k, N_SB_HEADS, HEAD_DIM),
                                  _heads(v, N_SB_HEADS, HEAD_DIM))
    o_mem = _memory_attention(q_mem, mem_h, w_mem_kv)
    return jnp.concatenate([_merge_heads(o), o_mem], axis=-1) @ w_out


def _fox_mixer(h, mem_h, w_in, b_f, w_mem_kv, w_out):
    q, k, v, f_logit, q_mem = jnp.split(
        h @ w_in, [FOX_W, 2 * FOX_W, 3 * FOX_W, 3 * FOX_W + N_FOX_HEADS], axis=-1)
    log_f = jax.nn.log_sigmoid((f_logit + b_f).astype(jnp.float32))
    log_fcum = jnp.cumsum(log_f, axis=1).transpose(0, 2, 1)
    o = _causal_softmax_attention(_heads(q, N_FOX_HEADS, HEAD_DIM), _heads(k, N_FOX_HEADS, HEAD_DIM),
                                  _heads(v, N_FOX_HEADS, HEAD_DIM), log_fcum)
    o_mem = _memory_attention(q_mem, mem_h, w_mem_kv)
    return jnp.concatenate([_merge_heads(o), o_mem], axis=-1) @ w_out


def _t5_bucket(dist):
    max_exact = NUM_BUCKETS // 2
    d = jnp.maximum(dist, 1).astype(jnp.float32)
    large = max_exact + (jnp.log(d / max_exact) / math.log(MAX_DISTANCE / max_exact)
                         * (NUM_BUCKETS - max_exact)).astype(jnp.int32)
    return jnp.where(dist < max_exact, dist, jnp.minimum(large, NUM_BUCKETS - 1))


def _swa_mixer(h, mem_h, rel_bias, w_in, sinks, w_mem_kv, w_out):
    B, S, _ = h.shape
    q, k, v, q_mem = jnp.split(h @ w_in, [SWA_W, SWA_W + SWA_KV_W, SWA_W + 2 * SWA_KV_W], axis=-1)
    nb = S // WINDOW
    G = N_SWA_HEADS // N_SWA_KV_HEADS
    qb = q.reshape(B, nb, WINDOW, N_SWA_KV_HEADS, G, SWA_HEAD_DIM)

    def window_keys(t):
        t = jnp.pad(t.reshape(B, S, N_SWA_KV_HEADS, SWA_HEAD_DIM), ((0, 0), (WINDOW, 0), (0, 0), (0, 0)))
        t = t.reshape(B, nb + 1, WINDOW, N_SWA_KV_HEADS, SWA_HEAD_DIM)
        return jnp.concatenate([t[:, :-1], t[:, 1:]], axis=2)

    kw, vw = window_keys(k), window_keys(v)
    logits = jnp.einsum('bnqhgd,bnkhd->bnhgqk', qb, kw).astype(jnp.float32) * SWA_HEAD_DIM ** -0.5
    qi = jnp.arange(WINDOW)[:, None]
    kj = jnp.arange(2 * WINDOW)[None, :]
    dist = WINDOW + qi - kj
    band = (dist >= 0) & (dist < WINDOW)
    real = (jnp.arange(nb)[:, None, None] * WINDOW + kj[None] - WINDOW) >= 0
    mask = band[None] & real
    bias = rel_bias[_t5_bucket(jnp.maximum(dist, 0))]
    bias = bias.transpose(2, 0, 1).reshape(N_SWA_KV_HEADS, G, WINDOW, 2 * WINDOW)
    logits = jnp.where(mask[None, :, None, None], logits + bias.astype(jnp.float32), -jnp.inf)
    sink = sinks.reshape(N_SWA_KV_HEADS, G)[None, None, :, :, None, None].astype(jnp.float32)
    m = jnp.maximum(jnp.max(logits, axis=-1, keepdims=True), sink)
    p = jnp.exp(logits - m)
    w = p / (jnp.sum(p, axis=-1, keepdims=True) + jnp.exp(sink - m))
    o = jnp.einsum('bnhgqk,bnkhd->bnqhgd', w.astype(vw.dtype), vw).reshape(B, S, SWA_W)
    o_mem = _memory_attention(q_mem, mem_h, w_mem_kv)
    return jnp.concatenate([o, o_mem], axis=-1) @ w_out


def _rope(x, positions):
    half = x.shape[-1] // 2
    inv_freq = ROPE_THETA ** (-jnp.arange(half, dtype=jnp.float32) / half)
    ang = positions.astype(jnp.float32)[:, :, None, None] * inv_freq
    cos, sin = jnp.cos(ang).astype(x.dtype), jnp.sin(ang).astype(x.dtype)
    x1, x2 = x[..., :half], x[..., half:]
    return jnp.concatenate([x1 * cos - x2 * sin, x2 * cos + x1 * sin], axis=-1)


def _mla_mixer(h, mem_h, positions, w_in, q_norm, w_uq, kv_norm, w_ukv, w_mem_kv, w_out):
    B, S, _ = h.shape
    c_q, c_kv, k_rope, q_mem = jnp.split(
        h @ w_in, [Q_LORA, Q_LORA + KV_LORA, Q_LORA + KV_LORA + QK_ROPE], axis=-1)
    q = (_rmsnorm(c_q, q_norm) @ w_uq).reshape(B, S, MLA_HEADS, QK_NOPE + QK_ROPE)
    kv = (_rmsnorm(c_kv, kv_norm) @ w_ukv).reshape(B, S, MLA_HEADS, QK_NOPE + V_HEAD)
    q_nope, q_pe = jnp.split(q, [QK_NOPE], axis=-1)
    k_nope, v = jnp.split(kv, [QK_NOPE], axis=-1)
    q_pe = _rope(q_pe, positions)
    k_pe = _rope(k_rope[:, :, None, :], positions)
    q = jnp.concatenate([q_nope, q_pe], axis=-1)
    k = jnp.concatenate([k_nope, jnp.broadcast_to(k_pe, (B, S, MLA_HEADS, QK_ROPE))], axis=-1)
    o = _causal_softmax_attention(q.transpose(0, 2, 1, 3), k.transpose(0, 2, 1, 3), v.transpose(0, 2, 1, 3))
    o_mem = _memory_attention(q_mem, mem_h, w_mem_kv)
    return jnp.concatenate([_merge_heads(o), o_mem], axis=-1) @ w_out


def _conv_ffn(h, w_up, conv_w, conv_b, w_down):
    S = h.shape[1]
    u = h @ w_up
    up = jnp.pad(u, ((0, 0), (CONV_WIDTH - 1, 0), (0, 0)))
    c = conv_b
    for tap in range(CONV_WIDTH):
        c = c + conv_w[tap] * up[:, tap:tap + S]
    gate, val = jnp.split(c, [D_FF], axis=-1)
    return (jax.nn.silu(gate) * val) @ w_down


def setup_inputs(seed: int = 0) -> dict:
    key = jax.random.key(seed)
    ks = iter(jax.random.split(key, 32))
    f32 = jnp.float32

    def w(shape, fan_in):
        return jax.random.normal(next(ks), shape, f32) * fan_in ** -0.5

    def gain(shape):
        return 1.0 + 0.05 * jax.random.normal(next(ks), shape, f32)

    na, nb, nc, nd = _n_uses(0), _n_uses(1), _n_uses(2), _n_uses(3)
    x = jax.random.normal(next(ks), (BATCH, SEQ, D_MODEL), f32)
    mem = jax.random.normal(next(ks), (BATCH, MEM_LEN, D_MODEL), f32)
    positions = (jax.random.randint(next(ks), (BATCH, 1), 0, 1024, dtype=jnp.int32)
                 + jnp.arange(SEQ, dtype=jnp.int32)[None, :])
    rel_bias = 0.5 * jax.random.normal(next(ks), (NUM_BUCKETS, N_SWA_HEADS), f32)
    attn_norm = gain((DEPTH, D_MODEL))
    mem_norm = gain((DEPTH, D_MODEL))
    w_mem_kv = w((DEPTH, D_MODEL, 2 * MEM_W), D_MODEL)
    ffn_norm = gain((DEPTH, D_MODEL))
    ffn_w_up = w((DEPTH, D_MODEL, 2 * D_FF), D_MODEL)
    ffn_conv_w = w((DEPTH, CONV_WIDTH, 2 * D_FF), CONV_WIDTH)
    ffn_conv_b = 0.02 * jax.random.normal(next(ks), (DEPTH, 2 * D_FF), f32)
    ffn_w_down = w((DEPTH, D_FF, D_MODEL), D_FF)
    final_norm = gain((D_MODEL,))
    sb_w_in = w((na, D_MODEL, 3 * SB_W + MEM_W), D_MODEL)
    sb_w_out = w((na, SB_W + MEM_W, D_MODEL), SB_W + MEM_W)
    fox_w_in = w((nb, D_MODEL, 3 * FOX_W + N_FOX_HEADS + MEM_W), D_MODEL)
    fox_b_f = 2.0 + 0.1 * jax.random.normal(next(ks), (nb, N_FOX_HEADS), f32)
    fox_w_out = w((nb, FOX_W + MEM_W, D_MODEL), FOX_W + MEM_W)
    swa_w_in = w((nc, D_MODEL, SWA_W + 2 * SWA_KV_W + MEM_W), D_MODEL)
    swa_sinks = 0.5 * jax.random.normal(next(ks), (nc, N_SWA_HEADS), f32)
    swa_w_out = w((nc, SWA_W + MEM_W, D_MODEL), SWA_W + MEM_W)
    mla_w_in = w((nd, D_MODEL, Q_LORA + KV_LORA + QK_ROPE + MEM_W), D_MODEL)
    mla_q_norm = gain((nd, Q_LORA))
    mla_w_uq = w((nd, Q_LORA, MLA_HEADS * (QK_NOPE + QK_ROPE)), Q_LORA)
    mla_kv_norm = gain((nd, KV_LORA))
    mla_w_ukv = w((nd, KV_LORA, MLA_HEADS * (QK_NOPE + V_HEAD)), KV_LORA)
    mla_w_out = w((nd, MLA_W + MEM_W, D_MODEL), MLA_W + MEM_W)
    return {'x': x, 'mem': mem, 'positions': positions, 'rel_bias': rel_bias,
            'attn_norm': attn_norm, 'mem_norm': mem_norm, 'w_mem_kv': w_mem_kv,
            'ffn_norm': ffn_norm, 'ffn_w_up': ffn_w_up, 'ffn_conv_w': ffn_conv_w,
            'ffn_conv_b': ffn_conv_b, 'ffn_w_down': ffn_w_down, 'final_norm': final_norm,
            'sb_w_in': sb_w_in, 'sb_w_out': sb_w_out,
            'fox_w_in': fox_w_in, 'fox_b_f': fox_b_f, 'fox_w_out': fox_w_out,
            'swa_w_in': swa_w_in, 'swa_sinks': swa_sinks, 'swa_w_out': swa_w_out,
            'mla_w_in': mla_w_in, 'mla_q_norm': mla_q_norm, 'mla_w_uq': mla_w_uq,
            'mla_kv_norm': mla_kv_norm, 'mla_w_ukv': mla_w_ukv, 'mla_w_out': mla_w_out}


def reference(x, mem, positions, rel_bias, attn_norm, mem_norm, w_mem_kv, ffn_norm, ffn_w_up,
              ffn_conv_w, ffn_conv_b, ffn_w_down, final_norm, sb_w_in, sb_w_out,
              fox_w_in, fox_b_f, fox_w_out, swa_w_in, swa_sinks, swa_w_out,
              mla_w_in, mla_q_norm, mla_w_uq, mla_kv_norm, mla_w_ukv, mla_w_out):
    for i in range(DEPTH):
        kind, j = i % N_MIXERS, i // N_MIXERS
        h = _rmsnorm(x, attn_norm[i])
        mem_h = _rmsnorm(mem, mem_norm[i])
        if kind == 0:
            y = _sb_mixer(h, mem_h, sb_w_in[j], w_mem_kv[i], sb_w_out[j])
        elif kind == 1:
            y = _fox_mixer(h, mem_h, fox_w_in[j], fox_b_f[j], w_mem_kv[i], fox_w_out[j])
        elif kind == 2:
            y = _swa_mixer(h, mem_h, rel_bias, swa_w_in[j], swa_sinks[j], w_mem_kv[i], swa_w_out[j])
        else:
            y = _mla_mixer(h, mem_h, positions, mla_w_in[j], mla_q_norm[j], mla_w_uq[j],
                           mla_kv_norm[j], mla_w_ukv[j], w_mem_kv[i], mla_w_out[j])
        x = x + y
        x = x + _conv_ffn(_rmsnorm(x, ffn_norm[i]), ffn_w_up[i], ffn_conv_w[i], ffn_conv_b[i], ffn_w_down[i])
    return _rmsnorm(x, final_norm)
```

```python
import functools
import math

import jax
import jax.numpy as jnp
from jax import lax
from jax.experimental import pallas as pl
from jax.experimental.pallas import tpu as pltpu

F32 = jnp.float32
BF16 = jnp.bfloat16

EPS = 1e-6
HEAD_DIM = 128
N_HEADS = 16
SWA_HEAD_DIM = 64
N_SWA_HEADS = 32
N_SWA_KV_HEADS = 4
SWA_GROUP = N_SWA_HEADS // N_SWA_KV_HEADS
WINDOW = 128
Q_LORA = 512
KV_LORA = 256
QK_NOPE = 128
QK_ROPE = 64
ROPE_THETA = 10000.0
MEM_HEADS = 4
MEM_W = MEM_HEADS * HEAD_DIM
NUM_BUCKETS = 32
MAX_DISTANCE = 128
CONV_WIDTH = 3

V7X_VMEM_BYTES = 64 * 1024 * 1024
VMEM_CAP = V7X_VMEM_BYTES - 8 * 1024 * 1024
LANES = 128
HALO = 16


def _cparams(sems, vmem_bytes):
    return pltpu.CompilerParams(dimension_semantics=sems,
                                vmem_limit_bytes=int(min(max(vmem_bytes, 16 * 2**20), VMEM_CAP)))


def _nbytes(shape, dtype):
    return math.prod(shape) * jnp.dtype(dtype).itemsize


def _rms(x, g):
    return x * lax.rsqrt(jnp.mean(x * x, axis=-1, keepdims=True) + EPS) * g


def _dot(a, b):
    return jnp.dot(a, b, preferred_element_type=F32)


def _dot_nt(a, b):
    return lax.dot_general(a, b, (((1,), (1,)), ((), ())), preferred_element_type=F32)


def _norm_matmul_kernel(x_ref, g_ref, w_ref, o_ref, xn_ref):
    @pl.when(pl.program_id(1) == 0)
    def _():
        xn_ref[...] = _rms(x_ref[...], g_ref[...]).astype(BF16)

    o_ref[...] = _dot(xn_ref[...], w_ref[...]).astype(o_ref.dtype)


def _norm_matmul(x, g, w, out_dtype, tm, tn):
    m, k = x.shape
    n = w.shape[1]
    vmem = (2 * (_nbytes((tm, k), F32) + _nbytes((k, tn), BF16) + _nbytes((tm, tn), out_dtype))
            + _nbytes((tm, k), BF16) + 2 * _nbytes((tm, k), F32) + _nbytes((tm, tn), F32))
    return pl.pallas_call(
        _norm_matmul_kernel,
        grid=(m // tm, n // tn),
        in_specs=[pl.BlockSpec((tm, k), lambda i, j: (i, 0)),
                  pl.BlockSpec((1, k), lambda i, j: (0, 0)),
                  pl.BlockSpec((k, tn), lambda i, j: (0, j))],
        out_specs=pl.BlockSpec((tm, tn), lambda i, j: (i, j)),
        out_shape=jax.ShapeDtypeStruct((m, n), out_dtype),
        scratch_shapes=[pltpu.VMEM((tm, k), BF16)],
        compiler_params=_cparams(("parallel", "arbitrary"), vmem),
        name="norm_matmul",
    )(x, g.reshape(1, k), w)


def _matmul_res_kernel(*refs, n_a):
    a_refs, w_refs = refs[:n_a], refs[n_a:2 * n_a]
    x_ref, o_ref = refs[2 * n_a], refs[2 * n_a + 1]
    acc = x_ref[...]
    for a_ref, w_ref in zip(a_refs, w_refs):
        acc = acc + _dot(a_ref[...].astype(BF16), w_ref[...])
    o_ref[...] = acc


def _matmul_res(a_list, w_list, x, tm, tn):
    m, n = x.shape
    vmem = 2 * (2 * _nbytes((tm, tn), F32)) + 2 * _nbytes((tm, tn), F32)
    in_specs = []
    for a in a_list:
        in_specs.append(pl.BlockSpec((tm, a.shape[1]), lambda i, j: (i, 0)))
        vmem += 2 * _nbytes((tm, a.shape[1]), a.dtype)
    for w in w_list:
        in_specs.append(pl.BlockSpec((w.shape[0], tn), lambda i, j: (0, j)))
        vmem += 2 * _nbytes((w.shape[0], tn), w.dtype)
    in_specs.append(pl.BlockSpec((tm, tn), lambda i, j: (i, j)))
    return pl.pallas_call(
        functools.partial(_matmul_res_kernel, n_a=len(a_list)),
        grid=(m // tm, n // tn),
        in_specs=in_specs,
        out_specs=pl.BlockSpec((tm, tn), lambda i, j: (i, j)),
        out_shape=jax.ShapeDtypeStruct((m, n), F32),
        compiler_params=_cparams(("parallel", "arbitrary"), vmem),
        name="matmul_res",
    )(*a_list, *w_list, x)


def _rmsnorm_kernel(x_ref, g_ref, o_ref):
    o_ref[...] = _rms(x_ref[...], g_ref[...])


def _rmsnorm(x, g, tm):
    m, k = x.shape
    return pl.pallas_call(
        _rmsnorm_kernel,
        grid=(m // tm,),
        in_specs=[pl.BlockSpec((tm, k), lambda i: (i, 0)), pl.BlockSpec((1, k), lambda i: (0, 0))],
        out_specs=pl.BlockSpec((tm, k), lambda i: (i, 0)),
        out_shape=jax.ShapeDtypeStruct((m, k), F32),
        compiler_params=_cparams(("parallel",), 8 * _nbytes((tm, k), F32)),
        name="final_rmsnorm",
    )(x, g.reshape(1, k))


def _ffn_up_kernel(x_ref, xh_ref, g_ref, wg_ref, wv_ref, cwg_ref, cwv_ref, cbg_ref, cbv_ref,
                   o_ref, xn_ref, ug_ref, uv_ref, *, tm, tiles_per_seq):
    i = pl.program_id(0)

    @pl.when(pl.program_id(1) == 0)
    def _():
        xn_ref[0:HALO, :] = _rms(xh_ref[...], g_ref[...]).astype(BF16)
        xn_ref[HALO:, :] = _rms(x_ref[...], g_ref[...]).astype(BF16)

    def conv(w_ref, cw_ref, cb_ref, u_ref):
        u_ref[...] = _dot(xn_ref[...], w_ref[...])

        @pl.when(i % tiles_per_seq == 0)
        def _():
            u_ref[0:HALO, :] = jnp.zeros((HALO, u_ref.shape[1]), F32)

        c = cb_ref[...]
        for tap in range(CONV_WIDTH):
            start = HALO - (CONV_WIDTH - 1) + tap
            c = c + cw_ref[tap:tap + 1, :] * u_ref[start:start + tm, :]
        return c

    gate = conv(wg_ref, cwg_ref, cbg_ref, ug_ref)
    val = conv(wv_ref, cwv_ref, cbv_ref, uv_ref)
    o_ref[...] = (gate * (1.0 / (1.0 + jnp.exp(-gate))) * val).astype(o_ref.dtype)


def _ffn_up(x, g, w_up, conv_w, conv_b, seq, tm, tn):
    m, k = x.shape
    d_ff = w_up.shape[1] // 2
    nj = d_ff // tn
    halo_blocks = tm // HALO
    vmem = (2 * (_nbytes((tm, k), F32) + _nbytes((HALO, k), F32) + 2 * _nbytes((k, tn), BF16)
                 + _nbytes((tm, tn), BF16))
            + _nbytes((tm + HALO, k), BF16) + 2 * _nbytes((tm + HALO, tn), F32)
            + 2 * _nbytes((tm, k), F32) + 6 * _nbytes((tm, tn), F32))
    return pl.pallas_call(
        functools.partial(_ffn_up_kernel, tm=tm, tiles_per_seq=seq // tm),
        grid=(m // tm, nj),
        in_specs=[pl.BlockSpec((tm, k), lambda i, j: (i, 0)),
                  pl.BlockSpec((HALO, k), lambda i, j: (jnp.maximum(i * halo_blocks - 1, 0), 0)),
                  pl.BlockSpec((1, k), lambda i, j: (0, 0)),
                  pl.BlockSpec((k, tn), lambda i, j: (0, j)),
                  pl.BlockSpec((k, tn), lambda i, j: (0, nj + j)),
                  pl.BlockSpec((CONV_WIDTH, tn), lambda i, j: (0, j)),
                  pl.BlockSpec((CONV_WIDTH, tn), lambda i, j: (0, nj + j)),
                  pl.BlockSpec((1, tn), lambda i, j: (0, j)),
                  pl.BlockSpec((1, tn), lambda i, j: (0, nj + j))],
        out_specs=pl.BlockSpec((tm, tn), lambda i, j: (i, j)),
        out_shape=jax.ShapeDtypeStruct((m, d_ff), BF16),
        scratch_shapes=[pltpu.VMEM((tm + HALO, k), BF16),
                        pltpu.VMEM((tm + HALO, tn), F32),
                        pltpu.VMEM((tm + HALO, tn), F32)],
        compiler_params=_cparams(("parallel", "arbitrary"), vmem),
        name="ffn_up_conv_gate",
    )(x, x, g.reshape(1, k), w_up, w_up, conv_w, conv_w,
      conv_b.reshape(1, -1), conv_b.reshape(1, -1))


def _diag_column(row):
    n = row.shape[1]
    eye = lax.broadcasted_iota(jnp.int32, (n, n), 0) == lax.broadcasted_iota(jnp.int32, (n, n), 1)
    return jnp.sum(jnp.where(eye, row, 0.0), axis=1, keepdims=True)


def _flash_kernel(*refs, tq, scale, has_bias):
    if has_bias:
        q_ref, k_ref, v_ref, cq_ref, ck_ref, o_ref, m_ref, l_ref, acc_ref = refs
    else:
        q_ref, k_ref, v_ref, o_ref, m_ref, l_ref, acc_ref = refs
    qi = pl.program_id(2)
    m_ref[...] = jnp.full(m_ref.shape, -jnp.inf, F32)
    l_ref[...] = jnp.zeros(l_ref.shape, F32)
    acc_ref[...] = jnp.zeros(acc_ref.shape, F32)
    q = q_ref[...]
    if has_bias:
        cq = _diag_column(cq_ref[...])

    def step(kb, on_diagonal):
        ks = pl.multiple_of(kb * tq, tq)
        s = _dot_nt(q, k_ref[pl.ds(ks, tq), :]) * scale
        if has_bias:
            s = s + (cq - ck_ref[kb])
        if on_diagonal:
            row = lax.broadcasted_iota(jnp.int32, (tq, tq), 0)
            col = lax.broadcasted_iota(jnp.int32, (tq, tq), 1)
            s = jnp.where(col <= row, s, -jnp.inf)
        m_prev = m_ref[...]
        m_new = jnp.maximum(m_prev, jnp.max(s, axis=1, keepdims=True))
        alpha = jnp.exp(m_prev - m_new)
        p = jnp.exp(s - m_new)
        l_ref[...] = alpha * l_ref[...] + jnp.sum(p, axis=1, keepdims=True)
        acc_ref[...] = alpha * acc_ref[...] + _dot(p.astype(BF16), v_ref[pl.ds(ks, tq), :])
        m_ref[...] = m_new

    def body(kb, carry):
        step(kb, False)
        return carry

    lax.fori_loop(0, qi, body, 0)
    step(qi, True)
    o_ref[...] = (acc_ref[...] / l_ref[...]).astype(o_ref.dtype)


def _flash_attention(q_arr, k_arr, v_arr, q_off, k_off, v_off, dk, dv, n_heads, scale, c=None, tq=256):
    b, s, _ = q_arr.shape
    nq = s // tq
    in_specs = [pl.BlockSpec((None, tq, dk), lambda bi, h, qi: (bi, qi, q_off + h)),
                pl.BlockSpec((None, s, dk), lambda bi, h, qi: (bi, 0, k_off + h)),
                pl.BlockSpec((None, s, dv), lambda bi, h, qi: (bi, 0, v_off + h))]
    args = [q_arr, k_arr, v_arr]
    if c is not None:
        in_specs += [pl.BlockSpec((None, None, None, 1, tq), lambda bi, h, qi: (bi, h, qi, 0, 0)),
                     pl.BlockSpec((None, None, nq, 1, tq), lambda bi, h, qi: (bi, h, 0, 0, 0))]
        args += [c, c]
    vmem = (2 * (_nbytes((tq, dk), BF16) + _nbytes((s, dk), BF16) + _nbytes((s, dv), BF16)
                 + _nbytes((tq, dv), BF16) + 2 * _nbytes((nq * 8, tq), F32))
            + 3 * _nbytes((tq, LANES), F32) + 8 * _nbytes((tq, tq), F32))
    return pl.pallas_call(
        functools.partial(_flash_kernel, tq=tq, scale=scale, has_bias=c is not None),
        grid=(b, n_heads, nq),
        in_specs=in_specs,
        out_specs=pl.BlockSpec((None, tq, dv), lambda bi, h, qi: (bi, qi, h)),
        out_shape=jax.ShapeDtypeStruct((b, s, n_heads * dv), BF16),
        scratch_shapes=[pltpu.VMEM((tq, 1), F32), pltpu.VMEM((tq, 1), F32), pltpu.VMEM((tq, dv), F32)],
        compiler_params=_cparams(("parallel", "parallel", "arbitrary"), vmem),
        name="flash_attention",
    )(*args)


def _sb_kernel(q_ref, k_ref, v_ref, o_ref, carry_ref, acc_ref, *, tq, scale):
    qi = pl.program_id(2)
    carry_ref[...] = jnp.zeros(carry_ref.shape, F32)
    acc_ref[...] = jnp.zeros(acc_ref.shape, F32)
    q = q_ref[...]
    row = lax.broadcasted_iota(jnp.int32, (tq, tq), 0)
    col = lax.broadcasted_iota(jnp.int32, (tq, tq), 1)
    suffix = jnp.where(row > col, 1.0, 0.0).astype(BF16)

    def step(kb, on_diagonal):
        ks = pl.multiple_of(kb * tq, tq)
        z = _dot_nt(q, k_ref[pl.ds(ks, tq), :]) * scale
        log_beta = jnp.minimum(z, 0.0) - jnp.log(1.0 + jnp.exp(-jnp.abs(z)))
        log_keep = log_beta - z
        if on_diagonal:
            strict = col < row
            log_keep = jnp.where(strict, log_keep, 0.0)
        hi = log_keep.astype(BF16)
        lo = (log_keep - hi.astype(F32)).astype(BF16)
        log_after = _dot(hi, suffix) + _dot(lo, suffix) + carry_ref[...]
        a = jnp.exp(log_beta + log_after)
        if on_diagonal:
            a = jnp.where(strict, a, 0.0)
        acc_ref[...] += _dot(a.astype(BF16), v_ref[pl.ds(ks, tq), :])
        carry_ref[...] += jnp.sum(log_keep, axis=1, keepdims=True)

    step(qi, True)

    def body(it, carry):
        step(qi - 1 - it, False)
        return carry

    lax.fori_loop(0, qi, body, 0)
    o_ref[...] = acc_ref[...].astype(o_ref.dtype)


def _sb_attention(proj, n_heads, tq=256):
    b, s, _ = proj.shape
    d = HEAD_DIM
    vmem = (2 * (2 * _nbytes((tq, d), BF16) + 2 * _nbytes((s, d), BF16))
            + 2 * _nbytes((tq, LANES), F32) + 12 * _nbytes((tq, tq), F32))
    return pl.pallas_call(
        functools.partial(_sb_kernel, tq=tq, scale=d ** -0.5),
        grid=(b, n_heads, s // tq),
        in_specs=[pl.BlockSpec((None, tq, d), lambda bi, h, qi: (bi, qi, h)),
                  pl.BlockSpec((None, s, d), lambda bi, h, qi: (bi, 0, n_heads + h)),
                  pl.BlockSpec((None, s, d), lambda bi, h, qi: (bi, 0, 2 * n_heads + h))],
        out_specs=pl.BlockSpec((None, tq, d), lambda bi, h, qi: (bi, qi, h)),
        out_shape=jax.ShapeDtypeStruct((b, s, n_heads * d), BF16),
        scratch_shapes=[pltpu.VMEM((tq, 1), F32), pltpu.VMEM((tq, d), F32)],
        compiler_params=_cparams(("parallel", "parallel", "arbitrary"), vmem),
        name="stick_breaking_attention",
    )(proj, proj, proj)


def _fox_gate_kernel(fl_ref, b_ref, o_ref, carry_ref):
    ts = fl_ref.shape[0]

    @pl.when(pl.program_id(1) == 0)
    def _():
        carry_ref[...] = jnp.zeros(carry_ref.shape, F32)

    z = fl_ref[...] + b_ref[...]
    log_f = jnp.minimum(z, 0.0) - jnp.log(1.0 + jnp.exp(-jnp.abs(z)))
    row = lax.broadcasted_iota(jnp.int32, (ts, ts), 0)
    col = lax.broadcasted_iota(jnp.int32, (ts, ts), 1)
    prefix = jnp.where(col <= row, 1.0, 0.0).astype(BF16)
    hi = log_f.astype(BF16)
    rest = log_f - hi.astype(F32)
    mid = rest.astype(BF16)
    lo = (rest - mid.astype(F32)).astype(BF16)
    c = _dot(prefix, hi) + _dot(prefix, mid) + _dot(prefix, lo) + carry_ref[0:1, :]
    o_ref[...] = c
    carry_ref[0:1, :] = c[ts - 1:ts, :]


def _fox_gates(f_logit, b_f, ts=256):
    b, s, w = f_logit.shape
    return pl.pallas_call(
        _fox_gate_kernel,
        grid=(b, s // ts),
        in_specs=[pl.BlockSpec((None, ts, w), lambda bi, i: (bi, i, 0)),
                  pl.BlockSpec((1, w), lambda bi, i: (0, 0))],
        out_specs=pl.BlockSpec((None, ts, w), lambda bi, i: (bi, i, 0)),
        out_shape=jax.ShapeDtypeStruct((b, s, w), F32),
        scratch_shapes=[pltpu.VMEM((8, w), F32)],
        compiler_params=_cparams(("parallel", "arbitrary"), 16 * 2**20),
        name="fox_gate_cumsum",
    )(f_logit, b_f.reshape(1, w))


def _mem_attn_kernel(q_ref, kv_ref, o_ref, *, scale):
    outs = []
    for h in range(MEM_HEADS):
        q = q_ref[:, h * HEAD_DIM:(h + 1) * HEAD_DIM].astype(BF16)
        k = kv_ref[:, h * HEAD_DIM:(h + 1) * HEAD_DIM]
        v = kv_ref[:, MEM_W + h * HEAD_DIM:MEM_W + (h + 1) * HEAD_DIM]
        s = _dot_nt(q, k) * scale
        p = jnp.exp(s - jnp.max(s, axis=1, keepdims=True))
        o = _dot(p.astype(BF16), v) / jnp.sum(p, axis=1, keepdims=True)
        outs.append(o.astype(o_ref.dtype))
    o_ref[...] = jnp.concatenate(outs, axis=1)


def _mem_attention(q_arr, q_block, mem_kv, tq=512):
    b, s, _ = q_arr.shape
    length = mem_kv.shape[1]
    vmem = (2 * (_nbytes((tq, MEM_W), q_arr.dtype) + _nbytes((length, 2 * MEM_W), BF16)
                 + _nbytes((tq, MEM_W), BF16)) + 8 * _nbytes((tq, length), F32))
    return pl.pallas_call(
        functools.partial(_mem_attn_kernel, scale=HEAD_DIM ** -0.5),
        grid=(b, s // tq),
        in_specs=[pl.BlockSpec((None, tq, MEM_W), lambda bi, i: (bi, i, q_block)),
                  pl.BlockSpec((None, length, 2 * MEM_W), lambda bi, i: (bi, 0, 0))],
        out_specs=pl.BlockSpec((None, tq, MEM_W), lambda bi, i: (bi, i, 0)),
        out_shape=jax.ShapeDtypeStruct((b, s, MEM_W), BF16),
        compiler_params=_cparams(("parallel", "parallel"), vmem),
        name="memory_attention",
    )(q_arr, mem_kv)


def _t5_bucket_table():
    max_exact = NUM_BUCKETS // 2
    qi = jnp.arange(WINDOW)[:, None]
    kj = jnp.arange(2 * WINDOW)[None, :]
    dist = jnp.maximum(WINDOW + qi - kj, 0)
    d = jnp.maximum(dist, 1).astype(F32)
    large = max_exact + (jnp.log(d / max_exact) / math.log(MAX_DISTANCE / max_exact)
                         * (NUM_BUCKETS - max_exact)).astype(jnp.int32)
    return jnp.where(dist < max_exact, dist, jnp.minimum(large, NUM_BUCKETS - 1)).astype(jnp.int32)


def _t5_bias_kernel(rb_ref, bucket_ref, o_ref):
    h = pl.program_id(0)
    bucket = bucket_ref[...]
    bias = jnp.zeros(bucket.shape, F32)
    for b in range(NUM_BUCKETS):
        bias = jnp.where(bucket == b, rb_ref[b, h], bias)
    o_ref[...] = bias


def _t5_bias(rel_bias):
    n_heads = rel_bias.shape[1]
    return pl.pallas_call(
        _t5_bias_kernel,
        grid=(n_heads,),
        in_specs=[pl.BlockSpec(memory_space=pltpu.SMEM),
                  pl.BlockSpec((WINDOW, 2 * WINDOW), lambda h: (0, 0))],
        out_specs=pl.BlockSpec((None, WINDOW, 2 * WINDOW), lambda h: (h, 0, 0)),
        out_shape=jax.ShapeDtypeStruct((n_heads, WINDOW, 2 * WINDOW), F32),
        compiler_params=_cparams(("parallel",), 16 * 2**20),
        name="t5_bias",
    )(rel_bias, _t5_bucket_table())


def _swa_kernel(sink_ref, q_ref, kvc_ref, kvp_ref, bias_ref, o_ref, *, scale):
    n = pl.program_id(1)
    d = SWA_HEAD_DIM
    kv_w = N_SWA_KV_HEADS * d
    qi = lax.broadcasted_iota(jnp.int32, (WINDOW, WINDOW), 0)
    kj = lax.broadcasted_iota(jnp.int32, (WINDOW, WINDOW), 1)
    prev_ok = jnp.logical_and(kj > qi, n > 0)
    cur_ok = kj <= qi
    outs = []
    for kvh in range(N_SWA_KV_HEADS):
        kc = kvc_ref[:, kvh * d:(kvh + 1) * d]
        kp = kvp_ref[:, kvh * d:(kvh + 1) * d]
        vc = kvc_ref[:, kv_w + kvh * d:kv_w + (kvh + 1) * d]
        vp = kvp_ref[:, kv_w + kvh * d:kv_w + (kvh + 1) * d]
        for g in range(SWA_GROUP):
            h = kvh * SWA_GROUP + g
            q = q_ref[:, h * d:(h + 1) * d]
            sp = jnp.where(prev_ok, _dot_nt(q, kp) * scale + bias_ref[h, :, 0:WINDOW], -jnp.inf)
            sc = jnp.where(cur_ok, _dot_nt(q, kc) * scale + bias_ref[h, :, WINDOW:2 * WINDOW], -jnp.inf)
            sink = sink_ref[h]
            m = jnp.maximum(jnp.maximum(jnp.max(sp, axis=1, keepdims=True),
                                        jnp.max(sc, axis=1, keepdims=True)), sink)
            pp = jnp.exp(sp - m)
            pc = jnp.exp(sc - m)
            denom = (jnp.sum(pp, axis=1, keepdims=True) + jnp.sum(pc, axis=1, keepdims=True)
                     + jnp.exp(sink - m))
            o = (_dot(pp.astype(BF16), vp) + _dot(pc.astype(BF16), vc)) / denom
            outs.append(o.astype(o_ref.dtype))
    o_ref[...] = jnp.concatenate(outs, axis=1)


def _swa_attention(proj, sinks, bias):
    b, s, _ = proj.shape
    q_w = N_SWA_HEADS * SWA_HEAD_DIM
    kv_w = 2 * N_SWA_KV_HEADS * SWA_HEAD_DIM
    kv_block = q_w // kv_w
    vmem = (2 * (2 * _nbytes((WINDOW, q_w), BF16) + 2 * _nbytes((WINDOW, kv_w), BF16)
                 + _nbytes(bias.shape, F32)) + 16 * 2**20)
    return pl.pallas_call(
        functools.partial(_swa_kernel, scale=SWA_HEAD_DIM ** -0.5),
        grid=(b, s // WINDOW),
        in_specs=[pl.BlockSpec(memory_space=pltpu.SMEM),
                  pl.BlockSpec((None, WINDOW, q_w), lambda bi, n: (bi, n, 0)),
                  pl.BlockSpec((None, WINDOW, kv_w), lambda bi, n: (bi, n, kv_block)),
                  pl.BlockSpec((None, WINDOW, kv_w), lambda bi, n: (bi, jnp.maximum(n - 1, 0), kv_block)),
                  pl.BlockSpec(bias.shape, lambda bi, n: (0, 0, 0))],
        out_specs=pl.BlockSpec((None, WINDOW, q_w), lambda bi, n: (bi, n, 0)),
        out_shape=jax.ShapeDtypeStruct((b, s, q_w), BF16),
        compiler_params=_cparams(("parallel", "arbitrary"), vmem),
        name="sliding_window_attention",
    )(sinks, proj, proj, proj, bias)


def _rope_table_kernel(pos_ref, invf_ref, o_ref):
    ang = pos_ref[...].astype(F32) * invf_ref[...]
    lane = lax.broadcasted_iota(jnp.int32, ang.shape, 1)
    half = QK_ROPE // 2
    sin_signed = jnp.where(lane < QK_ROPE + half, -jnp.sin(ang), jnp.sin(ang))
    o_ref[...] = jnp.where(lane < QK_ROPE, jnp.cos(ang), sin_signed)


def _rope_table(positions, tm=1024):
    m = positions.shape[0]
    half = QK_ROPE // 2
    inv_freq = ROPE_THETA ** (-jnp.arange(half, dtype=F32) / half)
    invf = jnp.tile(inv_freq, LANES // half).reshape(1, LANES)
    return pl.pallas_call(
        _rope_table_kernel,
        grid=(m // tm,),
        in_specs=[pl.BlockSpec((tm, 1), lambda i: (i, 0)), pl.BlockSpec((1, LANES), lambda i: (0, 0))],
        out_specs=pl.BlockSpec((tm, LANES), lambda i: (i, 0)),
        out_shape=jax.ShapeDtypeStruct((m, LANES), F32),
        compiler_params=_cparams(("parallel",), 32 * 2**20),
        name="rope_table",
    )(positions, invf)


def _apply_rope(x_and_partner, cs):
    z = x_and_partner * cs
    z = z + pltpu.roll(z, QK_ROPE, axis=1)
    lane = lax.broadcasted_iota(jnp.int32, z.shape, 1)
    return jnp.where(lane < QK_ROPE, z, 0.0)


def _mla_q_kernel(cq_ref, g_ref, w_ref, cs_ref, o_ref, cn_ref):
    @pl.when(pl.program_id(1) == 0)
    def _():
        cn_ref[...] = _rms(cq_ref[...], g_ref[...]).astype(BF16)

    y = _dot(cn_ref[...], w_ref[...])
    pe = _apply_rope(y[:, QK_NOPE:], cs_ref[...])
    o_ref[...] = jnp.concatenate([y[:, :QK_NOPE], pe], axis=1).astype(o_ref.dtype)


def _mla_q(proj, q_norm, w_uq_aug, cs, tm=512):
    m = proj.shape[0]
    hw = 2 * LANES
    n_heads = w_uq_aug.shape[1] // hw
    return pl.pallas_call(
        _mla_q_kernel,
        grid=(m // tm, n_heads),
        in_specs=[pl.BlockSpec((tm, Q_LORA), lambda i, h: (i, 1)),
                  pl.BlockSpec((1, Q_LORA), lambda i, h: (0, 0)),
                  pl.BlockSpec((Q_LORA, hw), lambda i, h: (0, h)),
                  pl.BlockSpec((tm, LANES), lambda i, h: (i, 0))],
        out_specs=pl.BlockSpec((tm, hw), lambda i, h: (i, h)),
        out_shape=jax.ShapeDtypeStruct((m, n_heads * hw), BF16),
        scratch_shapes=[pltpu.VMEM((tm, Q_LORA), BF16)],
        compiler_params=_cparams(("parallel", "arbitrary"), 32 * 2**20),
        name="mla_q_expand",
    )(proj, q_norm.reshape(1, Q_LORA), w_uq_aug, cs)


def _mla_kv_kernel(ckv_ref, g_ref, w_ref, kr_ref, cs_ref, k_ref, v_ref, cn_ref, kpe_ref):
    @pl.when(pl.program_id(1) == 0)
    def _():
        cn_ref[...] = _rms(ckv_ref[...], g_ref[...]).astype(BF16)
        kpe_ref[...] = _apply_rope(kr_ref[...], cs_ref[...])

    y = _dot(cn_ref[...], w_ref[...])
    k_ref[...] = jnp.concatenate([y[:, :QK_NOPE], kpe_ref[...]], axis=1).astype(k_ref.dtype)
    v_ref[...] = y[:, QK_NOPE:].astype(v_ref.dtype)


def _mla_kv(proj, kv_norm, w_ukv, cs, tm=512):
    m = proj.shape[0]
    hw = 2 * LANES
    n_heads = w_ukv.shape[1] // hw
    return pl.pallas_call(
        _mla_kv_kernel,
        grid=(m // tm, n_heads),
        in_specs=[pl.BlockSpec((tm, KV_LORA), lambda i, h: (i, 4)),
                  pl.BlockSpec((1, KV_LORA), lambda i, h: (0, 0)),
                  pl.BlockSpec((KV_LORA, hw), lambda i, h: (0, h)),
                  pl.BlockSpec((tm, LANES), lambda i, h: (i, 10)),
                  pl.BlockSpec((tm, LANES), lambda i, h: (i, 0))],
        out_specs=[pl.BlockSpec((tm, hw), lambda i, h: (i, h)),
                   pl.BlockSpec((tm, LANES), lambda i, h: (i, h))],
        out_shape=[jax.ShapeDtypeStruct((m, n_heads * hw), BF16),
                   jax.ShapeDtypeStruct((m, n_heads * LANES), BF16)],
        scratch_shapes=[pltpu.VMEM((tm, KV_LORA), BF16), pltpu.VMEM((tm, LANES), F32)],
        compiler_params=_cparams(("parallel", "arbitrary"), 32 * 2**20),
        name="mla_kv_expand",
    )(proj, kv_norm.reshape(1, KV_LORA), w_ukv, proj, cs)


def _swap_halves(w):
    half = w.shape[-1] // 2
    return jnp.concatenate([w[..., half:], w[..., :half]], axis=-1)


def _mixer_out(x2, o, o_mem, w_out, b, s):
    d = x2.shape[1]
    split = o.shape[-1]
    w = w_out.astype(BF16)
    return _matmul_res([o.reshape(b * s, split), o_mem.reshape(b * s, MEM_W)],
                       [w[:split], w[split:]], x2, tm=512, tn=min(512, d))


def _sb_layer(x2, mem_kv, g, w_in, w_out, b, s):
    proj = _norm_matmul(x2, g, w_in.astype(BF16), BF16, tm=512, tn=512).reshape(b, s, -1)
    o = _sb_attention(proj, N_HEADS)
    o_mem = _mem_attention(proj, 3 * N_HEADS * HEAD_DIM // MEM_W, mem_kv)
    return _mixer_out(x2, o, o_mem, w_out, b, s)


def _fox_layer(x2, mem_kv, g, w_in, b_f, w_out, b, s, tq=256):
    qkv_w = 3 * N_HEADS * HEAD_DIM
    w_main = jnp.concatenate([w_in[:, :qkv_w], w_in[:, qkv_w + N_HEADS:]], axis=1).astype(BF16)
    w_gate = jnp.pad(w_in[:, qkv_w:qkv_w + N_HEADS], ((0, 0), (0, LANES - N_HEADS))).astype(BF16)
    proj = _norm_matmul(x2, g, w_main, BF16, tm=512, tn=512).reshape(b, s, -1)
    f_logit = _norm_matmul(x2, g, w_gate, F32, tm=512, tn=LANES).reshape(b, s, LANES)
    c = _fox_gates(f_logit, jnp.pad(b_f, (0, LANES - N_HEADS)))
    c = c[:, :, :N_HEADS].transpose(0, 2, 1).reshape(b, N_HEADS, s // tq, 1, tq)
    o = _flash_attention(proj, proj, proj, 0, N_HEADS, 2 * N_HEADS, HEAD_DIM, HEAD_DIM, N_HEADS,
                         HEAD_DIM ** -0.5, c=c, tq=tq)
    o_mem = _mem_attention(proj, qkv_w // MEM_W, mem_kv)
    return _mixer_out(x2, o, o_mem, w_out, b, s)


def _swa_layer(x2, mem_kv, g, rel_bias, w_in, sinks, w_out, b, s):
    proj = _norm_matmul(x2, g, w_in.astype(BF16), BF16, tm=512, tn=512).reshape(b, s, -1)
    o = _swa_attention(proj, sinks, _t5_bias(rel_bias))
    q_w = N_SWA_HEADS * SWA_HEAD_DIM
    kv_w = 2 * N_SWA_KV_HEADS * SWA_HEAD_DIM
    o_mem = _mem_attention(proj, (q_w + kv_w) // MEM_W, mem_kv)
    return _mixer_out(x2, o, o_mem, w_out, b, s)


def _mla_layer(x2, mem_kv, g, positions, w_in, q_norm, w_uq, kv_norm, w_ukv, w_out, b, s):
    d = x2.shape[1]
    o1, o2, o3 = Q_LORA, Q_LORA + KV_LORA, Q_LORA + KV_LORA + QK_ROPE
    w_main = jnp.concatenate([w_in[:, o3:], w_in[:, :o3], _swap_halves(w_in[:, o2:o3])], axis=1).astype(BF16)
    proj = _norm_matmul(x2, g, w_main, F32, tm=512, tn=w_main.shape[1])
    cs = _rope_table(positions.reshape(b * s, 1))
    wq = w_uq.reshape(Q_LORA, N_HEADS, QK_NOPE + QK_ROPE)
    wq = jnp.concatenate([wq, _swap_halves(wq[:, :, QK_NOPE:])], axis=-1).reshape(Q_LORA, -1).astype(BF16)
    q = _mla_q(proj, q_norm, wq, cs).reshape(b, s, -1)
    k, v = _mla_kv(proj, kv_norm, w_ukv.astype(BF16), cs)
    k, v = k.reshape(b, s, -1), v.reshape(b, s, -1)
    o = _flash_attention(q, k, v, 0, 0, 0, 2 * LANES, HEAD_DIM, N_HEADS, (QK_NOPE + QK_ROPE) ** -0.5)
    o_mem = _mem_attention(proj.reshape(b, s, -1), 0, mem_kv)
    return _mixer_out(x2, o, o_mem, w_out, b, s)


def kernel(x, mem, positions, rel_bias, attn_norm, mem_norm, w_mem_kv, ffn_norm, ffn_w_up, ffn_conv_w, ffn_conv_b, ffn_w_down, final_norm, sb_w_in, sb_w_out, fox_w_in, fox_b_f, fox_w_out, swa_w_in, swa_sinks, swa_w_out, mla_w_in, mla_q_norm, mla_w_uq, mla_kv_norm, mla_w_ukv, mla_w_out):
    b, s, d = x.shape
    depth = attn_norm.shape[0]
    mem_len = mem.shape[1]
    x2 = x.reshape(b * s, d)
    mem2 = mem.reshape(b * mem_len, d)
    for i in range(depth):
        kind, j = i % 4, i // 4
        mem_kv = _norm_matmul(mem2, mem_norm[i], w_mem_kv[i].astype(BF16), BF16,
                              tm=b * mem_len, tn=2 * MEM_W).reshape(b, mem_len, 2 * MEM_W)
        g = attn_norm[i]
        if kind == 0:
            x2 = _sb_layer(x2, mem_kv, g, sb_w_in[j], sb_w_out[j], b, s)
        elif kind == 1:
            x2 = _fox_layer(x2, mem_kv, g, fox_w_in[j], fox_b_f[j], fox_w_out[j], b, s)
        elif kind == 2:
            x2 = _swa_layer(x2, mem_kv, g, rel_bias, swa_w_in[j], swa_sinks[j], swa_w_out[j], b, s)
        else:
            x2 = _mla_layer(x2, mem_kv, g, positions, mla_w_in[j], mla_q_norm[j], mla_w_uq[j],
                            mla_kv_norm[j], mla_w_ukv[j], mla_w_out[j], b, s)
        gated = _ffn_up(x2, ffn_norm[i], ffn_w_up[i].astype(BF16), ffn_conv_w[i], ffn_conv_b[i],
                        seq=s, tm=512, tn=512)
        x2 = _matmul_res([gated], [ffn_w_down[i].astype(BF16)], x2, tm=512, tn=512)
    return _rmsnorm(x2, final_norm, tm=512).reshape(b, s, d)
```

```python
import functools
import math

import jax
import jax.numpy as jnp
from jax import lax
from jax.experimental import pallas as pl
from jax.experimental.pallas import tpu as pltpu

F32 = jnp.float32
BF16 = jnp.bfloat16

EPS = 1e-6
HEAD_DIM = 128
N_HEADS = 16
SWA_HEAD_DIM = 64
N_SWA_HEADS = 32
N_SWA_KV_HEADS = 4
SWA_GROUP = N_SWA_HEADS // N_SWA_KV_HEADS
WINDOW = 128
Q_LORA = 512
KV_LORA = 256
QK_NOPE = 128
QK_ROPE = 64
ROPE_THETA = 10000.0
MEM_HEADS = 4
MEM_W = MEM_HEADS * HEAD_DIM
NUM_BUCKETS = 32
MAX_DISTANCE = 128
CONV_WIDTH = 3

V7X_VMEM_BYTES = 64 * 1024 * 1024
VMEM_CAP = V7X_VMEM_BYTES - 8 * 1024 * 1024
LANES = 128
HALO = 16


def _cparams(sems, vmem_bytes):
    return pltpu.CompilerParams(dimension_semantics=sems,
                                vmem_limit_bytes=int(min(max(vmem_bytes, 16 * 2**20), VMEM_CAP)))


def _nbytes(shape, dtype):
    return math.prod(shape) * jnp.dtype(dtype).itemsize


def _rms(x, g):
    return x * lax.rsqrt(jnp.mean(x * x, axis=-1, keepdims=True) + EPS) * g


def _dot(a, b):
    return jnp.dot(a, b, preferred_element_type=F32)


def _dot_nt(a, b):
    return lax.dot_general(a, b, (((1,), (1,)), ((), ())), preferred_element_type=F32)


def _norm_matmul_kernel(x_ref, g_ref, w_ref, o_ref, xn_ref):
    @pl.when(pl.program_id(1) == 0)
    def _():
        xn_ref[...] = _rms(x_ref[...], g_ref[...]).astype(BF16)

    o_ref[...] = _dot(xn_ref[...], w_ref[...]).astype(o_ref.dtype)


def _norm_matmul(x, g, w, out_dtype, tm, tn):
    m, k = x.shape
    n = w.shape[1]
    vmem = (2 * (_nbytes((tm, k), F32) + _nbytes((k, tn), BF16) + _nbytes((tm, tn), out_dtype))
            + _nbytes((tm, k), BF16) + 2 * _nbytes((tm, k), F32) + _nbytes((tm, tn), F32))
    return pl.pallas_call(
        _norm_matmul_kernel,
        grid=(m // tm, n // tn),
        in_specs=[pl.BlockSpec((tm, k), lambda i, j: (i, 0)),
                  pl.BlockSpec((1, k), lambda i, j: (0, 0)),
                  pl.BlockSpec((k, tn), lambda i, j: (0, j))],
        out_specs=pl.BlockSpec((tm, tn), lambda i, j: (i, j)),
        out_shape=jax.ShapeDtypeStruct((m, n), out_dtype),
        scratch_shapes=[pltpu.VMEM((tm, k), BF16)],
        compiler_params=_cparams(("parallel", "arbitrary"), vmem),
        name="norm_matmul",
    )(x, g.reshape(1, k), w)


def _matmul_res_kernel(*refs, n_a):
    a_refs, w_refs = refs[:n_a], refs[n_a:2 * n_a]
    x_ref, o_ref = refs[2 * n_a], refs[2 * n_a + 1]
    acc = x_ref[...]
    for a_ref, w_ref in zip(a_refs, w_refs):
        acc = acc + _dot(a_ref[...].astype(BF16), w_ref[...])
    o_ref[...] = acc


def _matmul_res(a_list, w_list, x, tm, tn):
    m, n = x.shape
    vmem = 2 * (2 * _nbytes((tm, tn), F32)) + 2 * _nbytes((tm, tn), F32)
    in_specs = []
    for a in a_list:
        in_specs.append(pl.BlockSpec((tm, a.shape[1]), lambda i, j: (i, 0)))
        vmem += 2 * _nbytes((tm, a.shape[1]), a.dtype)
    for w in w_list:
        in_specs.append(pl.BlockSpec((w.shape[0], tn), lambda i, j: (0, j)))
        vmem += 2 * _nbytes((w.shape[0], tn), w.dtype)
    in_specs.append(pl.BlockSpec((tm, tn), lambda i, j: (i, j)))
    return pl.pallas_call(
        functools.partial(_matmul_res_kernel, n_a=len(a_list)),
        grid=(m // tm, n // tn),
        in_specs=in_specs,
        out_specs=pl.BlockSpec((tm, tn), lambda i, j: (i, j)),
        out_shape=jax.ShapeDtypeStruct((m, n), F32),
        compiler_params=_cparams(("parallel", "arbitrary"), vmem),
        name="matmul_res",
    )(*a_list, *w_list, x)


def _rmsnorm_kernel(x_ref, g_ref, o_ref):
    o_ref[...] = _rms(x_ref[...], g_ref[...])


def _rmsnorm(x, g, tm):
    m, k = x.shape
    return pl.pallas_call(
        _rmsnorm_kernel,
        grid=(m // tm,),
        in_specs=[pl.BlockSpec((tm, k), lambda i: (i, 0)), pl.BlockSpec((1, k), lambda i: (0, 0))],
        out_specs=pl.BlockSpec((tm, k), lambda i: (i, 0)),
        out_shape=jax.ShapeDtypeStruct((m, k), F32),
        compiler_params=_cparams(("parallel",), 8 * _nbytes((tm, k), F32)),
        name="final_rmsnorm",
    )(x, g.reshape(1, k))


def _ffn_up_kernel(x_ref, xh_ref, g_ref, wg_ref, wv_ref, cwg_ref, cwv_ref, cbg_ref, cbv_ref,
                   o_ref, xn_ref, ug_ref, uv_ref, *, tm, tiles_per_seq):
    i = pl.program_id(0)

    @pl.when(pl.program_id(1) == 0)
    def _():
        xn_ref[0:HALO, :] = _rms(xh_ref[...], g_ref[...]).astype(BF16)
        xn_ref[HALO:, :] = _rms(x_ref[...], g_ref[...]).astype(BF16)

    def conv(w_ref, cw_ref, cb_ref, u_ref):
        u_ref[...] = _dot(xn_ref[...], w_ref[...])

        @pl.when(i % tiles_per_seq == 0)
        def _():
            u_ref[0:HALO, :] = jnp.zeros((HALO, u_ref.shape[1]), F32)

        c = cb_ref[...]
        for tap in range(CONV_WIDTH):
            start = HALO - (CONV_WIDTH - 1) + tap
            c = c + cw_ref[tap:tap + 1, :] * u_ref[start:start + tm, :]
        return c

    gate = conv(wg_ref, cwg_ref, cbg_ref, ug_ref)
    val = conv(wv_ref, cwv_ref, cbv_ref, uv_ref)
    o_ref[...] = (gate * (1.0 / (1.0 + jnp.exp(-gate))) * val).astype(o_ref.dtype)


def _ffn_up(x, g, w_up, conv_w, conv_b, seq, tm, tn):
    m, k = x.shape
    d_ff = w_up.shape[1] // 2
    nj = d_ff // tn
    halo_blocks = tm // HALO
    vmem = (2 * (_nbytes((tm, k), F32) + _nbytes((HALO, k), F32) + 2 * _nbytes((k, tn), BF16)
                 + _nbytes((tm, tn), BF16))
            + _nbytes((tm + HALO, k), BF16) + 2 * _nbytes((tm + HALO, tn), F32)
            + 2 * _nbytes((tm, k), F32) + 6 * _nbytes((tm, tn), F32))
    return pl.pallas_call(
        functools.partial(_ffn_up_kernel, tm=tm, tiles_per_seq=seq // tm),
        grid=(m // tm, nj),
        in_specs=[pl.BlockSpec((tm, k), lambda i, j: (i, 0)),
                  pl.BlockSpec((HALO, k), lambda i, j: (jnp.maximum(i * halo_blocks - 1, 0), 0)),
                  pl.BlockSpec((1, k), lambda i, j: (0, 0)),
                  pl.BlockSpec((k, tn), lambda i, j: (0, j)),
                  pl.BlockSpec((k, tn), lambda i, j: (0, nj + j)),
                  pl.BlockSpec((CONV_WIDTH, tn), lambda i, j: (0, j)),
                  pl.BlockSpec((CONV_WIDTH, tn), lambda i, j: (0, nj + j)),
                  pl.BlockSpec((1, tn), lambda i, j: (0, j)),
                  pl.BlockSpec((1, tn), lambda i, j: (0, nj + j))],
        out_specs=pl.BlockSpec((tm, tn), lambda i, j: (i, j)),
        out_shape=jax.ShapeDtypeStruct((m, d_ff), BF16),
        scratch_shapes=[pltpu.VMEM((tm + HALO, k), BF16),
                        pltpu.VMEM((tm + HALO, tn), F32),
                        pltpu.VMEM((tm + HALO, tn), F32)],
        compiler_params=_cparams(("parallel", "arbitrary"), vmem),
        name="ffn_up_conv_gate",
    )(x, x, g.reshape(1, k), w_up, w_up, conv_w, conv_w,
      conv_b.reshape(1, -1), conv_b.reshape(1, -1))


def _eye(n):
    return jnp.where(lax.broadcasted_iota(jnp.int32, (n, n), 0) == lax.broadcasted_iota(jnp.int32, (n, n), 1),
                     1.0, 0.0).astype(BF16)


def _transpose_bf16(x):
    return _dot_nt(_eye(x.shape[1]), x)


def _fill_v_transposed(v_ref, vt_ref, t):
    s, dv = v_ref.shape
    for c in range(s // t):
        vt_ref[0:dv, c * t:(c + 1) * t] = _transpose_bf16(v_ref[c * t:(c + 1) * t, :]).astype(BF16)
    if vt_ref.shape[0] > dv:
        vt_ref[dv:, :] = jnp.ones((vt_ref.shape[0] - dv, s), BF16)


def _flash_kernel(*refs, t, c1, has_aug):
    if has_aug:
        q_ref, k_ref, v_ref, qa_ref, ka_ref, o_ref, vt_ref, m_ref, acc_ref = refs
    else:
        q_ref, k_ref, v_ref, o_ref, vt_ref, m_ref, acc_ref = refs
    qi = pl.program_id(2)
    dv = v_ref.shape[1]

    @pl.when(qi == 0)
    def _():
        _fill_v_transposed(v_ref, vt_ref, t)

    q = q_ref[...]
    if has_aug:
        q = jnp.concatenate([q, qa_ref[...]], axis=1)
    m_ref[...] = jnp.full(m_ref.shape, -jnp.inf, F32)
    acc_ref[...] = jnp.zeros(acc_ref.shape, F32)

    def scores(kb):
        ks = pl.multiple_of(kb * t, t)
        k = k_ref[pl.ds(ks, t), :]
        if has_aug:
            k = jnp.concatenate([k, ka_ref[pl.ds(ks, t), :]], axis=1)
        return _dot_nt(k, q)

    def update(s, kb, on_diagonal):
        if on_diagonal:
            key = lax.broadcasted_iota(jnp.int32, (t, t), 0)
            qry = lax.broadcasted_iota(jnp.int32, (t, t), 1)
            s = jnp.where(key <= qry, s, -jnp.inf)
        m_prev = m_ref[...]
        m_new = jnp.maximum(m_prev, jnp.max(s, axis=0, keepdims=True))
        alpha = jnp.exp2((m_prev - m_new) * c1)
        p = jnp.exp2((s - m_new) * c1).astype(BF16)
        ks = pl.multiple_of(kb * t, t)
        acc_ref[...] = alpha * acc_ref[...] + _dot(vt_ref[:, pl.ds(ks, t)], p)
        m_ref[...] = m_new

    def body(kb, s):
        s_next = scores(kb + 1)
        update(s, kb, False)
        return s_next

    update(lax.fori_loop(0, qi, body, scores(0)), qi, True)
    acc = acc_ref[...]
    o_t = (acc[0:dv, :] / acc[dv:dv + 1, :]).astype(BF16)
    o_ref[...] = _transpose_bf16(o_t).astype(o_ref.dtype)


def _flash_attention(q_arr, k_arr, v_arr, q_off, k_off, v_off, dk, dv, n_heads, scale, aug=None, t=512):
    b, s, _ = q_arr.shape
    in_specs = [pl.BlockSpec((None, t, dk), lambda bi, h, qi: (bi, qi, q_off + h)),
                pl.BlockSpec((None, s, dk), lambda bi, h, qi: (bi, 0, k_off + h)),
                pl.BlockSpec((None, s, dv), lambda bi, h, qi: (bi, 0, v_off + h))]
    args = [q_arr, k_arr, v_arr]
    vmem = 2 * (_nbytes((t, dk), BF16) + _nbytes((s, dk), BF16) + _nbytes((s, dv), BF16) + _nbytes((t, dv), BF16))
    if aug is not None:
        in_specs += [pl.BlockSpec((None, None, t, LANES), lambda bi, h, qi: (bi, h, qi, 0)),
                     pl.BlockSpec((None, None, s, LANES), lambda bi, h, qi: (bi, h, 0, 0))]
        args += list(aug)
        vmem += 2 * (_nbytes((t, LANES), BF16) + _nbytes((s, LANES), BF16))
    acc_rows = dv + 16
    vmem += _nbytes((acc_rows, s), BF16) + 2 * _nbytes((acc_rows, t), F32) + 12 * _nbytes((t, t), F32)
    return pl.pallas_call(
        functools.partial(_flash_kernel, t=t, c1=scale * math.log2(math.e), has_aug=aug is not None),
        grid=(b, n_heads, s // t),
        in_specs=in_specs,
        out_specs=pl.BlockSpec((None, t, dv), lambda bi, h, qi: (bi, qi, h)),
        out_shape=jax.ShapeDtypeStruct((b, s, n_heads * dv), BF16),
        scratch_shapes=[pltpu.VMEM((acc_rows, s), BF16), pltpu.VMEM((1, t), F32),
                        pltpu.VMEM((acc_rows, t), F32)],
        compiler_params=_cparams(("parallel", "parallel", "arbitrary"), vmem),
        name="flash_attention",
    )(*args)


def _sb_kernel(q_ref, k_ref, v_ref, o_ref, vt_ref, carry_ref, acc_ref, *, t, scale):
    qi = pl.program_id(2)

    @pl.when(qi == 0)
    def _():
        _fill_v_transposed(v_ref, vt_ref, t)

    carry_ref[...] = jnp.zeros(carry_ref.shape, F32)
    acc_ref[...] = jnp.zeros(acc_ref.shape, F32)
    q = q_ref[...]
    key = lax.broadcasted_iota(jnp.int32, (t, t), 0)
    qry = lax.broadcasted_iota(jnp.int32, (t, t), 1)
    suffix = jnp.where(qry > key, 1.0, 0.0).astype(BF16)
    suffix = jnp.concatenate([suffix, suffix], axis=1)

    def scores(kb):
        ks = pl.multiple_of(kb * t, t)
        return _dot_nt(k_ref[pl.ds(ks, t), :], q)

    def update(s, kb, on_diagonal):
        z = s * scale
        log_beta = jnp.minimum(z, 0.0) - jnp.log(1.0 + jnp.exp(-jnp.abs(z)))
        log_keep = log_beta - z
        if on_diagonal:
            strict = key < qry
            log_keep = jnp.where(strict, log_keep, 0.0)
        hi = log_keep.astype(BF16)
        lo = (log_keep - hi.astype(F32)).astype(BF16)
        log_after = _dot(suffix, jnp.concatenate([hi, lo], axis=0)) + carry_ref[...]
        a = jnp.exp(log_beta + log_after)
        if on_diagonal:
            a = jnp.where(strict, a, 0.0)
        ks = pl.multiple_of(kb * t, t)
        acc_ref[...] += _dot(vt_ref[:, pl.ds(ks, t)], a.astype(BF16))
        carry_ref[...] += jnp.sum(log_keep, axis=0, keepdims=True)

    s_diag = scores(qi)
    s_prev = scores(jnp.maximum(qi - 1, 0))
    update(s_diag, qi, True)

    def body(it, s):
        kb = qi - 1 - it
        s_next = scores(jnp.maximum(kb - 1, 0))
        update(s, kb, False)
        return s_next

    lax.fori_loop(0, qi, body, s_prev)
    o_ref[...] = _transpose_bf16(acc_ref[...].astype(BF16)).astype(o_ref.dtype)


def _sb_attention(proj, n_heads, t=512):
    b, s, _ = proj.shape
    d = HEAD_DIM
    vmem = (2 * (2 * _nbytes((t, d), BF16) + 2 * _nbytes((s, d), BF16))
            + _nbytes((d, s), BF16) + 2 * _nbytes((d, t), F32) + 16 * _nbytes((t, t), F32))
    return pl.pallas_call(
        functools.partial(_sb_kernel, t=t, scale=d ** -0.5),
        grid=(b, n_heads, s // t),
        in_specs=[pl.BlockSpec((None, t, d), lambda bi, h, qi: (bi, qi, h)),
                  pl.BlockSpec((None, s, d), lambda bi, h, qi: (bi, 0, n_heads + h)),
                  pl.BlockSpec((None, s, d), lambda bi, h, qi: (bi, 0, 2 * n_heads + h))],
        out_specs=pl.BlockSpec((None, t, d), lambda bi, h, qi: (bi, qi, h)),
        out_shape=jax.ShapeDtypeStruct((b, s, n_heads * d), BF16),
        scratch_shapes=[pltpu.VMEM((d, s), BF16), pltpu.VMEM((1, t), F32), pltpu.VMEM((d, t), F32)],
        compiler_params=_cparams(("parallel", "parallel", "arbitrary"), vmem),
        name="stick_breaking_attention",
    )(proj, proj, proj)


def _split3(x):
    hi = x.astype(BF16)
    rest = x - hi.astype(F32)
    mid = rest.astype(BF16)
    lo = (rest - mid.astype(F32)).astype(BF16)
    return hi, mid, lo


def _fox_gate_kernel(fl_ref, b_ref, pq_ref, pk_ref, oq_ref, ok_ref, qa_ref, ka_ref, carry_ref, *, inv_scale):
    ts = fl_ref.shape[0]

    @pl.when(pl.program_id(1) == 0)
    def _():
        carry_ref[...] = jnp.zeros(carry_ref.shape, F32)

    z = fl_ref[...] + b_ref[...]
    log_f = jnp.minimum(z, 0.0) - jnp.log(1.0 + jnp.exp(-jnp.abs(z)))
    row = lax.broadcasted_iota(jnp.int32, (ts, ts), 0)
    col = lax.broadcasted_iota(jnp.int32, (ts, ts), 1)
    prefix = jnp.where(col <= row, 1.0, 0.0).astype(BF16)
    hi, mid, lo = _split3(log_f)
    c = _dot(prefix, hi) + _dot(prefix, mid) + _dot(prefix, lo) + carry_ref[0:1, :]
    carry_ref[0:1, :] = c[ts - 1:ts, :]
    parts = jnp.concatenate(_split3(c * inv_scale), axis=1)
    qa = (_dot(parts, pq_ref[...]) + oq_ref[...]).astype(BF16)
    ka = (_dot(parts, pk_ref[...]) + ok_ref[...]).astype(BF16)
    for h in range(qa_ref.shape[0]):
        qa_ref[h] = qa[:, h * LANES:(h + 1) * LANES]
        ka_ref[h] = ka[:, h * LANES:(h + 1) * LANES]


def _fox_aug_tables(n_heads):
    part, src = jnp.arange(3 * LANES) // LANES, jnp.arange(3 * LANES) % LANES
    head, lane = jnp.arange(n_heads * LANES) // LANES, jnp.arange(n_heads * LANES) % LANES
    mine = src[:, None] == head[None, :]
    pq = jnp.where(mine & (lane[None, :] == part[:, None] + 3), 1.0, 0.0).astype(BF16)
    pk = jnp.where(mine & (lane[None, :] == part[:, None]), -1.0, 0.0).astype(BF16)
    oq = jnp.where(lane < 3, 1.0, 0.0).astype(F32)[None]
    ok = jnp.where((lane >= 3) & (lane < 6), 1.0, 0.0).astype(F32)[None]
    return pq, pk, oq, ok


def _fox_gates(f_logit, b_f, n_heads, scale, ts=256):
    b, s, w = f_logit.shape
    hw = n_heads * LANES
    out = jax.ShapeDtypeStruct((b, n_heads, s, LANES), BF16)
    const = lambda bi, i: (0, 0)
    return pl.pallas_call(
        functools.partial(_fox_gate_kernel, inv_scale=1.0 / scale),
        grid=(b, s // ts),
        in_specs=[pl.BlockSpec((None, ts, w), lambda bi, i: (bi, i, 0)),
                  pl.BlockSpec((1, w), const),
                  pl.BlockSpec((3 * LANES, hw), const), pl.BlockSpec((3 * LANES, hw), const),
                  pl.BlockSpec((1, hw), const), pl.BlockSpec((1, hw), const)],
        out_specs=[pl.BlockSpec((None, n_heads, ts, LANES), lambda bi, i: (bi, 0, i, 0)),
                   pl.BlockSpec((None, n_heads, ts, LANES), lambda bi, i: (bi, 0, i, 0))],
        out_shape=[out, out],
        scratch_shapes=[pltpu.VMEM((8, w), F32)],
        compiler_params=_cparams(("parallel", "arbitrary"), 32 * 2**20),
        name="fox_gate_cumsum",
    )(f_logit, b_f.reshape(1, w), *_fox_aug_tables(n_heads))


def _mem_attn_kernel(q_ref, kv_ref, o_ref, *, scale):
    outs = []
    for h in range(MEM_HEADS):
        q = q_ref[:, h * HEAD_DIM:(h + 1) * HEAD_DIM].astype(BF16)
        k = kv_ref[:, h * HEAD_DIM:(h + 1) * HEAD_DIM]
        v = kv_ref[:, MEM_W + h * HEAD_DIM:MEM_W + (h + 1) * HEAD_DIM]
        s = _dot_nt(q, k) * scale
        p = jnp.exp(s - jnp.max(s, axis=1, keepdims=True))
        o = _dot(p.astype(BF16), v) / jnp.sum(p, axis=1, keepdims=True)
        outs.append(o.astype(o_ref.dtype))
    o_ref[...] = jnp.concatenate(outs, axis=1)


def _mem_attention(q_arr, q_block, mem_kv, tq=512):
    b, s, _ = q_arr.shape
    length = mem_kv.shape[1]
    vmem = (2 * (_nbytes((tq, MEM_W), q_arr.dtype) + _nbytes((length, 2 * MEM_W), BF16)
                 + _nbytes((tq, MEM_W), BF16)) + 8 * _nbytes((tq, length), F32))
    return pl.pallas_call(
        functools.partial(_mem_attn_kernel, scale=HEAD_DIM ** -0.5),
        grid=(b, s // tq),
        in_specs=[pl.BlockSpec((None, tq, MEM_W), lambda bi, i: (bi, i, q_block)),
                  pl.BlockSpec((None, length, 2 * MEM_W), lambda bi, i: (bi, 0, 0))],
        out_specs=pl.BlockSpec((None, tq, MEM_W), lambda bi, i: (bi, i, 0)),
        out_shape=jax.ShapeDtypeStruct((b, s, MEM_W), BF16),
        compiler_params=_cparams(("parallel", "parallel"), vmem),
        name="memory_attention",
    )(q_arr, mem_kv)


def _t5_bucket_table():
    max_exact = NUM_BUCKETS // 2
    qi = jnp.arange(WINDOW)[:, None]
    kj = jnp.arange(2 * WINDOW)[None, :]
    dist = jnp.maximum(WINDOW + qi - kj, 0)
    d = jnp.maximum(dist, 1).astype(F32)
    large = max_exact + (jnp.log(d / max_exact) / math.log(MAX_DISTANCE / max_exact)
                         * (NUM_BUCKETS - max_exact)).astype(jnp.int32)
    return jnp.where(dist < max_exact, dist, jnp.minimum(large, NUM_BUCKETS - 1)).astype(jnp.int32)


def _t5_bias_kernel(rb_ref, bucket_ref, o_ref):
    h = pl.program_id(0)
    bucket = bucket_ref[...]
    bias = jnp.zeros(bucket.shape, F32)
    for b in range(NUM_BUCKETS):
        bias = jnp.where(bucket == b, rb_ref[b, h], bias)
    o_ref[...] = bias


def _t5_bias(rel_bias):
    n_heads = rel_bias.shape[1]
    return pl.pallas_call(
        _t5_bias_kernel,
        grid=(n_heads,),
        in_specs=[pl.BlockSpec(memory_space=pltpu.SMEM),
                  pl.BlockSpec((WINDOW, 2 * WINDOW), lambda h: (0, 0))],
        out_specs=pl.BlockSpec((None, WINDOW, 2 * WINDOW), lambda h: (h, 0, 0)),
        out_shape=jax.ShapeDtypeStruct((n_heads, WINDOW, 2 * WINDOW), F32),
        compiler_params=_cparams(("parallel",), 16 * 2**20),
        name="t5_bias",
    )(rel_bias, _t5_bucket_table())


def _swa_kernel(sink_ref, q_ref, kvc_ref, kvp_ref, bias_ref, o_ref, *, scale):
    n = pl.program_id(1)
    d = SWA_HEAD_DIM
    kv_w = N_SWA_KV_HEADS * d
    qi = lax.broadcasted_iota(jnp.int32, (WINDOW, WINDOW), 0)
    kj = lax.broadcasted_iota(jnp.int32, (WINDOW, WINDOW), 1)
    prev_ok = jnp.logical_and(kj > qi, n > 0)
    cur_ok = kj <= qi
    outs = []
    for kvh in range(N_SWA_KV_HEADS):
        kc = kvc_ref[:, kvh * d:(kvh + 1) * d]
        kp = kvp_ref[:, kvh * d:(kvh + 1) * d]
        vc = kvc_ref[:, kv_w + kvh * d:kv_w + (kvh + 1) * d]
        vp = kvp_ref[:, kv_w + kvh * d:kv_w + (kvh + 1) * d]
        for g in range(SWA_GROUP):
            h = kvh * SWA_GROUP + g
            q = q_ref[:, h * d:(h + 1) * d]
            sp = jnp.where(prev_ok, _dot_nt(q, kp) * scale + bias_ref[h, :, 0:WINDOW], -jnp.inf)
            sc = jnp.where(cur_ok, _dot_nt(q, kc) * scale + bias_ref[h, :, WINDOW:2 * WINDOW], -jnp.inf)
            sink = sink_ref[h]
            m = jnp.maximum(jnp.maximum(jnp.max(sp, axis=1, keepdims=True),
                                        jnp.max(sc, axis=1, keepdims=True)), sink)
            pp = jnp.exp(sp - m)
            pc = jnp.exp(sc - m)
            denom = (jnp.sum(pp, axis=1, keepdims=True) + jnp.sum(pc, axis=1, keepdims=True)
                     + jnp.exp(sink - m))
            o = (_dot(pp.astype(BF16), vp) + _dot(pc.astype(BF16), vc)) / denom
            outs.append(o.astype(o_ref.dtype))
    o_ref[...] = jnp.concatenate(outs, axis=1)


def _swa_attention(proj, sinks, bias):
    b, s, _ = proj.shape
    q_w = N_SWA_HEADS * SWA_HEAD_DIM
    kv_w = 2 * N_SWA_KV_HEADS * SWA_HEAD_DIM
    kv_block = q_w // kv_w
    vmem = (2 * (2 * _nbytes((WINDOW, q_w), BF16) + 2 * _nbytes((WINDOW, kv_w), BF16)
                 + _nbytes(bias.shape, F32)) + 16 * 2**20)
    return pl.pallas_call(
        functools.partial(_swa_kernel, scale=SWA_HEAD_DIM ** -0.5),
        grid=(b, s // WINDOW),
        in_specs=[pl.BlockSpec(memory_space=pltpu.SMEM),
                  pl.BlockSpec((None, WINDOW, q_w), lambda bi, n: (bi, n, 0)),
                  pl.BlockSpec((None, WINDOW, kv_w), lambda bi, n: (bi, n, kv_block)),
                  pl.BlockSpec((None, WINDOW, kv_w), lambda bi, n: (bi, jnp.maximum(n - 1, 0), kv_block)),
                  pl.BlockSpec(bias.shape, lambda bi, n: (0, 0, 0))],
        out_specs=pl.BlockSpec((None, WINDOW, q_w), lambda bi, n: (bi, n, 0)),
        out_shape=jax.ShapeDtypeStruct((b, s, q_w), BF16),
        compiler_params=_cparams(("parallel", "arbitrary"), vmem),
        name="sliding_window_attention",
    )(sinks, proj, proj, proj, bias)


def _rope_table_kernel(pos_ref, invf_ref, o_ref):
    ang = pos_ref[...].astype(F32) * invf_ref[...]
    lane = lax.broadcasted_iota(jnp.int32, ang.shape, 1)
    half = QK_ROPE // 2
    sin_signed = jnp.where(lane < QK_ROPE + half, -jnp.sin(ang), jnp.sin(ang))
    o_ref[...] = jnp.where(lane < QK_ROPE, jnp.cos(ang), sin_signed)


def _rope_table(positions, tm=1024):
    m = positions.shape[0]
    half = QK_ROPE // 2
    inv_freq = ROPE_THETA ** (-jnp.arange(half, dtype=F32) / half)
    invf = jnp.tile(inv_freq, LANES // half).reshape(1, LANES)
    return pl.pallas_call(
        _rope_table_kernel,
        grid=(m // tm,),
        in_specs=[pl.BlockSpec((tm, 1), lambda i: (i, 0)), pl.BlockSpec((1, LANES), lambda i: (0, 0))],
        out_specs=pl.BlockSpec((tm, LANES), lambda i: (i, 0)),
        out_shape=jax.ShapeDtypeStruct((m, LANES), F32),
        compiler_params=_cparams(("parallel",), 32 * 2**20),
        name="rope_table",
    )(positions, invf)


def _apply_rope(x_and_partner, cs):
    z = x_and_partner * cs
    z = z + pltpu.roll(z, QK_ROPE, axis=1)
    lane = lax.broadcasted_iota(jnp.int32, z.shape, 1)
    return jnp.where(lane < QK_ROPE, z, 0.0)


def _mla_q_kernel(cq_ref, g_ref, w_ref, cs_ref, o_ref, cn_ref):
    @pl.when(pl.program_id(1) == 0)
    def _():
        cn_ref[...] = _rms(cq_ref[...], g_ref[...]).astype(BF16)

    y = _dot(cn_ref[...], w_ref[...])
    pe = _apply_rope(y[:, QK_NOPE:], cs_ref[...])
    o_ref[...] = jnp.concatenate([y[:, :QK_NOPE], pe], axis=1).astype(o_ref.dtype)


def _mla_q(proj, q_norm, w_uq_aug, cs, tm=512):
    m = proj.shape[0]
    hw = 2 * LANES
    n_heads = w_uq_aug.shape[1] // hw
    return pl.pallas_call(
        _mla_q_kernel,
        grid=(m // tm, n_heads),
        in_specs=[pl.BlockSpec((tm, Q_LORA), lambda i, h: (i, 1)),
                  pl.BlockSpec((1, Q_LORA), lambda i, h: (0, 0)),
                  pl.BlockSpec((Q_LORA, hw), lambda i, h: (0, h)),
                  pl.BlockSpec((tm, LANES), lambda i, h: (i, 0))],
        out_specs=pl.BlockSpec((tm, hw), lambda i, h: (i, h)),
        out_shape=jax.ShapeDtypeStruct((m, n_heads * hw), BF16),
        scratch_shapes=[pltpu.VMEM((tm, Q_LORA), BF16)],
        compiler_params=_cparams(("parallel", "arbitrary"), 32 * 2**20),
        name="mla_q_expand",
    )(proj, q_norm.reshape(1, Q_LORA), w_uq_aug, cs)


def _mla_kv_kernel(ckv_ref, g_ref, w_ref, kr_ref, cs_ref, k_ref, v_ref, cn_ref, kpe_ref):
    @pl.when(pl.program_id(1) == 0)
    def _():
        cn_ref[...] = _rms(ckv_ref[...], g_ref[...]).astype(BF16)
        kpe_ref[...] = _apply_rope(kr_ref[...], cs_ref[...])

    y = _dot(cn_ref[...], w_ref[...])
    k_ref[...] = jnp.concatenate([y[:, :QK_NOPE], kpe_ref[...]], axis=1).astype(k_ref.dtype)
    v_ref[...] = y[:, QK_NOPE:].astype(v_ref.dtype)


def _mla_kv(proj, kv_norm, w_ukv, cs, tm=512):
    m = proj.shape[0]
    hw = 2 * LANES
    n_heads = w_ukv.shape[1] // hw
    return pl.pallas_call(
        _mla_kv_kernel,
        grid=(m // tm, n_heads),
        in_specs=[pl.BlockSpec((tm, KV_LORA), lambda i, h: (i, 4)),
                  pl.BlockSpec((1, KV_LORA), lambda i, h: (0, 0)),
                  pl.BlockSpec((KV_LORA, hw), lambda i, h: (0, h)),
                  pl.BlockSpec((tm, LANES), lambda i, h: (i, 10)),
                  pl.BlockSpec((tm, LANES), lambda i, h: (i, 0))],
        out_specs=[pl.BlockSpec((tm, hw), lambda i, h: (i, h)),
                   pl.BlockSpec((tm, LANES), lambda i, h: (i, h))],
        out_shape=[jax.ShapeDtypeStruct((m, n_heads * hw), BF16),
                   jax.ShapeDtypeStruct((m, n_heads * LANES), BF16)],
        scratch_shapes=[pltpu.VMEM((tm, KV_LORA), BF16), pltpu.VMEM((tm, LANES), F32)],
        compiler_params=_cparams(("parallel", "arbitrary"), 32 * 2**20),
        name="mla_kv_expand",
    )(proj, kv_norm.reshape(1, KV_LORA), w_ukv, proj, cs)


def _swap_halves(w):
    half = w.shape[-1] // 2
    return jnp.concatenate([w[..., half:], w[..., :half]], axis=-1)


def _mixer_out(x2, o, o_mem, w_out, b, s):
    d = x2.shape[1]
    split = o.shape[-1]
    w = w_out.astype(BF16)
    return _matmul_res([o.reshape(b * s, split), o_mem.reshape(b * s, MEM_W)],
                       [w[:split], w[split:]], x2, tm=512, tn=min(512, d))


def _sb_layer(x2, mem_kv, g, w_in, w_out, b, s):
    proj = _norm_matmul(x2, g, w_in.astype(BF16), BF16, tm=512, tn=512).reshape(b, s, -1)
    o = _sb_attention(proj, N_HEADS)
    o_mem = _mem_attention(proj, 3 * N_HEADS * HEAD_DIM // MEM_W, mem_kv)
    return _mixer_out(x2, o, o_mem, w_out, b, s)


def _fox_layer(x2, mem_kv, g, w_in, b_f, w_out, b, s, tq=256):
    qkv_w = 3 * N_HEADS * HEAD_DIM
    w_main = jnp.concatenate([w_in[:, :qkv_w], w_in[:, qkv_w + N_HEADS:]], axis=1).astype(BF16)
    w_gate = jnp.pad(w_in[:, qkv_w:qkv_w + N_HEADS], ((0, 0), (0, LANES - N_HEADS))).astype(BF16)
    proj = _norm_matmul(x2, g, w_main, BF16, tm=512, tn=512).reshape(b, s, -1)
    f_logit = _norm_matmul(x2, g, w_gate, F32, tm=512, tn=LANES).reshape(b, s, LANES)
    scale = HEAD_DIM ** -0.5
    aug = _fox_gates(f_logit, jnp.pad(b_f, (0, LANES - N_HEADS)), N_HEADS, scale)
    o = _flash_attention(proj, proj, proj, 0, N_HEADS, 2 * N_HEADS, HEAD_DIM, HEAD_DIM, N_HEADS,
                         scale, aug=aug)
    o_mem = _mem_attention(proj, qkv_w // MEM_W, mem_kv)
    return _mixer_out(x2, o, o_mem, w_out, b, s)


def _swa_layer(x2, mem_kv, g, rel_bias, w_in, sinks, w_out, b, s):
    proj = _norm_matmul(x2, g, w_in.astype(BF16), BF16, tm=512, tn=512).reshape(b, s, -1)
    o = _swa_attention(proj, sinks, _t5_bias(rel_bias))
    q_w = N_SWA_HEADS * SWA_HEAD_DIM
    kv_w = 2 * N_SWA_KV_HEADS * SWA_HEAD_DIM
    o_mem = _mem_attention(proj, (q_w + kv_w) // MEM_W, mem_kv)
    return _mixer_out(x2, o, o_mem, w_out, b, s)


def _mla_layer(x2, mem_kv, g, positions, w_in, q_norm, w_uq, kv_norm, w_ukv, w_out, b, s):
    d = x2.shape[1]
    o1, o2, o3 = Q_LORA, Q_LORA + KV_LORA, Q_LORA + KV_LORA + QK_ROPE
    w_main = jnp.concatenate([w_in[:, o3:], w_in[:, :o3], _swap_halves(w_in[:, o2:o3])], axis=1).astype(BF16)
    proj = _norm_matmul(x2, g, w_main, F32, tm=512, tn=w_main.shape[1])
    cs = _rope_table(positions.reshape(b * s, 1))
    wq = w_uq.reshape(Q_LORA, N_HEADS, QK_NOPE + QK_ROPE)
    wq = jnp.concatenate([wq, _swap_halves(wq[:, :, QK_NOPE:])], axis=-1).reshape(Q_LORA, -1).astype(BF16)
    q = _mla_q(proj, q_norm, wq, cs).reshape(b, s, -1)
    k, v = _mla_kv(proj, kv_norm, w_ukv.astype(BF16), cs)
    k, v = k.reshape(b, s, -1), v.reshape(b, s, -1)
    o = _flash_attention(q, k, v, 0, 0, 0, 2 * LANES, HEAD_DIM, N_HEADS, (QK_NOPE + QK_ROPE) ** -0.5)
    o_mem = _mem_attention(proj.reshape(b, s, -1), 0, mem_kv)
    return _mixer_out(x2, o, o_mem, w_out, b, s)


def kernel(x, mem, positions, rel_bias, attn_norm, mem_norm, w_mem_kv, ffn_norm, ffn_w_up, ffn_conv_w, ffn_conv_b, ffn_w_down, final_norm, sb_w_in, sb_w_out, fox_w_in, fox_b_f, fox_w_out, swa_w_in, swa_sinks, swa_w_out, mla_w_in, mla_q_norm, mla_w_uq, mla_kv_norm, mla_w_ukv, mla_w_out):
    b, s, d = x.shape
    depth = attn_norm.shape[0]
    mem_len = mem.shape[1]
    x2 = x.reshape(b * s, d)
    mem2 = mem.reshape(b * mem_len, d)
    for i in range(depth):
        kind, j = i % 4, i // 4
        mem_kv = _norm_matmul(mem2, mem_norm[i], w_mem_kv[i].astype(BF16), BF16,
                              tm=b * mem_len, tn=2 * MEM_W).reshape(b, mem_len, 2 * MEM_W)
        g = attn_norm[i]
        if kind == 0:
            x2 = _sb_layer(x2, mem_kv, g, sb_w_in[j], sb_w_out[j], b, s)
        elif kind == 1:
            x2 = _fox_layer(x2, mem_kv, g, fox_w_in[j], fox_b_f[j], fox_w_out[j], b, s)
        elif kind == 2:
            x2 = _swa_layer(x2, mem_kv, g, rel_bias, swa_w_in[j], swa_sinks[j], swa_w_out[j], b, s)
        else:
            x2 = _mla_layer(x2, mem_kv, g, positions, mla_w_in[j], mla_q_norm[j], mla_w_uq[j],
                            mla_kv_norm[j], mla_w_ukv[j], mla_w_out[j], b, s)
        gated = _ffn_up(x2, ffn_norm[i], ffn_w_up[i].astype(BF16), ffn_conv_w[i], ffn_conv_b[i],
                        seq=s, tm=512, tn=512)
        x2 = _matmul_res([gated], [ffn_w_down[i].astype(BF16)], x2, tm=512, tn=512)
    return _rmsnorm(x2, final_norm, tm=512).reshape(b, s, d)
```

```python
import functools
import math

import jax
import jax.numpy as jnp
from jax import lax
from jax.experimental import pallas as pl
from jax.experimental.pallas import tpu as pltpu

F32 = jnp.float32
BF16 = jnp.bfloat16

EPS = 1e-6
LOG2E = math.log2(math.e)
HEAD_DIM = 128
N_HEADS = 16
SWA_HEAD_DIM = 64
N_SWA_HEADS = 32
N_SWA_KV_HEADS = 4
SWA_GROUP = N_SWA_HEADS // N_SWA_KV_HEADS
WINDOW = 128
Q_LORA = 512
KV_LORA = 256
QK_NOPE = 128
QK_ROPE = 64
ROPE_THETA = 10000.0
MEM_HEADS = 4
MEM_W = MEM_HEADS * HEAD_DIM
NUM_BUCKETS = 32
MAX_DISTANCE = 128
CONV_WIDTH = 3

V7X_VMEM_BYTES = 64 * 1024 * 1024
VMEM_CAP = V7X_VMEM_BYTES - 8 * 1024 * 1024
LANES = 128
HALO = 16


def _cparams(sems, vmem_bytes):
    return pltpu.CompilerParams(dimension_semantics=sems,
                                vmem_limit_bytes=int(min(max(vmem_bytes, 16 * 2**20), VMEM_CAP)))


def _nbytes(shape, dtype):
    return math.prod(shape) * jnp.dtype(dtype).itemsize


def _rms(x, g):
    return x * lax.rsqrt(jnp.mean(x * x, axis=-1, keepdims=True) + EPS) * g


def _dot(a, b):
    return jnp.dot(a, b, preferred_element_type=F32)


def _dot_nt(a, b):
    return lax.dot_general(a, b, (((1,), (1,)), ((), ())), preferred_element_type=F32)


def _norm_matmul_kernel(x_ref, g_ref, w_ref, cs_ref, o_ref, xn_ref):
    @pl.when(pl.program_id(1) == 0)
    def _():
        xn_ref[...] = _rms(x_ref[...], g_ref[...]).astype(BF16)

    o_ref[...] = (_dot(xn_ref[...], w_ref[...]) * cs_ref[...]).astype(o_ref.dtype)


def _norm_matmul(x, g, w, out_dtype, tm, tn, scaled_cols=0, col_scale=1.0):
    m, k = x.shape
    n = w.shape[1]
    cs = jnp.where(jnp.arange(n) < scaled_cols, col_scale, 1.0).astype(F32).reshape(1, n)
    vmem = (2 * (_nbytes((tm, k), F32) + _nbytes((k, tn), BF16) + _nbytes((tm, tn), out_dtype))
            + _nbytes((tm, k), BF16) + 2 * _nbytes((tm, k), F32) + _nbytes((tm, tn), F32))
    return pl.pallas_call(
        _norm_matmul_kernel,
        grid=(m // tm, n // tn),
        in_specs=[pl.BlockSpec((tm, k), lambda i, j: (i, 0)),
                  pl.BlockSpec((1, k), lambda i, j: (0, 0)),
                  pl.BlockSpec((k, tn), lambda i, j: (0, j)),
                  pl.BlockSpec((1, tn), lambda i, j: (0, j))],
        out_specs=pl.BlockSpec((tm, tn), lambda i, j: (i, j)),
        out_shape=jax.ShapeDtypeStruct((m, n), out_dtype),
        scratch_shapes=[pltpu.VMEM((tm, k), BF16)],
        compiler_params=_cparams(("parallel", "arbitrary"), vmem),
        name="norm_matmul",
    )(x, g.reshape(1, k), w, cs)


def _matmul_res_kernel(*refs, n_a):
    a_refs, w_refs = refs[:n_a], refs[n_a:2 * n_a]
    x_ref, o_ref = refs[2 * n_a], refs[2 * n_a + 1]
    acc = x_ref[...]
    for a_ref, w_ref in zip(a_refs, w_refs):
        acc = acc + _dot(a_ref[...].astype(BF16), w_ref[...])
    o_ref[...] = acc


def _matmul_res(a_list, w_list, x, tm, tn):
    m, n = x.shape
    vmem = 2 * (2 * _nbytes((tm, tn), F32)) + 2 * _nbytes((tm, tn), F32)
    in_specs = []
    for a in a_list:
        in_specs.append(pl.BlockSpec((tm, a.shape[1]), lambda i, j: (i, 0)))
        vmem += 2 * _nbytes((tm, a.shape[1]), a.dtype)
    for w in w_list:
        in_specs.append(pl.BlockSpec((w.shape[0], tn), lambda i, j: (0, j)))
        vmem += 2 * _nbytes((w.shape[0], tn), w.dtype)
    in_specs.append(pl.BlockSpec((tm, tn), lambda i, j: (i, j)))
    return pl.pallas_call(
        functools.partial(_matmul_res_kernel, n_a=len(a_list)),
        grid=(m // tm, n // tn),
        in_specs=in_specs,
        out_specs=pl.BlockSpec((tm, tn), lambda i, j: (i, j)),
        out_shape=jax.ShapeDtypeStruct((m, n), F32),
        compiler_params=_cparams(("parallel", "arbitrary"), vmem),
        name="matmul_res",
    )(*a_list, *w_list, x)


def _rmsnorm_kernel(x_ref, g_ref, o_ref):
    o_ref[...] = _rms(x_ref[...], g_ref[...])


def _rmsnorm(x, g, tm):
    m, k = x.shape
    return pl.pallas_call(
        _rmsnorm_kernel,
        grid=(m // tm,),
        in_specs=[pl.BlockSpec((tm, k), lambda i: (i, 0)), pl.BlockSpec((1, k), lambda i: (0, 0))],
        out_specs=pl.BlockSpec((tm, k), lambda i: (i, 0)),
        out_shape=jax.ShapeDtypeStruct((m, k), F32),
        compiler_params=_cparams(("parallel",), 8 * _nbytes((tm, k), F32)),
        name="final_rmsnorm",
    )(x, g.reshape(1, k))


def _ffn_up_kernel(x_ref, xh_ref, g_ref, wg_ref, wv_ref, cwg_ref, cwv_ref, cbg_ref, cbv_ref,
                   o_ref, xn_ref, *u_refs, rows, tiles_per_seq):
    i = pl.program_id(0)

    @pl.when(pl.program_id(1) == 0)
    def _():
        xn_ref[0:HALO, :] = _rms(xh_ref[...], g_ref[...]).astype(BF16)
        xn_ref[HALO:, :] = _rms(x_ref[...], g_ref[...]).astype(BF16)

    keep = jnp.where(i % tiles_per_seq == 0, 0.0, 1.0)

    for r in range(len(u_refs) // 2):
        xr = xn_ref[r * rows:r * rows + rows + HALO, :]

        def conv(w_ref, cw_ref, cb_ref, u_ref):
            u = _dot(xr, w_ref[...])
            if r == 0:
                u_ref[0:HALO, :] = u[0:HALO, :] * keep
                u_ref[HALO:, :] = u[HALO:, :]
            else:
                u_ref[...] = u
            c = cb_ref[...]
            for tap in range(CONV_WIDTH):
                start = HALO - (CONV_WIDTH - 1) + tap
                c = c + cw_ref[tap:tap + 1, :] * u_ref[start:start + rows, :]
            return c

        gate = conv(wg_ref, cwg_ref, cbg_ref, u_refs[2 * r])
        val = conv(wv_ref, cwv_ref, cbv_ref, u_refs[2 * r + 1])
        o_ref[r * rows:(r + 1) * rows, :] = (gate * (1.0 / (1.0 + jnp.exp(-gate))) * val).astype(o_ref.dtype)


def _ffn_up(x, g, w_up, conv_w, conv_b, seq, tm, tn, rows=256):
    m, k = x.shape
    d_ff = w_up.shape[1] // 2
    nj = d_ff // tn
    halo_blocks = tm // HALO
    n_chunks = tm // rows
    vmem = (2 * (_nbytes((tm, k), F32) + _nbytes((HALO, k), F32) + 2 * _nbytes((k, tn), BF16)
                 + _nbytes((tm, tn), BF16))
            + _nbytes((tm + HALO, k), BF16) + 2 * n_chunks * _nbytes((rows + HALO, tn), F32)
            + _nbytes((tm, k), F32) + 6 * n_chunks * _nbytes((rows, tn), F32))
    return pl.pallas_call(
        functools.partial(_ffn_up_kernel, rows=rows, tiles_per_seq=seq // tm),
        grid=(m // tm, nj),
        in_specs=[pl.BlockSpec((tm, k), lambda i, j: (i, 0)),
                  pl.BlockSpec((HALO, k), lambda i, j: (jnp.maximum(i * halo_blocks - 1, 0), 0)),
                  pl.BlockSpec((1, k), lambda i, j: (0, 0)),
                  pl.BlockSpec((k, tn), lambda i, j: (0, j)),
                  pl.BlockSpec((k, tn), lambda i, j: (0, nj + j)),
                  pl.BlockSpec((CONV_WIDTH, tn), lambda i, j: (0, j)),
                  pl.BlockSpec((CONV_WIDTH, tn), lambda i, j: (0, nj + j)),
                  pl.BlockSpec((1, tn), lambda i, j: (0, j)),
                  pl.BlockSpec((1, tn), lambda i, j: (0, nj + j))],
        out_specs=pl.BlockSpec((tm, tn), lambda i, j: (i, j)),
        out_shape=jax.ShapeDtypeStruct((m, d_ff), BF16),
        scratch_shapes=[pltpu.VMEM((tm + HALO, k), BF16)]
        + [pltpu.VMEM((rows + HALO, tn), F32)] * (2 * n_chunks),
        compiler_params=_cparams(("parallel", "arbitrary"), vmem),
        name="ffn_up_conv_gate",
    )(x, x, g.reshape(1, k), w_up, w_up, conv_w, conv_w,
      conv_b.reshape(1, -1), conv_b.reshape(1, -1))


def _eye(n):
    return jnp.where(lax.broadcasted_iota(jnp.int32, (n, n), 0) == lax.broadcasted_iota(jnp.int32, (n, n), 1),
                     1.0, 0.0).astype(BF16)


def _transpose_bf16(x):
    return _dot_nt(_eye(x.shape[1]), x)


def _fill_v_transposed(v_ref, vt_ref, t):
    s, dv = v_ref.shape
    for c in range(s // t):
        vt_ref[0:dv, c * t:(c + 1) * t] = _transpose_bf16(v_ref[c * t:(c + 1) * t, :]).astype(BF16)
    if vt_ref.shape[0] > dv:
        vt_ref[dv:, :] = jnp.ones((vt_ref.shape[0] - dv, s), BF16)


def _flash_kernel(*refs, t, has_aug):
    if has_aug:
        q_ref, k_ref, v_ref, qa_ref, ka_ref, o_ref, vt_ref, m_ref, acc_ref = refs
    else:
        q_ref, k_ref, v_ref, o_ref, vt_ref, m_ref, acc_ref = refs
    qi = pl.program_id(2)
    dv = v_ref.shape[1]

    @pl.when(qi == 0)
    def _():
        _fill_v_transposed(v_ref, vt_ref, t)

    q = q_ref[...]
    if has_aug:
        q = jnp.concatenate([q, qa_ref[...]], axis=1)
    m_ref[...] = jnp.full(m_ref.shape, -jnp.inf, F32)
    acc_ref[...] = jnp.zeros(acc_ref.shape, F32)

    def scores(kb):
        ks = pl.multiple_of(kb * t, t)
        k = k_ref[pl.ds(ks, t), :]
        if has_aug:
            k = jnp.concatenate([k, ka_ref[pl.ds(ks, t), :]], axis=1)
        return _dot_nt(k, q)

    def softmax(s, on_diagonal):
        if on_diagonal:
            key = lax.broadcasted_iota(jnp.int32, (t, t), 0)
            qry = lax.broadcasted_iota(jnp.int32, (t, t), 1)
            s = jnp.where(key <= qry, s, -jnp.inf)
        m_prev = m_ref[...]
        m_new = jnp.maximum(m_prev, jnp.max(s, axis=0, keepdims=True))
        m_ref[...] = m_new
        return jnp.exp2(s - m_new).astype(BF16), jnp.exp2(m_prev - m_new)

    def accumulate(p, alpha, kb):
        ks = pl.multiple_of(kb * t, t)
        acc_ref[...] = alpha * acc_ref[...] + _dot(vt_ref[:, pl.ds(ks, t)], p)

    def finish():
        acc = acc_ref[...]
        o_ref[...] = (acc[0:dv, :] / acc[dv:dv + 1, :]).T.astype(o_ref.dtype)

    def body(kb, s):
        s_next = scores(kb + 1)
        accumulate(*softmax(s, False), kb)
        return s_next

    s_diag = lax.fori_loop(0, qi, body, scores(0))
    accumulate(*softmax(s_diag, True), qi)
    finish()


def _flash_attention(q_arr, k_arr, v_arr, q_off, k_off, v_off, dk, dv, n_heads, aug=None, t=512):
    b, s, _ = q_arr.shape
    in_specs = [pl.BlockSpec((None, t, dk), lambda bi, h, qi: (bi, qi, q_off + h)),
                pl.BlockSpec((None, s, dk), lambda bi, h, qi: (bi, 0, k_off + h)),
                pl.BlockSpec((None, s, dv), lambda bi, h, qi: (bi, 0, v_off + h))]
    args = [q_arr, k_arr, v_arr]
    vmem = 2 * (_nbytes((t, dk), BF16) + _nbytes((s, dk), BF16) + _nbytes((s, dv), BF16) + _nbytes((t, dv), BF16))
    if aug is not None:
        in_specs += [pl.BlockSpec((None, None, t, LANES), lambda bi, h, qi: (bi, h, qi, 0)),
                     pl.BlockSpec((None, None, s, LANES), lambda bi, h, qi: (bi, h, 0, 0))]
        args += list(aug)
        vmem += 2 * (_nbytes((t, LANES), BF16) + _nbytes((s, LANES), BF16))
    acc_rows = dv + 16
    vmem += _nbytes((acc_rows, s), BF16) + 2 * _nbytes((acc_rows, t), F32) + 12 * _nbytes((t, t), F32)
    return pl.pallas_call(
        functools.partial(_flash_kernel, t=t, has_aug=aug is not None),
        grid=(b, n_heads, s // t),
        in_specs=in_specs,
        out_specs=pl.BlockSpec((None, t, dv), lambda bi, h, qi: (bi, qi, h)),
        out_shape=jax.ShapeDtypeStruct((b, s, n_heads * dv), BF16),
        scratch_shapes=[pltpu.VMEM((acc_rows, s), BF16), pltpu.VMEM((1, t), F32),
                        pltpu.VMEM((acc_rows, t), F32)],
        compiler_params=_cparams(("parallel", "parallel", "arbitrary"), vmem),
        name="flash_attention",
    )(*args)


def _sb_kernel(q_ref, k_ref, v_ref, o_ref, vt_ref, carry_ref, acc_ref, *, t, sub):
    qi = pl.program_id(2)

    @pl.when(qi == 0)
    def _():
        _fill_v_transposed(v_ref, vt_ref, t)

    carry_ref[...] = jnp.zeros(carry_ref.shape, F32)
    acc_ref[...] = jnp.zeros(acc_ref.shape, F32)
    q = q_ref[...]
    key = lax.broadcasted_iota(jnp.int32, (sub, t), 0)
    qry = lax.broadcasted_iota(jnp.int32, (sub, t), 1)
    r = lax.broadcasted_iota(jnp.int32, (sub + 16, sub), 0)
    c = lax.broadcasted_iota(jnp.int32, (sub + 16, sub), 1)
    suffix = jnp.where(((c > r) & (r < sub)) | (r == sub), 1.0, 0.0).astype(BF16)
    suffix = jnp.concatenate([suffix, suffix], axis=1)

    def scores(kb):
        ks = pl.multiple_of(kb * t, t)
        return _dot_nt(k_ref[pl.ds(ks, t), :], q)

    def weights(s, on_diagonal):
        later = carry_ref[...]
        out = [None] * (t // sub)
        for i in reversed(range(t // sub)):
            z = s[i * sub:(i + 1) * sub, :]
            neg_abs = lax.bitcast_convert_type(
                lax.bitcast_convert_type(z, jnp.uint32) | jnp.uint32(0x80000000), F32)
            log_beta = jnp.minimum(z, 0.0) - jnp.log2(1.0 + jnp.exp2(neg_abs))
            log_keep = log_beta - z
            if on_diagonal:
                strict = key + i * sub < qry
                log_keep = jnp.where(strict, log_keep, 0.0)
            hi = log_keep.astype(BF16)
            lo = (log_keep - hi.astype(F32)).astype(BF16)
            sums = _dot(suffix, jnp.concatenate([hi, lo], axis=0))
            a = jnp.exp2(log_beta + (sums[0:sub, :] + later))
            if on_diagonal:
                a = jnp.where(strict, a, 0.0)
            out[i] = a.astype(BF16)
            later = later + sums[sub:sub + 1, :]
        carry_ref[...] = later
        return jnp.concatenate(out, axis=0)

    def accumulate(a, kb):
        ks = pl.multiple_of(kb * t, t)
        acc_ref[...] += _dot(vt_ref[:, pl.ds(ks, t)], a)

    def finish():
        o_ref[...] = acc_ref[...].T.astype(o_ref.dtype)

    @pl.when(qi == 0)
    def _():
        accumulate(weights(scores(0), True), 0)
        finish()

    @pl.when(qi > 0)
    def _():
        s_ahead = scores(qi - 1)
        a_diag = weights(scores(qi), True)

        def body(i, carry):
            s, a = carry
            accumulate(a, qi - i)
            s_next = scores(qi - i - 2)
            return s_next, weights(s, False)

        s_last, a = lax.fori_loop(0, qi - 1, body, (s_ahead, a_diag))
        accumulate(a, 1)
        accumulate(weights(s_last, False), 0)
        finish()


def _sb_attention(proj, n_heads, t=512):
    b, s, _ = proj.shape
    d = HEAD_DIM
    vmem = (2 * (2 * _nbytes((t, d), BF16) + 2 * _nbytes((s, d), BF16))
            + _nbytes((d, s), BF16) + 2 * _nbytes((d, t), F32) + 16 * _nbytes((t, t), F32))
    return pl.pallas_call(
        functools.partial(_sb_kernel, t=t, sub=LANES),
        grid=(b, n_heads, s // t),
        in_specs=[pl.BlockSpec((None, t, d), lambda bi, h, qi: (bi, qi, h)),
                  pl.BlockSpec((None, s, d), lambda bi, h, qi: (bi, 0, n_heads + h)),
                  pl.BlockSpec((None, s, d), lambda bi, h, qi: (bi, 0, 2 * n_heads + h))],
        out_specs=pl.BlockSpec((None, t, d), lambda bi, h, qi: (bi, qi, h)),
        out_shape=jax.ShapeDtypeStruct((b, s, n_heads * d), BF16),
        scratch_shapes=[pltpu.VMEM((d, s), BF16), pltpu.VMEM((1, t), F32), pltpu.VMEM((d, t), F32)],
        compiler_params=_cparams(("parallel", "parallel", "arbitrary"), vmem),
        name="stick_breaking_attention",
    )(proj, proj, proj)


def _split3(x):
    hi = x.astype(BF16)
    rest = x - hi.astype(F32)
    mid = rest.astype(BF16)
    lo = (rest - mid.astype(F32)).astype(BF16)
    return hi, mid, lo


def _fox_gate_kernel(fl_ref, b_ref, pq_ref, pk_ref, oq_ref, ok_ref, qa_ref, ka_ref, carry_ref):
    ts = fl_ref.shape[0]

    @pl.when(pl.program_id(1) == 0)
    def _():
        carry_ref[...] = jnp.zeros(carry_ref.shape, F32)

    z = fl_ref[...] + b_ref[...]
    log_f = jnp.minimum(z, 0.0) - jnp.log(1.0 + jnp.exp(-jnp.abs(z)))
    row = lax.broadcasted_iota(jnp.int32, (ts, ts), 0)
    col = lax.broadcasted_iota(jnp.int32, (ts, ts), 1)
    prefix = jnp.where(col <= row, 1.0, 0.0).astype(BF16)
    hi, mid, lo = _split3(log_f)
    c = _dot(prefix, hi) + _dot(prefix, mid) + _dot(prefix, lo) + carry_ref[0:1, :]
    carry_ref[0:1, :] = c[ts - 1:ts, :]
    parts = jnp.concatenate(_split3(c * LOG2E), axis=1)
    qa = (_dot(parts, pq_ref[...]) + oq_ref[...]).astype(BF16)
    ka = (_dot(parts, pk_ref[...]) + ok_ref[...]).astype(BF16)
    for h in range(qa_ref.shape[0]):
        qa_ref[h] = qa[:, h * LANES:(h + 1) * LANES]
        ka_ref[h] = ka[:, h * LANES:(h + 1) * LANES]


def _fox_aug_tables(n_heads):
    part, src = jnp.arange(3 * LANES) // LANES, jnp.arange(3 * LANES) % LANES
    head, lane = jnp.arange(n_heads * LANES) // LANES, jnp.arange(n_heads * LANES) % LANES
    mine = src[:, None] == head[None, :]
    pq = jnp.where(mine & (lane[None, :] == part[:, None] + 3), 1.0, 0.0).astype(BF16)
    pk = jnp.where(mine & (lane[None, :] == part[:, None]), -1.0, 0.0).astype(BF16)
    oq = jnp.where(lane < 3, 1.0, 0.0).astype(F32)[None]
    ok = jnp.where((lane >= 3) & (lane < 6), 1.0, 0.0).astype(F32)[None]
    return pq, pk, oq, ok


def _fox_gates(f_logit, b_f, n_heads, ts=256):
    b, s, w = f_logit.shape
    hw = n_heads * LANES
    out = jax.ShapeDtypeStruct((b, n_heads, s, LANES), BF16)
    const = lambda bi, i: (0, 0)
    return pl.pallas_call(
        _fox_gate_kernel,
        grid=(b, s // ts),
        in_specs=[pl.BlockSpec((None, ts, w), lambda bi, i: (bi, i, 0)),
                  pl.BlockSpec((1, w), const),
                  pl.BlockSpec((3 * LANES, hw), const), pl.BlockSpec((3 * LANES, hw), const),
                  pl.BlockSpec((1, hw), const), pl.BlockSpec((1, hw), const)],
        out_specs=[pl.BlockSpec((None, n_heads, ts, LANES), lambda bi, i: (bi, 0, i, 0)),
                   pl.BlockSpec((None, n_heads, ts, LANES), lambda bi, i: (bi, 0, i, 0))],
        out_shape=[out, out],
        scratch_shapes=[pltpu.VMEM((8, w), F32)],
        compiler_params=_cparams(("parallel", "arbitrary"), 32 * 2**20),
        name="fox_gate_cumsum",
    )(f_logit, b_f.reshape(1, w), *_fox_aug_tables(n_heads))


def _mem_attn_kernel(q_ref, kv_ref, o_ref, *, scale):
    outs = []
    for h in range(MEM_HEADS):
        q = q_ref[:, h * HEAD_DIM:(h + 1) * HEAD_DIM].astype(BF16)
        k = kv_ref[:, h * HEAD_DIM:(h + 1) * HEAD_DIM]
        v = kv_ref[:, MEM_W + h * HEAD_DIM:MEM_W + (h + 1) * HEAD_DIM]
        s = _dot_nt(q, k) * scale
        p = jnp.exp(s - jnp.max(s, axis=1, keepdims=True))
        o = _dot(p.astype(BF16), v) / jnp.sum(p, axis=1, keepdims=True)
        outs.append(o.astype(o_ref.dtype))
    o_ref[...] = jnp.concatenate(outs, axis=1)


def _mem_attention(q_arr, q_block, mem_kv, tq=512):
    b, s, _ = q_arr.shape
    length = mem_kv.shape[1]
    vmem = (2 * (_nbytes((tq, MEM_W), q_arr.dtype) + _nbytes((length, 2 * MEM_W), BF16)
                 + _nbytes((tq, MEM_W), BF16)) + 8 * _nbytes((tq, length), F32))
    return pl.pallas_call(
        functools.partial(_mem_attn_kernel, scale=HEAD_DIM ** -0.5),
        grid=(b, s // tq),
        in_specs=[pl.BlockSpec((None, tq, MEM_W), lambda bi, i: (bi, i, q_block)),
                  pl.BlockSpec((None, length, 2 * MEM_W), lambda bi, i: (bi, 0, 0))],
        out_specs=pl.BlockSpec((None, tq, MEM_W), lambda bi, i: (bi, i, 0)),
        out_shape=jax.ShapeDtypeStruct((b, s, MEM_W), BF16),
        compiler_params=_cparams(("parallel", "parallel"), vmem),
        name="memory_attention",
    )(q_arr, mem_kv)


def _t5_bucket_table():
    max_exact = NUM_BUCKETS // 2
    qi = jnp.arange(WINDOW)[:, None]
    kj = jnp.arange(2 * WINDOW)[None, :]
    dist = jnp.maximum(WINDOW + qi - kj, 0)
    d = jnp.maximum(dist, 1).astype(F32)
    large = max_exact + (jnp.log(d / max_exact) / math.log(MAX_DISTANCE / max_exact)
                         * (NUM_BUCKETS - max_exact)).astype(jnp.int32)
    return jnp.where(dist < max_exact, dist, jnp.minimum(large, NUM_BUCKETS - 1)).astype(jnp.int32)


def _t5_bias_kernel(rb_ref, bucket_ref, o_ref):
    h = pl.program_id(0)
    bucket = bucket_ref[...]
    bias = jnp.zeros(bucket.shape, F32)
    for b in range(NUM_BUCKETS):
        bias = jnp.where(bucket == b, rb_ref[b, h], bias)
    o_ref[...] = bias


def _t5_bias(rel_bias):
    n_heads = rel_bias.shape[1]
    return pl.pallas_call(
        _t5_bias_kernel,
        grid=(n_heads,),
        in_specs=[pl.BlockSpec(memory_space=pltpu.SMEM),
                  pl.BlockSpec((WINDOW, 2 * WINDOW), lambda h: (0, 0))],
        out_specs=pl.BlockSpec((None, WINDOW, 2 * WINDOW), lambda h: (h, 0, 0)),
        out_shape=jax.ShapeDtypeStruct((n_heads, WINDOW, 2 * WINDOW), F32),
        compiler_params=_cparams(("parallel",), 16 * 2**20),
        name="t5_bias",
    )(rel_bias, _t5_bucket_table())


def _swa_kernel(sink_ref, q_ref, kvc_ref, kvp_ref, bias_ref, o_ref, *, scale):
    n = pl.program_id(1)
    d = SWA_HEAD_DIM
    kv_w = N_SWA_KV_HEADS * d
    qi = lax.broadcasted_iota(jnp.int32, (WINDOW, WINDOW), 0)
    kj = lax.broadcasted_iota(jnp.int32, (WINDOW, WINDOW), 1)
    prev_ok = jnp.logical_and(kj > qi, n > 0)
    cur_ok = kj <= qi
    outs = []
    for kvh in range(N_SWA_KV_HEADS):
        kc = kvc_ref[:, kvh * d:(kvh + 1) * d]
        kp = kvp_ref[:, kvh * d:(kvh + 1) * d]
        vc = kvc_ref[:, kv_w + kvh * d:kv_w + (kvh + 1) * d]
        vp = kvp_ref[:, kv_w + kvh * d:kv_w + (kvh + 1) * d]
        for g in range(SWA_GROUP):
            h = kvh * SWA_GROUP + g
            q = q_ref[:, h * d:(h + 1) * d]
            sp = jnp.where(prev_ok, _dot_nt(q, kp) * scale + bias_ref[h, :, 0:WINDOW], -jnp.inf)
            sc = jnp.where(cur_ok, _dot_nt(q, kc) * scale + bias_ref[h, :, WINDOW:2 * WINDOW], -jnp.inf)
            sink = sink_ref[h]
            m = jnp.maximum(jnp.maximum(jnp.max(sp, axis=1, keepdims=True),
                                        jnp.max(sc, axis=1, keepdims=True)), sink)
            pp = jnp.exp(sp - m)
            pc = jnp.exp(sc - m)
            denom = (jnp.sum(pp, axis=1, keepdims=True) + jnp.sum(pc, axis=1, keepdims=True)
                     + jnp.exp(sink - m))
            o = (_dot(pp.astype(BF16), vp) + _dot(pc.astype(BF16), vc)) / denom
            outs.append(o.astype(o_ref.dtype))
    o_ref[...] = jnp.concatenate(outs, axis=1)


def _swa_attention(proj, sinks, bias):
    b, s, _ = proj.shape
    q_w = N_SWA_HEADS * SWA_HEAD_DIM
    kv_w = 2 * N_SWA_KV_HEADS * SWA_HEAD_DIM
    kv_block = q_w // kv_w
    vmem = (2 * (2 * _nbytes((WINDOW, q_w), BF16) + 2 * _nbytes((WINDOW, kv_w), BF16)
                 + _nbytes(bias.shape, F32)) + 16 * 2**20)
    return pl.pallas_call(
        functools.partial(_swa_kernel, scale=SWA_HEAD_DIM ** -0.5),
        grid=(b, s // WINDOW),
        in_specs=[pl.BlockSpec(memory_space=pltpu.SMEM),
                  pl.BlockSpec((None, WINDOW, q_w), lambda bi, n: (bi, n, 0)),
                  pl.BlockSpec((None, WINDOW, kv_w), lambda bi, n: (bi, n, kv_block)),
                  pl.BlockSpec((None, WINDOW, kv_w), lambda bi, n: (bi, jnp.maximum(n - 1, 0), kv_block)),
                  pl.BlockSpec(bias.shape, lambda bi, n: (0, 0, 0))],
        out_specs=pl.BlockSpec((None, WINDOW, q_w), lambda bi, n: (bi, n, 0)),
        out_shape=jax.ShapeDtypeStruct((b, s, q_w), BF16),
        compiler_params=_cparams(("parallel", "arbitrary"), vmem),
        name="sliding_window_attention",
    )(sinks, proj, proj, proj, bias)


def _rope_table_kernel(pos_ref, invf_ref, o_ref):
    ang = pos_ref[...].astype(F32) * invf_ref[...]
    lane = lax.broadcasted_iota(jnp.int32, ang.shape, 1)
    half = QK_ROPE // 2
    sin_signed = jnp.where(lane < QK_ROPE + half, -jnp.sin(ang), jnp.sin(ang))
    o_ref[...] = jnp.where(lane < QK_ROPE, jnp.cos(ang), sin_signed)


def _rope_table(positions, tm=1024):
    m = positions.shape[0]
    half = QK_ROPE // 2
    inv_freq = ROPE_THETA ** (-jnp.arange(half, dtype=F32) / half)
    invf = jnp.tile(inv_freq, LANES // half).reshape(1, LANES)
    return pl.pallas_call(
        _rope_table_kernel,
        grid=(m // tm,),
        in_specs=[pl.BlockSpec((tm, 1), lambda i: (i, 0)), pl.BlockSpec((1, LANES), lambda i: (0, 0))],
        out_specs=pl.BlockSpec((tm, LANES), lambda i: (i, 0)),
        out_shape=jax.ShapeDtypeStruct((m, LANES), F32),
        compiler_params=_cparams(("parallel",), 32 * 2**20),
        name="rope_table",
    )(positions, invf)


def _apply_rope(x_and_partner, cs):
    z = x_and_partner * cs
    z = z + pltpu.roll(z, QK_ROPE, axis=1)
    lane = lax.broadcasted_iota(jnp.int32, z.shape, 1)
    return jnp.where(lane < QK_ROPE, z, 0.0)


def _mla_q_kernel(cq_ref, g_ref, w_ref, cs_ref, o_ref, cn_ref):
    @pl.when(pl.program_id(1) == 0)
    def _():
        cn_ref[...] = _rms(cq_ref[...], g_ref[...]).astype(BF16)

    y = _dot(cn_ref[...], w_ref[...]) * ((QK_NOPE + QK_ROPE) ** -0.5 * LOG2E)
    pe = _apply_rope(y[:, QK_NOPE:], cs_ref[...])
    o_ref[...] = jnp.concatenate([y[:, :QK_NOPE], pe], axis=1).astype(o_ref.dtype)


def _mla_q(proj, q_norm, w_uq_aug, cs, tm=512):
    m = proj.shape[0]
    hw = 2 * LANES
    n_heads = w_uq_aug.shape[1] // hw
    return pl.pallas_call(
        _mla_q_kernel,
        grid=(m // tm, n_heads),
        in_specs=[pl.BlockSpec((tm, Q_LORA), lambda i, h: (i, 1)),
                  pl.BlockSpec((1, Q_LORA), lambda i, h: (0, 0)),
                  pl.BlockSpec((Q_LORA, hw), lambda i, h: (0, h)),
                  pl.BlockSpec((tm, LANES), lambda i, h: (i, 0))],
        out_specs=pl.BlockSpec((tm, hw), lambda i, h: (i, h)),
        out_shape=jax.ShapeDtypeStruct((m, n_heads * hw), BF16),
        scratch_shapes=[pltpu.VMEM((tm, Q_LORA), BF16)],
        compiler_params=_cparams(("parallel", "arbitrary"), 32 * 2**20),
        name="mla_q_expand",
    )(proj, q_norm.reshape(1, Q_LORA), w_uq_aug, cs)


def _mla_kv_kernel(ckv_ref, g_ref, w_ref, kr_ref, cs_ref, k_ref, v_ref, cn_ref, kpe_ref):
    @pl.when(pl.program_id(1) == 0)
    def _():
        cn_ref[...] = _rms(ckv_ref[...], g_ref[...]).astype(BF16)
        kpe_ref[...] = _apply_rope(kr_ref[...], cs_ref[...])

    y = _dot(cn_ref[...], w_ref[...])
    k_ref[...] = jnp.concatenate([y[:, :QK_NOPE], kpe_ref[...]], axis=1).astype(k_ref.dtype)
    v_ref[...] = y[:, QK_NOPE:].astype(v_ref.dtype)


def _mla_kv(proj, kv_norm, w_ukv, cs, tm=512):
    m = proj.shape[0]
    hw = 2 * LANES
    n_heads = w_ukv.shape[1] // hw
    return pl.pallas_call(
        _mla_kv_kernel,
        grid=(m // tm, n_heads),
        in_specs=[pl.BlockSpec((tm, KV_LORA), lambda i, h: (i, 4)),
                  pl.BlockSpec((1, KV_LORA), lambda i, h: (0, 0)),
                  pl.BlockSpec((KV_LORA, hw), lambda i, h: (0, h)),
                  pl.BlockSpec((tm, LANES), lambda i, h: (i, 10)),
                  pl.BlockSpec((tm, LANES), lambda i, h: (i, 0))],
        out_specs=[pl.BlockSpec((tm, hw), lambda i, h: (i, h)),
                   pl.BlockSpec((tm, LANES), lambda i, h: (i, h))],
        out_shape=[jax.ShapeDtypeStruct((m, n_heads * hw), BF16),
                   jax.ShapeDtypeStruct((m, n_heads * LANES), BF16)],
        scratch_shapes=[pltpu.VMEM((tm, KV_LORA), BF16), pltpu.VMEM((tm, LANES), F32)],
        compiler_params=_cparams(("parallel", "arbitrary"), 32 * 2**20),
        name="mla_kv_expand",
    )(proj, kv_norm.reshape(1, KV_LORA), w_ukv, proj, cs)


def _swap_halves(w):
    half = w.shape[-1] // 2
    return jnp.concatenate([w[..., half:], w[..., :half]], axis=-1)


def _mixer_out(x2, o, o_mem, w_out, b, s):
    d = x2.shape[1]
    split = o.shape[-1]
    w = w_out.astype(BF16)
    return _matmul_res([o.reshape(b * s, split), o_mem.reshape(b * s, MEM_W)],
                       [w[:split], w[split:]], x2, tm=512, tn=min(512, d))


def _sb_layer(x2, mem_kv, g, w_in, w_out, b, s):
    proj = _norm_matmul(x2, g, w_in.astype(BF16), BF16, tm=512, tn=512, scaled_cols=N_HEADS * HEAD_DIM,
                        col_scale=HEAD_DIM ** -0.5 * LOG2E).reshape(b, s, -1)
    o = _sb_attention(proj, N_HEADS)
    o_mem = _mem_attention(proj, 3 * N_HEADS * HEAD_DIM // MEM_W, mem_kv)
    return _mixer_out(x2, o, o_mem, w_out, b, s)


def _fox_layer(x2, mem_kv, g, w_in, b_f, w_out, b, s):
    qkv_w = 3 * N_HEADS * HEAD_DIM
    w_main = jnp.concatenate([w_in[:, :qkv_w], w_in[:, qkv_w + N_HEADS:]], axis=1).astype(BF16)
    w_gate = jnp.pad(w_in[:, qkv_w:qkv_w + N_HEADS], ((0, 0), (0, LANES - N_HEADS))).astype(BF16)
    proj = _norm_matmul(x2, g, w_main, BF16, tm=512, tn=512, scaled_cols=N_HEADS * HEAD_DIM,
                        col_scale=HEAD_DIM ** -0.5 * LOG2E).reshape(b, s, -1)
    f_logit = _norm_matmul(x2, g, w_gate, F32, tm=512, tn=LANES).reshape(b, s, LANES)
    aug = _fox_gates(f_logit, jnp.pad(b_f, (0, LANES - N_HEADS)), N_HEADS)
    o = _flash_attention(proj, proj, proj, 0, N_HEADS, 2 * N_HEADS, HEAD_DIM, HEAD_DIM, N_HEADS, aug=aug)
    o_mem = _mem_attention(proj, qkv_w // MEM_W, mem_kv)
    return _mixer_out(x2, o, o_mem, w_out, b, s)


def _swa_layer(x2, mem_kv, g, rel_bias, w_in, sinks, w_out, b, s):
    proj = _norm_matmul(x2, g, w_in.astype(BF16), BF16, tm=512, tn=512).reshape(b, s, -1)
    o = _swa_attention(proj, sinks, _t5_bias(rel_bias))
    q_w = N_SWA_HEADS * SWA_HEAD_DIM
    kv_w = 2 * N_SWA_KV_HEADS * SWA_HEAD_DIM
    o_mem = _mem_attention(proj, (q_w + kv_w) // MEM_W, mem_kv)
    return _mixer_out(x2, o, o_mem, w_out, b, s)


def _mla_layer(x2, mem_kv, g, positions, w_in, q_norm, w_uq, kv_norm, w_ukv, w_out, b, s):
    d = x2.shape[1]
    o1, o2, o3 = Q_LORA, Q_LORA + KV_LORA, Q_LORA + KV_LORA + QK_ROPE
    w_main = jnp.concatenate([w_in[:, o3:], w_in[:, :o3], _swap_halves(w_in[:, o2:o3])], axis=1).astype(BF16)
    proj = _norm_matmul(x2, g, w_main, F32, tm=512, tn=w_main.shape[1])
    cs = _rope_table(positions.reshape(b * s, 1))
    wq = w_uq.reshape(Q_LORA, N_HEADS, QK_NOPE + QK_ROPE)
    wq = jnp.concatenate([wq, _swap_halves(wq[:, :, QK_NOPE:])], axis=-1).reshape(Q_LORA, -1).astype(BF16)
    q = _mla_q(proj, q_norm, wq, cs).reshape(b, s, -1)
    k, v = _mla_kv(proj, kv_norm, w_ukv.astype(BF16), cs)
    k, v = k.reshape(b, s, -1), v.reshape(b, s, -1)
    o = _flash_attention(q, k, v, 0, 0, 0, 2 * LANES, HEAD_DIM, N_HEADS)
    o_mem = _mem_attention(proj.reshape(b, s, -1), 0, mem_kv)
    return _mixer_out(x2, o, o_mem, w_out, b, s)


def kernel(x, mem, positions, rel_bias, attn_norm, mem_norm, w_mem_kv, ffn_norm, ffn_w_up, ffn_conv_w, ffn_conv_b, ffn_w_down, final_norm, sb_w_in, sb_w_out, fox_w_in, fox_b_f, fox_w_out, swa_w_in, swa_sinks, swa_w_out, mla_w_in, mla_q_norm, mla_w_uq, mla_kv_norm, mla_w_ukv, mla_w_out):
    b, s, d = x.shape
    depth = attn_norm.shape[0]
    mem_len = mem.shape[1]
    x2 = x.reshape(b * s, d)
    mem2 = mem.reshape(b * mem_len, d)
    for i in range(depth):
        kind, j = i % 4, i // 4
        mem_kv = _norm_matmul(mem2, mem_norm[i], w_mem_kv[i].astype(BF16), BF16,
                              tm=b * mem_len, tn=2 * MEM_W).reshape(b, mem_len, 2 * MEM_W)
        g = attn_norm[i]
        if kind == 0:
            x2 = _sb_layer(x2, mem_kv, g, sb_w_in[j], sb_w_out[j], b, s)
        elif kind == 1:
            x2 = _fox_layer(x2, mem_kv, g, fox_w_in[j], fox_b_f[j], fox_w_out[j], b, s)
        elif kind == 2:
            x2 = _swa_layer(x2, mem_kv, g, rel_bias, swa_w_in[j], swa_sinks[j], swa_w_out[j], b, s)
        else:
            x2 = _mla_layer(x2, mem_kv, g, positions, mla_w_in[j], mla_q_norm[j], mla_w_uq[j],
                            mla_kv_norm[j], mla_w_ukv[j], mla_w_out[j], b, s)
        gated = _ffn_up(x2, ffn_norm[i], ffn_w_up[i].astype(BF16), ffn_conv_w[i], ffn_conv_b[i],
                        seq=s, tm=1024, tn=512)
        x2 = _matmul_res([gated], [ffn_w_down[i].astype(BF16)], x2, tm=512, tn=512)
    return _rmsnorm(x2, final_norm, tm=512).reshape(b, s, d)
```

```python
import functools
import math

import jax
import jax.numpy as jnp
from jax import lax
from jax.experimental import pallas as pl
from jax.experimental.pallas import tpu as pltpu

F32 = jnp.float32
BF16 = jnp.bfloat16

EPS = 1e-6
LOG2E = math.log2(math.e)
HEAD_DIM = 128
N_HEADS = 16
SWA_HEAD_DIM = 64
N_SWA_HEADS = 32
N_SWA_KV_HEADS = 4
SWA_GROUP = N_SWA_HEADS // N_SWA_KV_HEADS
WINDOW = 128
Q_LORA = 512
KV_LORA = 256
QK_NOPE = 128
QK_ROPE = 64
ROPE_THETA = 10000.0
MEM_HEADS = 4
MEM_W = MEM_HEADS * HEAD_DIM
NUM_BUCKETS = 32
MAX_DISTANCE = 128
CONV_WIDTH = 3

V7X_VMEM_BYTES = 64 * 1024 * 1024
VMEM_CAP = V7X_VMEM_BYTES - 8 * 1024 * 1024
LANES = 128
HALO = 16


def _cparams(sems, vmem_bytes):
    return pltpu.CompilerParams(dimension_semantics=sems,
                                vmem_limit_bytes=int(min(max(vmem_bytes, 16 * 2**20), VMEM_CAP)))


def _nbytes(shape, dtype):
    return math.prod(shape) * jnp.dtype(dtype).itemsize


def _rms(x, g):
    return x * lax.rsqrt(jnp.mean(x * x, axis=-1, keepdims=True) + EPS) * g


def _dot(a, b):
    return jnp.dot(a, b, preferred_element_type=F32)


def _dot_nt(a, b):
    return lax.dot_general(a, b, (((1,), (1,)), ((), ())), preferred_element_type=F32)


def _norm_matmul_kernel(x_ref, g_ref, w_ref, cs_ref, o_ref, xn_ref):
    @pl.when(pl.program_id(1) == 0)
    def _():
        xn_ref[...] = _rms(x_ref[...], g_ref[...]).astype(BF16)

    o_ref[...] = (_dot(xn_ref[...], w_ref[...]) * cs_ref[...]).astype(o_ref.dtype)


def _w_tile_spec(w, layer, k, tn, col_block):
    if w.ndim == 3:
        return pl.BlockSpec((None, k, tn), lambda i, j: (layer, 0, col_block(j)))
    return pl.BlockSpec((k, tn), lambda i, j: (0, col_block(j)))


def _norm_matmul(x, g, w, out_dtype, tm, tn, scaled_cols=0, col_scale=1.0, layer=None):
    m, k = x.shape
    n = w.shape[-1]
    cs = jnp.where(jnp.arange(n) < scaled_cols, col_scale, 1.0).astype(F32).reshape(1, n)
    vmem = (2 * (_nbytes((tm, k), F32) + _nbytes((k, tn), BF16) + _nbytes((tm, tn), out_dtype))
            + _nbytes((tm, k), BF16) + 2 * _nbytes((tm, k), F32) + _nbytes((tm, tn), F32))
    return pl.pallas_call(
        _norm_matmul_kernel,
        grid=(m // tm, n // tn),
        in_specs=[pl.BlockSpec((tm, k), lambda i, j: (i, 0)),
                  pl.BlockSpec((1, k), lambda i, j: (0, 0)),
                  _w_tile_spec(w, layer, k, tn, lambda j: j),
                  pl.BlockSpec((1, tn), lambda i, j: (0, j))],
        out_specs=pl.BlockSpec((tm, tn), lambda i, j: (i, j)),
        out_shape=jax.ShapeDtypeStruct((m, n), out_dtype),
        scratch_shapes=[pltpu.VMEM((tm, k), BF16)],
        compiler_params=_cparams(("parallel", "arbitrary"), vmem),
        name="norm_matmul",
    )(x, g.reshape(1, k), w, cs)


def _matmul_res_kernel(*refs, n_a):
    a_refs, w_refs = refs[:n_a], refs[n_a:2 * n_a]
    x_ref, o_ref = refs[2 * n_a], refs[2 * n_a + 1]
    wb_refs = refs[2 * n_a + 2:]

    @pl.when(pl.program_id(1) == 0)
    def _():
        for w_ref, wb_ref in zip(w_refs, wb_refs):
            wb_ref[...] = w_ref[...].astype(BF16)

    acc = x_ref[...]
    for a_ref, wb_ref in zip(a_refs, wb_refs):
        acc = acc + _dot(a_ref[...], wb_ref[...])
    o_ref[...] = acc


def _matmul_res(a_list, w, layer, x, tm, tn):
    m, n = x.shape
    vmem = 6 * _nbytes((tm, tn), F32)
    a_specs, w_specs, scratch = [], [], []
    row = 0
    for a in a_list:
        kp = a.shape[1]
        row_block = row // kp
        assert row_block * kp == row
        a_specs.append(pl.BlockSpec((tm, kp), lambda j, i: (i, 0)))
        w_specs.append(pl.BlockSpec((None, kp, tn), lambda j, i, row_block=row_block: (layer, row_block, j)))
        scratch.append(pltpu.VMEM((kp, tn), BF16))
        vmem += 2 * _nbytes((tm, kp), a.dtype) + 2 * _nbytes((kp, tn), F32) + 2 * _nbytes((kp, tn), BF16)
        row += kp
    assert row == w.shape[1]
    return pl.pallas_call(
        functools.partial(_matmul_res_kernel, n_a=len(a_list)),
        grid=(n // tn, m // tm),
        in_specs=a_specs + w_specs + [pl.BlockSpec((tm, tn), lambda j, i: (i, j))],
        out_specs=pl.BlockSpec((tm, tn), lambda j, i: (i, j)),
        out_shape=jax.ShapeDtypeStruct((m, n), F32),
        scratch_shapes=scratch,
        compiler_params=_cparams(("parallel", "arbitrary"), vmem),
        name="matmul_res",
    )(*a_list, *([w] * len(a_list)), x)


def _rmsnorm_kernel(x_ref, g_ref, o_ref):
    o_ref[...] = _rms(x_ref[...], g_ref[...])


def _rmsnorm(x, g, tm):
    m, k = x.shape
    return pl.pallas_call(
        _rmsnorm_kernel,
        grid=(m // tm,),
        in_specs=[pl.BlockSpec((tm, k), lambda i: (i, 0)), pl.BlockSpec((1, k), lambda i: (0, 0))],
        out_specs=pl.BlockSpec((tm, k), lambda i: (i, 0)),
        out_shape=jax.ShapeDtypeStruct((m, k), F32),
        compiler_params=_cparams(("parallel",), 8 * _nbytes((tm, k), F32)),
        name="final_rmsnorm",
    )(x, g.reshape(1, k))


def _ffn_up_kernel(x_ref, xh_ref, g_ref, wg_ref, wv_ref, cwg_ref, cwv_ref, cbg_ref, cbv_ref,
                   o_ref, xn_ref, *u_refs, rows, tiles_per_seq):
    i = pl.program_id(0)

    @pl.when(pl.program_id(1) == 0)
    def _():
        xn_ref[0:HALO, :] = _rms(xh_ref[...], g_ref[...]).astype(BF16)
        xn_ref[HALO:, :] = _rms(x_ref[...], g_ref[...]).astype(BF16)

    keep = jnp.where(i % tiles_per_seq == 0, 0.0, 1.0)

    for r in range(len(u_refs) // 2):
        xr = xn_ref[r * rows:r * rows + rows + HALO, :]

        def conv(w_ref, cw_ref, cb_ref, u_ref):
            u = _dot(xr, w_ref[...])
            if r == 0:
                u_ref[0:HALO, :] = u[0:HALO, :] * keep
                u_ref[HALO:, :] = u[HALO:, :]
            else:
                u_ref[...] = u
            c = cb_ref[...]
            for tap in range(CONV_WIDTH):
                start = HALO - (CONV_WIDTH - 1) + tap
                c = c + cw_ref[tap:tap + 1, :] * u_ref[start:start + rows, :]
            return c

        gate = conv(wg_ref, cwg_ref, cbg_ref, u_refs[2 * r])
        val = conv(wv_ref, cwv_ref, cbv_ref, u_refs[2 * r + 1])
        o_ref[r * rows:(r + 1) * rows, :] = (gate * (1.0 / (1.0 + jnp.exp(-gate))) * val).astype(o_ref.dtype)


def _ffn_up(x, g, w_up, layer, conv_w, conv_b, seq, tm, tn, rows=256):
    m, k = x.shape
    d_ff = w_up.shape[-1] // 2
    nj = d_ff // tn
    halo_blocks = tm // HALO
    n_chunks = tm // rows
    vmem = (2 * (_nbytes((tm, k), F32) + _nbytes((HALO, k), F32) + 2 * _nbytes((k, tn), BF16)
                 + _nbytes((tm, tn), BF16))
            + _nbytes((tm + HALO, k), BF16) + 2 * n_chunks * _nbytes((rows + HALO, tn), F32)
            + _nbytes((tm, k), F32) + 6 * n_chunks * _nbytes((rows, tn), F32))
    return pl.pallas_call(
        functools.partial(_ffn_up_kernel, rows=rows, tiles_per_seq=seq // tm),
        grid=(m // tm, nj),
        in_specs=[pl.BlockSpec((tm, k), lambda i, j: (i, 0)),
                  pl.BlockSpec((HALO, k), lambda i, j: (jnp.maximum(i * halo_blocks - 1, 0), 0)),
                  pl.BlockSpec((1, k), lambda i, j: (0, 0)),
                  _w_tile_spec(w_up, layer, k, tn, lambda j: j),
                  _w_tile_spec(w_up, layer, k, tn, lambda j: nj + j),
                  pl.BlockSpec((CONV_WIDTH, tn), lambda i, j: (0, j)),
                  pl.BlockSpec((CONV_WIDTH, tn), lambda i, j: (0, nj + j)),
                  pl.BlockSpec((1, tn), lambda i, j: (0, j)),
                  pl.BlockSpec((1, tn), lambda i, j: (0, nj + j))],
        out_specs=pl.BlockSpec((tm, tn), lambda i, j: (i, j)),
        out_shape=jax.ShapeDtypeStruct((m, d_ff), BF16),
        scratch_shapes=[pltpu.VMEM((tm + HALO, k), BF16)]
        + [pltpu.VMEM((rows + HALO, tn), F32)] * (2 * n_chunks),
        compiler_params=_cparams(("parallel", "arbitrary"), vmem),
        name="ffn_up_conv_gate",
    )(x, x, g.reshape(1, k), w_up, w_up, conv_w, conv_w,
      conv_b.reshape(1, -1), conv_b.reshape(1, -1))


def _eye(n):
    return jnp.where(lax.broadcasted_iota(jnp.int32, (n, n), 0) == lax.broadcasted_iota(jnp.int32, (n, n), 1),
                     1.0, 0.0).astype(BF16)


def _transpose_bf16(x):
    return _dot_nt(_eye(x.shape[1]), x)


def _fill_v_transposed(v_ref, vt_ref, t):
    s, dv = v_ref.shape
    for c in range(s // t):
        vt_ref[0:dv, c * t:(c + 1) * t] = _transpose_bf16(v_ref[c * t:(c + 1) * t, :]).astype(BF16)
    if vt_ref.shape[0] > dv:
        vt_ref[dv:, :] = jnp.ones((vt_ref.shape[0] - dv, s), BF16)


def _flash_kernel(*refs, t, has_aug):
    if has_aug:
        q_ref, k_ref, v_ref, qa_ref, ka_ref, o_ref, vt_ref, m_ref, acc_ref = refs
    else:
        q_ref, k_ref, v_ref, o_ref, vt_ref, m_ref, acc_ref = refs
    qi = pl.program_id(2)
    dv = v_ref.shape[1]

    @pl.when(qi == 0)
    def _():
        _fill_v_transposed(v_ref, vt_ref, t)

    q = q_ref[...]
    if has_aug:
        q = jnp.concatenate([q, qa_ref[...]], axis=1)
    m_ref[...] = jnp.full(m_ref.shape, -jnp.inf, F32)
    acc_ref[...] = jnp.zeros(acc_ref.shape, F32)

    def scores(kb):
        ks = pl.multiple_of(kb * t, t)
        k = k_ref[pl.ds(ks, t), :]
        if has_aug:
            k = jnp.concatenate([k, ka_ref[pl.ds(ks, t), :]], axis=1)
        return _dot_nt(k, q)

    def softmax(s, on_diagonal):
        if on_diagonal:
            key = lax.broadcasted_iota(jnp.int32, (t, t), 0)
            qry = lax.broadcasted_iota(jnp.int32, (t, t), 1)
            s = jnp.where(key <= qry, s, -jnp.inf)
        m_prev = m_ref[...]
        m_new = jnp.maximum(m_prev, jnp.max(s, axis=0, keepdims=True))
        m_ref[...] = m_new
        return jnp.exp2(s - m_new).astype(BF16), jnp.exp2(m_prev - m_new)

    def accumulate(p, alpha, kb):
        ks = pl.multiple_of(kb * t, t)
        acc_ref[...] = alpha * acc_ref[...] + _dot(vt_ref[:, pl.ds(ks, t)], p)

    def finish():
        acc = acc_ref[...]
        o_ref[...] = (acc[0:dv, :] / acc[dv:dv + 1, :]).T.astype(o_ref.dtype)

    def body(kb, s):
        s_next = scores(kb + 1)
        accumulate(*softmax(s, False), kb)
        return s_next

    s_diag = lax.fori_loop(0, qi, body, scores(0))
    accumulate(*softmax(s_diag, True), qi)
    finish()


def _flash_attention(q_arr, k_arr, v_arr, q_off, k_off, v_off, dk, dv, n_heads, aug=None, t=512):
    b, s, _ = q_arr.shape
    in_specs = [pl.BlockSpec((None, t, dk), lambda bi, h, qi: (bi, qi, q_off + h)),
                pl.BlockSpec((None, s, dk), lambda bi, h, qi: (bi, 0, k_off + h)),
                pl.BlockSpec((None, s, dv), lambda bi, h, qi: (bi, 0, v_off + h))]
    args = [q_arr, k_arr, v_arr]
    vmem = 2 * (_nbytes((t, dk), BF16) + _nbytes((s, dk), BF16) + _nbytes((s, dv), BF16) + _nbytes((t, dv), BF16))
    if aug is not None:
        in_specs += [pl.BlockSpec((None, None, t, LANES), lambda bi, h, qi: (bi, h, qi, 0)),
                     pl.BlockSpec((None, None, s, LANES), lambda bi, h, qi: (bi, h, 0, 0))]
        args += list(aug)
        vmem += 2 * (_nbytes((t, LANES), BF16) + _nbytes((s, LANES), BF16))
    acc_rows = dv + 16
    vmem += _nbytes((acc_rows, s), BF16) + 2 * _nbytes((acc_rows, t), F32) + 12 * _nbytes((t, t), F32)
    return pl.pallas_call(
        functools.partial(_flash_kernel, t=t, has_aug=aug is not None),
        grid=(b, n_heads, s // t),
        in_specs=in_specs,
        out_specs=pl.BlockSpec((None, t, dv), lambda bi, h, qi: (bi, qi, h)),
        out_shape=jax.ShapeDtypeStruct((b, s, n_heads * dv), BF16),
        scratch_shapes=[pltpu.VMEM((acc_rows, s), BF16), pltpu.VMEM((1, t), F32),
                        pltpu.VMEM((acc_rows, t), F32)],
        compiler_params=_cparams(("parallel", "parallel", "arbitrary"), vmem),
        name="flash_attention",
    )(*args)


def _sb_kernel(q_ref, k_ref, v_ref, o_ref, vt_ref, carry_ref, acc_ref, *, t, sub):
    qi = pl.program_id(2)

    @pl.when(qi == 0)
    def _():
        _fill_v_transposed(v_ref, vt_ref, t)

    carry_ref[...] = jnp.zeros(carry_ref.shape, F32)
    acc_ref[...] = jnp.zeros(acc_ref.shape, F32)
    q = q_ref[...]
    key = lax.broadcasted_iota(jnp.int32, (sub, t), 0)
    qry = lax.broadcasted_iota(jnp.int32, (sub, t), 1)
    r = lax.broadcasted_iota(jnp.int32, (sub + 16, sub), 0)
    c = lax.broadcasted_iota(jnp.int32, (sub + 16, sub), 1)
    suffix = jnp.where(((c > r) & (r < sub)) | (r == sub), 1.0, 0.0).astype(BF16)
    suffix = jnp.concatenate([suffix, suffix], axis=1)

    def scores(kb):
        ks = pl.multiple_of(kb * t, t)
        return _dot_nt(k_ref[pl.ds(ks, t), :], q)

    def weights(s, on_diagonal):
        later = carry_ref[...]
        out = [None] * (t // sub)
        for i in reversed(range(t // sub)):
            z = s[i * sub:(i + 1) * sub, :]
            neg_abs = lax.bitcast_convert_type(
                lax.bitcast_convert_type(z, jnp.uint32) | jnp.uint32(0x80000000), F32)
            log_beta = jnp.minimum(z, 0.0) - jnp.log2(1.0 + jnp.exp2(neg_abs))
            log_keep = log_beta - z
            if on_diagonal:
                strict = key + i * sub < qry
                log_keep = jnp.where(strict, log_keep, 0.0)
            hi = log_keep.astype(BF16)
            lo = (log_keep - hi.astype(F32)).astype(BF16)
            sums = _dot(suffix, jnp.concatenate([hi, lo], axis=0))
            a = jnp.exp2(log_beta + (sums[0:sub, :] + later))
            if on_diagonal:
                a = jnp.where(strict, a, 0.0)
            out[i] = a.astype(BF16)
            later = later + sums[sub:sub + 1, :]
        carry_ref[...] = later
        return jnp.concatenate(out, axis=0)

    def accumulate(a, kb):
        ks = pl.multiple_of(kb * t, t)
        acc_ref[...] += _dot(vt_ref[:, pl.ds(ks, t)], a)

    def finish():
        o_ref[...] = acc_ref[...].T.astype(o_ref.dtype)

    @pl.when(qi == 0)
    def _():
        accumulate(weights(scores(0), True), 0)
        finish()

    @pl.when(qi > 0)
    def _():
        s_ahead = scores(qi - 1)
        a_diag = weights(scores(qi), True)

        def body(i, carry):
            s, a = carry
            accumulate(a, qi - i)
            s_next = scores(qi - i - 2)
            return s_next, weights(s, False)

        s_last, a = lax.fori_loop(0, qi - 1, body, (s_ahead, a_diag))
        accumulate(a, 1)
        accumulate(weights(s_last, False), 0)
        finish()


def _sb_attention(proj, n_heads, t=512):
    b, s, _ = proj.shape
    d = HEAD_DIM
    vmem = (2 * (2 * _nbytes((t, d), BF16) + 2 * _nbytes((s, d), BF16))
            + _nbytes((d, s), BF16) + 2 * _nbytes((d, t), F32) + 16 * _nbytes((t, t), F32))
    return pl.pallas_call(
        functools.partial(_sb_kernel, t=t, sub=LANES),
        grid=(b, n_heads, s // t),
        in_specs=[pl.BlockSpec((None, t, d), lambda bi, h, qi: (bi, qi, h)),
                  pl.BlockSpec((None, s, d), lambda bi, h, qi: (bi, 0, n_heads + h)),
                  pl.BlockSpec((None, s, d), lambda bi, h, qi: (bi, 0, 2 * n_heads + h))],
        out_specs=pl.BlockSpec((None, t, d), lambda bi, h, qi: (bi, qi, h)),
        out_shape=jax.ShapeDtypeStruct((b, s, n_heads * d), BF16),
        scratch_shapes=[pltpu.VMEM((d, s), BF16), pltpu.VMEM((1, t), F32), pltpu.VMEM((d, t), F32)],
        compiler_params=_cparams(("parallel", "parallel", "arbitrary"), vmem),
        name="stick_breaking_attention",
    )(proj, proj, proj)


def _split3(x):
    hi = x.astype(BF16)
    rest = x - hi.astype(F32)
    mid = rest.astype(BF16)
    lo = (rest - mid.astype(F32)).astype(BF16)
    return hi, mid, lo


def _fox_gate_kernel(fl_ref, b_ref, pq_ref, pk_ref, oq_ref, ok_ref, qa_ref, ka_ref, carry_ref):
    ts = fl_ref.shape[0]

    @pl.when(pl.program_id(1) == 0)
    def _():
        carry_ref[...] = jnp.zeros(carry_ref.shape, F32)

    z = fl_ref[...] + b_ref[...]
    log_f = jnp.minimum(z, 0.0) - jnp.log(1.0 + jnp.exp(-jnp.abs(z)))
    row = lax.broadcasted_iota(jnp.int32, (ts, ts), 0)
    col = lax.broadcasted_iota(jnp.int32, (ts, ts), 1)
    prefix = jnp.where(col <= row, 1.0, 0.0).astype(BF16)
    hi, mid, lo = _split3(log_f)
    c = _dot(prefix, hi) + _dot(prefix, mid) + _dot(prefix, lo) + carry_ref[0:1, :]
    carry_ref[0:1, :] = c[ts - 1:ts, :]
    parts = jnp.concatenate(_split3(c * LOG2E), axis=1)
    qa = (_dot(parts, pq_ref[...]) + oq_ref[...]).astype(BF16)
    ka = (_dot(parts, pk_ref[...]) + ok_ref[...]).astype(BF16)
    for h in range(qa_ref.shape[0]):
        qa_ref[h] = qa[:, h * LANES:(h + 1) * LANES]
        ka_ref[h] = ka[:, h * LANES:(h + 1) * LANES]


def _fox_aug_tables(n_heads):
    part, src = jnp.arange(3 * LANES) // LANES, jnp.arange(3 * LANES) % LANES
    head, lane = jnp.arange(n_heads * LANES) // LANES, jnp.arange(n_heads * LANES) % LANES
    mine = src[:, None] == head[None, :]
    pq = jnp.where(mine & (lane[None, :] == part[:, None] + 3), 1.0, 0.0).astype(BF16)
    pk = jnp.where(mine & (lane[None, :] == part[:, None]), -1.0, 0.0).astype(BF16)
    oq = jnp.where(lane < 3, 1.0, 0.0).astype(F32)[None]
    ok = jnp.where((lane >= 3) & (lane < 6), 1.0, 0.0).astype(F32)[None]
    return pq, pk, oq, ok


def _fox_gates(f_logit, b_f, n_heads, ts=256):
    b, s, w = f_logit.shape
    hw = n_heads * LANES
    out = jax.ShapeDtypeStruct((b, n_heads, s, LANES), BF16)
    const = lambda bi, i: (0, 0)
    return pl.pallas_call(
        _fox_gate_kernel,
        grid=(b, s // ts),
        in_specs=[pl.BlockSpec((None, ts, w), lambda bi, i: (bi, i, 0)),
                  pl.BlockSpec((1, w), const),
                  pl.BlockSpec((3 * LANES, hw), const), pl.BlockSpec((3 * LANES, hw), const),
                  pl.BlockSpec((1, hw), const), pl.BlockSpec((1, hw), const)],
        out_specs=[pl.BlockSpec((None, n_heads, ts, LANES), lambda bi, i: (bi, 0, i, 0)),
                   pl.BlockSpec((None, n_heads, ts, LANES), lambda bi, i: (bi, 0, i, 0))],
        out_shape=[out, out],
        scratch_shapes=[pltpu.VMEM((8, w), F32)],
        compiler_params=_cparams(("parallel", "arbitrary"), 32 * 2**20),
        name="fox_gate_cumsum",
    )(f_logit, b_f.reshape(1, w), *_fox_aug_tables(n_heads))


def _mem_attn_kernel(q_ref, kv_ref, o_ref, *, scale):
    outs = []
    for h in range(MEM_HEADS):
        q = q_ref[:, h * HEAD_DIM:(h + 1) * HEAD_DIM].astype(BF16)
        k = kv_ref[:, h * HEAD_DIM:(h + 1) * HEAD_DIM]
        v = kv_ref[:, MEM_W + h * HEAD_DIM:MEM_W + (h + 1) * HEAD_DIM]
        s = _dot_nt(q, k) * scale
        p = jnp.exp(s - jnp.max(s, axis=1, keepdims=True))
        o = _dot(p.astype(BF16), v) / jnp.sum(p, axis=1, keepdims=True)
        outs.append(o.astype(o_ref.dtype))
    o_ref[...] = jnp.concatenate(outs, axis=1)


def _mem_attention(q_arr, q_block, mem_kv, tq=512):
    b, s, _ = q_arr.shape
    length = mem_kv.shape[1]
    vmem = (2 * (_nbytes((tq, MEM_W), q_arr.dtype) + _nbytes((length, 2 * MEM_W), BF16)
                 + _nbytes((tq, MEM_W), BF16)) + 8 * _nbytes((tq, length), F32))
    return pl.pallas_call(
        functools.partial(_mem_attn_kernel, scale=HEAD_DIM ** -0.5),
        grid=(b, s // tq),
        in_specs=[pl.BlockSpec((None, tq, MEM_W), lambda bi, i: (bi, i, q_block)),
                  pl.BlockSpec((None, length, 2 * MEM_W), lambda bi, i: (bi, 0, 0))],
        out_specs=pl.BlockSpec((None, tq, MEM_W), lambda bi, i: (bi, i, 0)),
        out_shape=jax.ShapeDtypeStruct((b, s, MEM_W), BF16),
        compiler_params=_cparams(("parallel", "parallel"), vmem),
        name="memory_attention",
    )(q_arr, mem_kv)


def _t5_bucket_table():
    max_exact = NUM_BUCKETS // 2
    kj = jnp.arange(2 * WINDOW)[:, None]
    qi = jnp.arange(WINDOW)[None, :]
    signed = WINDOW + qi - kj
    dist = jnp.maximum(signed, 0)
    d = jnp.maximum(dist, 1).astype(F32)
    large = max_exact + (jnp.log(d / max_exact) / math.log(MAX_DISTANCE / max_exact)
                         * (NUM_BUCKETS - max_exact)).astype(jnp.int32)
    bucket = jnp.where(dist < max_exact, dist, jnp.minimum(large, NUM_BUCKETS - 1))
    return jnp.where((signed >= 0) & (signed < WINDOW), bucket, -1).astype(jnp.int32)


def _t5_bias_kernel(rb_ref, bucket_ref, o_ref):
    h = pl.program_id(0)
    bucket = bucket_ref[...]
    bias = jnp.full(bucket.shape, -jnp.inf, F32)
    for b in range(NUM_BUCKETS):
        bias = jnp.where(bucket == b, rb_ref[b, h] * LOG2E, bias)
    o_ref[...] = bias


def _t5_bias(rel_bias):
    n_heads = rel_bias.shape[1]
    return pl.pallas_call(
        _t5_bias_kernel,
        grid=(n_heads,),
        in_specs=[pl.BlockSpec(memory_space=pltpu.SMEM),
                  pl.BlockSpec((2 * WINDOW, WINDOW), lambda h: (0, 0))],
        out_specs=pl.BlockSpec((None, 2 * WINDOW, WINDOW), lambda h: (h, 0, 0)),
        out_shape=jax.ShapeDtypeStruct((n_heads, 2 * WINDOW, WINDOW), F32),
        compiler_params=_cparams(("parallel",), 16 * 2**20),
        name="t5_bias",
    )(rel_bias, _t5_bucket_table())


def _swa_kernel(sink_ref, q_ref, kvc_ref, kvp_ref, bias_ref, o_ref):
    n = pl.program_id(1)
    d = SWA_HEAD_DIM
    kv_w = N_SWA_KV_HEADS * d
    k_win = jnp.concatenate([kvp_ref[:, 0:kv_w], kvc_ref[:, 0:kv_w]], axis=0)
    v_win = jnp.concatenate([kvp_ref[:, kv_w:2 * kv_w], kvc_ref[:, kv_w:2 * kv_w]], axis=0)
    v_t = v_win.astype(F32).T.astype(BF16)
    no_prev = jnp.where(n == 0, -jnp.inf, 0.0)
    outs = []
    for kvh in range(N_SWA_KV_HEADS):
        k = k_win[:, kvh * d:(kvh + 1) * d]
        v = v_t[kvh * d:(kvh + 1) * d, :]
        for g in range(SWA_GROUP):
            h = kvh * SWA_GROUP + g
            s = _dot_nt(k, q_ref[:, h * d:(h + 1) * d]) + bias_ref[h]
            s = jnp.concatenate([s[0:WINDOW, :] + no_prev, s[WINDOW:, :]], axis=0)
            sink = sink_ref[h] * LOG2E
            m = jnp.maximum(jnp.max(s, axis=0, keepdims=True), sink)
            p = jnp.exp2(s - m)
            denom = jnp.sum(p, axis=0, keepdims=True) + jnp.exp2(sink - m)
            outs.append(_dot(v, p.astype(BF16)) / denom)
    o_ref[...] = jnp.concatenate(outs, axis=0).T.astype(o_ref.dtype)


def _swa_attention(proj, sinks, bias):
    b, s, _ = proj.shape
    q_w = N_SWA_HEADS * SWA_HEAD_DIM
    kv_w = 2 * N_SWA_KV_HEADS * SWA_HEAD_DIM
    kv_block = q_w // kv_w
    vmem = (2 * (2 * _nbytes((WINDOW, q_w), BF16) + 2 * _nbytes((WINDOW, kv_w), BF16)
                 + _nbytes(bias.shape, F32)) + 16 * 2**20)
    return pl.pallas_call(
        _swa_kernel,
        grid=(b, s // WINDOW),
        in_specs=[pl.BlockSpec(memory_space=pltpu.SMEM),
                  pl.BlockSpec((None, WINDOW, q_w), lambda bi, n: (bi, n, 0)),
                  pl.BlockSpec((None, WINDOW, kv_w), lambda bi, n: (bi, n, kv_block)),
                  pl.BlockSpec((None, WINDOW, kv_w), lambda bi, n: (bi, jnp.maximum(n - 1, 0), kv_block)),
                  pl.BlockSpec(bias.shape, lambda bi, n: (0, 0, 0))],
        out_specs=pl.BlockSpec((None, WINDOW, q_w), lambda bi, n: (bi, n, 0)),
        out_shape=jax.ShapeDtypeStruct((b, s, q_w), BF16),
        compiler_params=_cparams(("parallel", "arbitrary"), vmem),
        name="sliding_window_attention",
    )(sinks, proj, proj, proj, bias)


def _rope_table_kernel(pos_ref, invf_ref, o_ref):
    ang = pos_ref[...].astype(F32) * invf_ref[...]
    lane = lax.broadcasted_iota(jnp.int32, ang.shape, 1)
    half = QK_ROPE // 2
    sin_signed = jnp.where(lane < QK_ROPE + half, -jnp.sin(ang), jnp.sin(ang))
    o_ref[...] = jnp.where(lane < QK_ROPE, jnp.cos(ang), sin_signed)


def _rope_table(positions, tm=1024):
    m = positions.shape[0]
    half = QK_ROPE // 2
    inv_freq = ROPE_THETA ** (-jnp.arange(half, dtype=F32) / half)
    invf = jnp.tile(inv_freq, LANES // half).reshape(1, LANES)
    return pl.pallas_call(
        _rope_table_kernel,
        grid=(m // tm,),
        in_specs=[pl.BlockSpec((tm, 1), lambda i: (i, 0)), pl.BlockSpec((1, LANES), lambda i: (0, 0))],
        out_specs=pl.BlockSpec((tm, LANES), lambda i: (i, 0)),
        out_shape=jax.ShapeDtypeStruct((m, LANES), F32),
        compiler_params=_cparams(("parallel",), 32 * 2**20),
        name="rope_table",
    )(positions, invf)


def _apply_rope(x_and_partner, cs):
    z = x_and_partner * cs
    z = z + pltpu.roll(z, QK_ROPE, axis=1)
    lane = lax.broadcasted_iota(jnp.int32, z.shape, 1)
    return jnp.where(lane < QK_ROPE, z, 0.0)


def _mla_q_kernel(cq_ref, g_ref, w_ref, cs_ref, o_ref, cn_ref):
    @pl.when(pl.program_id(1) == 0)
    def _():
        cn_ref[...] = _rms(cq_ref[...], g_ref[...]).astype(BF16)

    y = _dot(cn_ref[...], w_ref[...]) * ((QK_NOPE + QK_ROPE) ** -0.5 * LOG2E)
    pe = _apply_rope(y[:, QK_NOPE:], cs_ref[...])
    o_ref[...] = jnp.concatenate([y[:, :QK_NOPE], pe], axis=1).astype(o_ref.dtype)


def _mla_q(proj, q_norm, w_uq_aug, cs, tm=512):
    m = proj.shape[0]
    hw = 2 * LANES
    n_heads = w_uq_aug.shape[1] // hw
    return pl.pallas_call(
        _mla_q_kernel,
        grid=(m // tm, n_heads),
        in_specs=[pl.BlockSpec((tm, Q_LORA), lambda i, h: (i, 1)),
                  pl.BlockSpec((1, Q_LORA), lambda i, h: (0, 0)),
                  pl.BlockSpec((Q_LORA, hw), lambda i, h: (0, h)),
                  pl.BlockSpec((tm, LANES), lambda i, h: (i, 0))],
        out_specs=pl.BlockSpec((tm, hw), lambda i, h: (i, h)),
        out_shape=jax.ShapeDtypeStruct((m, n_heads * hw), BF16),
        scratch_shapes=[pltpu.VMEM((tm, Q_LORA), BF16)],
        compiler_params=_cparams(("parallel", "arbitrary"), 32 * 2**20),
        name="mla_q_expand",
    )(proj, q_norm.reshape(1, Q_LORA), w_uq_aug, cs)


def _mla_kv_kernel(ckv_ref, g_ref, w_ref, kr_ref, cs_ref, k_ref, v_ref, cn_ref, kpe_ref):
    @pl.when(pl.program_id(1) == 0)
    def _():
        cn_ref[...] = _rms(ckv_ref[...], g_ref[...]).astype(BF16)
        kpe_ref[...] = _apply_rope(kr_ref[...], cs_ref[...])

    y = _dot(cn_ref[...], w_ref[...])
    k_ref[...] = jnp.concatenate([y[:, :QK_NOPE], kpe_ref[...]], axis=1).astype(k_ref.dtype)
    v_ref[...] = y[:, QK_NOPE:].astype(v_ref.dtype)


def _mla_kv(proj, kv_norm, w_ukv, cs, tm=512):
    m = proj.shape[0]
    hw = 2 * LANES
    n_heads = w_ukv.shape[1] // hw
    return pl.pallas_call(
        _mla_kv_kernel,
        grid=(m // tm, n_heads),
        in_specs=[pl.BlockSpec((tm, KV_LORA), lambda i, h: (i, 4)),
                  pl.BlockSpec((1, KV_LORA), lambda i, h: (0, 0)),
                  pl.BlockSpec((KV_LORA, hw), lambda i, h: (0, h)),
                  pl.BlockSpec((tm, LANES), lambda i, h: (i, 10)),
                  pl.BlockSpec((tm, LANES), lambda i, h: (i, 0))],
        out_specs=[pl.BlockSpec((tm, hw), lambda i, h: (i, h)),
                   pl.BlockSpec((tm, LANES), lambda i, h: (i, h))],
        out_shape=[jax.ShapeDtypeStruct((m, n_heads * hw), BF16),
                   jax.ShapeDtypeStruct((m, n_heads * LANES), BF16)],
        scratch_shapes=[pltpu.VMEM((tm, KV_LORA), BF16), pltpu.VMEM((tm, LANES), F32)],
        compiler_params=_cparams(("parallel", "arbitrary"), 32 * 2**20),
        name="mla_kv_expand",
    )(proj, kv_norm.reshape(1, KV_LORA), w_ukv, proj, cs)


def _swap_halves(w):
    half = w.shape[-1] // 2
    return jnp.concatenate([w[..., half:], w[..., :half]], axis=-1)


def _mixer_out(x2, o, o_mem, w_out, b, s):
    return _matmul_res([o.reshape(b * s, o.shape[-1]), o_mem.reshape(b * s, MEM_W)],
                       w_out[0], w_out[1], x2, tm=512, tn=512)


def _sb_layer(x2, mem_kv, g, w_in, w_out, b, s):
    proj = _norm_matmul(x2, g, w_in.astype(BF16), BF16, tm=512, tn=512, scaled_cols=N_HEADS * HEAD_DIM,
                        col_scale=HEAD_DIM ** -0.5 * LOG2E).reshape(b, s, -1)
    o = _sb_attention(proj, N_HEADS)
    o_mem = _mem_attention(proj, 3 * N_HEADS * HEAD_DIM // MEM_W, mem_kv)
    return _mixer_out(x2, o, o_mem, w_out, b, s)


def _fox_layer(x2, mem_kv, g, w_in, b_f, w_out, b, s):
    qkv_w = 3 * N_HEADS * HEAD_DIM
    w_main = jnp.concatenate([w_in[:, :qkv_w], w_in[:, qkv_w + N_HEADS:]], axis=1).astype(BF16)
    w_gate = jnp.pad(w_in[:, qkv_w:qkv_w + N_HEADS], ((0, 0), (0, LANES - N_HEADS))).astype(BF16)
    proj = _norm_matmul(x2, g, w_main, BF16, tm=512, tn=512, scaled_cols=N_HEADS * HEAD_DIM,
                        col_scale=HEAD_DIM ** -0.5 * LOG2E).reshape(b, s, -1)
    f_logit = _norm_matmul(x2, g, w_gate, F32, tm=512, tn=LANES).reshape(b, s, LANES)
    aug = _fox_gates(f_logit, jnp.pad(b_f, (0, LANES - N_HEADS)), N_HEADS)
    o = _flash_attention(proj, proj, proj, 0, N_HEADS, 2 * N_HEADS, HEAD_DIM, HEAD_DIM, N_HEADS, aug=aug)
    o_mem = _mem_attention(proj, qkv_w // MEM_W, mem_kv)
    return _mixer_out(x2, o, o_mem, w_out, b, s)


def _swa_layer(x2, mem_kv, g, rel_bias, w_in, sinks, w_out, b, s):
    proj = _norm_matmul(x2, g, w_in.astype(BF16), BF16, tm=512, tn=512,
                        scaled_cols=N_SWA_HEADS * SWA_HEAD_DIM,
                        col_scale=SWA_HEAD_DIM ** -0.5 * LOG2E).reshape(b, s, -1)
    o = _swa_attention(proj, sinks, _t5_bias(rel_bias))
    q_w = N_SWA_HEADS * SWA_HEAD_DIM
    kv_w = 2 * N_SWA_KV_HEADS * SWA_HEAD_DIM
    o_mem = _mem_attention(proj, (q_w + kv_w) // MEM_W, mem_kv)
    return _mixer_out(x2, o, o_mem, w_out, b, s)


def _mla_layer(x2, mem_kv, g, positions, w_in, q_norm, w_uq, kv_norm, w_ukv, w_out, b, s):
    d = x2.shape[1]
    o1, o2, o3 = Q_LORA, Q_LORA + KV_LORA, Q_LORA + KV_LORA + QK_ROPE
    w_main = jnp.concatenate([w_in[:, o3:], w_in[:, :o3], _swap_halves(w_in[:, o2:o3])], axis=1).astype(BF16)
    proj = _norm_matmul(x2, g, w_main, F32, tm=512, tn=w_main.shape[1])
    cs = _rope_table(positions.reshape(b * s, 1))
    wq = w_uq.reshape(Q_LORA, N_HEADS, QK_NOPE + QK_ROPE)
    wq = jnp.concatenate([wq, _swap_halves(wq[:, :, QK_NOPE:])], axis=-1).reshape(Q_LORA, -1).astype(BF16)
    q = _mla_q(proj, q_norm, wq, cs).reshape(b, s, -1)
    k, v = _mla_kv(proj, kv_norm, w_ukv.astype(BF16), cs)
    k, v = k.reshape(b, s, -1), v.reshape(b, s, -1)
    o = _flash_attention(q, k, v, 0, 0, 0, 2 * LANES, HEAD_DIM, N_HEADS)
    o_mem = _mem_attention(proj.reshape(b, s, -1), 0, mem_kv)
    return _mixer_out(x2, o, o_mem, w_out, b, s)


def kernel(x, mem, positions, rel_bias, attn_norm, mem_norm, w_mem_kv, ffn_norm, ffn_w_up, ffn_conv_w, ffn_conv_b, ffn_w_down, final_norm, sb_w_in, sb_w_out, fox_w_in, fox_b_f, fox_w_out, swa_w_in, swa_sinks, swa_w_out, mla_w_in, mla_q_norm, mla_w_uq, mla_kv_norm, mla_w_ukv, mla_w_out):
    b, s, d = x.shape
    depth = attn_norm.shape[0]
    mem_len = mem.shape[1]
    x2 = x.reshape(b * s, d)
    mem2 = mem.reshape(b * mem_len, d)
    w_up_bf16 = ffn_w_up.astype(BF16)
    w_mem_kv_bf16 = w_mem_kv.astype(BF16)
    for i in range(depth):
        kind, j = i % 4, i // 4
        mem_kv = _norm_matmul(mem2, mem_norm[i], w_mem_kv_bf16, BF16, tm=b * mem_len, tn=2 * MEM_W,
                              layer=i).reshape(b, mem_len, 2 * MEM_W)
        g = attn_norm[i]
        if kind == 0:
            x2 = _sb_layer(x2, mem_kv, g, sb_w_in[j], (sb_w_out, j), b, s)
        elif kind == 1:
            x2 = _fox_layer(x2, mem_kv, g, fox_w_in[j], fox_b_f[j], (fox_w_out, j), b, s)
        elif kind == 2:
            x2 = _swa_layer(x2, mem_kv, g, rel_bias, swa_w_in[j], swa_sinks[j], (swa_w_out, j), b, s)
        else:
            x2 = _mla_layer(x2, mem_kv, g, positions, mla_w_in[j], mla_q_norm[j], mla_w_uq[j],
                            mla_kv_norm[j], mla_w_ukv[j], (mla_w_out, j), b, s)
        gated = _ffn_up(x2, ffn_norm[i], w_up_bf16, i, ffn_conv_w[i], ffn_conv_b[i], seq=s, tm=1024, tn=512)
        x2 = _matmul_res([gated], ffn_w_down, i, x2, tm=512, tn=512)
    return _rmsnorm(x2, final_norm, tm=512).reshape(b, s, d)
```

```python
import functools
import math

import jax
import jax.numpy as jnp
from jax import lax
from jax.experimental import pallas as pl
from jax.experimental.pallas import tpu as pltpu

F32 = jnp.float32
BF16 = jnp.bfloat16

EPS = 1e-6
LOG2E = math.log2(math.e)
HEAD_DIM = 128
N_HEADS = 16
SWA_HEAD_DIM = 64
N_SWA_HEADS = 32
N_SWA_KV_HEADS = 4
SWA_GROUP = N_SWA_HEADS // N_SWA_KV_HEADS
WINDOW = 128
Q_LORA = 512
KV_LORA = 256
QK_NOPE = 128
QK_ROPE = 64
ROPE_THETA = 10000.0
MEM_HEADS = 4
MEM_W = MEM_HEADS * HEAD_DIM
NUM_BUCKETS = 32
MAX_DISTANCE = 128
CONV_WIDTH = 3

V7X_VMEM_BYTES = 64 * 1024 * 1024
VMEM_CAP = V7X_VMEM_BYTES - 8 * 1024 * 1024
LANES = 128
IN_PROJ_ROWS = 1024
TAIL = 8


def _cparams(sems, vmem_bytes):
    return pltpu.CompilerParams(dimension_semantics=sems,
                                vmem_limit_bytes=int(min(max(vmem_bytes, 16 * 2**20), VMEM_CAP)))


def _nbytes(shape, dtype):
    return math.prod(shape) * jnp.dtype(dtype).itemsize


def _rms(x, g):
    return x * lax.rsqrt(jnp.mean(x * x, axis=-1, keepdims=True) + EPS) * g


def _dot(a, b):
    return jnp.dot(a, b, preferred_element_type=F32)


def _dot_nt(a, b):
    return lax.dot_general(a, b, (((1,), (1,)), ((), ())), preferred_element_type=F32)


def _norm_matmul_kernel(x_ref, g_ref, w_ref, cs_ref, o_ref, xn_ref):
    @pl.when(pl.program_id(1) == 0)
    def _():
        xn_ref[...] = _rms(x_ref[...], g_ref[...]).astype(BF16)

    o_ref[...] = (_dot(xn_ref[...], w_ref[...]) * cs_ref[...]).astype(o_ref.dtype)


def _w_tile_spec(w, layer, k, tn, col_block):
    if w.ndim == 3:
        return pl.BlockSpec((None, k, tn), lambda i, j: (layer, 0, col_block(j)))
    return pl.BlockSpec((k, tn), lambda i, j: (0, col_block(j)))


def _norm_matmul(x, g, w, out_dtype, tm, tn, scaled_cols=0, col_scale=1.0, layer=None):
    m, k = x.shape
    n = w.shape[-1]
    cs = jnp.where(jnp.arange(n) < scaled_cols, col_scale, 1.0).astype(F32).reshape(1, n)
    vmem = (2 * (_nbytes((tm, k), F32) + _nbytes((k, tn), BF16) + _nbytes((tm, tn), out_dtype))
            + _nbytes((tm, k), BF16) + 2 * _nbytes((tm, k), F32) + _nbytes((tm, tn), F32))
    return pl.pallas_call(
        _norm_matmul_kernel,
        grid=(m // tm, n // tn),
        in_specs=[pl.BlockSpec((tm, k), lambda i, j: (i, 0)),
                  pl.BlockSpec((1, k), lambda i, j: (0, 0)),
                  _w_tile_spec(w, layer, k, tn, lambda j: j),
                  pl.BlockSpec((1, tn), lambda i, j: (0, j))],
        out_specs=pl.BlockSpec((tm, tn), lambda i, j: (i, j)),
        out_shape=jax.ShapeDtypeStruct((m, n), out_dtype),
        scratch_shapes=[pltpu.VMEM((tm, k), BF16)],
        compiler_params=_cparams(("parallel", "arbitrary"), vmem),
        name="norm_matmul",
    )(x, g.reshape(1, k), w, cs)


def _matmul_res_kernel(*refs, n_a):
    a_refs, w_refs = refs[:n_a], refs[n_a:2 * n_a]
    x_ref, o_ref = refs[2 * n_a], refs[2 * n_a + 1]
    wb_refs = refs[2 * n_a + 2:]

    @pl.when(pl.program_id(1) == 0)
    def _():
        for w_ref, wb_ref in zip(w_refs, wb_refs):
            wb_ref[...] = w_ref[...].astype(BF16)

    acc = x_ref[...]
    for a_ref, wb_ref in zip(a_refs, wb_refs):
        acc = acc + _dot(a_ref[...], wb_ref[...])
    o_ref[...] = acc


def _matmul_res(a_list, w, layer, x, tm, tn):
    m, n = x.shape
    vmem = 6 * _nbytes((tm, tn), F32)
    a_specs, w_specs, scratch = [], [], []
    row = 0
    for a in a_list:
        kp = a.shape[1]
        row_block = row // kp
        assert row_block * kp == row
        a_specs.append(pl.BlockSpec((tm, kp), lambda j, i: (i, 0)))
        w_specs.append(pl.BlockSpec((None, kp, tn), lambda j, i, row_block=row_block: (layer, row_block, j)))
        scratch.append(pltpu.VMEM((kp, tn), BF16))
        vmem += 2 * _nbytes((tm, kp), a.dtype) + 2 * _nbytes((kp, tn), F32) + 2 * _nbytes((kp, tn), BF16)
        row += kp
    assert row == w.shape[1]
    return pl.pallas_call(
        functools.partial(_matmul_res_kernel, n_a=len(a_list)),
        grid=(n // tn, m // tm),
        in_specs=a_specs + w_specs + [pl.BlockSpec((tm, tn), lambda j, i: (i, j))],
        out_specs=pl.BlockSpec((tm, tn), lambda j, i: (i, j)),
        out_shape=jax.ShapeDtypeStruct((m, n), F32),
        scratch_shapes=scratch,
        compiler_params=_cparams(("parallel", "arbitrary"), vmem),
        name="matmul_res",
    )(*a_list, *([w] * len(a_list)), x)


def _rmsnorm_kernel(x_ref, g_ref, o_ref):
    o_ref[...] = _rms(x_ref[...], g_ref[...]).astype(o_ref.dtype)


def _rmsnorm(x, g, out_dtype, tm):
    m, k = x.shape
    return pl.pallas_call(
        _rmsnorm_kernel,
        grid=(m // tm,),
        in_specs=[pl.BlockSpec((tm, k), lambda i: (i, 0)), pl.BlockSpec((1, k), lambda i: (0, 0))],
        out_specs=pl.BlockSpec((tm, k), lambda i: (i, 0)),
        out_shape=jax.ShapeDtypeStruct((m, k), out_dtype),
        compiler_params=_cparams(("parallel",), 8 * _nbytes((tm, k), F32)),
        name="rmsnorm",
    )(x, g.reshape(1, k))


def _ffn_up_kernel(xn_ref, wg_ref, wv_ref, cwg_ref, cwv_ref, cbg_ref, cbv_ref, o_ref,
                   wgb_ref, wvb_ref, tail_g_ref, tail_v_ref, *u_refs, rows, tiles_per_seq):
    i = pl.program_id(1)

    @pl.when(i == 0)
    def _():
        wgb_ref[...] = wg_ref[...].astype(BF16)
        wvb_ref[...] = wv_ref[...].astype(BF16)

    @pl.when(i % tiles_per_seq == 0)
    def _():
        tail_g_ref[...] = jnp.zeros(tail_g_ref.shape, F32)
        tail_v_ref[...] = jnp.zeros(tail_v_ref.shape, F32)

    n_chunks = len(u_refs) // 2

    for r in range(n_chunks):
        xr = xn_ref[r * rows:(r + 1) * rows, :]

        def conv(wb_ref, cw_ref, cb_ref, u_ref, before):
            u_ref[TAIL:, :] = _dot(xr, wb_ref[...])
            u_ref[0:TAIL, :] = before
            c = cb_ref[...]
            for tap in range(CONV_WIDTH):
                start = TAIL - (CONV_WIDTH - 1) + tap
                c = c + cw_ref[tap:tap + 1, :] * u_ref[start:start + rows, :]
            return c

        if r == 0:
            before_g, before_v = tail_g_ref[...], tail_v_ref[...]
        else:
            before_g, before_v = u_refs[2 * r - 2][rows:, :], u_refs[2 * r - 1][rows:, :]
        gate = conv(wgb_ref, cwg_ref, cbg_ref, u_refs[2 * r], before_g)
        val = conv(wvb_ref, cwv_ref, cbv_ref, u_refs[2 * r + 1], before_v)
        o_ref[r * rows:(r + 1) * rows, :] = (gate * (1.0 / (1.0 + jnp.exp(-gate))) * val).astype(o_ref.dtype)

    tail_g_ref[...] = u_refs[2 * n_chunks - 2][rows:, :]
    tail_v_ref[...] = u_refs[2 * n_chunks - 1][rows:, :]


def _ffn_up(xn, w_up, layer, conv_w, conv_b, seq, tm, tn, rows=256):
    m, k = xn.shape
    d_ff = w_up.shape[-1] // 2
    nj = d_ff // tn
    n_chunks = tm // rows
    vmem = (2 * (_nbytes((tm, k), BF16) + 2 * _nbytes((k, tn), F32) + _nbytes((tm, tn), BF16))
            + 2 * _nbytes((k, tn), BF16) + 2 * n_chunks * _nbytes((rows + TAIL, tn), F32)
            + 2 * _nbytes((k, tn), F32) + 6 * n_chunks * _nbytes((rows, tn), F32))
    return pl.pallas_call(
        functools.partial(_ffn_up_kernel, rows=rows, tiles_per_seq=seq // tm),
        grid=(nj, m // tm),
        in_specs=[pl.BlockSpec((tm, k), lambda j, i: (i, 0)),
                  pl.BlockSpec((None, k, tn), lambda j, i: (layer, 0, j)),
                  pl.BlockSpec((None, k, tn), lambda j, i: (layer, 0, nj + j)),
                  pl.BlockSpec((CONV_WIDTH, tn), lambda j, i: (0, j)),
                  pl.BlockSpec((CONV_WIDTH, tn), lambda j, i: (0, nj + j)),
                  pl.BlockSpec((1, tn), lambda j, i: (0, j)),
                  pl.BlockSpec((1, tn), lambda j, i: (0, nj + j))],
        out_specs=pl.BlockSpec((tm, tn), lambda j, i: (i, j)),
        out_shape=jax.ShapeDtypeStruct((m, d_ff), BF16),
        scratch_shapes=[pltpu.VMEM((k, tn), BF16), pltpu.VMEM((k, tn), BF16),
                        pltpu.VMEM((TAIL, tn), F32), pltpu.VMEM((TAIL, tn), F32)]
        + [pltpu.VMEM((rows + TAIL, tn), F32)] * (2 * n_chunks),
        compiler_params=_cparams(("arbitrary", "arbitrary"), vmem),
        name="ffn_up_conv_gate",
    )(xn, w_up, w_up, conv_w, conv_w, conv_b.reshape(1, -1), conv_b.reshape(1, -1))


def _eye(n):
    return jnp.where(lax.broadcasted_iota(jnp.int32, (n, n), 0) == lax.broadcasted_iota(jnp.int32, (n, n), 1),
                     1.0, 0.0).astype(BF16)


def _transpose_bf16(x):
    return _dot_nt(_eye(x.shape[1]), x)


def _fill_v_transposed(v_ref, vt_ref, t):
    s, dv = v_ref.shape
    for c in range(s // t):
        vt_ref[0:dv, c * t:(c + 1) * t] = _transpose_bf16(v_ref[c * t:(c + 1) * t, :]).astype(BF16)
    if vt_ref.shape[0] > dv:
        vt_ref[dv:, :] = jnp.ones((vt_ref.shape[0] - dv, s), BF16)


def _flash_kernel(*refs, t, dk, dv, group, has_aug):
    if has_aug:
        q_ref, k_ref, v_ref, qa_ref, ka_ref, o_ref, vt_ref, m_ref, acc_ref = refs
    else:
        q_ref, k_ref, v_ref, o_ref, vt_ref, m_ref, acc_ref = refs
    qi = pl.program_id(2)
    heads = range(group)

    @pl.when(qi == 0)
    def _():
        for g in heads:
            _fill_v_transposed(v_ref.at[:, g * dv:(g + 1) * dv], vt_ref.at[g], t)

    qs = []
    for g in heads:
        q = q_ref[:, g * dk:(g + 1) * dk]
        qs.append(jnp.concatenate([q, qa_ref[g]], axis=1) if has_aug else q)
    m_ref[...] = jnp.full(m_ref.shape, -jnp.inf, F32)
    acc_ref[...] = jnp.zeros(acc_ref.shape, F32)

    def scores(g, kb):
        ks = pl.multiple_of(kb * t, t)
        k = k_ref[pl.ds(ks, t), g * dk:(g + 1) * dk]
        if has_aug:
            k = jnp.concatenate([k, ka_ref[g, pl.ds(ks, t), :]], axis=1)
        return _dot_nt(k, qs[g])

    def update(g, s, kb, on_diagonal):
        if on_diagonal:
            key = lax.broadcasted_iota(jnp.int32, (t, t), 0)
            qry = lax.broadcasted_iota(jnp.int32, (t, t), 1)
            s = jnp.where(key <= qry, s, -jnp.inf)
        m_prev = m_ref[g]
        m_new = jnp.maximum(m_prev, jnp.max(s, axis=0, keepdims=True))
        m_ref[g] = m_new
        p = jnp.exp2(s - m_new).astype(BF16)
        ks = pl.multiple_of(kb * t, t)
        acc_ref[g] = jnp.exp2(m_prev - m_new) * acc_ref[g] + _dot(vt_ref[g, :, pl.ds(ks, t)], p)

    def body(kb, ss):
        nxt = tuple(scores(g, kb + 1) for g in heads)
        for g in heads:
            update(g, ss[g], kb, False)
        return nxt

    ss = lax.fori_loop(0, qi, body, tuple(scores(g, 0) for g in heads))
    outs = []
    for g in heads:
        update(g, ss[g], qi, True)
        acc = acc_ref[g]
        outs.append((acc[0:dv, :] / acc[dv:dv + 1, :]).T)
    o_ref[...] = jnp.concatenate(outs, axis=1).astype(o_ref.dtype)


def _flash_attention(q_arr, k_arr, v_arr, q_off, k_off, v_off, dk, dv, n_heads, aug=None, t=512, group=2):
    b, s, _ = q_arr.shape
    assert q_off % group == 0 and k_off % group == 0 and v_off % group == 0 and n_heads % group == 0
    qo, ko, vo = q_off // group, k_off // group, v_off // group
    in_specs = [pl.BlockSpec((None, t, group * dk), lambda bi, h, qi: (bi, qi, qo + h)),
                pl.BlockSpec((None, s, group * dk), lambda bi, h, qi: (bi, 0, ko + h)),
                pl.BlockSpec((None, s, group * dv), lambda bi, h, qi: (bi, 0, vo + h))]
    args = [q_arr, k_arr, v_arr]
    vmem = 2 * group * (_nbytes((t, dk), BF16) + _nbytes((s, dk), BF16) + _nbytes((s, dv), BF16)
                        + _nbytes((t, dv), BF16))
    if aug is not None:
        in_specs += [pl.BlockSpec((None, group, t, LANES), lambda bi, h, qi: (bi, h, qi, 0)),
                     pl.BlockSpec((None, group, s, LANES), lambda bi, h, qi: (bi, h, 0, 0))]
        args += list(aug)
        vmem += 2 * group * (_nbytes((t, LANES), BF16) + _nbytes((s, LANES), BF16))
    acc_rows = dv + 16
    vmem += group * (_nbytes((acc_rows, s), BF16) + 2 * _nbytes((acc_rows, t), F32) + 8 * _nbytes((t, t), F32))
    return pl.pallas_call(
        functools.partial(_flash_kernel, t=t, dk=dk, dv=dv, group=group, has_aug=aug is not None),
        grid=(b, n_heads // group, s // t),
        in_specs=in_specs,
        out_specs=pl.BlockSpec((None, t, group * dv), lambda bi, h, qi: (bi, qi, h)),
        out_shape=jax.ShapeDtypeStruct((b, s, n_heads * dv), BF16),
        scratch_shapes=[pltpu.VMEM((group, acc_rows, s), BF16), pltpu.VMEM((group, 1, t), F32),
                        pltpu.VMEM((group, acc_rows, t), F32)],
        compiler_params=_cparams(("parallel", "parallel", "arbitrary"), vmem),
        name="flash_attention",
    )(*args)


def _sb_kernel(q_ref, k_ref, v_ref, o_ref, vt_ref, carry_ref, acc_ref, *, t, sub):
    qi = pl.program_id(2)

    @pl.when(qi == 0)
    def _():
        _fill_v_transposed(v_ref, vt_ref, t)

    carry_ref[...] = jnp.zeros(carry_ref.shape, F32)
    acc_ref[...] = jnp.zeros(acc_ref.shape, F32)
    q = q_ref[...]
    key = lax.broadcasted_iota(jnp.int32, (sub, t), 0)
    qry = lax.broadcasted_iota(jnp.int32, (sub, t), 1)
    r = lax.broadcasted_iota(jnp.int32, (sub + 16, sub), 0)
    c = lax.broadcasted_iota(jnp.int32, (sub + 16, sub), 1)
    suffix = jnp.where(((c > r) & (r < sub)) | (r == sub), 1.0, 0.0).astype(BF16)
    suffix = jnp.concatenate([suffix, suffix], axis=1)

    def scores(kb):
        ks = pl.multiple_of(kb * t, t)
        return _dot_nt(k_ref[pl.ds(ks, t), :], q)

    def weights(s, on_diagonal):
        later = carry_ref[...]
        out = [None] * (t // sub)
        for i in reversed(range(t // sub)):
            z = s[i * sub:(i + 1) * sub, :]
            neg_abs = lax.bitcast_convert_type(
                lax.bitcast_convert_type(z, jnp.uint32) | jnp.uint32(0x80000000), F32)
            log_beta = jnp.minimum(z, 0.0) - jnp.log2(1.0 + jnp.exp2(neg_abs))
            log_keep = log_beta - z
            if on_diagonal:
                strict = key + i * sub < qry
                log_keep = jnp.where(strict, log_keep, 0.0)
            hi = log_keep.astype(BF16)
            lo = (log_keep - hi.astype(F32)).astype(BF16)
            sums = _dot(suffix, jnp.concatenate([hi, lo], axis=0))
            a = jnp.exp2(log_beta + (sums[0:sub, :] + later))
            if on_diagonal:
                a = jnp.where(strict, a, 0.0)
            out[i] = a.astype(BF16)
            later = later + sums[sub:sub + 1, :]
        carry_ref[...] = later
        return jnp.concatenate(out, axis=0)

    def accumulate(a, kb):
        ks = pl.multiple_of(kb * t, t)
        acc_ref[...] += _dot(vt_ref[:, pl.ds(ks, t)], a)

    def finish():
        o_ref[...] = acc_ref[...].T.astype(o_ref.dtype)

    @pl.when(qi == 0)
    def _():
        accumulate(weights(scores(0), True), 0)
        finish()

    @pl.when(qi > 0)
    def _():
        s_ahead = scores(qi - 1)
        a_diag = weights(scores(qi), True)

        def body(i, carry):
            s, a = carry
            accumulate(a, qi - i)
            s_next = scores(qi - i - 2)
            return s_next, weights(s, False)

        s_last, a = lax.fori_loop(0, qi - 1, body, (s_ahead, a_diag))
        accumulate(a, 1)
        accumulate(weights(s_last, False), 0)
        finish()


def _sb_attention(proj, n_heads, t=512):
    b, s, _ = proj.shape
    d = HEAD_DIM
    vmem = (2 * (2 * _nbytes((t, d), BF16) + 2 * _nbytes((s, d), BF16))
            + _nbytes((d, s), BF16) + 2 * _nbytes((d, t), F32) + 16 * _nbytes((t, t), F32))
    return pl.pallas_call(
        functools.partial(_sb_kernel, t=t, sub=LANES),
        grid=(b, n_heads, s // t),
        in_specs=[pl.BlockSpec((None, t, d), lambda bi, h, qi: (bi, qi, h)),
                  pl.BlockSpec((None, s, d), lambda bi, h, qi: (bi, 0, n_heads + h)),
                  pl.BlockSpec((None, s, d), lambda bi, h, qi: (bi, 0, 2 * n_heads + h))],
        out_specs=pl.BlockSpec((None, t, d), lambda bi, h, qi: (bi, qi, h)),
        out_shape=jax.ShapeDtypeStruct((b, s, n_heads * d), BF16),
        scratch_shapes=[pltpu.VMEM((d, s), BF16), pltpu.VMEM((1, t), F32), pltpu.VMEM((d, t), F32)],
        compiler_params=_cparams(("parallel", "parallel", "arbitrary"), vmem),
        name="stick_breaking_attention",
    )(proj, proj, proj)


def _split3(x):
    hi = x.astype(BF16)
    rest = x - hi.astype(F32)
    mid = rest.astype(BF16)
    lo = (rest - mid.astype(F32)).astype(BF16)
    return hi, mid, lo


def _fox_gate_kernel(fl_ref, b_ref, pq_ref, pk_ref, oq_ref, ok_ref, qa_ref, ka_ref, carry_ref):
    ts = fl_ref.shape[0]

    @pl.when(pl.program_id(1) == 0)
    def _():
        carry_ref[...] = jnp.zeros(carry_ref.shape, F32)

    z = fl_ref[...] + b_ref[...]
    log_f = jnp.minimum(z, 0.0) - jnp.log(1.0 + jnp.exp(-jnp.abs(z)))
    row = lax.broadcasted_iota(jnp.int32, (ts, ts), 0)
    col = lax.broadcasted_iota(jnp.int32, (ts, ts), 1)
    prefix = jnp.where(col <= row, 1.0, 0.0).astype(BF16)
    hi, mid, lo = _split3(log_f)
    c = _dot(prefix, hi) + _dot(prefix, mid) + _dot(prefix, lo) + carry_ref[0:1, :]
    carry_ref[0:1, :] = c[ts - 1:ts, :]
    parts = jnp.concatenate(_split3(c * LOG2E), axis=1)
    qa = (_dot(parts, pq_ref[...]) + oq_ref[...]).astype(BF16)
    ka = (_dot(parts, pk_ref[...]) + ok_ref[...]).astype(BF16)
    for h in range(qa_ref.shape[0]):
        qa_ref[h] = qa[:, h * LANES:(h + 1) * LANES]
        ka_ref[h] = ka[:, h * LANES:(h + 1) * LANES]


def _fox_aug_tables(n_heads):
    part, src = jnp.arange(3 * LANES) // LANES, jnp.arange(3 * LANES) % LANES
    head, lane = jnp.arange(n_heads * LANES) // LANES, jnp.arange(n_heads * LANES) % LANES
    mine = src[:, None] == head[None, :]
    pq = jnp.where(mine & (lane[None, :] == part[:, None] + 3), 1.0, 0.0).astype(BF16)
    pk = jnp.where(mine & (lane[None, :] == part[:, None]), -1.0, 0.0).astype(BF16)
    oq = jnp.where(lane < 3, 1.0, 0.0).astype(F32)[None]
    ok = jnp.where((lane >= 3) & (lane < 6), 1.0, 0.0).astype(F32)[None]
    return pq, pk, oq, ok


def _fox_gates(f_logit, b_f, n_heads, ts=256):
    b, s, w = f_logit.shape
    hw = n_heads * LANES
    out = jax.ShapeDtypeStruct((b, n_heads, s, LANES), BF16)
    const = lambda bi, i: (0, 0)
    return pl.pallas_call(
        _fox_gate_kernel,
        grid=(b, s // ts),
        in_specs=[pl.BlockSpec((None, ts, w), lambda bi, i: (bi, i, 0)),
                  pl.BlockSpec((1, w), const),
                  pl.BlockSpec((3 * LANES, hw), const), pl.BlockSpec((3 * LANES, hw), const),
                  pl.BlockSpec((1, hw), const), pl.BlockSpec((1, hw), const)],
        out_specs=[pl.BlockSpec((None, n_heads, ts, LANES), lambda bi, i: (bi, 0, i, 0)),
                   pl.BlockSpec((None, n_heads, ts, LANES), lambda bi, i: (bi, 0, i, 0))],
        out_shape=[out, out],
        scratch_shapes=[pltpu.VMEM((8, w), F32)],
        compiler_params=_cparams(("parallel", "arbitrary"), 32 * 2**20),
        name="fox_gate_cumsum",
    )(f_logit, b_f.reshape(1, w), *_fox_aug_tables(n_heads))


def _mem_attn_kernel(q_ref, kv_ref, o_ref, *, scale):
    outs = []
    for h in range(MEM_HEADS):
        q = q_ref[:, h * HEAD_DIM:(h + 1) * HEAD_DIM].astype(BF16)
        k = kv_ref[:, h * HEAD_DIM:(h + 1) * HEAD_DIM]
        v = kv_ref[:, MEM_W + h * HEAD_DIM:MEM_W + (h + 1) * HEAD_DIM]
        s = _dot_nt(q, k) * scale
        p = jnp.exp(s - jnp.max(s, axis=1, keepdims=True))
        o = _dot(p.astype(BF16), v) / jnp.sum(p, axis=1, keepdims=True)
        outs.append(o.astype(o_ref.dtype))
    o_ref[...] = jnp.concatenate(outs, axis=1)


def _mem_attention(q_arr, q_block, mem_kv, tq=512):
    b, s, _ = q_arr.shape
    length = mem_kv.shape[1]
    vmem = (2 * (_nbytes((tq, MEM_W), q_arr.dtype) + _nbytes((length, 2 * MEM_W), BF16)
                 + _nbytes((tq, MEM_W), BF16)) + 8 * _nbytes((tq, length), F32))
    return pl.pallas_call(
        functools.partial(_mem_attn_kernel, scale=HEAD_DIM ** -0.5),
        grid=(b, s // tq),
        in_specs=[pl.BlockSpec((None, tq, MEM_W), lambda bi, i: (bi, i, q_block)),
                  pl.BlockSpec((None, length, 2 * MEM_W), lambda bi, i: (bi, 0, 0))],
        out_specs=pl.BlockSpec((None, tq, MEM_W), lambda bi, i: (bi, i, 0)),
        out_shape=jax.ShapeDtypeStruct((b, s, MEM_W), BF16),
        compiler_params=_cparams(("parallel", "parallel"), vmem),
        name="memory_attention",
    )(q_arr, mem_kv)


def _t5_bucket_table():
    max_exact = NUM_BUCKETS // 2
    kj = jnp.arange(2 * WINDOW)[:, None]
    qi = jnp.arange(WINDOW)[None, :]
    signed = WINDOW + qi - kj
    dist = jnp.maximum(signed, 0)
    d = jnp.maximum(dist, 1).astype(F32)
    large = max_exact + (jnp.log(d / max_exact) / math.log(MAX_DISTANCE / max_exact)
                         * (NUM_BUCKETS - max_exact)).astype(jnp.int32)
    bucket = jnp.where(dist < max_exact, dist, jnp.minimum(large, NUM_BUCKETS - 1))
    return jnp.where((signed >= 0) & (signed < WINDOW), bucket, -1).astype(jnp.int32)


def _t5_bias_kernel(rb_ref, bucket_ref, o_ref):
    h = pl.program_id(0)
    bucket = bucket_ref[...]
    bias = jnp.full(bucket.shape, -jnp.inf, F32)
    for b in range(NUM_BUCKETS):
        bias = jnp.where(bucket == b, rb_ref[b, h] * LOG2E, bias)
    o_ref[...] = bias


def _t5_bias(rel_bias):
    n_heads = rel_bias.shape[1]
    return pl.pallas_call(
        _t5_bias_kernel,
        grid=(n_heads,),
        in_specs=[pl.BlockSpec(memory_space=pltpu.SMEM),
                  pl.BlockSpec((2 * WINDOW, WINDOW), lambda h: (0, 0))],
        out_specs=pl.BlockSpec((None, 2 * WINDOW, WINDOW), lambda h: (h, 0, 0)),
        out_shape=jax.ShapeDtypeStruct((n_heads, 2 * WINDOW, WINDOW), F32),
        compiler_params=_cparams(("parallel",), 16 * 2**20),
        name="t5_bias",
    )(rel_bias, _t5_bucket_table())


def _swa_kernel(sink_ref, q_ref, kvc_ref, kvp_ref, bias_ref, o_ref):
    n = pl.program_id(1)
    d = SWA_HEAD_DIM
    kv_w = N_SWA_KV_HEADS * d
    k_win = jnp.concatenate([kvp_ref[:, 0:kv_w], kvc_ref[:, 0:kv_w]], axis=0)
    v_win = jnp.concatenate([kvp_ref[:, kv_w:2 * kv_w], kvc_ref[:, kv_w:2 * kv_w]], axis=0)
    v_t = v_win.astype(F32).T.astype(BF16)
    no_prev = jnp.where(n == 0, -jnp.inf, 0.0)
    outs = []
    for kvh in range(N_SWA_KV_HEADS):
        k = k_win[:, kvh * d:(kvh + 1) * d]
        v = v_t[kvh * d:(kvh + 1) * d, :]
        for g in range(SWA_GROUP):
            h = kvh * SWA_GROUP + g
            s = _dot_nt(k, q_ref[:, h * d:(h + 1) * d]) + bias_ref[h]
            s = jnp.concatenate([s[0:WINDOW, :] + no_prev, s[WINDOW:, :]], axis=0)
            sink = sink_ref[h] * LOG2E
            m = jnp.maximum(jnp.max(s, axis=0, keepdims=True), sink)
            p = jnp.exp2(s - m)
            denom = jnp.sum(p, axis=0, keepdims=True) + jnp.exp2(sink - m)
            outs.append(_dot(v, p.astype(BF16)) / denom)
    o_ref[...] = jnp.concatenate(outs, axis=0).T.astype(o_ref.dtype)


def _swa_attention(proj, sinks, bias):
    b, s, _ = proj.shape
    q_w = N_SWA_HEADS * SWA_HEAD_DIM
    kv_w = 2 * N_SWA_KV_HEADS * SWA_HEAD_DIM
    kv_block = q_w // kv_w
    vmem = (2 * (2 * _nbytes((WINDOW, q_w), BF16) + 2 * _nbytes((WINDOW, kv_w), BF16)
                 + _nbytes(bias.shape, F32)) + 16 * 2**20)
    return pl.pallas_call(
        _swa_kernel,
        grid=(b, s // WINDOW),
        in_specs=[pl.BlockSpec(memory_space=pltpu.SMEM),
                  pl.BlockSpec((None, WINDOW, q_w), lambda bi, n: (bi, n, 0)),
                  pl.BlockSpec((None, WINDOW, kv_w), lambda bi, n: (bi, n, kv_block)),
                  pl.BlockSpec((None, WINDOW, kv_w), lambda bi, n: (bi, jnp.maximum(n - 1, 0), kv_block)),
                  pl.BlockSpec(bias.shape, lambda bi, n: (0, 0, 0))],
        out_specs=pl.BlockSpec((None, WINDOW, q_w), lambda bi, n: (bi, n, 0)),
        out_shape=jax.ShapeDtypeStruct((b, s, q_w), BF16),
        compiler_params=_cparams(("parallel", "arbitrary"), vmem),
        name="sliding_window_attention",
    )(sinks, proj, proj, proj, bias)


def _rope_table_kernel(pos_ref, invf_ref, o_ref):
    ang = pos_ref[...].astype(F32) * invf_ref[...]
    lane = lax.broadcasted_iota(jnp.int32, ang.shape, 1)
    half = QK_ROPE // 2
    sin_signed = jnp.where(lane < QK_ROPE + half, -jnp.sin(ang), jnp.sin(ang))
    o_ref[...] = jnp.where(lane < QK_ROPE, jnp.cos(ang), sin_signed)


def _rope_table(positions, tm=1024):
    m = positions.shape[0]
    half = QK_ROPE // 2
    inv_freq = ROPE_THETA ** (-jnp.arange(half, dtype=F32) / half)
    invf = jnp.tile(inv_freq, LANES // half).reshape(1, LANES)
    return pl.pallas_call(
        _rope_table_kernel,
        grid=(m // tm,),
        in_specs=[pl.BlockSpec((tm, 1), lambda i: (i, 0)), pl.BlockSpec((1, LANES), lambda i: (0, 0))],
        out_specs=pl.BlockSpec((tm, LANES), lambda i: (i, 0)),
        out_shape=jax.ShapeDtypeStruct((m, LANES), F32),
        compiler_params=_cparams(("parallel",), 32 * 2**20),
        name="rope_table",
    )(positions, invf)


def _apply_rope(x_and_partner, cs):
    z = x_and_partner * cs
    z = z + pltpu.roll(z, QK_ROPE, axis=1)
    lane = lax.broadcasted_iota(jnp.int32, z.shape, 1)
    return jnp.where(lane < QK_ROPE, z, 0.0)


def _mla_q_kernel(cq_ref, g_ref, w_ref, cs_ref, o_ref, cn_ref):
    @pl.when(pl.program_id(1) == 0)
    def _():
        cn_ref[...] = _rms(cq_ref[...], g_ref[...]).astype(BF16)

    y = _dot(cn_ref[...], w_ref[...]) * ((QK_NOPE + QK_ROPE) ** -0.5 * LOG2E)
    cs = cs_ref[...]
    parts = []
    for h in range(y.shape[1] // (2 * LANES)):
        yh = y[:, h * 2 * LANES:(h + 1) * 2 * LANES]
        parts += [yh[:, :QK_NOPE], _apply_rope(yh[:, QK_NOPE:], cs)]
    o_ref[...] = jnp.concatenate(parts, axis=1).astype(o_ref.dtype)


def _mla_q(proj, q_norm, w_uq_aug, cs, tm=512, heads_per_step=4):
    m = proj.shape[0]
    hw = 2 * LANES * heads_per_step
    n_heads = w_uq_aug.shape[1] // hw
    return pl.pallas_call(
        _mla_q_kernel,
        grid=(m // tm, n_heads),
        in_specs=[pl.BlockSpec((tm, Q_LORA), lambda i, h: (i, 1)),
                  pl.BlockSpec((1, Q_LORA), lambda i, h: (0, 0)),
                  pl.BlockSpec((Q_LORA, hw), lambda i, h: (0, h)),
                  pl.BlockSpec((tm, LANES), lambda i, h: (i, 0))],
        out_specs=pl.BlockSpec((tm, hw), lambda i, h: (i, h)),
        out_shape=jax.ShapeDtypeStruct((m, n_heads * hw), BF16),
        scratch_shapes=[pltpu.VMEM((tm, Q_LORA), BF16)],
        compiler_params=_cparams(("parallel", "arbitrary"), 32 * 2**20),
        name="mla_q_expand",
    )(proj, q_norm.reshape(1, Q_LORA), w_uq_aug, cs)


def _mla_kv_kernel(ckv_ref, g_ref, w_ref, kr_ref, cs_ref, k_ref, v_ref, cn_ref, kpe_ref):
    @pl.when(pl.program_id(1) == 0)
    def _():
        cn_ref[...] = _rms(ckv_ref[...], g_ref[...]).astype(BF16)
        kpe_ref[...] = _apply_rope(kr_ref[...], cs_ref[...])

    y = _dot(cn_ref[...], w_ref[...])
    kpe = kpe_ref[...]
    k_parts, v_parts = [], []
    for h in range(y.shape[1] // (2 * LANES)):
        yh = y[:, h * 2 * LANES:(h + 1) * 2 * LANES]
        k_parts += [yh[:, :QK_NOPE], kpe]
        v_parts.append(yh[:, QK_NOPE:])
    k_ref[...] = jnp.concatenate(k_parts, axis=1).astype(k_ref.dtype)
    v_ref[...] = jnp.concatenate(v_parts, axis=1).astype(v_ref.dtype)


def _mla_kv(proj, kv_norm, w_ukv, cs, tm=512, heads_per_step=4):
    m = proj.shape[0]
    hw = 2 * LANES * heads_per_step
    n_heads = w_ukv.shape[1] // hw
    return pl.pallas_call(
        _mla_kv_kernel,
        grid=(m // tm, n_heads),
        in_specs=[pl.BlockSpec((tm, KV_LORA), lambda i, h: (i, 4)),
                  pl.BlockSpec((1, KV_LORA), lambda i, h: (0, 0)),
                  pl.BlockSpec((KV_LORA, hw), lambda i, h: (0, h)),
                  pl.BlockSpec((tm, LANES), lambda i, h: (i, 10)),
                  pl.BlockSpec((tm, LANES), lambda i, h: (i, 0))],
        out_specs=[pl.BlockSpec((tm, hw), lambda i, h: (i, h)),
                   pl.BlockSpec((tm, hw // 2), lambda i, h: (i, h))],
        out_shape=[jax.ShapeDtypeStruct((m, n_heads * hw), BF16),
                   jax.ShapeDtypeStruct((m, n_heads * hw // 2), BF16)],
        scratch_shapes=[pltpu.VMEM((tm, KV_LORA), BF16), pltpu.VMEM((tm, LANES), F32)],
        compiler_params=_cparams(("parallel", "arbitrary"), 32 * 2**20),
        name="mla_kv_expand",
    )(proj, kv_norm.reshape(1, KV_LORA), w_ukv, proj, cs)


def _swap_halves(w):
    half = w.shape[-1] // 2
    return jnp.concatenate([w[..., half:], w[..., :half]], axis=-1)


def _mixer_out(x2, o, o_mem, w_out, b, s):
    return _matmul_res([o.reshape(b * s, o.shape[-1]), o_mem.reshape(b * s, MEM_W)],
                       w_out[0], w_out[1], x2, tm=512, tn=512)


def _sb_layer(x2, mem_kv, g, w_in, w_out, b, s):
    proj = _norm_matmul(x2, g, w_in.astype(BF16), BF16, tm=IN_PROJ_ROWS, tn=512, scaled_cols=N_HEADS * HEAD_DIM,
                        col_scale=HEAD_DIM ** -0.5 * LOG2E).reshape(b, s, -1)
    o = _sb_attention(proj, N_HEADS)
    o_mem = _mem_attention(proj, 3 * N_HEADS * HEAD_DIM // MEM_W, mem_kv)
    return _mixer_out(x2, o, o_mem, w_out, b, s)


def _fox_layer(x2, mem_kv, g, w_in, b_f, w_out, b, s):
    qkv_w = 3 * N_HEADS * HEAD_DIM
    w_main = jnp.concatenate([w_in[:, :qkv_w], w_in[:, qkv_w + N_HEADS:]], axis=1).astype(BF16)
    w_gate = jnp.pad(w_in[:, qkv_w:qkv_w + N_HEADS], ((0, 0), (0, LANES - N_HEADS))).astype(BF16)
    proj = _norm_matmul(x2, g, w_main, BF16, tm=IN_PROJ_ROWS, tn=512, scaled_cols=N_HEADS * HEAD_DIM,
                        col_scale=HEAD_DIM ** -0.5 * LOG2E).reshape(b, s, -1)
    f_logit = _norm_matmul(x2, g, w_gate, F32, tm=512, tn=LANES).reshape(b, s, LANES)
    aug = _fox_gates(f_logit, jnp.pad(b_f, (0, LANES - N_HEADS)), N_HEADS)
    o = _flash_attention(proj, proj, proj, 0, N_HEADS, 2 * N_HEADS, HEAD_DIM, HEAD_DIM, N_HEADS, aug=aug)
    o_mem = _mem_attention(proj, qkv_w // MEM_W, mem_kv)
    return _mixer_out(x2, o, o_mem, w_out, b, s)


def _swa_layer(x2, mem_kv, g, rel_bias, w_in, sinks, w_out, b, s):
    proj = _norm_matmul(x2, g, w_in.astype(BF16), BF16, tm=IN_PROJ_ROWS, tn=512,
                        scaled_cols=N_SWA_HEADS * SWA_HEAD_DIM,
                        col_scale=SWA_HEAD_DIM ** -0.5 * LOG2E).reshape(b, s, -1)
    o = _swa_attention(proj, sinks, _t5_bias(rel_bias))
    q_w = N_SWA_HEADS * SWA_HEAD_DIM
    kv_w = 2 * N_SWA_KV_HEADS * SWA_HEAD_DIM
    o_mem = _mem_attention(proj, (q_w + kv_w) // MEM_W, mem_kv)
    return _mixer_out(x2, o, o_mem, w_out, b, s)


def _mla_layer(x2, mem_kv, g, positions, w_in, q_norm, w_uq, kv_norm, w_ukv, w_out, b, s):
    d = x2.shape[1]
    o1, o2, o3 = Q_LORA, Q_LORA + KV_LORA, Q_LORA + KV_LORA + QK_ROPE
    w_main = jnp.concatenate([w_in[:, o3:], w_in[:, :o3], _swap_halves(w_in[:, o2:o3])], axis=1).astype(BF16)
    proj = _norm_matmul(x2, g, w_main, F32, tm=512, tn=w_main.shape[1])
    cs = _rope_table(positions.reshape(b * s, 1))
    wq = w_uq.reshape(Q_LORA, N_HEADS, QK_NOPE + QK_ROPE)
    wq = jnp.concatenate([wq, _swap_halves(wq[:, :, QK_NOPE:])], axis=-1).reshape(Q_LORA, -1).astype(BF16)
    q = _mla_q(proj, q_norm, wq, cs).reshape(b, s, -1)
    k, v = _mla_kv(proj, kv_norm, w_ukv.astype(BF16), cs)
    k, v = k.reshape(b, s, -1), v.reshape(b, s, -1)
    o = _flash_attention(q, k, v, 0, 0, 0, 2 * LANES, HEAD_DIM, N_HEADS)
    o_mem = _mem_attention(proj.reshape(b, s, -1), 0, mem_kv)
    return _mixer_out(x2, o, o_mem, w_out, b, s)


def kernel(x, mem, positions, rel_bias, attn_norm, mem_norm, w_mem_kv, ffn_norm, ffn_w_up, ffn_conv_w, ffn_conv_b, ffn_w_down, final_norm, sb_w_in, sb_w_out, fox_w_in, fox_b_f, fox_w_out, swa_w_in, swa_sinks, swa_w_out, mla_w_in, mla_q_norm, mla_w_uq, mla_kv_norm, mla_w_ukv, mla_w_out):
    b, s, d = x.shape
    depth = attn_norm.shape[0]
    mem_len = mem.shape[1]
    x2 = x.reshape(b * s, d)
    mem2 = mem.reshape(b * mem_len, d)
    w_mem_kv_bf16 = w_mem_kv.astype(BF16)
    for i in range(depth):
        kind, j = i % 4, i // 4
        mem_kv = _norm_matmul(mem2, mem_norm[i], w_mem_kv_bf16, BF16, tm=b * mem_len, tn=2 * MEM_W,
                              layer=i).reshape(b, mem_len, 2 * MEM_W)
        g = attn_norm[i]
        if kind == 0:
            x2 = _sb_layer(x2, mem_kv, g, sb_w_in[j], (sb_w_out, j), b, s)
        elif kind == 1:
            x2 = _fox_layer(x2, mem_kv, g, fox_w_in[j], fox_b_f[j], (fox_w_out, j), b, s)
        elif kind == 2:
            x2 = _swa_layer(x2, mem_kv, g, rel_bias, swa_w_in[j], swa_sinks[j], (swa_w_out, j), b, s)
        else:
            x2 = _mla_layer(x2, mem_kv, g, positions, mla_w_in[j], mla_q_norm[j], mla_w_uq[j],
                            mla_kv_norm[j], mla_w_ukv[j], (mla_w_out, j), b, s)
        gated = _ffn_up(_rmsnorm(x2, ffn_norm[i], BF16, tm=512), ffn_w_up, i, ffn_conv_w[i], ffn_conv_b[i],
                        seq=s, tm=1024, tn=512)
        x2 = _matmul_res([gated], ffn_w_down, i, x2, tm=512, tn=512)
    return _rmsnorm(x2, final_norm, F32, tm=512).reshape(b, s, d)
```

```python
import functools
import math

import jax
import jax.numpy as jnp
from jax import lax
from jax.experimental import pallas as pl
from jax.experimental.pallas import tpu as pltpu

F32 = jnp.float32
BF16 = jnp.bfloat16

EPS = 1e-6
LOG2E = math.log2(math.e)
HEAD_DIM = 128
N_HEADS = 16
SWA_HEAD_DIM = 64
N_SWA_HEADS = 32
N_SWA_KV_HEADS = 4
SWA_GROUP = N_SWA_HEADS // N_SWA_KV_HEADS
WINDOW = 128
Q_LORA = 512
KV_LORA = 256
QK_NOPE = 128
QK_ROPE = 64
ROPE_THETA = 10000.0
MEM_HEADS = 4
MEM_W = MEM_HEADS * HEAD_DIM
NUM_BUCKETS = 32
MAX_DISTANCE = 128
CONV_WIDTH = 3

V7X_VMEM_BYTES = 64 * 1024 * 1024
VMEM_CAP = V7X_VMEM_BYTES - 8 * 1024 * 1024
LANES = 128
IN_PROJ_ROWS = 1024
TAIL = 8


def _cparams(sems, vmem_bytes):
    return pltpu.CompilerParams(dimension_semantics=sems,
                                vmem_limit_bytes=int(min(max(vmem_bytes, 16 * 2**20), VMEM_CAP)))


def _nbytes(shape, dtype):
    return math.prod(shape) * jnp.dtype(dtype).itemsize


def _rms(x, g):
    return x * lax.rsqrt(jnp.mean(x * x, axis=-1, keepdims=True) + EPS) * g


def _dot(a, b):
    return jnp.dot(a, b, preferred_element_type=F32)


def _dot_nt(a, b):
    return lax.dot_general(a, b, (((1,), (1,)), ((), ())), preferred_element_type=F32)


def _norm_matmul_kernel(x_ref, g_ref, w_ref, cs_ref, o_ref, xn_ref):
    @pl.when(pl.program_id(1) == 0)
    def _():
        xn_ref[...] = _rms(x_ref[...], g_ref[...]).astype(BF16)

    o_ref[...] = (_dot(xn_ref[...], w_ref[...]) * cs_ref[...]).astype(o_ref.dtype)


def _w_tile_spec(w, layer, k, tn, col_block):
    if w.ndim == 3:
        return pl.BlockSpec((None, k, tn), lambda i, j: (layer, 0, col_block(j)))
    return pl.BlockSpec((k, tn), lambda i, j: (0, col_block(j)))


def _norm_matmul(x, g, w, out_dtype, tm, tn, scaled_cols=0, col_scale=1.0, layer=None):
    m, k = x.shape
    n = w.shape[-1]
    cs = jnp.where(jnp.arange(n) < scaled_cols, col_scale, 1.0).astype(F32).reshape(1, n)
    vmem = (2 * (_nbytes((tm, k), F32) + _nbytes((k, tn), BF16) + _nbytes((tm, tn), out_dtype))
            + _nbytes((tm, k), BF16) + 2 * _nbytes((tm, k), F32) + _nbytes((tm, tn), F32))
    return pl.pallas_call(
        _norm_matmul_kernel,
        grid=(m // tm, n // tn),
        in_specs=[pl.BlockSpec((tm, k), lambda i, j: (i, 0)),
                  pl.BlockSpec((1, k), lambda i, j: (0, 0)),
                  _w_tile_spec(w, layer, k, tn, lambda j: j),
                  pl.BlockSpec((1, tn), lambda i, j: (0, j))],
        out_specs=pl.BlockSpec((tm, tn), lambda i, j: (i, j)),
        out_shape=jax.ShapeDtypeStruct((m, n), out_dtype),
        scratch_shapes=[pltpu.VMEM((tm, k), BF16)],
        compiler_params=_cparams(("parallel", "arbitrary"), vmem),
        name="norm_matmul",
    )(x, g.reshape(1, k), w, cs)


def _matmul_res_kernel(*refs, n_a):
    a_refs, w_refs = refs[:n_a], refs[n_a:2 * n_a]
    x_ref, o_ref = refs[2 * n_a], refs[2 * n_a + 1]
    wb_refs = refs[2 * n_a + 2:]

    @pl.when(pl.program_id(1) == 0)
    def _():
        for w_ref, wb_ref in zip(w_refs, wb_refs):
            wb_ref[...] = w_ref[...].astype(BF16)

    acc = x_ref[...]
    for a_ref, wb_ref in zip(a_refs, wb_refs):
        acc = acc + _dot(a_ref[...], wb_ref[...])
    o_ref[...] = acc


def _matmul_res(a_list, w, layer, x, tm, tn):
    m, n = x.shape
    vmem = 6 * _nbytes((tm, tn), F32)
    a_specs, w_specs, scratch = [], [], []
    row = 0
    for a in a_list:
        kp = a.shape[1]
        row_block = row // kp
        assert row_block * kp == row
        a_specs.append(pl.BlockSpec((tm, kp), lambda j, i: (i, 0)))
        w_specs.append(pl.BlockSpec((None, kp, tn), lambda j, i, row_block=row_block: (layer, row_block, j)))
        scratch.append(pltpu.VMEM((kp, tn), BF16))
        vmem += 2 * _nbytes((tm, kp), a.dtype) + 2 * _nbytes((kp, tn), F32) + 2 * _nbytes((kp, tn), BF16)
        row += kp
    assert row == w.shape[1]
    return pl.pallas_call(
        functools.partial(_matmul_res_kernel, n_a=len(a_list)),
        grid=(n // tn, m // tm),
        in_specs=a_specs + w_specs + [pl.BlockSpec((tm, tn), lambda j, i: (i, j))],
        out_specs=pl.BlockSpec((tm, tn), lambda j, i: (i, j)),
        out_shape=jax.ShapeDtypeStruct((m, n), F32),
        scratch_shapes=scratch,
        compiler_params=_cparams(("parallel", "arbitrary"), vmem),
        name="matmul_res",
    )(*a_list, *([w] * len(a_list)), x)


def _rmsnorm_kernel(x_ref, g_ref, o_ref):
    o_ref[...] = _rms(x_ref[...], g_ref[...]).astype(o_ref.dtype)


def _rmsnorm(x, g, out_dtype, tm):
    m, k = x.shape
    return pl.pallas_call(
        _rmsnorm_kernel,
        grid=(m // tm,),
        in_specs=[pl.BlockSpec((tm, k), lambda i: (i, 0)), pl.BlockSpec((1, k), lambda i: (0, 0))],
        out_specs=pl.BlockSpec((tm, k), lambda i: (i, 0)),
        out_shape=jax.ShapeDtypeStruct((m, k), out_dtype),
        compiler_params=_cparams(("parallel",), 8 * _nbytes((tm, k), F32)),
        name="rmsnorm",
    )(x, g.reshape(1, k))


def _ffn_up_kernel(xn_ref, wg_ref, wv_ref, cwg_ref, cwv_ref, cbg_ref, cbv_ref, o_ref,
                   wgb_ref, wvb_ref, tail_g_ref, tail_v_ref, *u_refs, rows, tiles_per_seq):
    i = pl.program_id(1)

    @pl.when(i == 0)
    def _():
        wgb_ref[...] = wg_ref[...].astype(BF16)
        wvb_ref[...] = wv_ref[...].astype(BF16)

    @pl.when(i % tiles_per_seq == 0)
    def _():
        tail_g_ref[...] = jnp.zeros(tail_g_ref.shape, F32)
        tail_v_ref[...] = jnp.zeros(tail_v_ref.shape, F32)

    n_chunks = len(u_refs) // 2

    for r in range(n_chunks):
        xr = xn_ref[r * rows:(r + 1) * rows, :]

        def conv(wb_ref, cw_ref, cb_ref, u_ref, before):
            u_ref[TAIL:, :] = _dot(xr, wb_ref[...])
            u_ref[0:TAIL, :] = before
            c = cb_ref[...]
            for tap in range(CONV_WIDTH):
                start = TAIL - (CONV_WIDTH - 1) + tap
                c = c + cw_ref[tap:tap + 1, :] * u_ref[start:start + rows, :]
            return c

        if r == 0:
            before_g, before_v = tail_g_ref[...], tail_v_ref[...]
        else:
            before_g, before_v = u_refs[2 * r - 2][rows:, :], u_refs[2 * r - 1][rows:, :]
        gate = conv(wgb_ref, cwg_ref, cbg_ref, u_refs[2 * r], before_g)
        val = conv(wvb_ref, cwv_ref, cbv_ref, u_refs[2 * r + 1], before_v)
        o_ref[r * rows:(r + 1) * rows, :] = (gate * (1.0 / (1.0 + jnp.exp(-gate))) * val).astype(o_ref.dtype)

    tail_g_ref[...] = u_refs[2 * n_chunks - 2][rows:, :]
    tail_v_ref[...] = u_refs[2 * n_chunks - 1][rows:, :]


def _ffn_up(xn, w_up, layer, conv_w, conv_b, seq, tm, tn, rows=128):
    m, k = xn.shape
    d_ff = w_up.shape[-1] // 2
    nj = d_ff // tn
    n_chunks = tm // rows
    vmem = (2 * (_nbytes((tm, k), BF16) + 2 * _nbytes((k, tn), F32) + _nbytes((tm, tn), BF16))
            + 2 * _nbytes((k, tn), BF16) + 2 * n_chunks * _nbytes((rows + TAIL, tn), F32)
            + 2 * _nbytes((k, tn), F32) + 6 * n_chunks * _nbytes((rows, tn), F32))
    return pl.pallas_call(
        functools.partial(_ffn_up_kernel, rows=rows, tiles_per_seq=seq // tm),
        grid=(nj, m // tm),
        in_specs=[pl.BlockSpec((tm, k), lambda j, i: (i, 0)),
                  pl.BlockSpec((None, k, tn), lambda j, i: (layer, 0, j)),
                  pl.BlockSpec((None, k, tn), lambda j, i: (layer, 0, nj + j)),
                  pl.BlockSpec((CONV_WIDTH, tn), lambda j, i: (0, j)),
                  pl.BlockSpec((CONV_WIDTH, tn), lambda j, i: (0, nj + j)),
                  pl.BlockSpec((1, tn), lambda j, i: (0, j)),
                  pl.BlockSpec((1, tn), lambda j, i: (0, nj + j))],
        out_specs=pl.BlockSpec((tm, tn), lambda j, i: (i, j)),
        out_shape=jax.ShapeDtypeStruct((m, d_ff), BF16),
        scratch_shapes=[pltpu.VMEM((k, tn), BF16), pltpu.VMEM((k, tn), BF16),
                        pltpu.VMEM((TAIL, tn), F32), pltpu.VMEM((TAIL, tn), F32)]
        + [pltpu.VMEM((rows + TAIL, tn), F32)] * (2 * n_chunks),
        compiler_params=_cparams(("arbitrary", "arbitrary"), vmem),
        name="ffn_up_conv_gate",
    )(xn, w_up, w_up, conv_w, conv_w, conv_b.reshape(1, -1), conv_b.reshape(1, -1))


def _eye(n):
    return jnp.where(lax.broadcasted_iota(jnp.int32, (n, n), 0) == lax.broadcasted_iota(jnp.int32, (n, n), 1),
                     1.0, 0.0).astype(BF16)


def _transpose_bf16(x):
    return _dot_nt(_eye(x.shape[1]), x)


def _fill_v_transposed(v_ref, vt_ref, t):
    s, dv = v_ref.shape
    for c in range(s // t):
        vt_ref[0:dv, c * t:(c + 1) * t] = _transpose_bf16(v_ref[c * t:(c + 1) * t, :]).astype(BF16)
    if vt_ref.shape[0] > dv:
        vt_ref[dv:, :] = jnp.ones((vt_ref.shape[0] - dv, s), BF16)


def _flash_kernel(*refs, t, dk, dv, group, has_aug):
    if has_aug:
        q_ref, k_ref, v_ref, qa_ref, ka_ref, o_ref, vt_ref, m_ref, acc_ref = refs
    else:
        q_ref, k_ref, v_ref, o_ref, vt_ref, m_ref, acc_ref = refs
    qi = pl.program_id(2)
    heads = range(group)

    @pl.when(qi == 0)
    def _():
        for g in heads:
            _fill_v_transposed(v_ref.at[:, g * dv:(g + 1) * dv], vt_ref.at[g], t)

    qs = []
    for g in heads:
        q = q_ref[:, g * dk:(g + 1) * dk]
        qs.append(jnp.concatenate([q, qa_ref[g]], axis=1) if has_aug else q)
    m_ref[...] = jnp.full(m_ref.shape, -jnp.inf, F32)
    acc_ref[...] = jnp.zeros(acc_ref.shape, F32)

    def scores(g, kb):
        ks = pl.multiple_of(kb * t, t)
        k = k_ref[pl.ds(ks, t), g * dk:(g + 1) * dk]
        if has_aug:
            k = jnp.concatenate([k, ka_ref[g, pl.ds(ks, t), :]], axis=1)
        return _dot_nt(k, qs[g])

    def update(g, s, kb, on_diagonal):
        if on_diagonal:
            key = lax.broadcasted_iota(jnp.int32, (t, t), 0)
            qry = lax.broadcasted_iota(jnp.int32, (t, t), 1)
            s = jnp.where(key <= qry, s, -jnp.inf)
        m_prev = m_ref[g]
        m_new = jnp.maximum(m_prev, jnp.max(s, axis=0, keepdims=True))
        m_ref[g] = m_new
        p = jnp.exp2(s - m_new).astype(BF16)
        ks = pl.multiple_of(kb * t, t)
        acc_ref[g] = jnp.exp2(m_prev - m_new) * acc_ref[g] + _dot(vt_ref[g, :, pl.ds(ks, t)], p)

    def body(kb, ss):
        nxt = tuple(scores(g, kb + 1) for g in heads)
        for g in heads:
            update(g, ss[g], kb, False)
        return nxt

    ss = lax.fori_loop(0, qi, body, tuple(scores(g, 0) for g in heads))
    outs = []
    for g in heads:
        update(g, ss[g], qi, True)
        acc = acc_ref[g]
        outs.append((acc[0:dv, :] / acc[dv:dv + 1, :]).T)
    o_ref[...] = jnp.concatenate(outs, axis=1).astype(o_ref.dtype)


def _flash_attention(q_arr, k_arr, v_arr, q_off, k_off, v_off, dk, dv, n_heads, aug=None, t=512, group=2):
    b, s, _ = q_arr.shape
    assert q_off % group == 0 and k_off % group == 0 and v_off % group == 0 and n_heads % group == 0
    qo, ko, vo = q_off // group, k_off // group, v_off // group
    in_specs = [pl.BlockSpec((None, t, group * dk), lambda bi, h, qi: (bi, qi, qo + h)),
                pl.BlockSpec((None, s, group * dk), lambda bi, h, qi: (bi, 0, ko + h)),
                pl.BlockSpec((None, s, group * dv), lambda bi, h, qi: (bi, 0, vo + h))]
    args = [q_arr, k_arr, v_arr]
    vmem = 2 * group * (_nbytes((t, dk), BF16) + _nbytes((s, dk), BF16) + _nbytes((s, dv), BF16)
                        + _nbytes((t, dv), BF16))
    if aug is not None:
        in_specs += [pl.BlockSpec((None, group, t, LANES), lambda bi, h, qi: (bi, h, qi, 0)),
                     pl.BlockSpec((None, group, s, LANES), lambda bi, h, qi: (bi, h, 0, 0))]
        args += list(aug)
        vmem += 2 * group * (_nbytes((t, LANES), BF16) + _nbytes((s, LANES), BF16))
    acc_rows = dv + 16
    vmem += group * (_nbytes((acc_rows, s), BF16) + 2 * _nbytes((acc_rows, t), F32) + 8 * _nbytes((t, t), F32))
    return pl.pallas_call(
        functools.partial(_flash_kernel, t=t, dk=dk, dv=dv, group=group, has_aug=aug is not None),
        grid=(b, n_heads // group, s // t),
        in_specs=in_specs,
        out_specs=pl.BlockSpec((None, t, group * dv), lambda bi, h, qi: (bi, qi, h)),
        out_shape=jax.ShapeDtypeStruct((b, s, n_heads * dv), BF16),
        scratch_shapes=[pltpu.VMEM((group, acc_rows, s), BF16), pltpu.VMEM((group, 1, t), F32),
                        pltpu.VMEM((group, acc_rows, t), F32)],
        compiler_params=_cparams(("parallel", "parallel", "arbitrary"), vmem),
        name="flash_attention",
    )(*args)


def _sb_kernel(q_ref, k_ref, v_ref, o_ref, vt_ref, carry_ref, acc_ref, *, t, sub):
    qi = pl.program_id(2)

    @pl.when(qi == 0)
    def _():
        _fill_v_transposed(v_ref, vt_ref, t)

    carry_ref[...] = jnp.zeros(carry_ref.shape, F32)
    acc_ref[...] = jnp.zeros(acc_ref.shape, F32)
    q = q_ref[...]
    key = lax.broadcasted_iota(jnp.int32, (sub, t), 0)
    qry = lax.broadcasted_iota(jnp.int32, (sub, t), 1)
    r = lax.broadcasted_iota(jnp.int32, (sub + 16, sub), 0)
    c = lax.broadcasted_iota(jnp.int32, (sub + 16, sub), 1)
    suffix = jnp.where(((c > r) & (r < sub)) | (r == sub), 1.0, 0.0).astype(BF16)
    suffix = jnp.concatenate([suffix, suffix], axis=1)

    def scores(kb):
        ks = pl.multiple_of(kb * t, t)
        return _dot_nt(k_ref[pl.ds(ks, t), :], q)

    def weights(s, on_diagonal):
        later = carry_ref[...]
        out = [None] * (t // sub)
        for i in reversed(range(t // sub)):
            z = s[i * sub:(i + 1) * sub, :]
            neg_abs = lax.bitcast_convert_type(
                lax.bitcast_convert_type(z, jnp.uint32) | jnp.uint32(0x80000000), F32)
            log_beta = jnp.minimum(z, 0.0) - jnp.log2(1.0 + jnp.exp2(neg_abs))
            log_keep = log_beta - z
            if on_diagonal:
                strict = key + i * sub < qry
                log_keep = jnp.where(strict, log_keep, 0.0)
            hi = log_keep.astype(BF16)
            lo = (log_keep - hi.astype(F32)).astype(BF16)
            sums = _dot(suffix, jnp.concatenate([hi, lo], axis=0))
            a = jnp.exp2(log_beta + (sums[0:sub, :] + later))
            if on_diagonal:
                a = jnp.where(strict, a, 0.0)
            out[i] = a.astype(BF16)
            later = later + sums[sub:sub + 1, :]
        carry_ref[...] = later
        return jnp.concatenate(out, axis=0)

    def accumulate(a, kb):
        ks = pl.multiple_of(kb * t, t)
        acc_ref[...] += _dot(vt_ref[:, pl.ds(ks, t)], a)

    def finish():
        o_ref[...] = acc_ref[...].T.astype(o_ref.dtype)

    @pl.when(qi == 0)
    def _():
        accumulate(weights(scores(0), True), 0)
        finish()

    @pl.when(qi > 0)
    def _():
        s_ahead = scores(qi - 1)
        a_diag = weights(scores(qi), True)

        def body(i, carry):
            s, a = carry
            accumulate(a, qi - i)
            s_next = scores(qi - i - 2)
            return s_next, weights(s, False)

        s_last, a = lax.fori_loop(0, qi - 1, body, (s_ahead, a_diag))
        accumulate(a, 1)
        accumulate(weights(s_last, False), 0)
        finish()


def _sb_attention(proj, n_heads, t=512):
    b, s, _ = proj.shape
    d = HEAD_DIM
    vmem = (2 * (2 * _nbytes((t, d), BF16) + 2 * _nbytes((s, d), BF16))
            + _nbytes((d, s), BF16) + 2 * _nbytes((d, t), F32) + 16 * _nbytes((t, t), F32))
    return pl.pallas_call(
        functools.partial(_sb_kernel, t=t, sub=LANES),
        grid=(b, n_heads, s // t),
        in_specs=[pl.BlockSpec((None, t, d), lambda bi, h, qi: (bi, qi, h)),
                  pl.BlockSpec((None, s, d), lambda bi, h, qi: (bi, 0, n_heads + h)),
                  pl.BlockSpec((None, s, d), lambda bi, h, qi: (bi, 0, 2 * n_heads + h))],
        out_specs=pl.BlockSpec((None, t, d), lambda bi, h, qi: (bi, qi, h)),
        out_shape=jax.ShapeDtypeStruct((b, s, n_heads * d), BF16),
        scratch_shapes=[pltpu.VMEM((d, s), BF16), pltpu.VMEM((1, t), F32), pltpu.VMEM((d, t), F32)],
        compiler_params=_cparams(("parallel", "parallel", "arbitrary"), vmem),
        name="stick_breaking_attention",
    )(proj, proj, proj)


def _split3(x):
    hi = x.astype(BF16)
    rest = x - hi.astype(F32)
    mid = rest.astype(BF16)
    lo = (rest - mid.astype(F32)).astype(BF16)
    return hi, mid, lo


def _fox_gate_kernel(fl_ref, b_ref, pq_ref, pk_ref, oq_ref, ok_ref, qa_ref, ka_ref, carry_ref):
    ts = fl_ref.shape[0]

    @pl.when(pl.program_id(1) == 0)
    def _():
        carry_ref[...] = jnp.zeros(carry_ref.shape, F32)

    z = fl_ref[...] + b_ref[...]
    log_f = jnp.minimum(z, 0.0) - jnp.log(1.0 + jnp.exp(-jnp.abs(z)))
    row = lax.broadcasted_iota(jnp.int32, (ts, ts), 0)
    col = lax.broadcasted_iota(jnp.int32, (ts, ts), 1)
    prefix = jnp.where(col <= row, 1.0, 0.0).astype(BF16)
    hi, mid, lo = _split3(log_f)
    c = _dot(prefix, hi) + _dot(prefix, mid) + _dot(prefix, lo) + carry_ref[0:1, :]
    carry_ref[0:1, :] = c[ts - 1:ts, :]
    parts = jnp.concatenate(_split3(c * LOG2E), axis=1)
    qa = (_dot(parts, pq_ref[...]) + oq_ref[...]).astype(BF16)
    ka = (_dot(parts, pk_ref[...]) + ok_ref[...]).astype(BF16)
    for h in range(qa_ref.shape[0]):
        qa_ref[h] = qa[:, h * LANES:(h + 1) * LANES]
        ka_ref[h] = ka[:, h * LANES:(h + 1) * LANES]


def _fox_aug_tables(n_heads):
    part, src = jnp.arange(3 * LANES) // LANES, jnp.arange(3 * LANES) % LANES
    head, lane = jnp.arange(n_heads * LANES) // LANES, jnp.arange(n_heads * LANES) % LANES
    mine = src[:, None] == head[None, :]
    pq = jnp.where(mine & (lane[None, :] == part[:, None] + 3), 1.0, 0.0).astype(BF16)
    pk = jnp.where(mine & (lane[None, :] == part[:, None]), -1.0, 0.0).astype(BF16)
    oq = jnp.where(lane < 3, 1.0, 0.0).astype(F32)[None]
    ok = jnp.where((lane >= 3) & (lane < 6), 1.0, 0.0).astype(F32)[None]
    return pq, pk, oq, ok


def _fox_gates(f_logit, b_f, n_heads, ts=256):
    b, s, w = f_logit.shape
    hw = n_heads * LANES
    out = jax.ShapeDtypeStruct((b, n_heads, s, LANES), BF16)
    const = lambda bi, i: (0, 0)
    return pl.pallas_call(
        _fox_gate_kernel,
        grid=(b, s // ts),
        in_specs=[pl.BlockSpec((None, ts, w), lambda bi, i: (bi, i, 0)),
                  pl.BlockSpec((1, w), const),
                  pl.BlockSpec((3 * LANES, hw), const), pl.BlockSpec((3 * LANES, hw), const),
                  pl.BlockSpec((1, hw), const), pl.BlockSpec((1, hw), const)],
        out_specs=[pl.BlockSpec((None, n_heads, ts, LANES), lambda bi, i: (bi, 0, i, 0)),
                   pl.BlockSpec((None, n_heads, ts, LANES), lambda bi, i: (bi, 0, i, 0))],
        out_shape=[out, out],
        scratch_shapes=[pltpu.VMEM((8, w), F32)],
        compiler_params=_cparams(("parallel", "arbitrary"), 32 * 2**20),
        name="fox_gate_cumsum",
    )(f_logit, b_f.reshape(1, w), *_fox_aug_tables(n_heads))


def _mem_attn_kernel(q_ref, kv_ref, o_ref, *, scale):
    outs = []
    for h in range(MEM_HEADS):
        q = q_ref[:, h * HEAD_DIM:(h + 1) * HEAD_DIM].astype(BF16)
        k = kv_ref[:, h * HEAD_DIM:(h + 1) * HEAD_DIM]
        v = kv_ref[:, MEM_W + h * HEAD_DIM:MEM_W + (h + 1) * HEAD_DIM]
        s = _dot_nt(q, k) * scale
        p = jnp.exp(s - jnp.max(s, axis=1, keepdims=True))
        o = _dot(p.astype(BF16), v) / jnp.sum(p, axis=1, keepdims=True)
        outs.append(o.astype(o_ref.dtype))
    o_ref[...] = jnp.concatenate(outs, axis=1)


def _mem_attention(q_arr, q_block, mem_kv, tq=512):
    b, s, _ = q_arr.shape
    length = mem_kv.shape[1]
    vmem = (2 * (_nbytes((tq, MEM_W), q_arr.dtype) + _nbytes((length, 2 * MEM_W), BF16)
                 + _nbytes((tq, MEM_W), BF16)) + 8 * _nbytes((tq, length), F32))
    return pl.pallas_call(
        functools.partial(_mem_attn_kernel, scale=HEAD_DIM ** -0.5),
        grid=(b, s // tq),
        in_specs=[pl.BlockSpec((None, tq, MEM_W), lambda bi, i: (bi, i, q_block)),
                  pl.BlockSpec((None, length, 2 * MEM_W), lambda bi, i: (bi, 0, 0))],
        out_specs=pl.BlockSpec((None, tq, MEM_W), lambda bi, i: (bi, i, 0)),
        out_shape=jax.ShapeDtypeStruct((b, s, MEM_W), BF16),
        compiler_params=_cparams(("parallel", "parallel"), vmem),
        name="memory_attention",
    )(q_arr, mem_kv)


def _t5_bucket_table():
    max_exact = NUM_BUCKETS // 2
    kj = jnp.arange(2 * WINDOW)[:, None]
    qi = jnp.arange(WINDOW)[None, :]
    signed = WINDOW + qi - kj
    dist = jnp.maximum(signed, 0)
    d = jnp.maximum(dist, 1).astype(F32)
    large = max_exact + (jnp.log(d / max_exact) / math.log(MAX_DISTANCE / max_exact)
                         * (NUM_BUCKETS - max_exact)).astype(jnp.int32)
    bucket = jnp.where(dist < max_exact, dist, jnp.minimum(large, NUM_BUCKETS - 1))
    return jnp.where((signed >= 0) & (signed < WINDOW), bucket, -1).astype(jnp.int32)


def _t5_bias_kernel(rb_ref, bucket_ref, o_ref):
    kvh = pl.program_id(0)
    bucket = bucket_ref[...]
    for g in range(SWA_GROUP):
        bias = jnp.full(bucket.shape, -jnp.inf, F32)
        for b in range(NUM_BUCKETS):
            bias = jnp.where(bucket == b, rb_ref[b, kvh * SWA_GROUP + g] * LOG2E, bias)
        o_ref[:, g * WINDOW:(g + 1) * WINDOW] = bias


def _t5_bias(rel_bias):
    return pl.pallas_call(
        _t5_bias_kernel,
        grid=(N_SWA_KV_HEADS,),
        in_specs=[pl.BlockSpec(memory_space=pltpu.SMEM),
                  pl.BlockSpec((2 * WINDOW, WINDOW), lambda h: (0, 0))],
        out_specs=pl.BlockSpec((None, 2 * WINDOW, SWA_GROUP * WINDOW), lambda h: (h, 0, 0)),
        out_shape=jax.ShapeDtypeStruct((N_SWA_KV_HEADS, 2 * WINDOW, SWA_GROUP * WINDOW), F32),
        compiler_params=_cparams(("parallel",), 16 * 2**20),
        name="t5_bias",
    )(rel_bias, _t5_bucket_table())


def _swa_kernel(sink_ref, q_ref, kvc_ref, kvp_ref, bias_ref, o_ref):
    n = pl.program_id(1)
    d = SWA_HEAD_DIM
    kv_w = N_SWA_KV_HEADS * d
    k_win = jnp.concatenate([kvp_ref[:, 0:kv_w], kvc_ref[:, 0:kv_w]], axis=0)
    v_win = jnp.concatenate([kvp_ref[:, kv_w:2 * kv_w], kvc_ref[:, kv_w:2 * kv_w]], axis=0)
    v_t = v_win.astype(F32).T.astype(BF16)
    no_prev = jnp.where(n == 0, -jnp.inf, 0.0)
    outs = []
    for kvh in range(N_SWA_KV_HEADS):
        heads = range(kvh * SWA_GROUP, (kvh + 1) * SWA_GROUP)
        q = jnp.concatenate([q_ref[:, h * d:(h + 1) * d] for h in heads], axis=0)
        sink = jnp.concatenate([jnp.full((1, WINDOW), sink_ref[h] * LOG2E, F32) for h in heads], axis=1)
        s = _dot_nt(k_win[:, kvh * d:(kvh + 1) * d], q) + bias_ref[kvh]
        s = jnp.concatenate([s[0:WINDOW, :] + no_prev, s[WINDOW:, :]], axis=0)
        m = jnp.maximum(jnp.max(s, axis=0, keepdims=True), sink)
        p = jnp.exp2(s - m)
        denom = jnp.sum(p, axis=0, keepdims=True) + jnp.exp2(sink - m)
        o_t = _dot(v_t[kvh * d:(kvh + 1) * d, :], p.astype(BF16)) / denom
        outs += [o_t[:, g * WINDOW:(g + 1) * WINDOW] for g in range(SWA_GROUP)]
    o_ref[...] = jnp.concatenate(outs, axis=0).T.astype(o_ref.dtype)


def _swa_attention(proj, sinks, bias):
    b, s, _ = proj.shape
    q_w = N_SWA_HEADS * SWA_HEAD_DIM
    kv_w = 2 * N_SWA_KV_HEADS * SWA_HEAD_DIM
    kv_block = q_w // kv_w
    vmem = (2 * (2 * _nbytes((WINDOW, q_w), BF16) + 2 * _nbytes((WINDOW, kv_w), BF16)
                 + _nbytes(bias.shape, F32)) + 16 * 2**20)
    return pl.pallas_call(
        _swa_kernel,
        grid=(b, s // WINDOW),
        in_specs=[pl.BlockSpec(memory_space=pltpu.SMEM),
                  pl.BlockSpec((None, WINDOW, q_w), lambda bi, n: (bi, n, 0)),
                  pl.BlockSpec((None, WINDOW, kv_w), lambda bi, n: (bi, n, kv_block)),
                  pl.BlockSpec((None, WINDOW, kv_w), lambda bi, n: (bi, jnp.maximum(n - 1, 0), kv_block)),
                  pl.BlockSpec(bias.shape, lambda bi, n: (0, 0, 0))],
        out_specs=pl.BlockSpec((None, WINDOW, q_w), lambda bi, n: (bi, n, 0)),
        out_shape=jax.ShapeDtypeStruct((b, s, q_w), BF16),
        compiler_params=_cparams(("parallel", "arbitrary"), vmem),
        name="sliding_window_attention",
    )(sinks, proj, proj, proj, bias)


def _rope_table_kernel(pos_ref, invf_ref, o_ref):
    ang = pos_ref[...].astype(F32) * invf_ref[...]
    lane = lax.broadcasted_iota(jnp.int32, ang.shape, 1)
    half = QK_ROPE // 2
    sin_signed = jnp.where(lane < QK_ROPE + half, -jnp.sin(ang), jnp.sin(ang))
    o_ref[...] = jnp.where(lane < QK_ROPE, jnp.cos(ang), sin_signed)


def _rope_table(positions, tm=1024):
    m = positions.shape[0]
    half = QK_ROPE // 2
    inv_freq = ROPE_THETA ** (-jnp.arange(half, dtype=F32) / half)
    invf = jnp.tile(inv_freq, LANES // half).reshape(1, LANES)
    return pl.pallas_call(
        _rope_table_kernel,
        grid=(m // tm,),
        in_specs=[pl.BlockSpec((tm, 1), lambda i: (i, 0)), pl.BlockSpec((1, LANES), lambda i: (0, 0))],
        out_specs=pl.BlockSpec((tm, LANES), lambda i: (i, 0)),
        out_shape=jax.ShapeDtypeStruct((m, LANES), F32),
        compiler_params=_cparams(("parallel",), 32 * 2**20),
        name="rope_table",
    )(positions, invf)


def _apply_rope(x_and_partner, cs):
    z = x_and_partner * cs
    z = z + pltpu.roll(z, QK_ROPE, axis=1)
    lane = lax.broadcasted_iota(jnp.int32, z.shape, 1)
    return jnp.where(lane < QK_ROPE, z, 0.0)


def _mla_q_kernel(cq_ref, g_ref, w_ref, cs_ref, o_ref, cn_ref):
    @pl.when(pl.program_id(1) == 0)
    def _():
        cn_ref[...] = _rms(cq_ref[...], g_ref[...]).astype(BF16)

    y = _dot(cn_ref[...], w_ref[...]) * ((QK_NOPE + QK_ROPE) ** -0.5 * LOG2E)
    cs = cs_ref[...]
    parts = []
    for h in range(y.shape[1] // (2 * LANES)):
        yh = y[:, h * 2 * LANES:(h + 1) * 2 * LANES]
        parts += [yh[:, :QK_NOPE], _apply_rope(yh[:, QK_NOPE:], cs)]
    o_ref[...] = jnp.concatenate(parts, axis=1).astype(o_ref.dtype)


def _mla_q(proj, q_norm, w_uq_aug, cs, tm=512, heads_per_step=4):
    m = proj.shape[0]
    hw = 2 * LANES * heads_per_step
    n_heads = w_uq_aug.shape[1] // hw
    return pl.pallas_call(
        _mla_q_kernel,
        grid=(m // tm, n_heads),
        in_specs=[pl.BlockSpec((tm, Q_LORA), lambda i, h: (i, 1)),
                  pl.BlockSpec((1, Q_LORA), lambda i, h: (0, 0)),
                  pl.BlockSpec((Q_LORA, hw), lambda i, h: (0, h)),
                  pl.BlockSpec((tm, LANES), lambda i, h: (i, 0))],
        out_specs=pl.BlockSpec((tm, hw), lambda i, h: (i, h)),
        out_shape=jax.ShapeDtypeStruct((m, n_heads * hw), BF16),
        scratch_shapes=[pltpu.VMEM((tm, Q_LORA), BF16)],
        compiler_params=_cparams(("parallel", "arbitrary"), 32 * 2**20),
        name="mla_q_expand",
    )(proj, q_norm.reshape(1, Q_LORA), w_uq_aug, cs)


def _mla_kv_kernel(ckv_ref, g_ref, w_ref, kr_ref, cs_ref, k_ref, v_ref, cn_ref, kpe_ref):
    @pl.when(pl.program_id(1) == 0)
    def _():
        cn_ref[...] = _rms(ckv_ref[...], g_ref[...]).astype(BF16)
        kpe_ref[...] = _apply_rope(kr_ref[...], cs_ref[...])

    y = _dot(cn_ref[...], w_ref[...])
    kpe = kpe_ref[...]
    k_parts, v_parts = [], []
    for h in range(y.shape[1] // (2 * LANES)):
        yh = y[:, h * 2 * LANES:(h + 1) * 2 * LANES]
        k_parts += [yh[:, :QK_NOPE], kpe]
        v_parts.append(yh[:, QK_NOPE:])
    k_ref[...] = jnp.concatenate(k_parts, axis=1).astype(k_ref.dtype)
    v_ref[...] = jnp.concatenate(v_parts, axis=1).astype(v_ref.dtype)


def _mla_kv(proj, kv_norm, w_ukv, cs, tm=512, heads_per_step=4):
    m = proj.shape[0]
    hw = 2 * LANES * heads_per_step
    n_heads = w_ukv.shape[1] // hw
    return pl.pallas_call(
        _mla_kv_kernel,
        grid=(m // tm, n_heads),
        in_specs=[pl.BlockSpec((tm, KV_LORA), lambda i, h: (i, 4)),
                  pl.BlockSpec((1, KV_LORA), lambda i, h: (0, 0)),
                  pl.BlockSpec((KV_LORA, hw), lambda i, h: (0, h)),
                  pl.BlockSpec((tm, LANES), lambda i, h: (i, 10)),
                  pl.BlockSpec((tm, LANES), lambda i, h: (i, 0))],
        out_specs=[pl.BlockSpec((tm, hw), lambda i, h: (i, h)),
                   pl.BlockSpec((tm, hw // 2), lambda i, h: (i, h))],
        out_shape=[jax.ShapeDtypeStruct((m, n_heads * hw), BF16),
                   jax.ShapeDtypeStruct((m, n_heads * hw // 2), BF16)],
        scratch_shapes=[pltpu.VMEM((tm, KV_LORA), BF16), pltpu.VMEM((tm, LANES), F32)],
        compiler_params=_cparams(("parallel", "arbitrary"), 32 * 2**20),
        name="mla_kv_expand",
    )(proj, kv_norm.reshape(1, KV_LORA), w_ukv, proj, cs)


def _swap_halves(w):
    half = w.shape[-1] // 2
    return jnp.concatenate([w[..., half:], w[..., :half]], axis=-1)


def _mixer_out(x2, o, o_mem, w_out, b, s):
    return _matmul_res([o.reshape(b * s, o.shape[-1]), o_mem.reshape(b * s, MEM_W)],
                       w_out[0], w_out[1], x2, tm=1024, tn=512)


def _sb_layer(x2, mem_kv, g, w_in, w_out, b, s):
    proj = _norm_matmul(x2, g, w_in.astype(BF16), BF16, tm=IN_PROJ_ROWS, tn=512, scaled_cols=N_HEADS * HEAD_DIM,
                        col_scale=HEAD_DIM ** -0.5 * LOG2E).reshape(b, s, -1)
    o = _sb_attention(proj, N_HEADS)
    o_mem = _mem_attention(proj, 3 * N_HEADS * HEAD_DIM // MEM_W, mem_kv)
    return _mixer_out(x2, o, o_mem, w_out, b, s)


def _fox_layer(x2, mem_kv, g, w_in, b_f, w_out, b, s):
    qkv_w = 3 * N_HEADS * HEAD_DIM
    w_main = jnp.concatenate([w_in[:, :qkv_w], w_in[:, qkv_w + N_HEADS:]], axis=1).astype(BF16)
    w_gate = jnp.pad(w_in[:, qkv_w:qkv_w + N_HEADS], ((0, 0), (0, LANES - N_HEADS))).astype(BF16)
    proj = _norm_matmul(x2, g, w_main, BF16, tm=IN_PROJ_ROWS, tn=512, scaled_cols=N_HEADS * HEAD_DIM,
                        col_scale=HEAD_DIM ** -0.5 * LOG2E).reshape(b, s, -1)
    f_logit = _norm_matmul(x2, g, w_gate, F32, tm=512, tn=LANES).reshape(b, s, LANES)
    aug = _fox_gates(f_logit, jnp.pad(b_f, (0, LANES - N_HEADS)), N_HEADS)
    o = _flash_attention(proj, proj, proj, 0, N_HEADS, 2 * N_HEADS, HEAD_DIM, HEAD_DIM, N_HEADS, aug=aug)
    o_mem = _mem_attention(proj, qkv_w // MEM_W, mem_kv)
    return _mixer_out(x2, o, o_mem, w_out, b, s)


def _swa_layer(x2, mem_kv, g, rel_bias, w_in, sinks, w_out, b, s):
    proj = _norm_matmul(x2, g, w_in.astype(BF16), BF16, tm=IN_PROJ_ROWS, tn=512,
                        scaled_cols=N_SWA_HEADS * SWA_HEAD_DIM,
                        col_scale=SWA_HEAD_DIM ** -0.5 * LOG2E).reshape(b, s, -1)
    o = _swa_attention(proj, sinks, _t5_bias(rel_bias))
    q_w = N_SWA_HEADS * SWA_HEAD_DIM
    kv_w = 2 * N_SWA_KV_HEADS * SWA_HEAD_DIM
    o_mem = _mem_attention(proj, (q_w + kv_w) // MEM_W, mem_kv)
    return _mixer_out(x2, o, o_mem, w_out, b, s)


def _mla_layer(x2, mem_kv, g, positions, w_in, q_norm, w_uq, kv_norm, w_ukv, w_out, b, s):
    d = x2.shape[1]
    o1, o2, o3 = Q_LORA, Q_LORA + KV_LORA, Q_LORA + KV_LORA + QK_ROPE
    w_main = jnp.concatenate([w_in[:, o3:], w_in[:, :o3], _swap_halves(w_in[:, o2:o3])], axis=1).astype(BF16)
    proj = _norm_matmul(x2, g, w_main, F32, tm=512, tn=w_main.shape[1])
    cs = _rope_table(positions.reshape(b * s, 1))
    wq = w_uq.reshape(Q_LORA, N_HEADS, QK_NOPE + QK_ROPE)
    wq = jnp.concatenate([wq, _swap_halves(wq[:, :, QK_NOPE:])], axis=-1).reshape(Q_LORA, -1).astype(BF16)
    q = _mla_q(proj, q_norm, wq, cs).reshape(b, s, -1)
    k, v = _mla_kv(proj, kv_norm, w_ukv.astype(BF16), cs)
    k, v = k.reshape(b, s, -1), v.reshape(b, s, -1)
    o = _flash_attention(q, k, v, 0, 0, 0, 2 * LANES, HEAD_DIM, N_HEADS)
    o_mem = _mem_attention(proj.reshape(b, s, -1), 0, mem_kv)
    return _mixer_out(x2, o, o_mem, w_out, b, s)


def kernel(x, mem, positions, rel_bias, attn_norm, mem_norm, w_mem_kv, ffn_norm, ffn_w_up, ffn_conv_w, ffn_conv_b, ffn_w_down, final_norm, sb_w_in, sb_w_out, fox_w_in, fox_b_f, fox_w_out, swa_w_in, swa_sinks, swa_w_out, mla_w_in, mla_q_norm, mla_w_uq, mla_kv_norm, mla_w_ukv, mla_w_out):
    b, s, d = x.shape
    depth = attn_norm.shape[0]
    mem_len = mem.shape[1]
    x2 = x.reshape(b * s, d)
    mem2 = mem.reshape(b * mem_len, d)
    w_mem_kv_bf16 = w_mem_kv.astype(BF16)
    for i in range(depth):
        kind, j = i % 4, i // 4
        mem_kv = _norm_matmul(mem2, mem_norm[i], w_mem_kv_bf16, BF16, tm=b * mem_len, tn=2 * MEM_W,
                              layer=i).reshape(b, mem_len, 2 * MEM_W)
        g = attn_norm[i]
        if kind == 0:
            x2 = _sb_layer(x2, mem_kv, g, sb_w_in[j], (sb_w_out, j), b, s)
        elif kind == 1:
            x2 = _fox_layer(x2, mem_kv, g, fox_w_in[j], fox_b_f[j], (fox_w_out, j), b, s)
        elif kind == 2:
            x2 = _swa_layer(x2, mem_kv, g, rel_bias, swa_w_in[j], swa_sinks[j], (swa_w_out, j), b, s)
        else:
            x2 = _mla_layer(x2, mem_kv, g, positions, mla_w_in[j], mla_q_norm[j], mla_w_uq[j],
                            mla_kv_norm[j], mla_w_ukv[j], (mla_w_out, j), b, s)
        gated = _ffn_up(_rmsnorm(x2, ffn_norm[i], BF16, tm=512), ffn_w_up, i, ffn_conv_w[i], ffn_conv_b[i],
                        seq=s, tm=1024, tn=512)
        x2 = _matmul_res([gated], ffn_w_down, i, x2, tm=512, tn=512)
    return _rmsnorm(x2, final_norm, F32, tm=512).reshape(b, s, d)
```

```python
import functools
import math

import jax
import jax.numpy as jnp
from jax import lax
from jax.experimental import pallas as pl
from jax.experimental.pallas import tpu as pltpu

F32 = jnp.float32
BF16 = jnp.bfloat16

EPS = 1e-6
LOG2E = math.log2(math.e)
HEAD_DIM = 128
N_HEADS = 16
SWA_HEAD_DIM = 64
N_SWA_HEADS = 32
N_SWA_KV_HEADS = 4
SWA_GROUP = N_SWA_HEADS // N_SWA_KV_HEADS
WINDOW = 128
Q_LORA = 512
KV_LORA = 256
QK_NOPE = 128
QK_ROPE = 64
ROPE_THETA = 10000.0
MEM_HEADS = 4
MEM_W = MEM_HEADS * HEAD_DIM
NUM_BUCKETS = 32
MAX_DISTANCE = 128
CONV_WIDTH = 3

V7X_VMEM_BYTES = 64 * 1024 * 1024
VMEM_CAP = V7X_VMEM_BYTES - 8 * 1024 * 1024
LANES = 128
IN_PROJ_ROWS = 1024
TAIL = 8


def _cparams(sems, vmem_bytes):
    return pltpu.CompilerParams(dimension_semantics=sems,
                                vmem_limit_bytes=int(min(max(vmem_bytes, 16 * 2**20), VMEM_CAP)))


def _nbytes(shape, dtype):
    return math.prod(shape) * jnp.dtype(dtype).itemsize


def _rms(x, g):
    return x * lax.rsqrt(jnp.mean(x * x, axis=-1, keepdims=True) + EPS) * g


def _dot(a, b):
    return jnp.dot(a, b, preferred_element_type=F32)


def _dot_nt(a, b):
    return lax.dot_general(a, b, (((1,), (1,)), ((), ())), preferred_element_type=F32)


def _norm_matmul_kernel(x_ref, g_ref, w_ref, cs_ref, o_ref, xn_ref):
    @pl.when(pl.program_id(1) == 0)
    def _():
        xn_ref[...] = _rms(x_ref[...], g_ref[...]).astype(BF16)

    o_ref[...] = (_dot(xn_ref[...], w_ref[...]) * cs_ref[...]).astype(o_ref.dtype)


def _w_tile_spec(w, layer, k, tn, col_block):
    if w.ndim == 3:
        return pl.BlockSpec((None, k, tn), lambda i, j: (layer, 0, col_block(j)))
    return pl.BlockSpec((k, tn), lambda i, j: (0, col_block(j)))


def _norm_matmul(x, g, w, out_dtype, tm, tn, scaled_cols=0, col_scale=1.0, layer=None):
    m, k = x.shape
    n = w.shape[-1]
    cs = jnp.where(jnp.arange(n) < scaled_cols, col_scale, 1.0).astype(F32).reshape(1, n)
    vmem = (2 * (_nbytes((tm, k), F32) + _nbytes((k, tn), BF16) + _nbytes((tm, tn), out_dtype))
            + _nbytes((tm, k), BF16) + 2 * _nbytes((tm, k), F32) + _nbytes((tm, tn), F32))
    return pl.pallas_call(
        _norm_matmul_kernel,
        grid=(m // tm, n // tn),
        in_specs=[pl.BlockSpec((tm, k), lambda i, j: (i, 0)),
                  pl.BlockSpec((1, k), lambda i, j: (0, 0)),
                  _w_tile_spec(w, layer, k, tn, lambda j: j),
                  pl.BlockSpec((1, tn), lambda i, j: (0, j))],
        out_specs=pl.BlockSpec((tm, tn), lambda i, j: (i, j)),
        out_shape=jax.ShapeDtypeStruct((m, n), out_dtype),
        scratch_shapes=[pltpu.VMEM((tm, k), BF16)],
        compiler_params=_cparams(("parallel", "arbitrary"), vmem),
        name="norm_matmul",
    )(x, g.reshape(1, k), w, cs)


def _matmul_res_kernel(*refs, n_a):
    a_refs, w_refs = refs[:n_a], refs[n_a:2 * n_a]
    x_ref, o_ref = refs[2 * n_a], refs[2 * n_a + 1]
    wb_refs = refs[2 * n_a + 2:]

    @pl.when(pl.program_id(1) == 0)
    def _():
        for w_ref, wb_ref in zip(w_refs, wb_refs):
            wb_ref[...] = w_ref[...].astype(BF16)

    acc = x_ref[...]
    for a_ref, wb_ref in zip(a_refs, wb_refs):
        acc = acc + _dot(a_ref[...], wb_ref[...])
    o_ref[...] = acc


def _matmul_res(a_list, w, layer, x, tm, tn):
    m, n = x.shape
    vmem = 6 * _nbytes((tm, tn), F32)
    a_specs, w_specs, scratch = [], [], []
    row = 0
    for a in a_list:
        kp = a.shape[1]
        row_block = row // kp
        assert row_block * kp == row
        a_specs.append(pl.BlockSpec((tm, kp), lambda j, i: (i, 0)))
        w_specs.append(pl.BlockSpec((None, kp, tn), lambda j, i, row_block=row_block: (layer, row_block, j)))
        scratch.append(pltpu.VMEM((kp, tn), BF16))
        vmem += 2 * _nbytes((tm, kp), a.dtype) + 2 * _nbytes((kp, tn), F32) + 2 * _nbytes((kp, tn), BF16)
        row += kp
    assert row == w.shape[1]
    return pl.pallas_call(
        functools.partial(_matmul_res_kernel, n_a=len(a_list)),
        grid=(n // tn, m // tm),
        in_specs=a_specs + w_specs + [pl.BlockSpec((tm, tn), lambda j, i: (i, j))],
        out_specs=pl.BlockSpec((tm, tn), lambda j, i: (i, j)),
        out_shape=jax.ShapeDtypeStruct((m, n), F32),
        scratch_shapes=scratch,
        compiler_params=_cparams(("parallel", "arbitrary"), vmem),
        name="matmul_res",
    )(*a_list, *([w] * len(a_list)), x)


def _rmsnorm_kernel(x_ref, g_ref, o_ref):
    o_ref[...] = _rms(x_ref[...], g_ref[...]).astype(o_ref.dtype)


def _rmsnorm(x, g, out_dtype, tm):
    m, k = x.shape
    return pl.pallas_call(
        _rmsnorm_kernel,
        grid=(m // tm,),
        in_specs=[pl.BlockSpec((tm, k), lambda i: (i, 0)), pl.BlockSpec((1, k), lambda i: (0, 0))],
        out_specs=pl.BlockSpec((tm, k), lambda i: (i, 0)),
        out_shape=jax.ShapeDtypeStruct((m, k), out_dtype),
        compiler_params=_cparams(("parallel",), 8 * _nbytes((tm, k), F32)),
        name="rmsnorm",
    )(x, g.reshape(1, k))


def _ffn_up_kernel(xn_ref, wg_ref, wv_ref, cwg_ref, cwv_ref, cbg_ref, cbv_ref, o_ref,
                   wgb_ref, wvb_ref, tail_g_ref, tail_v_ref, *u_refs, rows, tiles_per_seq):
    i = pl.program_id(1)

    @pl.when(i == 0)
    def _():
        wgb_ref[...] = wg_ref[...].astype(BF16)
        wvb_ref[...] = wv_ref[...].astype(BF16)

    @pl.when(i % tiles_per_seq == 0)
    def _():
        tail_g_ref[...] = jnp.zeros(tail_g_ref.shape, F32)
        tail_v_ref[...] = jnp.zeros(tail_v_ref.shape, F32)

    n_chunks = len(u_refs) // 2

    for r in range(n_chunks):
        xr = xn_ref[r * rows:(r + 1) * rows, :]

        def conv(wb_ref, cw_ref, cb_ref, u_ref, before):
            u_ref[TAIL:, :] = _dot(xr, wb_ref[...])
            u_ref[0:TAIL, :] = before
            c = cb_ref[...]
            for tap in range(CONV_WIDTH):
                start = TAIL - (CONV_WIDTH - 1) + tap
                c = c + cw_ref[tap:tap + 1, :] * u_ref[start:start + rows, :]
            return c

        if r == 0:
            before_g, before_v = tail_g_ref[...], tail_v_ref[...]
        else:
            before_g, before_v = u_refs[2 * r - 2][rows:, :], u_refs[2 * r - 1][rows:, :]
        gate = conv(wgb_ref, cwg_ref, cbg_ref, u_refs[2 * r], before_g)
        val = conv(wvb_ref, cwv_ref, cbv_ref, u_refs[2 * r + 1], before_v)
        o_ref[r * rows:(r + 1) * rows, :] = (gate * (1.0 / (1.0 + jnp.exp(-gate))) * val).astype(o_ref.dtype)

    tail_g_ref[...] = u_refs[2 * n_chunks - 2][rows:, :]
    tail_v_ref[...] = u_refs[2 * n_chunks - 1][rows:, :]


def _ffn_up(xn, w_up, layer, conv_w, conv_b, seq, tm, tn, rows=256):
    m, k = xn.shape
    d_ff = w_up.shape[-1] // 2
    nj = d_ff // tn
    n_chunks = tm // rows
    vmem = (2 * (_nbytes((tm, k), BF16) + 2 * _nbytes((k, tn), F32) + _nbytes((tm, tn), BF16))
            + 2 * _nbytes((k, tn), BF16) + 2 * n_chunks * _nbytes((rows + TAIL, tn), F32)
            + 2 * _nbytes((k, tn), F32) + 6 * n_chunks * _nbytes((rows, tn), F32))
    return pl.pallas_call(
        functools.partial(_ffn_up_kernel, rows=rows, tiles_per_seq=seq // tm),
        grid=(nj, m // tm),
        in_specs=[pl.BlockSpec((tm, k), lambda j, i: (i, 0)),
                  pl.BlockSpec((None, k, tn), lambda j, i: (layer, 0, j)),
                  pl.BlockSpec((None, k, tn), lambda j, i: (layer, 0, nj + j)),
                  pl.BlockSpec((CONV_WIDTH, tn), lambda j, i: (0, j)),
                  pl.BlockSpec((CONV_WIDTH, tn), lambda j, i: (0, nj + j)),
                  pl.BlockSpec((1, tn), lambda j, i: (0, j)),
                  pl.BlockSpec((1, tn), lambda j, i: (0, nj + j))],
        out_specs=pl.BlockSpec((tm, tn), lambda j, i: (i, j)),
        out_shape=jax.ShapeDtypeStruct((m, d_ff), BF16),
        scratch_shapes=[pltpu.VMEM((k, tn), BF16), pltpu.VMEM((k, tn), BF16),
                        pltpu.VMEM((TAIL, tn), F32), pltpu.VMEM((TAIL, tn), F32)]
        + [pltpu.VMEM((rows + TAIL, tn), F32)] * (2 * n_chunks),
        compiler_params=_cparams(("arbitrary", "arbitrary"), vmem),
        name="ffn_up_conv_gate",
    )(xn, w_up, w_up, conv_w, conv_w, conv_b.reshape(1, -1), conv_b.reshape(1, -1))


def _eye(n):
    return jnp.where(lax.broadcasted_iota(jnp.int32, (n, n), 0) == lax.broadcasted_iota(jnp.int32, (n, n), 1),
                     1.0, 0.0).astype(BF16)


def _transpose_bf16(x):
    return _dot_nt(_eye(x.shape[1]), x)


def _fill_v_transposed(v_ref, vt_ref, t):
    s, dv = v_ref.shape
    for c in range(s // t):
        vt_ref[0:dv, c * t:(c + 1) * t] = _transpose_bf16(v_ref[c * t:(c + 1) * t, :]).astype(BF16)
    if vt_ref.shape[0] > dv:
        vt_ref[dv:, :] = jnp.ones((vt_ref.shape[0] - dv, s), BF16)


def _flash_kernel(*refs, t, dk, dv, group, has_aug):
    if has_aug:
        q_ref, k_ref, v_ref, qa_ref, ka_ref, o_ref, vt_ref, m_ref, acc_ref = refs
    else:
        q_ref, k_ref, v_ref, o_ref, vt_ref, m_ref, acc_ref = refs
    qi = pl.program_id(2)
    heads = range(group)

    @pl.when(qi == 0)
    def _():
        for g in heads:
            _fill_v_transposed(v_ref.at[:, g * dv:(g + 1) * dv], vt_ref.at[g], t)

    qs = []
    for g in heads:
        q = q_ref[:, g * dk:(g + 1) * dk]
        qs.append(jnp.concatenate([q, qa_ref[g]], axis=1) if has_aug else q)
    m_ref[...] = jnp.full(m_ref.shape, -jnp.inf, F32)
    acc_ref[...] = jnp.zeros(acc_ref.shape, F32)

    def scores(g, kb):
        ks = pl.multiple_of(kb * t, t)
        k = k_ref[pl.ds(ks, t), g * dk:(g + 1) * dk]
        if has_aug:
            k = jnp.concatenate([k, ka_ref[g, pl.ds(ks, t), :]], axis=1)
        return _dot_nt(k, qs[g])

    def col_max(s):
        return jnp.max(s, axis=0, keepdims=True)

    def update(g, s, s_max, kb):
        m_prev = m_ref[g]
        m_new = jnp.maximum(m_prev, s_max)
        m_ref[g] = m_new
        p = jnp.exp2(s - m_new).astype(BF16)
        ks = pl.multiple_of(kb * t, t)
        acc_ref[g] = jnp.exp2(m_prev - m_new) * acc_ref[g] + _dot(vt_ref[g, :, pl.ds(ks, t)], p)

    def body(kb, carry):
        nxt = []
        for g in heads:
            s_next = scores(g, kb + 1)
            nxt.append((s_next, col_max(s_next)))
        for g in heads:
            update(g, *carry[g], kb)
        return tuple(nxt)

    first = []
    for g in heads:
        s0 = scores(g, 0)
        first.append((s0, col_max(s0)))
    last = lax.fori_loop(0, qi, body, tuple(first))
    key = lax.broadcasted_iota(jnp.int32, (t, t), 0)
    qry = lax.broadcasted_iota(jnp.int32, (t, t), 1)
    outs = []
    for g in heads:
        s = jnp.where(key <= qry, last[g][0], -jnp.inf)
        update(g, s, col_max(s), qi)
        acc = acc_ref[g]
        outs.append((acc[0:dv, :] / acc[dv:dv + 1, :]).T)
    o_ref[...] = jnp.concatenate(outs, axis=1).astype(o_ref.dtype)


def _flash_attention(q_arr, k_arr, v_arr, q_off, k_off, v_off, dk, dv, n_heads, aug=None, t=512, group=2):
    b, s, _ = q_arr.shape
    assert q_off % group == 0 and k_off % group == 0 and v_off % group == 0 and n_heads % group == 0
    qo, ko, vo = q_off // group, k_off // group, v_off // group
    in_specs = [pl.BlockSpec((None, t, group * dk), lambda bi, h, qi: (bi, qi, qo + h)),
                pl.BlockSpec((None, s, group * dk), lambda bi, h, qi: (bi, 0, ko + h)),
                pl.BlockSpec((None, s, group * dv), lambda bi, h, qi: (bi, 0, vo + h))]
    args = [q_arr, k_arr, v_arr]
    vmem = 2 * group * (_nbytes((t, dk), BF16) + _nbytes((s, dk), BF16) + _nbytes((s, dv), BF16)
                        + _nbytes((t, dv), BF16))
    if aug is not None:
        in_specs += [pl.BlockSpec((None, group, t, LANES), lambda bi, h, qi: (bi, h, qi, 0)),
                     pl.BlockSpec((None, group, s, LANES), lambda bi, h, qi: (bi, h, 0, 0))]
        args += list(aug)
        vmem += 2 * group * (_nbytes((t, LANES), BF16) + _nbytes((s, LANES), BF16))
    acc_rows = dv + 16
    vmem += group * (_nbytes((acc_rows, s), BF16) + 2 * _nbytes((acc_rows, t), F32) + 8 * _nbytes((t, t), F32))
    return pl.pallas_call(
        functools.partial(_flash_kernel, t=t, dk=dk, dv=dv, group=group, has_aug=aug is not None),
        grid=(b, n_heads // group, s // t),
        in_specs=in_specs,
        out_specs=pl.BlockSpec((None, t, group * dv), lambda bi, h, qi: (bi, qi, h)),
        out_shape=jax.ShapeDtypeStruct((b, s, n_heads * dv), BF16),
        scratch_shapes=[pltpu.VMEM((group, acc_rows, s), BF16), pltpu.VMEM((group, 1, t), F32),
                        pltpu.VMEM((group, acc_rows, t), F32)],
        compiler_params=_cparams(("parallel", "parallel", "arbitrary"), vmem),
        name="flash_attention",
    )(*args)


def _sb_kernel(q_ref, k_ref, v_ref, o_ref, vt_ref, carry_ref, acc_ref, *, t, sub, d, group):
    qi = pl.program_id(2)
    heads = range(group)

    @pl.when(qi == 0)
    def _():
        for g in heads:
            _fill_v_transposed(v_ref.at[:, g * d:(g + 1) * d], vt_ref.at[g], t)

    carry_ref[...] = jnp.zeros(carry_ref.shape, F32)
    acc_ref[...] = jnp.zeros(acc_ref.shape, F32)
    qs = [q_ref[:, g * d:(g + 1) * d] for g in heads]
    key = lax.broadcasted_iota(jnp.int32, (sub, t), 0)
    qry = lax.broadcasted_iota(jnp.int32, (sub, t), 1)
    r = lax.broadcasted_iota(jnp.int32, (sub + 16, sub), 0)
    c = lax.broadcasted_iota(jnp.int32, (sub + 16, sub), 1)
    suffix = jnp.where(((c > r) & (r < sub)) | (r == sub), 1.0, 0.0).astype(BF16)
    suffix = jnp.concatenate([suffix, suffix], axis=1)

    def scores(g, kb):
        ks = pl.multiple_of(kb * t, t)
        return _dot_nt(k_ref[pl.ds(ks, t), g * d:(g + 1) * d], qs[g])

    def weights(g, s, on_diagonal):
        later = carry_ref[g]
        out = [None] * (t // sub)
        for i in reversed(range(t // sub)):
            z = s[i * sub:(i + 1) * sub, :]
            neg_abs = lax.bitcast_convert_type(
                lax.bitcast_convert_type(z, jnp.uint32) | jnp.uint32(0x80000000), F32)
            log_beta = jnp.minimum(z, 0.0) - jnp.log2(1.0 + jnp.exp2(neg_abs))
            log_keep = log_beta - z
            if on_diagonal:
                strict = key + i * sub < qry
                log_keep = jnp.where(strict, log_keep, 0.0)
            hi = log_keep.astype(BF16)
            lo = (log_keep - hi.astype(F32)).astype(BF16)
            sums = _dot(suffix, jnp.concatenate([hi, lo], axis=0))
            a = jnp.exp2(log_beta + (sums[0:sub, :] + later))
            if on_diagonal:
                a = jnp.where(strict, a, 0.0)
            out[i] = a.astype(BF16)
            later = later + sums[sub:sub + 1, :]
        carry_ref[g] = later
        return jnp.concatenate(out, axis=0)

    def accumulate(g, a, kb):
        ks = pl.multiple_of(kb * t, t)
        acc_ref[g] += _dot(vt_ref[g, :, pl.ds(ks, t)], a)

    def finish():
        o_ref[...] = jnp.concatenate([acc_ref[g].T for g in heads], axis=1).astype(o_ref.dtype)

    @pl.when(qi == 0)
    def _():
        for g in heads:
            accumulate(g, weights(g, scores(g, 0), True), 0)
        finish()

    @pl.when(qi > 0)
    def _():
        first = tuple((scores(g, qi - 1), weights(g, scores(g, qi), True)) for g in heads)

        def body(i, carry):
            nxt = []
            for g in heads:
                s, a = carry[g]
                accumulate(g, a, qi - i)
                nxt.append((scores(g, qi - i - 2), weights(g, s, False)))
            return tuple(nxt)

        last = lax.fori_loop(0, qi - 1, body, first)
        for g in heads:
            s_last, a = last[g]
            accumulate(g, a, 1)
            accumulate(g, weights(g, s_last, False), 0)
        finish()


def _sb_attention(proj, n_heads, t=512, group=2):
    b, s, _ = proj.shape
    d = HEAD_DIM
    assert n_heads % group == 0
    blocks = n_heads // group
    vmem = group * (2 * (2 * _nbytes((t, d), BF16) + 2 * _nbytes((s, d), BF16))
                    + _nbytes((d, s), BF16) + 2 * _nbytes((d, t), F32) + 12 * _nbytes((t, t), F32))
    return pl.pallas_call(
        functools.partial(_sb_kernel, t=t, sub=LANES, d=d, group=group),
        grid=(b, blocks, s // t),
        in_specs=[pl.BlockSpec((None, t, group * d), lambda bi, h, qi: (bi, qi, h)),
                  pl.BlockSpec((None, s, group * d), lambda bi, h, qi: (bi, 0, blocks + h)),
                  pl.BlockSpec((None, s, group * d), lambda bi, h, qi: (bi, 0, 2 * blocks + h))],
        out_specs=pl.BlockSpec((None, t, group * d), lambda bi, h, qi: (bi, qi, h)),
        out_shape=jax.ShapeDtypeStruct((b, s, n_heads * d), BF16),
        scratch_shapes=[pltpu.VMEM((group, d, s), BF16), pltpu.VMEM((group, 1, t), F32),
                        pltpu.VMEM((group, d, t), F32)],
        compiler_params=_cparams(("parallel", "parallel", "arbitrary"), vmem),
        name="stick_breaking_attention",
    )(proj, proj, proj)


def _split3(x):
    hi = x.astype(BF16)
    rest = x - hi.astype(F32)
    mid = rest.astype(BF16)
    lo = (rest - mid.astype(F32)).astype(BF16)
    return hi, mid, lo


def _fox_gate_kernel(fl_ref, b_ref, pq_ref, pk_ref, oq_ref, ok_ref, qa_ref, ka_ref, carry_ref):
    ts = fl_ref.shape[0]

    @pl.when(pl.program_id(1) == 0)
    def _():
        carry_ref[...] = jnp.zeros(carry_ref.shape, F32)

    z = fl_ref[...] + b_ref[...]
    log_f = jnp.minimum(z, 0.0) - jnp.log(1.0 + jnp.exp(-jnp.abs(z)))
    row = lax.broadcasted_iota(jnp.int32, (ts, ts), 0)
    col = lax.broadcasted_iota(jnp.int32, (ts, ts), 1)
    prefix = jnp.where(col <= row, 1.0, 0.0).astype(BF16)
    hi, mid, lo = _split3(log_f)
    c = _dot(prefix, hi) + _dot(prefix, mid) + _dot(prefix, lo) + carry_ref[0:1, :]
    carry_ref[0:1, :] = c[ts - 1:ts, :]
    parts = jnp.concatenate(_split3(c * LOG2E), axis=1)
    qa = (_dot(parts, pq_ref[...]) + oq_ref[...]).astype(BF16)
    ka = (_dot(parts, pk_ref[...]) + ok_ref[...]).astype(BF16)
    for h in range(qa_ref.shape[0]):
        qa_ref[h] = qa[:, h * LANES:(h + 1) * LANES]
        ka_ref[h] = ka[:, h * LANES:(h + 1) * LANES]


def _fox_aug_tables(n_heads):
    part, src = jnp.arange(3 * LANES) // LANES, jnp.arange(3 * LANES) % LANES
    head, lane = jnp.arange(n_heads * LANES) // LANES, jnp.arange(n_heads * LANES) % LANES
    mine = src[:, None] == head[None, :]
    pq = jnp.where(mine & (lane[None, :] == part[:, None] + 3), 1.0, 0.0).astype(BF16)
    pk = jnp.where(mine & (lane[None, :] == part[:, None]), -1.0, 0.0).astype(BF16)
    oq = jnp.where(lane < 3, 1.0, 0.0).astype(F32)[None]
    ok = jnp.where((lane >= 3) & (lane < 6), 1.0, 0.0).astype(F32)[None]
    return pq, pk, oq, ok


def _fox_gates(f_logit, b_f, n_heads, ts=256):
    b, s, w = f_logit.shape
    hw = n_heads * LANES
    out = jax.ShapeDtypeStruct((b, n_heads, s, LANES), BF16)
    const = lambda bi, i: (0, 0)
    return pl.pallas_call(
        _fox_gate_kernel,
        grid=(b, s // ts),
        in_specs=[pl.BlockSpec((None, ts, w), lambda bi, i: (bi, i, 0)),
                  pl.BlockSpec((1, w), const),
                  pl.BlockSpec((3 * LANES, hw), const), pl.BlockSpec((3 * LANES, hw), const),
                  pl.BlockSpec((1, hw), const), pl.BlockSpec((1, hw), const)],
        out_specs=[pl.BlockSpec((None, n_heads, ts, LANES), lambda bi, i: (bi, 0, i, 0)),
                   pl.BlockSpec((None, n_heads, ts, LANES), lambda bi, i: (bi, 0, i, 0))],
        out_shape=[out, out],
        scratch_shapes=[pltpu.VMEM((8, w), F32)],
        compiler_params=_cparams(("parallel", "arbitrary"), 32 * 2**20),
        name="fox_gate_cumsum",
    )(f_logit, b_f.reshape(1, w), *_fox_aug_tables(n_heads))


def _mem_attn_kernel(q_ref, kv_ref, o_ref, *, scale):
    outs = []
    for h in range(MEM_HEADS):
        q = q_ref[:, h * HEAD_DIM:(h + 1) * HEAD_DIM].astype(BF16)
        k = kv_ref[:, h * HEAD_DIM:(h + 1) * HEAD_DIM]
        v = kv_ref[:, MEM_W + h * HEAD_DIM:MEM_W + (h + 1) * HEAD_DIM]
        s = _dot_nt(q, k) * scale
        p = jnp.exp(s - jnp.max(s, axis=1, keepdims=True))
        o = _dot(p.astype(BF16), v) / jnp.sum(p, axis=1, keepdims=True)
        outs.append(o.astype(o_ref.dtype))
    o_ref[...] = jnp.concatenate(outs, axis=1)


def _mem_attention(q_arr, q_block, mem_kv, tq=512):
    b, s, _ = q_arr.shape
    length = mem_kv.shape[1]
    vmem = (2 * (_nbytes((tq, MEM_W), q_arr.dtype) + _nbytes((length, 2 * MEM_W), BF16)
                 + _nbytes((tq, MEM_W), BF16)) + 8 * _nbytes((tq, length), F32))
    return pl.pallas_call(
        functools.partial(_mem_attn_kernel, scale=HEAD_DIM ** -0.5),
        grid=(b, s // tq),
        in_specs=[pl.BlockSpec((None, tq, MEM_W), lambda bi, i: (bi, i, q_block)),
                  pl.BlockSpec((None, length, 2 * MEM_W), lambda bi, i: (bi, 0, 0))],
        out_specs=pl.BlockSpec((None, tq, MEM_W), lambda bi, i: (bi, i, 0)),
        out_shape=jax.ShapeDtypeStruct((b, s, MEM_W), BF16),
        compiler_params=_cparams(("parallel", "parallel"), vmem),
        name="memory_attention",
    )(q_arr, mem_kv)


def _t5_bucket_table():
    max_exact = NUM_BUCKETS // 2
    kj = jnp.arange(2 * WINDOW)[:, None]
    qi = jnp.arange(WINDOW)[None, :]
    signed = WINDOW + qi - kj
    dist = jnp.maximum(signed, 0)
    d = jnp.maximum(dist, 1).astype(F32)
    large = max_exact + (jnp.log(d / max_exact) / math.log(MAX_DISTANCE / max_exact)
                         * (NUM_BUCKETS - max_exact)).astype(jnp.int32)
    bucket = jnp.where(dist < max_exact, dist, jnp.minimum(large, NUM_BUCKETS - 1))
    return jnp.where((signed >= 0) & (signed < WINDOW), bucket, -1).astype(jnp.int32)


def _t5_bias_kernel(rb_ref, bucket_ref, o_ref):
    kvh = pl.program_id(0)
    bucket = bucket_ref[...]
    for g in range(SWA_GROUP):
        bias = jnp.full(bucket.shape, -jnp.inf, F32)
        for b in range(NUM_BUCKETS):
            bias = jnp.where(bucket == b, rb_ref[b, kvh * SWA_GROUP + g] * LOG2E, bias)
        o_ref[:, g * WINDOW:(g + 1) * WINDOW] = bias


def _t5_bias(rel_bias):
    return pl.pallas_call(
        _t5_bias_kernel,
        grid=(N_SWA_KV_HEADS,),
        in_specs=[pl.BlockSpec(memory_space=pltpu.SMEM),
                  pl.BlockSpec((2 * WINDOW, WINDOW), lambda h: (0, 0))],
        out_specs=pl.BlockSpec((None, 2 * WINDOW, SWA_GROUP * WINDOW), lambda h: (h, 0, 0)),
        out_shape=jax.ShapeDtypeStruct((N_SWA_KV_HEADS, 2 * WINDOW, SWA_GROUP * WINDOW), F32),
        compiler_params=_cparams(("parallel",), 16 * 2**20),
        name="t5_bias",
    )(rel_bias, _t5_bucket_table())


def _swa_kernel(sink_ref, q_ref, kvc_ref, kvp_ref, bias_ref, o_ref):
    n = pl.program_id(1)
    d = SWA_HEAD_DIM
    kv_w = N_SWA_KV_HEADS * d
    k_win = jnp.concatenate([kvp_ref[:, 0:kv_w], kvc_ref[:, 0:kv_w]], axis=0)
    v_win = jnp.concatenate([kvp_ref[:, kv_w:2 * kv_w], kvc_ref[:, kv_w:2 * kv_w]], axis=0)
    v_t = v_win.astype(F32).T.astype(BF16)
    no_prev = jnp.where(n == 0, -jnp.inf, 0.0)
    outs = []
    for kvh in range(N_SWA_KV_HEADS):
        heads = range(kvh * SWA_GROUP, (kvh + 1) * SWA_GROUP)
        q = jnp.concatenate([q_ref[:, h * d:(h + 1) * d] for h in heads], axis=0)
        sink = jnp.concatenate([jnp.full((1, WINDOW), sink_ref[h] * LOG2E, F32) for h in heads], axis=1)
        s = _dot_nt(k_win[:, kvh * d:(kvh + 1) * d], q) + bias_ref[kvh]
        s = jnp.concatenate([s[0:WINDOW, :] + no_prev, s[WINDOW:, :]], axis=0)
        m = jnp.maximum(jnp.max(s, axis=0, keepdims=True), sink)
        p = jnp.exp2(s - m)
        denom = jnp.sum(p, axis=0, keepdims=True) + jnp.exp2(sink - m)
        o_t = _dot(v_t[kvh * d:(kvh + 1) * d, :], p.astype(BF16)) / denom
        outs += [o_t[:, g * WINDOW:(g + 1) * WINDOW] for g in range(SWA_GROUP)]
    o_ref[...] = jnp.concatenate(outs, axis=0).T.astype(o_ref.dtype)


def _swa_attention(proj, sinks, bias):
    b, s, _ = proj.shape
    q_w = N_SWA_HEADS * SWA_HEAD_DIM
    kv_w = 2 * N_SWA_KV_HEADS * SWA_HEAD_DIM
    kv_block = q_w // kv_w
    vmem = (2 * (2 * _nbytes((WINDOW, q_w), BF16) + 2 * _nbytes((WINDOW, kv_w), BF16)
                 + _nbytes(bias.shape, F32)) + 16 * 2**20)
    return pl.pallas_call(
        _swa_kernel,
        grid=(b, s // WINDOW),
        in_specs=[pl.BlockSpec(memory_space=pltpu.SMEM),
                  pl.BlockSpec((None, WINDOW, q_w), lambda bi, n: (bi, n, 0)),
                  pl.BlockSpec((None, WINDOW, kv_w), lambda bi, n: (bi, n, kv_block)),
                  pl.BlockSpec((None, WINDOW, kv_w), lambda bi, n: (bi, jnp.maximum(n - 1, 0), kv_block)),
                  pl.BlockSpec(bias.shape, lambda bi, n: (0, 0, 0))],
        out_specs=pl.BlockSpec((None, WINDOW, q_w), lambda bi, n: (bi, n, 0)),
        out_shape=jax.ShapeDtypeStruct((b, s, q_w), BF16),
        compiler_params=_cparams(("parallel", "arbitrary"), vmem),
        name="sliding_window_attention",
    )(sinks, proj, proj, proj, bias)


def _rope_table_kernel(pos_ref, invf_ref, o_ref):
    ang = pos_ref[...].astype(F32) * invf_ref[...]
    lane = lax.broadcasted_iota(jnp.int32, ang.shape, 1)
    half = QK_ROPE // 2
    sin_signed = jnp.where(lane < QK_ROPE + half, -jnp.sin(ang), jnp.sin(ang))
    o_ref[...] = jnp.where(lane < QK_ROPE, jnp.cos(ang), sin_signed)


def _rope_table(positions, tm=1024):
    m = positions.shape[0]
    half = QK_ROPE // 2
    inv_freq = ROPE_THETA ** (-jnp.arange(half, dtype=F32) / half)
    invf = jnp.tile(inv_freq, LANES // half).reshape(1, LANES)
    return pl.pallas_call(
        _rope_table_kernel,
        grid=(m // tm,),
        in_specs=[pl.BlockSpec((tm, 1), lambda i: (i, 0)), pl.BlockSpec((1, LANES), lambda i: (0, 0))],
        out_specs=pl.BlockSpec((tm, LANES), lambda i: (i, 0)),
        out_shape=jax.ShapeDtypeStruct((m, LANES), F32),
        compiler_params=_cparams(("parallel",), 32 * 2**20),
        name="rope_table",
    )(positions, invf)


def _apply_rope(x_and_partner, cs):
    z = x_and_partner * cs
    z = z + pltpu.roll(z, QK_ROPE, axis=1)
    lane = lax.broadcasted_iota(jnp.int32, z.shape, 1)
    return jnp.where(lane < QK_ROPE, z, 0.0)


def _mla_q_kernel(cq_ref, g_ref, w_ref, cs_ref, o_ref, cn_ref):
    @pl.when(pl.program_id(1) == 0)
    def _():
        cn_ref[...] = _rms(cq_ref[...], g_ref[...]).astype(BF16)

    y = _dot(cn_ref[...], w_ref[...]) * ((QK_NOPE + QK_ROPE) ** -0.5 * LOG2E)
    cs = cs_ref[...]
    parts = []
    for h in range(y.shape[1] // (2 * LANES)):
        yh = y[:, h * 2 * LANES:(h + 1) * 2 * LANES]
        parts += [yh[:, :QK_NOPE], _apply_rope(yh[:, QK_NOPE:], cs)]
    o_ref[...] = jnp.concatenate(parts, axis=1).astype(o_ref.dtype)


def _mla_q(proj, q_norm, w_uq_aug, cs, tm=512, heads_per_step=4):
    m = proj.shape[0]
    hw = 2 * LANES * heads_per_step
    n_heads = w_uq_aug.shape[1] // hw
    return pl.pallas_call(
        _mla_q_kernel,
        grid=(m // tm, n_heads),
        in_specs=[pl.BlockSpec((tm, Q_LORA), lambda i, h: (i, 1)),
                  pl.BlockSpec((1, Q_LORA), lambda i, h: (0, 0)),
                  pl.BlockSpec((Q_LORA, hw), lambda i, h: (0, h)),
                  pl.BlockSpec((tm, LANES), lambda i, h: (i, 0))],
        out_specs=pl.BlockSpec((tm, hw), lambda i, h: (i, h)),
        out_shape=jax.ShapeDtypeStruct((m, n_heads * hw), BF16),
        scratch_shapes=[pltpu.VMEM((tm, Q_LORA), BF16)],
        compiler_params=_cparams(("parallel", "arbitrary"), 32 * 2**20),
        name="mla_q_expand",
    )(proj, q_norm.reshape(1, Q_LORA), w_uq_aug, cs)


def _mla_kv_kernel(ckv_ref, g_ref, w_ref, kr_ref, cs_ref, k_ref, v_ref, cn_ref, kpe_ref):
    @pl.when(pl.program_id(1) == 0)
    def _():
        cn_ref[...] = _rms(ckv_ref[...], g_ref[...]).astype(BF16)
        kpe_ref[...] = _apply_rope(kr_ref[...], cs_ref[...])

    y = _dot(cn_ref[...], w_ref[...])
    kpe = kpe_ref[...]
    k_parts, v_parts = [], []
    for h in range(y.shape[1] // (2 * LANES)):
        yh = y[:, h * 2 * LANES:(h + 1) * 2 * LANES]
        k_parts += [yh[:, :QK_NOPE], kpe]
        v_parts.append(yh[:, QK_NOPE:])
    k_ref[...] = jnp.concatenate(k_parts, axis=1).astype(k_ref.dtype)
    v_ref[...] = jnp.concatenate(v_parts, axis=1).astype(v_ref.dtype)


def _mla_kv(proj, kv_norm, w_ukv, cs, tm=512, heads_per_step=4):
    m = proj.shape[0]
    hw = 2 * LANES * heads_per_step
    n_heads = w_ukv.shape[1] // hw
    return pl.pallas_call(
        _mla_kv_kernel,
        grid=(m // tm, n_heads),
        in_specs=[pl.BlockSpec((tm, KV_LORA), lambda i, h: (i, 4)),
                  pl.BlockSpec((1, KV_LORA), lambda i, h: (0, 0)),
                  pl.BlockSpec((KV_LORA, hw), lambda i, h: (0, h)),
                  pl.BlockSpec((tm, LANES), lambda i, h: (i, 10)),
                  pl.BlockSpec((tm, LANES), lambda i, h: (i, 0))],
        out_specs=[pl.BlockSpec((tm, hw), lambda i, h: (i, h)),
                   pl.BlockSpec((tm, hw // 2), lambda i, h: (i, h))],
        out_shape=[jax.ShapeDtypeStruct((m, n_heads * hw), BF16),
                   jax.ShapeDtypeStruct((m, n_heads * hw // 2), BF16)],
        scratch_shapes=[pltpu.VMEM((tm, KV_LORA), BF16), pltpu.VMEM((tm, LANES), F32)],
        compiler_params=_cparams(("parallel", "arbitrary"), 32 * 2**20),
        name="mla_kv_expand",
    )(proj, kv_norm.reshape(1, KV_LORA), w_ukv, proj, cs)


def _swap_halves(w):
    half = w.shape[-1] // 2
    return jnp.concatenate([w[..., half:], w[..., :half]], axis=-1)


def _mixer_out(x2, o, o_mem, w_out, b, s):
    return _matmul_res([o.reshape(b * s, o.shape[-1]), o_mem.reshape(b * s, MEM_W)],
                       w_out[0], w_out[1], x2, tm=1024, tn=512)


def _sb_layer(x2, mem_kv, g, w_in, w_out, b, s):
    proj = _norm_matmul(x2, g, w_in.astype(BF16), BF16, tm=IN_PROJ_ROWS, tn=512, scaled_cols=N_HEADS * HEAD_DIM,
                        col_scale=HEAD_DIM ** -0.5 * LOG2E).reshape(b, s, -1)
    o = _sb_attention(proj, N_HEADS)
    o_mem = _mem_attention(proj, 3 * N_HEADS * HEAD_DIM // MEM_W, mem_kv)
    return _mixer_out(x2, o, o_mem, w_out, b, s)


def _fox_layer(x2, mem_kv, g, w_in, b_f, w_out, b, s):
    qkv_w = 3 * N_HEADS * HEAD_DIM
    w_main = jnp.concatenate([w_in[:, :qkv_w], w_in[:, qkv_w + N_HEADS:]], axis=1).astype(BF16)
    w_gate = jnp.pad(w_in[:, qkv_w:qkv_w + N_HEADS], ((0, 0), (0, LANES - N_HEADS))).astype(BF16)
    proj = _norm_matmul(x2, g, w_main, BF16, tm=IN_PROJ_ROWS, tn=512, scaled_cols=N_HEADS * HEAD_DIM,
                        col_scale=HEAD_DIM ** -0.5 * LOG2E).reshape(b, s, -1)
    f_logit = _norm_matmul(x2, g, w_gate, F32, tm=512, tn=LANES).reshape(b, s, LANES)
    aug = _fox_gates(f_logit, jnp.pad(b_f, (0, LANES - N_HEADS)), N_HEADS)
    o = _flash_attention(proj, proj, proj, 0, N_HEADS, 2 * N_HEADS, HEAD_DIM, HEAD_DIM, N_HEADS, aug=aug)
    o_mem = _mem_attention(proj, qkv_w // MEM_W, mem_kv)
    return _mixer_out(x2, o, o_mem, w_out, b, s)


def _swa_layer(x2, mem_kv, g, rel_bias, w_in, sinks, w_out, b, s):
    proj = _norm_matmul(x2, g, w_in.astype(BF16), BF16, tm=IN_PROJ_ROWS, tn=512,
                        scaled_cols=N_SWA_HEADS * SWA_HEAD_DIM,
                        col_scale=SWA_HEAD_DIM ** -0.5 * LOG2E).reshape(b, s, -1)
    o = _swa_attention(proj, sinks, _t5_bias(rel_bias))
    q_w = N_SWA_HEADS * SWA_HEAD_DIM
    kv_w = 2 * N_SWA_KV_HEADS * SWA_HEAD_DIM
    o_mem = _mem_attention(proj, (q_w + kv_w) // MEM_W, mem_kv)
    return _mixer_out(x2, o, o_mem, w_out, b, s)


def _mla_layer(x2, mem_kv, g, positions, w_in, q_norm, w_uq, kv_norm, w_ukv, w_out, b, s):
    d = x2.shape[1]
    o1, o2, o3 = Q_LORA, Q_LORA + KV_LORA, Q_LORA + KV_LORA + QK_ROPE
    w_main = jnp.concatenate([w_in[:, o3:], w_in[:, :o3], _swap_halves(w_in[:, o2:o3])], axis=1).astype(BF16)
    proj = _norm_matmul(x2, g, w_main, F32, tm=512, tn=w_main.shape[1])
    cs = _rope_table(positions.reshape(b * s, 1))
    wq = w_uq.reshape(Q_LORA, N_HEADS, QK_NOPE + QK_ROPE)
    wq = jnp.concatenate([wq, _swap_halves(wq[:, :, QK_NOPE:])], axis=-1).reshape(Q_LORA, -1).astype(BF16)
    q = _mla_q(proj, q_norm, wq, cs).reshape(b, s, -1)
    k, v = _mla_kv(proj, kv_norm, w_ukv.astype(BF16), cs)
    k, v = k.reshape(b, s, -1), v.reshape(b, s, -1)
    o = _flash_attention(q, k, v, 0, 0, 0, 2 * LANES, HEAD_DIM, N_HEADS)
    o_mem = _mem_attention(proj.reshape(b, s, -1), 0, mem_kv)
    return _mixer_out(x2, o, o_mem, w_out, b, s)


def kernel(x, mem, positions, rel_bias, attn_norm, mem_norm, w_mem_kv, ffn_norm, ffn_w_up, ffn_conv_w, ffn_conv_b, ffn_w_down, final_norm, sb_w_in, sb_w_out, fox_w_in, fox_b_f, fox_w_out, swa_w_in, swa_sinks, swa_w_out, mla_w_in, mla_q_norm, mla_w_uq, mla_kv_norm, mla_w_ukv, mla_w_out):
    b, s, d = x.shape
    depth = attn_norm.shape[0]
    mem_len = mem.shape[1]
    x2 = x.reshape(b * s, d)
    mem2 = mem.reshape(b * mem_len, d)
    w_mem_kv_bf16 = w_mem_kv.astype(BF16)
    for i in range(depth):
        kind, j = i % 4, i // 4
        mem_kv = _norm_matmul(mem2, mem_norm[i], w_mem_kv_bf16, BF16, tm=b * mem_len, tn=2 * MEM_W,
                              layer=i).reshape(b, mem_len, 2 * MEM_W)
        g = attn_norm[i]
        if kind == 0:
            x2 = _sb_layer(x2, mem_kv, g, sb_w_in[j], (sb_w_out, j), b, s)
        elif kind == 1:
            x2 = _fox_layer(x2, mem_kv, g, fox_w_in[j], fox_b_f[j], (fox_w_out, j), b, s)
        elif kind == 2:
            x2 = _swa_layer(x2, mem_kv, g, rel_bias, swa_w_in[j], swa_sinks[j], (swa_w_out, j), b, s)
        else:
            x2 = _mla_layer(x2, mem_kv, g, positions, mla_w_in[j], mla_q_norm[j], mla_w_uq[j],
                            mla_kv_norm[j], mla_w_ukv[j], (mla_w_out, j), b, s)
        gated = _ffn_up(_rmsnorm(x2, ffn_norm[i], BF16, tm=512), ffn_w_up, i, ffn_conv_w[i], ffn_conv_b[i],
                        seq=s, tm=1024, tn=512)
        x2 = _matmul_res([gated], ffn_w_down, i, x2, tm=512, tn=512)
    return _rmsnorm(x2, final_norm, F32, tm=512).reshape(b, s, d)
```

```python
import functools
import math

import jax
import jax.numpy as jnp
from jax import lax
from jax.experimental import pallas as pl
from jax.experimental.pallas import tpu as pltpu

F32 = jnp.float32
BF16 = jnp.bfloat16

EPS = 1e-6
LOG2E = math.log2(math.e)
HEAD_DIM = 128
N_HEADS = 16
SWA_HEAD_DIM = 64
N_SWA_HEADS = 32
N_SWA_KV_HEADS = 4
SWA_GROUP = N_SWA_HEADS // N_SWA_KV_HEADS
WINDOW = 128
Q_LORA = 512
KV_LORA = 256
QK_NOPE = 128
QK_ROPE = 64
ROPE_THETA = 10000.0
MEM_HEADS = 4
MEM_W = MEM_HEADS * HEAD_DIM
NUM_BUCKETS = 32
MAX_DISTANCE = 128
CONV_WIDTH = 3

V7X_VMEM_BYTES = 64 * 1024 * 1024
VMEM_CAP = V7X_VMEM_BYTES - 8 * 1024 * 1024
LANES = 128
IN_PROJ_ROWS = 1024
TAIL = 8


def _cparams(sems, vmem_bytes):
    return pltpu.CompilerParams(dimension_semantics=sems,
                                vmem_limit_bytes=int(min(max(vmem_bytes, 16 * 2**20), VMEM_CAP)))


def _nbytes(shape, dtype):
    return math.prod(shape) * jnp.dtype(dtype).itemsize


def _rms(x, g):
    return x * lax.rsqrt(jnp.mean(x * x, axis=-1, keepdims=True) + EPS) * g


def _dot(a, b):
    return jnp.dot(a, b, preferred_element_type=F32)


def _dot_nt(a, b):
    return lax.dot_general(a, b, (((1,), (1,)), ((), ())), preferred_element_type=F32)


def _norm_matmul_kernel(x_ref, g_ref, w_ref, cs_ref, o_ref, xn_ref):
    @pl.when(pl.program_id(1) == 0)
    def _():
        xn_ref[...] = _rms(x_ref[...], g_ref[...]).astype(BF16)

    o_ref[...] = (_dot(xn_ref[...], w_ref[...]) * cs_ref[...]).astype(o_ref.dtype)


def _w_tile_spec(w, layer, k, tn, col_block):
    if w.ndim == 3:
        return pl.BlockSpec((None, k, tn), lambda i, j: (layer, 0, col_block(j)))
    return pl.BlockSpec((k, tn), lambda i, j: (0, col_block(j)))


def _norm_matmul(x, g, w, out_dtype, tm, tn, scaled_cols=0, col_scale=1.0, layer=None):
    m, k = x.shape
    n = w.shape[-1]
    cs = jnp.where(jnp.arange(n) < scaled_cols, col_scale, 1.0).astype(F32).reshape(1, n)
    vmem = (2 * (_nbytes((tm, k), F32) + _nbytes((k, tn), BF16) + _nbytes((tm, tn), out_dtype))
            + _nbytes((tm, k), BF16) + 2 * _nbytes((tm, k), F32) + _nbytes((tm, tn), F32))
    return pl.pallas_call(
        _norm_matmul_kernel,
        grid=(m // tm, n // tn),
        in_specs=[pl.BlockSpec((tm, k), lambda i, j: (i, 0)),
                  pl.BlockSpec((1, k), lambda i, j: (0, 0)),
                  _w_tile_spec(w, layer, k, tn, lambda j: j),
                  pl.BlockSpec((1, tn), lambda i, j: (0, j))],
        out_specs=pl.BlockSpec((tm, tn), lambda i, j: (i, j)),
        out_shape=jax.ShapeDtypeStruct((m, n), out_dtype),
        scratch_shapes=[pltpu.VMEM((tm, k), BF16)],
        compiler_params=_cparams(("parallel", "arbitrary"), vmem),
        name="norm_matmul",
    )(x, g.reshape(1, k), w, cs)


def _matmul_res_kernel(*refs, n_a):
    a_refs, w_refs = refs[:n_a], refs[n_a:2 * n_a]
    x_ref, o_ref = refs[2 * n_a], refs[2 * n_a + 1]
    wb_refs = refs[2 * n_a + 2:]

    @pl.when(pl.program_id(1) == 0)
    def _():
        for w_ref, wb_ref in zip(w_refs, wb_refs):
            wb_ref[...] = w_ref[...].astype(BF16)

    acc = x_ref[...]
    for a_ref, wb_ref in zip(a_refs, wb_refs):
        acc = acc + _dot(a_ref[...], wb_ref[...])
    o_ref[...] = acc


def _matmul_res(a_list, w, layer, x, tm, tn):
    m, n = x.shape
    vmem = 6 * _nbytes((tm, tn), F32)
    a_specs, w_specs, scratch = [], [], []
    row = 0
    for a in a_list:
        kp = a.shape[1]
        row_block = row // kp
        assert row_block * kp == row
        a_specs.append(pl.BlockSpec((tm, kp), lambda j, i: (i, 0)))
        w_specs.append(pl.BlockSpec((None, kp, tn), lambda j, i, row_block=row_block: (layer, row_block, j)))
        scratch.append(pltpu.VMEM((kp, tn), BF16))
        vmem += 2 * _nbytes((tm, kp), a.dtype) + 2 * _nbytes((kp, tn), F32) + 2 * _nbytes((kp, tn), BF16)
        row += kp
    assert row == w.shape[1]
    return pl.pallas_call(
        functools.partial(_matmul_res_kernel, n_a=len(a_list)),
        grid=(n // tn, m // tm),
        in_specs=a_specs + w_specs + [pl.BlockSpec((tm, tn), lambda j, i: (i, j))],
        out_specs=pl.BlockSpec((tm, tn), lambda j, i: (i, j)),
        out_shape=jax.ShapeDtypeStruct((m, n), F32),
        scratch_shapes=scratch,
        compiler_params=_cparams(("parallel", "arbitrary"), vmem),
        name="matmul_res",
    )(*a_list, *([w] * len(a_list)), x)


def _rmsnorm_kernel(x_ref, g_ref, o_ref):
    o_ref[...] = _rms(x_ref[...], g_ref[...]).astype(o_ref.dtype)


def _rmsnorm(x, g, out_dtype, tm):
    m, k = x.shape
    return pl.pallas_call(
        _rmsnorm_kernel,
        grid=(m // tm,),
        in_specs=[pl.BlockSpec((tm, k), lambda i: (i, 0)), pl.BlockSpec((1, k), lambda i: (0, 0))],
        out_specs=pl.BlockSpec((tm, k), lambda i: (i, 0)),
        out_shape=jax.ShapeDtypeStruct((m, k), out_dtype),
        compiler_params=_cparams(("parallel",), 8 * _nbytes((tm, k), F32)),
        name="rmsnorm",
    )(x, g.reshape(1, k))


def _ffn_up_kernel(xn_ref, wg_ref, wv_ref, cwg_ref, cwv_ref, cbg_ref, cbv_ref, o_ref,
                   wgb_ref, wvb_ref, tail_g_ref, tail_v_ref, *u_refs, rows, tiles_per_seq):
    i = pl.program_id(1)

    @pl.when(i == 0)
    def _():
        wgb_ref[...] = wg_ref[...].astype(BF16)
        wvb_ref[...] = wv_ref[...].astype(BF16)

    @pl.when(i % tiles_per_seq == 0)
    def _():
        tail_g_ref[...] = jnp.zeros(tail_g_ref.shape, F32)
        tail_v_ref[...] = jnp.zeros(tail_v_ref.shape, F32)

    n_chunks = len(u_refs) // 2

    for r in range(n_chunks):
        xr = xn_ref[r * rows:(r + 1) * rows, :]

        def conv(wb_ref, cw_ref, cb_ref, u_ref, before):
            u_ref[TAIL:, :] = _dot(xr, wb_ref[...])
            u_ref[0:TAIL, :] = before
            c = cb_ref[...]
            for tap in range(CONV_WIDTH):
                start = TAIL - (CONV_WIDTH - 1) + tap
                c = c + cw_ref[tap:tap + 1, :] * u_ref[start:start + rows, :]
            return c

        if r == 0:
            before_g, before_v = tail_g_ref[...], tail_v_ref[...]
        else:
            before_g, before_v = u_refs[2 * r - 2][rows:, :], u_refs[2 * r - 1][rows:, :]
        gate = conv(wgb_ref, cwg_ref, cbg_ref, u_refs[2 * r], before_g)
        val = conv(wvb_ref, cwv_ref, cbv_ref, u_refs[2 * r + 1], before_v)
        o_ref[r * rows:(r + 1) * rows, :] = (gate * (1.0 / (1.0 + jnp.exp(-gate))) * val).astype(o_ref.dtype)

    tail_g_ref[...] = u_refs[2 * n_chunks - 2][rows:, :]
    tail_v_ref[...] = u_refs[2 * n_chunks - 1][rows:, :]


def _ffn_up(xn, w_up, layer, conv_w, conv_b, seq, tm, tn, rows=256):
    m, k = xn.shape
    d_ff = w_up.shape[-1] // 2
    nj = d_ff // tn
    n_chunks = tm // rows
    vmem = (2 * (_nbytes((tm, k), BF16) + 2 * _nbytes((k, tn), F32) + _nbytes((tm, tn), BF16))
            + 2 * _nbytes((k, tn), BF16) + 2 * n_chunks * _nbytes((rows + TAIL, tn), F32)
            + 2 * _nbytes((k, tn), F32) + 6 * n_chunks * _nbytes((rows, tn), F32))
    return pl.pallas_call(
        functools.partial(_ffn_up_kernel, rows=rows, tiles_per_seq=seq // tm),
        grid=(nj, m // tm),
        in_specs=[pl.BlockSpec((tm, k), lambda j, i: (i, 0)),
                  pl.BlockSpec((None, k, tn), lambda j, i: (layer, 0, j)),
                  pl.BlockSpec((None, k, tn), lambda j, i: (layer, 0, nj + j)),
                  pl.BlockSpec((CONV_WIDTH, tn), lambda j, i: (0, j)),
                  pl.BlockSpec((CONV_WIDTH, tn), lambda j, i: (0, nj + j)),
                  pl.BlockSpec((1, tn), lambda j, i: (0, j)),
                  pl.BlockSpec((1, tn), lambda j, i: (0, nj + j))],
        out_specs=pl.BlockSpec((tm, tn), lambda j, i: (i, j)),
        out_shape=jax.ShapeDtypeStruct((m, d_ff), BF16),
        scratch_shapes=[pltpu.VMEM((k, tn), BF16), pltpu.VMEM((k, tn), BF16),
                        pltpu.VMEM((TAIL, tn), F32), pltpu.VMEM((TAIL, tn), F32)]
        + [pltpu.VMEM((rows + TAIL, tn), F32)] * (2 * n_chunks),
        compiler_params=_cparams(("arbitrary", "arbitrary"), vmem),
        name="ffn_up_conv_gate",
    )(xn, w_up, w_up, conv_w, conv_w, conv_b.reshape(1, -1), conv_b.reshape(1, -1))


def _eye(n):
    return jnp.where(lax.broadcasted_iota(jnp.int32, (n, n), 0) == lax.broadcasted_iota(jnp.int32, (n, n), 1),
                     1.0, 0.0).astype(BF16)


def _transpose_bf16(x):
    return _dot_nt(_eye(x.shape[1]), x)


def _fill_v_transposed(v_ref, vt_ref, t):
    s, dv = v_ref.shape
    for c in range(s // t):
        vt_ref[0:dv, c * t:(c + 1) * t] = _transpose_bf16(v_ref[c * t:(c + 1) * t, :]).astype(BF16)
    if vt_ref.shape[0] > dv:
        vt_ref[dv:, :] = jnp.ones((vt_ref.shape[0] - dv, s), BF16)


def _flash_kernel(*refs, t, dk, dv, group, has_aug):
    if has_aug:
        q_ref, k_ref, v_ref, qa_ref, ka_ref, o_ref, vt_ref, m_ref, acc_ref, s_ref = refs
    else:
        q_ref, k_ref, v_ref, o_ref, vt_ref, m_ref, acc_ref, s_ref = refs
    qi = pl.program_id(2)
    heads = range(group)

    @pl.when(qi == 0)
    def _():
        for g in heads:
            _fill_v_transposed(v_ref.at[:, g * dv:(g + 1) * dv], vt_ref.at[g], t)

    qs = []
    for g in heads:
        q = q_ref[:, g * dk:(g + 1) * dk]
        qs.append(jnp.concatenate([q, qa_ref[g]], axis=1) if has_aug else q)
    m_ref[...] = jnp.full(m_ref.shape, -jnp.inf, F32)
    acc_ref[...] = jnp.zeros(acc_ref.shape, F32)

    def scores(g, kb):
        ks = pl.multiple_of(kb * t, t)
        k = k_ref[pl.ds(ks, t), g * dk:(g + 1) * dk]
        if has_aug:
            k = jnp.concatenate([k, ka_ref[g, pl.ds(ks, t), :]], axis=1)
        return _dot_nt(k, qs[g])

    def col_max(s):
        return jnp.max(s, axis=0, keepdims=True)

    def update(g, s, s_max, kb):
        m_prev = m_ref[g]
        m_new = jnp.maximum(m_prev, s_max)
        m_ref[g] = m_new
        p = jnp.exp2(s - m_new).astype(BF16)
        ks = pl.multiple_of(kb * t, t)
        acc_ref[g] = jnp.exp2(m_prev - m_new) * acc_ref[g] + _dot(vt_ref[g, :, pl.ds(ks, t)], p)

    def body(kb, s_max):
        nxt = []
        for g in heads:
            update(g, s_ref[g], s_max[g], kb)
            s_next = scores(g, kb + 1)
            nxt.append(col_max(s_next))
            s_ref[g] = s_next
        return tuple(nxt)

    first = []
    for g in heads:
        s0 = scores(g, 0)
        first.append(col_max(s0))
        s_ref[g] = s0
    lax.fori_loop(0, qi, body, tuple(first))
    key = lax.broadcasted_iota(jnp.int32, (t, t), 0)
    qry = lax.broadcasted_iota(jnp.int32, (t, t), 1)
    outs = []
    for g in heads:
        s = jnp.where(key <= qry, s_ref[g], -jnp.inf)
        update(g, s, col_max(s), qi)
        acc = acc_ref[g]
        outs.append((acc[0:dv, :] / acc[dv:dv + 1, :]).T)
    o_ref[...] = jnp.concatenate(outs, axis=1).astype(o_ref.dtype)


def _flash_attention(q_arr, k_arr, v_arr, q_off, k_off, v_off, dk, dv, n_heads, aug=None, t=512, group=2):
    b, s, _ = q_arr.shape
    assert q_off % group == 0 and k_off % group == 0 and v_off % group == 0 and n_heads % group == 0
    qo, ko, vo = q_off // group, k_off // group, v_off // group
    in_specs = [pl.BlockSpec((None, t, group * dk), lambda bi, h, qi: (bi, qi, qo + h)),
                pl.BlockSpec((None, s, group * dk), lambda bi, h, qi: (bi, 0, ko + h)),
                pl.BlockSpec((None, s, group * dv), lambda bi, h, qi: (bi, 0, vo + h))]
    args = [q_arr, k_arr, v_arr]
    vmem = 2 * group * (_nbytes((t, dk), BF16) + _nbytes((s, dk), BF16) + _nbytes((s, dv), BF16)
                        + _nbytes((t, dv), BF16))
    if aug is not None:
        in_specs += [pl.BlockSpec((None, group, t, LANES), lambda bi, h, qi: (bi, h, qi, 0)),
                     pl.BlockSpec((None, group, s, LANES), lambda bi, h, qi: (bi, h, 0, 0))]
        args += list(aug)
        vmem += 2 * group * (_nbytes((t, LANES), BF16) + _nbytes((s, LANES), BF16))
    acc_rows = dv + 16
    vmem += group * (_nbytes((acc_rows, s), BF16) + 2 * _nbytes((acc_rows, t), F32) + 8 * _nbytes((t, t), F32))
    return pl.pallas_call(
        functools.partial(_flash_kernel, t=t, dk=dk, dv=dv, group=group, has_aug=aug is not None),
        grid=(b, n_heads // group, s // t),
        in_specs=in_specs,
        out_specs=pl.BlockSpec((None, t, group * dv), lambda bi, h, qi: (bi, qi, h)),
        out_shape=jax.ShapeDtypeStruct((b, s, n_heads * dv), BF16),
        scratch_shapes=[pltpu.VMEM((group, acc_rows, s), BF16), pltpu.VMEM((group, 1, t), F32),
                        pltpu.VMEM((group, acc_rows, t), F32), pltpu.VMEM((group, t, t), F32)],
        compiler_params=_cparams(("parallel", "parallel", "arbitrary"), vmem),
        name="flash_attention",
    )(*args)


def _sb_kernel(q_ref, k_ref, v_ref, o_ref, vt_ref, carry_ref, acc_ref, s_ref, a_ref, *, t, sub, d, group):
    qi = pl.program_id(2)
    heads = range(group)

    @pl.when(qi == 0)
    def _():
        for g in heads:
            _fill_v_transposed(v_ref.at[:, g * d:(g + 1) * d], vt_ref.at[g], t)

    carry_ref[...] = jnp.zeros(carry_ref.shape, F32)
    acc_ref[...] = jnp.zeros(acc_ref.shape, F32)
    qs = [q_ref[:, g * d:(g + 1) * d] for g in heads]
    key = lax.broadcasted_iota(jnp.int32, (sub, t), 0)
    qry = lax.broadcasted_iota(jnp.int32, (sub, t), 1)
    r = lax.broadcasted_iota(jnp.int32, (sub + 16, sub), 0)
    c = lax.broadcasted_iota(jnp.int32, (sub + 16, sub), 1)
    suffix = jnp.where(((c > r) & (r < sub)) | (r == sub), 1.0, 0.0).astype(BF16)
    suffix = jnp.concatenate([suffix, suffix], axis=1)

    def scores(g, kb):
        ks = pl.multiple_of(kb * t, t)
        return _dot_nt(k_ref[pl.ds(ks, t), g * d:(g + 1) * d], qs[g])

    def weights(g, s, on_diagonal):
        later = carry_ref[g]
        out = [None] * (t // sub)
        for i in reversed(range(t // sub)):
            z = s[i * sub:(i + 1) * sub, :]
            neg_abs = lax.bitcast_convert_type(
                lax.bitcast_convert_type(z, jnp.uint32) | jnp.uint32(0x80000000), F32)
            log_beta = jnp.minimum(z, 0.0) - jnp.log2(1.0 + jnp.exp2(neg_abs))
            log_keep = log_beta - z
            if on_diagonal:
                strict = key + i * sub < qry
                log_keep = jnp.where(strict, log_keep, 0.0)
            hi = log_keep.astype(BF16)
            lo = (log_keep - hi.astype(F32)).astype(BF16)
            sums = _dot(suffix, jnp.concatenate([hi, lo], axis=0))
            a = jnp.exp2(log_beta + (sums[0:sub, :] + later))
            if on_diagonal:
                a = jnp.where(strict, a, 0.0)
            out[i] = a.astype(BF16)
            later = later + sums[sub:sub + 1, :]
        carry_ref[g] = later
        return jnp.concatenate(out, axis=0)

    def accumulate(g, a, kb):
        ks = pl.multiple_of(kb * t, t)
        acc_ref[g] += _dot(vt_ref[g, :, pl.ds(ks, t)], a)

    def finish():
        o_ref[...] = jnp.concatenate([acc_ref[g].T for g in heads], axis=1).astype(o_ref.dtype)

    @pl.when(qi == 0)
    def _():
        for g in heads:
            accumulate(g, weights(g, scores(g, 0), True), 0)
        finish()

    @pl.when(qi > 0)
    def _():
        for g in heads:
            a_ref[g] = weights(g, scores(g, qi), True)
            s_ref[g] = scores(g, qi - 1)

        def body(i, carry):
            for g in heads:
                accumulate(g, a_ref[g], qi - i)
                a_ref[g] = weights(g, s_ref[g], False)
                s_ref[g] = scores(g, qi - i - 2)
            return carry

        lax.fori_loop(0, qi - 1, body, 0)
        for g in heads:
            accumulate(g, a_ref[g], 1)
            accumulate(g, weights(g, s_ref[g], False), 0)
        finish()


def _sb_attention(proj, n_heads, t=512, group=2):
    b, s, _ = proj.shape
    d = HEAD_DIM
    assert n_heads % group == 0
    blocks = n_heads // group
    vmem = group * (2 * (2 * _nbytes((t, d), BF16) + 2 * _nbytes((s, d), BF16))
                    + _nbytes((d, s), BF16) + 2 * _nbytes((d, t), F32) + 12 * _nbytes((t, t), F32))
    return pl.pallas_call(
        functools.partial(_sb_kernel, t=t, sub=LANES, d=d, group=group),
        grid=(b, blocks, s // t),
        in_specs=[pl.BlockSpec((None, t, group * d), lambda bi, h, qi: (bi, qi, h)),
                  pl.BlockSpec((None, s, group * d), lambda bi, h, qi: (bi, 0, blocks + h)),
                  pl.BlockSpec((None, s, group * d), lambda bi, h, qi: (bi, 0, 2 * blocks + h))],
        out_specs=pl.BlockSpec((None, t, group * d), lambda bi, h, qi: (bi, qi, h)),
        out_shape=jax.ShapeDtypeStruct((b, s, n_heads * d), BF16),
        scratch_shapes=[pltpu.VMEM((group, d, s), BF16), pltpu.VMEM((group, 1, t), F32),
                        pltpu.VMEM((group, d, t), F32), pltpu.VMEM((group, t, t), F32),
                        pltpu.VMEM((group, t, t), BF16)],
        compiler_params=_cparams(("parallel", "parallel", "arbitrary"), vmem),
        name="stick_breaking_attention",
    )(proj, proj, proj)


def _split3(x):
    hi = x.astype(BF16)
    rest = x - hi.astype(F32)
    mid = rest.astype(BF16)
    lo = (rest - mid.astype(F32)).astype(BF16)
    return hi, mid, lo


def _fox_gate_kernel(fl_ref, b_ref, pq_ref, pk_ref, oq_ref, ok_ref, qa_ref, ka_ref, carry_ref):
    ts = fl_ref.shape[0]

    @pl.when(pl.program_id(1) == 0)
    def _():
        carry_ref[...] = jnp.zeros(carry_ref.shape, F32)

    z = fl_ref[...] + b_ref[...]
    log_f = jnp.minimum(z, 0.0) - jnp.log(1.0 + jnp.exp(-jnp.abs(z)))
    row = lax.broadcasted_iota(jnp.int32, (ts, ts), 0)
    col = lax.broadcasted_iota(jnp.int32, (ts, ts), 1)
    prefix = jnp.where(col <= row, 1.0, 0.0).astype(BF16)
    hi, mid, lo = _split3(log_f)
    c = _dot(prefix, hi) + _dot(prefix, mid) + _dot(prefix, lo) + carry_ref[0:1, :]
    carry_ref[0:1, :] = c[ts - 1:ts, :]
    parts = jnp.concatenate(_split3(c * LOG2E), axis=1)
    qa = (_dot(parts, pq_ref[...]) + oq_ref[...]).astype(BF16)
    ka = (_dot(parts, pk_ref[...]) + ok_ref[...]).astype(BF16)
    for h in range(qa_ref.shape[0]):
        qa_ref[h] = qa[:, h * LANES:(h + 1) * LANES]
        ka_ref[h] = ka[:, h * LANES:(h + 1) * LANES]


def _fox_aug_tables(n_heads):
    part, src = jnp.arange(3 * LANES) // LANES, jnp.arange(3 * LANES) % LANES
    head, lane = jnp.arange(n_heads * LANES) // LANES, jnp.arange(n_heads * LANES) % LANES
    mine = src[:, None] == head[None, :]
    pq = jnp.where(mine & (lane[None, :] == part[:, None] + 3), 1.0, 0.0).astype(BF16)
    pk = jnp.where(mine & (lane[None, :] == part[:, None]), -1.0, 0.0).astype(BF16)
    oq = jnp.where(lane < 3, 1.0, 0.0).astype(F32)[None]
    ok = jnp.where((lane >= 3) & (lane < 6), 1.0, 0.0).astype(F32)[None]
    return pq, pk, oq, ok


def _fox_gates(f_logit, b_f, n_heads, ts=256):
    b, s, w = f_logit.shape
    hw = n_heads * LANES
    out = jax.ShapeDtypeStruct((b, n_heads, s, LANES), BF16)
    const = lambda bi, i: (0, 0)
    return pl.pallas_call(
        _fox_gate_kernel,
        grid=(b, s // ts),
        in_specs=[pl.BlockSpec((None, ts, w), lambda bi, i: (bi, i, 0)),
                  pl.BlockSpec((1, w), const),
                  pl.BlockSpec((3 * LANES, hw), const), pl.BlockSpec((3 * LANES, hw), const),
                  pl.BlockSpec((1, hw), const), pl.BlockSpec((1, hw), const)],
        out_specs=[pl.BlockSpec((None, n_heads, ts, LANES), lambda bi, i: (bi, 0, i, 0)),
                   pl.BlockSpec((None, n_heads, ts, LANES), lambda bi, i: (bi, 0, i, 0))],
        out_shape=[out, out],
        scratch_shapes=[pltpu.VMEM((8, w), F32)],
        compiler_params=_cparams(("parallel", "arbitrary"), 32 * 2**20),
        name="fox_gate_cumsum",
    )(f_logit, b_f.reshape(1, w), *_fox_aug_tables(n_heads))


def _mem_attn_kernel(q_ref, kv_ref, o_ref, *, scale):
    outs = []
    for h in range(MEM_HEADS):
        q = q_ref[:, h * HEAD_DIM:(h + 1) * HEAD_DIM].astype(BF16)
        k = kv_ref[:, h * HEAD_DIM:(h + 1) * HEAD_DIM]
        v = kv_ref[:, MEM_W + h * HEAD_DIM:MEM_W + (h + 1) * HEAD_DIM]
        s = _dot_nt(q, k) * scale
        p = jnp.exp(s - jnp.max(s, axis=1, keepdims=True))
        o = _dot(p.astype(BF16), v) / jnp.sum(p, axis=1, keepdims=True)
        outs.append(o.astype(o_ref.dtype))
    o_ref[...] = jnp.concatenate(outs, axis=1)


def _mem_attention(q_arr, q_block, mem_kv, tq=512):
    b, s, _ = q_arr.shape
    length = mem_kv.shape[1]
    vmem = (2 * (_nbytes((tq, MEM_W), q_arr.dtype) + _nbytes((length, 2 * MEM_W), BF16)
                 + _nbytes((tq, MEM_W), BF16)) + 8 * _nbytes((tq, length), F32))
    return pl.pallas_call(
        functools.partial(_mem_attn_kernel, scale=HEAD_DIM ** -0.5),
        grid=(b, s // tq),
        in_specs=[pl.BlockSpec((None, tq, MEM_W), lambda bi, i: (bi, i, q_block)),
                  pl.BlockSpec((None, length, 2 * MEM_W), lambda bi, i: (bi, 0, 0))],
        out_specs=pl.BlockSpec((None, tq, MEM_W), lambda bi, i: (bi, i, 0)),
        out_shape=jax.ShapeDtypeStruct((b, s, MEM_W), BF16),
        compiler_params=_cparams(("parallel", "parallel"), vmem),
        name="memory_attention",
    )(q_arr, mem_kv)


def _t5_bucket_table():
    max_exact = NUM_BUCKETS // 2
    kj = jnp.arange(2 * WINDOW)[:, None]
    qi = jnp.arange(WINDOW)[None, :]
    signed = WINDOW + qi - kj
    dist = jnp.maximum(signed, 0)
    d = jnp.maximum(dist, 1).astype(F32)
    large = max_exact + (jnp.log(d / max_exact) / math.log(MAX_DISTANCE / max_exact)
                         * (NUM_BUCKETS - max_exact)).astype(jnp.int32)
    bucket = jnp.where(dist < max_exact, dist, jnp.minimum(large, NUM_BUCKETS - 1))
    return jnp.where((signed >= 0) & (signed < WINDOW), bucket, -1).astype(jnp.int32)


def _t5_bias_kernel(rb_ref, bucket_ref, o_ref):
    kvh = pl.program_id(0)
    bucket = bucket_ref[...]
    for g in range(SWA_GROUP):
        bias = jnp.full(bucket.shape, -jnp.inf, F32)
        for b in range(NUM_BUCKETS):
            bias = jnp.where(bucket == b, rb_ref[b, kvh * SWA_GROUP + g] * LOG2E, bias)
        o_ref[:, g * WINDOW:(g + 1) * WINDOW] = bias


def _t5_bias(rel_bias):
    return pl.pallas_call(
        _t5_bias_kernel,
        grid=(N_SWA_KV_HEADS,),
        in_specs=[pl.BlockSpec(memory_space=pltpu.SMEM),
                  pl.BlockSpec((2 * WINDOW, WINDOW), lambda h: (0, 0))],
        out_specs=pl.BlockSpec((None, 2 * WINDOW, SWA_GROUP * WINDOW), lambda h: (h, 0, 0)),
        out_shape=jax.ShapeDtypeStruct((N_SWA_KV_HEADS, 2 * WINDOW, SWA_GROUP * WINDOW), F32),
        compiler_params=_cparams(("parallel",), 16 * 2**20),
        name="t5_bias",
    )(rel_bias, _t5_bucket_table())


def _swa_kernel(sink_ref, q_ref, kvc_ref, kvp_ref, bias_ref, o_ref):
    n = pl.program_id(1)
    d = SWA_HEAD_DIM
    kv_w = N_SWA_KV_HEADS * d
    k_win = jnp.concatenate([kvp_ref[:, 0:kv_w], kvc_ref[:, 0:kv_w]], axis=0)
    v_win = jnp.concatenate([kvp_ref[:, kv_w:2 * kv_w], kvc_ref[:, kv_w:2 * kv_w]], axis=0)
    v_t = v_win.astype(F32).T.astype(BF16)
    no_prev = jnp.where(n == 0, -jnp.inf, 0.0)
    outs = []
    for kvh in range(N_SWA_KV_HEADS):
        heads = range(kvh * SWA_GROUP, (kvh + 1) * SWA_GROUP)
        q = jnp.concatenate([q_ref[:, h * d:(h + 1) * d] for h in heads], axis=0)
        sink = jnp.concatenate([jnp.full((1, WINDOW), sink_ref[h] * LOG2E, F32) for h in heads], axis=1)
        s = _dot_nt(k_win[:, kvh * d:(kvh + 1) * d], q) + bias_ref[kvh]
        s = jnp.concatenate([s[0:WINDOW, :] + no_prev, s[WINDOW:, :]], axis=0)
        m = jnp.maximum(jnp.max(s, axis=0, keepdims=True), sink)
        p = jnp.exp2(s - m)
        denom = jnp.sum(p, axis=0, keepdims=True) + jnp.exp2(sink - m)
        o_t = _dot(v_t[kvh * d:(kvh + 1) * d, :], p.astype(BF16)) / denom
        outs += [o_t[:, g * WINDOW:(g + 1) * WINDOW] for g in range(SWA_GROUP)]
    o_ref[...] = jnp.concatenate(outs, axis=0).T.astype(o_ref.dtype)


def _swa_attention(proj, sinks, bias):
    b, s, _ = proj.shape
    q_w = N_SWA_HEADS * SWA_HEAD_DIM
    kv_w = 2 * N_SWA_KV_HEADS * SWA_HEAD_DIM
    kv_block = q_w // kv_w
    vmem = (2 * (2 * _nbytes((WINDOW, q_w), BF16) + 2 * _nbytes((WINDOW, kv_w), BF16)
                 + _nbytes(bias.shape, F32)) + 16 * 2**20)
    return pl.pallas_call(
        _swa_kernel,
        grid=(b, s // WINDOW),
        in_specs=[pl.BlockSpec(memory_space=pltpu.SMEM),
                  pl.BlockSpec((None, WINDOW, q_w), lambda bi, n: (bi, n, 0)),
                  pl.BlockSpec((None, WINDOW, kv_w), lambda bi, n: (bi, n, kv_block)),
                  pl.BlockSpec((None, WINDOW, kv_w), lambda bi, n: (bi, jnp.maximum(n - 1, 0), kv_block)),
                  pl.BlockSpec(bias.shape, lambda bi, n: (0, 0, 0))],
        out_specs=pl.BlockSpec((None, WINDOW, q_w), lambda bi, n: (bi, n, 0)),
        out_shape=jax.ShapeDtypeStruct((b, s, q_w), BF16),
        compiler_params=_cparams(("parallel", "arbitrary"), vmem),
        name="sliding_window_attention",
    )(sinks, proj, proj, proj, bias)


def _rope_table_kernel(pos_ref, invf_ref, o_ref):
    ang = pos_ref[...].astype(F32) * invf_ref[...]
    lane = lax.broadcasted_iota(jnp.int32, ang.shape, 1)
    half = QK_ROPE // 2
    sin_signed = jnp.where(lane < QK_ROPE + half, -jnp.sin(ang), jnp.sin(ang))
    o_ref[...] = jnp.where(lane < QK_ROPE, jnp.cos(ang), sin_signed)


def _rope_table(positions, tm=1024):
    m = positions.shape[0]
    half = QK_ROPE // 2
    inv_freq = ROPE_THETA ** (-jnp.arange(half, dtype=F32) / half)
    invf = jnp.tile(inv_freq, LANES // half).reshape(1, LANES)
    return pl.pallas_call(
        _rope_table_kernel,
        grid=(m // tm,),
        in_specs=[pl.BlockSpec((tm, 1), lambda i: (i, 0)), pl.BlockSpec((1, LANES), lambda i: (0, 0))],
        out_specs=pl.BlockSpec((tm, LANES), lambda i: (i, 0)),
        out_shape=jax.ShapeDtypeStruct((m, LANES), F32),
        compiler_params=_cparams(("parallel",), 32 * 2**20),
        name="rope_table",
    )(positions, invf)


def _apply_rope(x_and_partner, cs):
    z = x_and_partner * cs
    z = z + pltpu.roll(z, QK_ROPE, axis=1)
    lane = lax.broadcasted_iota(jnp.int32, z.shape, 1)
    return jnp.where(lane < QK_ROPE, z, 0.0)


def _mla_q_kernel(cq_ref, g_ref, w_ref, cs_ref, o_ref, cn_ref):
    @pl.when(pl.program_id(1) == 0)
    def _():
        cn_ref[...] = _rms(cq_ref[...], g_ref[...]).astype(BF16)

    y = _dot(cn_ref[...], w_ref[...]) * ((QK_NOPE + QK_ROPE) ** -0.5 * LOG2E)
    cs = cs_ref[...]
    parts = []
    for h in range(y.shape[1] // (2 * LANES)):
        yh = y[:, h * 2 * LANES:(h + 1) * 2 * LANES]
        parts += [yh[:, :QK_NOPE], _apply_rope(yh[:, QK_NOPE:], cs)]
    o_ref[...] = jnp.concatenate(parts, axis=1).astype(o_ref.dtype)


def _mla_q(proj, q_norm, w_uq_aug, cs, tm=512, heads_per_step=4):
    m = proj.shape[0]
    hw = 2 * LANES * heads_per_step
    n_heads = w_uq_aug.shape[1] // hw
    return pl.pallas_call(
        _mla_q_kernel,
        grid=(m // tm, n_heads),
        in_specs=[pl.BlockSpec((tm, Q_LORA), lambda i, h: (i, 1)),
                  pl.BlockSpec((1, Q_LORA), lambda i, h: (0, 0)),
                  pl.BlockSpec((Q_LORA, hw), lambda i, h: (0, h)),
                  pl.BlockSpec((tm, LANES), lambda i, h: (i, 0))],
        out_specs=pl.BlockSpec((tm, hw), lambda i, h: (i, h)),
        out_shape=jax.ShapeDtypeStruct((m, n_heads * hw), BF16),
        scratch_shapes=[pltpu.VMEM((tm, Q_LORA), BF16)],
        compiler_params=_cparams(("parallel", "arbitrary"), 32 * 2**20),
        name="mla_q_expand",
    )(proj, q_norm.reshape(1, Q_LORA), w_uq_aug, cs)


def _mla_kv_kernel(ckv_ref, g_ref, w_ref, kr_ref, cs_ref, k_ref, v_ref, cn_ref, kpe_ref):
    @pl.when(pl.program_id(1) == 0)
    def _():
        cn_ref[...] = _rms(ckv_ref[...], g_ref[...]).astype(BF16)
        kpe_ref[...] = _apply_rope(kr_ref[...], cs_ref[...])

    y = _dot(cn_ref[...], w_ref[...])
    kpe = kpe_ref[...]
    k_parts, v_parts = [], []
    for h in range(y.shape[1] // (2 * LANES)):
        yh = y[:, h * 2 * LANES:(h + 1) * 2 * LANES]
        k_parts += [yh[:, :QK_NOPE], kpe]
        v_parts.append(yh[:, QK_NOPE:])
    k_ref[...] = jnp.concatenate(k_parts, axis=1).astype(k_ref.dtype)
    v_ref[...] = jnp.concatenate(v_parts, axis=1).astype(v_ref.dtype)


def _mla_kv(proj, kv_norm, w_ukv, cs, tm=512, heads_per_step=4):
    m = proj.shape[0]
    hw = 2 * LANES * heads_per_step
    n_heads = w_ukv.shape[1] // hw
    return pl.pallas_call(
        _mla_kv_kernel,
        grid=(m // tm, n_heads),
        in_specs=[pl.BlockSpec((tm, KV_LORA), lambda i, h: (i, 4)),
                  pl.BlockSpec((1, KV_LORA), lambda i, h: (0, 0)),
                  pl.BlockSpec((KV_LORA, hw), lambda i, h: (0, h)),
                  pl.BlockSpec((tm, LANES), lambda i, h: (i, 10)),
                  pl.BlockSpec((tm, LANES), lambda i, h: (i, 0))],
        out_specs=[pl.BlockSpec((tm, hw), lambda i, h: (i, h)),
                   pl.BlockSpec((tm, hw // 2), lambda i, h: (i, h))],
        out_shape=[jax.ShapeDtypeStruct((m, n_heads * hw), BF16),
                   jax.ShapeDtypeStruct((m, n_heads * hw // 2), BF16)],
        scratch_shapes=[pltpu.VMEM((tm, KV_LORA), BF16), pltpu.VMEM((tm, LANES), F32)],
        compiler_params=_cparams(("parallel", "arbitrary"), 32 * 2**20),
        name="mla_kv_expand",
    )(proj, kv_norm.reshape(1, KV_LORA), w_ukv, proj, cs)


def _swap_halves(w):
    half = w.shape[-1] // 2
    return jnp.concatenate([w[..., half:], w[..., :half]], axis=-1)


def _mixer_out(x2, o, o_mem, w_out, b, s):
    return _matmul_res([o.reshape(b * s, o.shape[-1]), o_mem.reshape(b * s, MEM_W)],
                       w_out[0], w_out[1], x2, tm=1024, tn=512)


def _sb_layer(x2, mem_kv, g, w_in, w_out, b, s):
    proj = _norm_matmul(x2, g, w_in.astype(BF16), BF16, tm=IN_PROJ_ROWS, tn=512, scaled_cols=N_HEADS * HEAD_DIM,
                        col_scale=HEAD_DIM ** -0.5 * LOG2E).reshape(b, s, -1)
    o = _sb_attention(proj, N_HEADS)
    o_mem = _mem_attention(proj, 3 * N_HEADS * HEAD_DIM // MEM_W, mem_kv)
    return _mixer_out(x2, o, o_mem, w_out, b, s)


def _fox_layer(x2, mem_kv, g, w_in, b_f, w_out, b, s):
    qkv_w = 3 * N_HEADS * HEAD_DIM
    w_main = jnp.concatenate([w_in[:, :qkv_w], w_in[:, qkv_w + N_HEADS:]], axis=1).astype(BF16)
    w_gate = jnp.pad(w_in[:, qkv_w:qkv_w + N_HEADS], ((0, 0), (0, LANES - N_HEADS))).astype(BF16)
    proj = _norm_matmul(x2, g, w_main, BF16, tm=IN_PROJ_ROWS, tn=512, scaled_cols=N_HEADS * HEAD_DIM,
                        col_scale=HEAD_DIM ** -0.5 * LOG2E).reshape(b, s, -1)
    f_logit = _norm_matmul(x2, g, w_gate, F32, tm=512, tn=LANES).reshape(b, s, LANES)
    aug = _fox_gates(f_logit, jnp.pad(b_f, (0, LANES - N_HEADS)), N_HEADS)
    o = _flash_attention(proj, proj, proj, 0, N_HEADS, 2 * N_HEADS, HEAD_DIM, HEAD_DIM, N_HEADS, aug=aug)
    o_mem = _mem_attention(proj, qkv_w // MEM_W, mem_kv)
    return _mixer_out(x2, o, o_mem, w_out, b, s)


def _swa_layer(x2, mem_kv, g, rel_bias, w_in, sinks, w_out, b, s):
    proj = _norm_matmul(x2, g, w_in.astype(BF16), BF16, tm=IN_PROJ_ROWS, tn=512,
                        scaled_cols=N_SWA_HEADS * SWA_HEAD_DIM,
                        col_scale=SWA_HEAD_DIM ** -0.5 * LOG2E).reshape(b, s, -1)
    o = _swa_attention(proj, sinks, _t5_bias(rel_bias))
    q_w = N_SWA_HEADS * SWA_HEAD_DIM
    kv_w = 2 * N_SWA_KV_HEADS * SWA_HEAD_DIM
    o_mem = _mem_attention(proj, (q_w + kv_w) // MEM_W, mem_kv)
    return _mixer_out(x2, o, o_mem, w_out, b, s)


def _mla_layer(x2, mem_kv, g, positions, w_in, q_norm, w_uq, kv_norm, w_ukv, w_out, b, s):
    d = x2.shape[1]
    o1, o2, o3 = Q_LORA, Q_LORA + KV_LORA, Q_LORA + KV_LORA + QK_ROPE
    w_main = jnp.concatenate([w_in[:, o3:], w_in[:, :o3], _swap_halves(w_in[:, o2:o3])], axis=1).astype(BF16)
    proj = _norm_matmul(x2, g, w_main, F32, tm=512, tn=w_main.shape[1])
    cs = _rope_table(positions.reshape(b * s, 1))
    wq = w_uq.reshape(Q_LORA, N_HEADS, QK_NOPE + QK_ROPE)
    wq = jnp.concatenate([wq, _swap_halves(wq[:, :, QK_NOPE:])], axis=-1).reshape(Q_LORA, -1).astype(BF16)
    q = _mla_q(proj, q_norm, wq, cs).reshape(b, s, -1)
    k, v = _mla_kv(proj, kv_norm, w_ukv.astype(BF16), cs)
    k, v = k.reshape(b, s, -1), v.reshape(b, s, -1)
    o = _flash_attention(q, k, v, 0, 0, 0, 2 * LANES, HEAD_DIM, N_HEADS)
    o_mem = _mem_attention(proj.reshape(b, s, -1), 0, mem_kv)
    return _mixer_out(x2, o, o_mem, w_out, b, s)


def kernel(x, mem, positions, rel_bias, attn_norm, mem_norm, w_mem_kv, ffn_norm, ffn_w_up, ffn_conv_w, ffn_conv_b, ffn_w_down, final_norm, sb_w_in, sb_w_out, fox_w_in, fox_b_f, fox_w_out, swa_w_in, swa_sinks, swa_w_out, mla_w_in, mla_q_norm, mla_w_uq, mla_kv_norm, mla_w_ukv, mla_w_out):
    b, s, d = x.shape
    depth = attn_norm.shape[0]
    mem_len = mem.shape[1]
    x2 = x.reshape(b * s, d)
    mem2 = mem.reshape(b * mem_len, d)
    w_mem_kv_bf16 = w_mem_kv.astype(BF16)
    for i in range(depth):
        kind, j = i % 4, i // 4
        mem_kv = _norm_matmul(mem2, mem_norm[i], w_mem_kv_bf16, BF16, tm=b * mem_len, tn=2 * MEM_W,
                              layer=i).reshape(b, mem_len, 2 * MEM_W)
        g = attn_norm[i]
        if kind == 0:
            x2 = _sb_layer(x2, mem_kv, g, sb_w_in[j], (sb_w_out, j), b, s)
        elif kind == 1:
            x2 = _fox_layer(x2, mem_kv, g, fox_w_in[j], fox_b_f[j], (fox_w_out, j), b, s)
        elif kind == 2:
            x2 = _swa_layer(x2, mem_kv, g, rel_bias, swa_w_in[j], swa_sinks[j], (swa_w_out, j), b, s)
        else:
            x2 = _mla_layer(x2, mem_kv, g, positions, mla_w_in[j], mla_q_norm[j], mla_w_uq[j],
                            mla_kv_norm[j], mla_w_ukv[j], (mla_w_out, j), b, s)
        gated = _ffn_up(_rmsnorm(x2, ffn_norm[i], BF16, tm=512), ffn_w_up, i, ffn_conv_w[i], ffn_conv_b[i],
                        seq=s, tm=1024, tn=512)
        x2 = _matmul_res([gated], ffn_w_down, i, x2, tm=512, tn=512)
    return _rmsnorm(x2, final_norm, F32, tm=512).reshape(b, s, d)
```

```python
import functools
import math

import jax
import jax.numpy as jnp
from jax import lax
from jax.experimental import pallas as pl
from jax.experimental.pallas import tpu as pltpu

F32 = jnp.float32
BF16 = jnp.bfloat16

EPS = 1e-6
LOG2E = math.log2(math.e)
HEAD_DIM = 128
N_HEADS = 16
SWA_HEAD_DIM = 64
N_SWA_HEADS = 32
N_SWA_KV_HEADS = 4
SWA_GROUP = N_SWA_HEADS // N_SWA_KV_HEADS
WINDOW = 128
Q_LORA = 512
KV_LORA = 256
QK_NOPE = 128
QK_ROPE = 64
ROPE_THETA = 10000.0
MEM_HEADS = 4
MEM_W = MEM_HEADS * HEAD_DIM
NUM_BUCKETS = 32
MAX_DISTANCE = 128
CONV_WIDTH = 3

V7X_VMEM_BYTES = 64 * 1024 * 1024
VMEM_CAP = V7X_VMEM_BYTES - 8 * 1024 * 1024
LANES = 128
IN_PROJ_ROWS = 1024
TAIL = 8


def _cparams(sems, vmem_bytes):
    return pltpu.CompilerParams(dimension_semantics=sems,
                                vmem_limit_bytes=int(min(max(vmem_bytes, 16 * 2**20), VMEM_CAP)))


def _nbytes(shape, dtype):
    return math.prod(shape) * jnp.dtype(dtype).itemsize


def _rms(x, g):
    return x * lax.rsqrt(jnp.mean(x * x, axis=-1, keepdims=True) + EPS) * g


def _dot(a, b):
    return jnp.dot(a, b, preferred_element_type=F32)


def _dot_nt(a, b):
    return lax.dot_general(a, b, (((1,), (1,)), ((), ())), preferred_element_type=F32)


def _norm_matmul_kernel(x_ref, g_ref, w_ref, cs_ref, o_ref, xn_ref):
    @pl.when(pl.program_id(1) == 0)
    def _():
        xn_ref[...] = _rms(x_ref[...], g_ref[...]).astype(BF16)

    o_ref[...] = (_dot(xn_ref[...], w_ref[...]) * cs_ref[...]).astype(o_ref.dtype)


def _w_tile_spec(w, layer, k, tn, col_block):
    if w.ndim == 3:
        return pl.BlockSpec((None, k, tn), lambda i, j: (layer, 0, col_block(j)))
    return pl.BlockSpec((k, tn), lambda i, j: (0, col_block(j)))


def _norm_matmul(x, g, w, out_dtype, tm, tn, scaled_cols=0, col_scale=1.0, layer=None):
    m, k = x.shape
    n = w.shape[-1]
    cs = jnp.where(jnp.arange(n) < scaled_cols, col_scale, 1.0).astype(F32).reshape(1, n)
    vmem = (2 * (_nbytes((tm, k), F32) + _nbytes((k, tn), BF16) + _nbytes((tm, tn), out_dtype))
            + _nbytes((tm, k), BF16) + 2 * _nbytes((tm, k), F32) + _nbytes((tm, tn), F32))
    return pl.pallas_call(
        _norm_matmul_kernel,
        grid=(m // tm, n // tn),
        in_specs=[pl.BlockSpec((tm, k), lambda i, j: (i, 0)),
                  pl.BlockSpec((1, k), lambda i, j: (0, 0)),
                  _w_tile_spec(w, layer, k, tn, lambda j: j),
                  pl.BlockSpec((1, tn), lambda i, j: (0, j))],
        out_specs=pl.BlockSpec((tm, tn), lambda i, j: (i, j)),
        out_shape=jax.ShapeDtypeStruct((m, n), out_dtype),
        scratch_shapes=[pltpu.VMEM((tm, k), BF16)],
        compiler_params=_cparams(("parallel", "arbitrary"), vmem),
        name="norm_matmul",
    )(x, g.reshape(1, k), w, cs)


def _matmul_res_kernel(*refs, n_a):
    a_refs, w_refs = refs[:n_a], refs[n_a:2 * n_a]
    x_ref, o_ref = refs[2 * n_a], refs[2 * n_a + 1]
    wb_refs = refs[2 * n_a + 2:]

    @pl.when(pl.program_id(1) == 0)
    def _():
        for w_ref, wb_ref in zip(w_refs, wb_refs):
            wb_ref[...] = w_ref[...].astype(BF16)

    acc = x_ref[...]
    for a_ref, wb_ref in zip(a_refs, wb_refs):
        acc = acc + _dot(a_ref[...], wb_ref[...])
    o_ref[...] = acc


def _matmul_res(a_list, w, layer, x, tm, tn):
    m, n = x.shape
    vmem = 6 * _nbytes((tm, tn), F32)
    a_specs, w_specs, scratch = [], [], []
    row = 0
    for a in a_list:
        kp = a.shape[1]
        row_block = row // kp
        assert row_block * kp == row
        a_specs.append(pl.BlockSpec((tm, kp), lambda j, i: (i, 0)))
        w_specs.append(pl.BlockSpec((None, kp, tn), lambda j, i, row_block=row_block: (layer, row_block, j)))
        scratch.append(pltpu.VMEM((kp, tn), BF16))
        vmem += 2 * _nbytes((tm, kp), a.dtype) + 2 * _nbytes((kp, tn), F32) + 2 * _nbytes((kp, tn), BF16)
        row += kp
    assert row == w.shape[1]
    return pl.pallas_call(
        functools.partial(_matmul_res_kernel, n_a=len(a_list)),
        grid=(n // tn, m // tm),
        in_specs=a_specs + w_specs + [pl.BlockSpec((tm, tn), lambda j, i: (i, j))],
        out_specs=pl.BlockSpec((tm, tn), lambda j, i: (i, j)),
        out_shape=jax.ShapeDtypeStruct((m, n), F32),
        scratch_shapes=scratch,
        compiler_params=_cparams(("parallel", "arbitrary"), vmem),
        name="matmul_res",
    )(*a_list, *([w] * len(a_list)), x)


def _rmsnorm_kernel(x_ref, g_ref, o_ref):
    o_ref[...] = _rms(x_ref[...], g_ref[...]).astype(o_ref.dtype)


def _rmsnorm(x, g, out_dtype, tm):
    m, k = x.shape
    return pl.pallas_call(
        _rmsnorm_kernel,
        grid=(m // tm,),
        in_specs=[pl.BlockSpec((tm, k), lambda i: (i, 0)), pl.BlockSpec((1, k), lambda i: (0, 0))],
        out_specs=pl.BlockSpec((tm, k), lambda i: (i, 0)),
        out_shape=jax.ShapeDtypeStruct((m, k), out_dtype),
        compiler_params=_cparams(("parallel",), 8 * _nbytes((tm, k), F32)),
        name="rmsnorm",
    )(x, g.reshape(1, k))


def _ffn_up_kernel(xn_ref, wg_ref, wv_ref, cwg_ref, cwv_ref, cbg_ref, cbv_ref, o_ref,
                   wgb_ref, wvb_ref, tail_g_ref, tail_v_ref, *u_refs, rows, tiles_per_seq):
    i = pl.program_id(1)

    @pl.when(i == 0)
    def _():
        wgb_ref[...] = wg_ref[...].astype(BF16)
        wvb_ref[...] = wv_ref[...].astype(BF16)

    @pl.when(i % tiles_per_seq == 0)
    def _():
        tail_g_ref[...] = jnp.zeros(tail_g_ref.shape, F32)
        tail_v_ref[...] = jnp.zeros(tail_v_ref.shape, F32)

    n_chunks = len(u_refs) // 2

    for r in range(n_chunks):
        xr = xn_ref[r * rows:(r + 1) * rows, :]

        def conv(wb_ref, cw_ref, cb_ref, u_ref, before):
            u_ref[TAIL:, :] = _dot(xr, wb_ref[...])
            u_ref[0:TAIL, :] = before
            c = cb_ref[...]
            for tap in range(CONV_WIDTH):
                start = TAIL - (CONV_WIDTH - 1) + tap
                c = c + cw_ref[tap:tap + 1, :] * u_ref[start:start + rows, :]
            return c

        if r == 0:
            before_g, before_v = tail_g_ref[...], tail_v_ref[...]
        else:
            before_g, before_v = u_refs[2 * r - 2][rows:, :], u_refs[2 * r - 1][rows:, :]
        gate = conv(wgb_ref, cwg_ref, cbg_ref, u_refs[2 * r], before_g)
        val = conv(wvb_ref, cwv_ref, cbv_ref, u_refs[2 * r + 1], before_v)
        o_ref[r * rows:(r + 1) * rows, :] = (gate * (1.0 / (1.0 + jnp.exp(-gate))) * val).astype(o_ref.dtype)

    tail_g_ref[...] = u_refs[2 * n_chunks - 2][rows:, :]
    tail_v_ref[...] = u_refs[2 * n_chunks - 1][rows:, :]


def _ffn_up(xn, w_up, layer, conv_w, conv_b, seq, tm, tn, rows=256):
    m, k = xn.shape
    d_ff = w_up.shape[-1] // 2
    nj = d_ff // tn
    n_chunks = tm // rows
    vmem = (2 * (_nbytes((tm, k), BF16) + 2 * _nbytes((k, tn), F32) + _nbytes((tm, tn), BF16))
            + 2 * _nbytes((k, tn), BF16) + 2 * n_chunks * _nbytes((rows + TAIL, tn), F32)
            + 2 * _nbytes((k, tn), F32) + 6 * n_chunks * _nbytes((rows, tn), F32))
    return pl.pallas_call(
        functools.partial(_ffn_up_kernel, rows=rows, tiles_per_seq=seq // tm),
        grid=(nj, m // tm),
        in_specs=[pl.BlockSpec((tm, k), lambda j, i: (i, 0)),
                  pl.BlockSpec((None, k, tn), lambda j, i: (layer, 0, j)),
                  pl.BlockSpec((None, k, tn), lambda j, i: (layer, 0, nj + j)),
                  pl.BlockSpec((CONV_WIDTH, tn), lambda j, i: (0, j)),
                  pl.BlockSpec((CONV_WIDTH, tn), lambda j, i: (0, nj + j)),
                  pl.BlockSpec((1, tn), lambda j, i: (0, j)),
                  pl.BlockSpec((1, tn), lambda j, i: (0, nj + j))],
        out_specs=pl.BlockSpec((tm, tn), lambda j, i: (i, j)),
        out_shape=jax.ShapeDtypeStruct((m, d_ff), BF16),
        scratch_shapes=[pltpu.VMEM((k, tn), BF16), pltpu.VMEM((k, tn), BF16),
                        pltpu.VMEM((TAIL, tn), F32), pltpu.VMEM((TAIL, tn), F32)]
        + [pltpu.VMEM((rows + TAIL, tn), F32)] * (2 * n_chunks),
        compiler_params=_cparams(("arbitrary", "arbitrary"), vmem),
        name="ffn_up_conv_gate",
    )(xn, w_up, w_up, conv_w, conv_w, conv_b.reshape(1, -1), conv_b.reshape(1, -1))


def _eye(n):
    return jnp.where(lax.broadcasted_iota(jnp.int32, (n, n), 0) == lax.broadcasted_iota(jnp.int32, (n, n), 1),
                     1.0, 0.0).astype(BF16)


def _transpose_bf16(x):
    return _dot_nt(_eye(x.shape[1]), x)


def _fill_v_transposed(v_ref, vt_ref, t):
    s, dv = v_ref.shape
    for c in range(s // t):
        vt_ref[0:dv, c * t:(c + 1) * t] = _transpose_bf16(v_ref[c * t:(c + 1) * t, :]).astype(BF16)
    if vt_ref.shape[0] > dv:
        vt_ref[dv:, :] = jnp.ones((vt_ref.shape[0] - dv, s), BF16)


def _flash_kernel(*refs, t, dk, dv, group, has_aug):
    if has_aug:
        q_ref, k_ref, v_ref, qa_ref, ka_ref, o_ref, vt_ref, m_ref, acc_ref, s_ref = refs
    else:
        q_ref, k_ref, v_ref, o_ref, vt_ref, m_ref, acc_ref, s_ref = refs
    qi = pl.program_id(2)
    heads = range(group)

    @pl.when(qi == 0)
    def _():
        for g in heads:
            _fill_v_transposed(v_ref.at[:, g * dv:(g + 1) * dv], vt_ref.at[g], t)

    qs = []
    for g in heads:
        q = q_ref[:, g * dk:(g + 1) * dk]
        qs.append(jnp.concatenate([q, qa_ref[g]], axis=1) if has_aug else q)
    m_ref[...] = jnp.full(m_ref.shape, -jnp.inf, F32)
    acc_ref[...] = jnp.zeros(acc_ref.shape, F32)

    def scores(g, kb):
        ks = pl.multiple_of(kb * t, t)
        k = k_ref[pl.ds(ks, t), g * dk:(g + 1) * dk]
        if has_aug:
            k = jnp.concatenate([k, ka_ref[g, pl.ds(ks, t), :]], axis=1)
        return _dot_nt(k, qs[g])

    def col_max(s):
        return jnp.max(s, axis=0, keepdims=True)

    def update(g, s, s_max, kb):
        m_prev = m_ref[g]
        m_new = jnp.maximum(m_prev, s_max)
        m_ref[g] = m_new
        p = jnp.exp2(s - m_new).astype(BF16)
        ks = pl.multiple_of(kb * t, t)
        acc_ref[g] = jnp.exp2(m_prev - m_new) * acc_ref[g] + _dot(vt_ref[g, :, pl.ds(ks, t)], p)

    def body(kb, s_max):
        nxt = []
        for g in heads:
            update(g, s_ref[g], s_max[g], kb)
            s_next = scores(g, kb + 1)
            nxt.append(col_max(s_next))
            s_ref[g] = s_next
        return tuple(nxt)

    first = []
    for g in heads:
        s0 = scores(g, 0)
        first.append(col_max(s0))
        s_ref[g] = s0
    lax.fori_loop(0, qi, body, tuple(first))
    key = lax.broadcasted_iota(jnp.int32, (t, t), 0)
    qry = lax.broadcasted_iota(jnp.int32, (t, t), 1)
    outs = []
    for g in heads:
        s = jnp.where(key <= qry, s_ref[g], -jnp.inf)
        update(g, s, col_max(s), qi)
        acc = acc_ref[g]
        outs.append((acc[0:dv, :] / acc[dv:dv + 1, :]).T)
    o_ref[...] = jnp.concatenate(outs, axis=1).astype(o_ref.dtype)


def _flash_attention(q_arr, k_arr, v_arr, q_off, k_off, v_off, dk, dv, n_heads, aug=None, t=512, group=4):
    b, s, _ = q_arr.shape
    assert q_off % group == 0 and k_off % group == 0 and v_off % group == 0 and n_heads % group == 0
    qo, ko, vo = q_off // group, k_off // group, v_off // group
    in_specs = [pl.BlockSpec((None, t, group * dk), lambda bi, h, qi: (bi, qi, qo + h)),
                pl.BlockSpec((None, s, group * dk), lambda bi, h, qi: (bi, 0, ko + h)),
                pl.BlockSpec((None, s, group * dv), lambda bi, h, qi: (bi, 0, vo + h))]
    args = [q_arr, k_arr, v_arr]
    vmem = 2 * group * (_nbytes((t, dk), BF16) + _nbytes((s, dk), BF16) + _nbytes((s, dv), BF16)
                        + _nbytes((t, dv), BF16))
    if aug is not None:
        in_specs += [pl.BlockSpec((None, group, t, LANES), lambda bi, h, qi: (bi, h, qi, 0)),
                     pl.BlockSpec((None, group, s, LANES), lambda bi, h, qi: (bi, h, 0, 0))]
        args += list(aug)
        vmem += 2 * group * (_nbytes((t, LANES), BF16) + _nbytes((s, LANES), BF16))
    acc_rows = dv + 16
    vmem += group * (_nbytes((acc_rows, s), BF16) + 2 * _nbytes((acc_rows, t), F32) + 8 * _nbytes((t, t), F32))
    return pl.pallas_call(
        functools.partial(_flash_kernel, t=t, dk=dk, dv=dv, group=group, has_aug=aug is not None),
        grid=(b, n_heads // group, s // t),
        in_specs=in_specs,
        out_specs=pl.BlockSpec((None, t, group * dv), lambda bi, h, qi: (bi, qi, h)),
        out_shape=jax.ShapeDtypeStruct((b, s, n_heads * dv), BF16),
        scratch_shapes=[pltpu.VMEM((group, acc_rows, s), BF16), pltpu.VMEM((group, 1, t), F32),
                        pltpu.VMEM((group, acc_rows, t), F32), pltpu.VMEM((group, t, t), F32)],
        compiler_params=_cparams(("parallel", "parallel", "arbitrary"), vmem),
        name="flash_attention",
    )(*args)


def _sb_kernel(q_ref, k_ref, v_ref, o_ref, vt_ref, carry_ref, acc_ref, s_ref, a_ref, *, t, sub, d, group):
    qi = pl.program_id(2)
    heads = range(group)

    @pl.when(qi == 0)
    def _():
        for g in heads:
            _fill_v_transposed(v_ref.at[:, g * d:(g + 1) * d], vt_ref.at[g], t)

    carry_ref[...] = jnp.zeros(carry_ref.shape, F32)
    acc_ref[...] = jnp.zeros(acc_ref.shape, F32)
    qs = [q_ref[:, g * d:(g + 1) * d] for g in heads]
    key = lax.broadcasted_iota(jnp.int32, (sub, t), 0)
    qry = lax.broadcasted_iota(jnp.int32, (sub, t), 1)
    r = lax.broadcasted_iota(jnp.int32, (sub + 16, sub), 0)
    c = lax.broadcasted_iota(jnp.int32, (sub + 16, sub), 1)
    suffix = jnp.where(((c > r) & (r < sub)) | (r == sub), 1.0, 0.0).astype(BF16)
    suffix = jnp.concatenate([suffix, suffix], axis=1)

    def scores(g, kb):
        ks = pl.multiple_of(kb * t, t)
        return _dot_nt(k_ref[pl.ds(ks, t), g * d:(g + 1) * d], qs[g])

    def weights(g, s, on_diagonal):
        later = carry_ref[g]
        out = [None] * (t // sub)
        for i in reversed(range(t // sub)):
            z = s[i * sub:(i + 1) * sub, :]
            neg_abs = lax.bitcast_convert_type(
                lax.bitcast_convert_type(z, jnp.uint32) | jnp.uint32(0x80000000), F32)
            log_beta = jnp.minimum(z, 0.0) - jnp.log2(1.0 + jnp.exp2(neg_abs))
            log_keep = log_beta - z
            if on_diagonal:
                strict = key + i * sub < qry
                log_keep = jnp.where(strict, log_keep, 0.0)
            hi = log_keep.astype(BF16)
            lo = (log_keep - hi.astype(F32)).astype(BF16)
            sums = _dot(suffix, jnp.concatenate([hi, lo], axis=0))
            a = jnp.exp2(log_beta + (sums[0:sub, :] + later))
            if on_diagonal:
                a = jnp.where(strict, a, 0.0)
            out[i] = a.astype(BF16)
            later = later + sums[sub:sub + 1, :]
        carry_ref[g] = later
        return jnp.concatenate(out, axis=0)

    def accumulate(g, a, kb):
        ks = pl.multiple_of(kb * t, t)
        acc_ref[g] += _dot(vt_ref[g, :, pl.ds(ks, t)], a)

    def finish():
        o_ref[...] = jnp.concatenate([acc_ref[g].T for g in heads], axis=1).astype(o_ref.dtype)

    @pl.when(qi == 0)
    def _():
        for g in heads:
            accumulate(g, weights(g, scores(g, 0), True), 0)
        finish()

    @pl.when(qi > 0)
    def _():
        for g in heads:
            a_ref[g] = weights(g, scores(g, qi), True)
            s_ref[g] = scores(g, qi - 1)

        def body(i, carry):
            for g in heads:
                accumulate(g, a_ref[g], qi - i)
                a_ref[g] = weights(g, s_ref[g], False)
                s_ref[g] = scores(g, qi - i - 2)
            return carry

        lax.fori_loop(0, qi - 1, body, 0)
        for g in heads:
            accumulate(g, a_ref[g], 1)
            accumulate(g, weights(g, s_ref[g], False), 0)
        finish()


def _sb_attention(proj, n_heads, t=512, group=2):
    b, s, _ = proj.shape
    d = HEAD_DIM
    assert n_heads % group == 0
    blocks = n_heads // group
    vmem = group * (2 * (2 * _nbytes((t, d), BF16) + 2 * _nbytes((s, d), BF16))
                    + _nbytes((d, s), BF16) + 2 * _nbytes((d, t), F32) + 12 * _nbytes((t, t), F32))
    return pl.pallas_call(
        functools.partial(_sb_kernel, t=t, sub=LANES, d=d, group=group),
        grid=(b, blocks, s // t),
        in_specs=[pl.BlockSpec((None, t, group * d), lambda bi, h, qi: (bi, qi, h)),
                  pl.BlockSpec((None, s, group * d), lambda bi, h, qi: (bi, 0, blocks + h)),
                  pl.BlockSpec((None, s, group * d), lambda bi, h, qi: (bi, 0, 2 * blocks + h))],
        out_specs=pl.BlockSpec((None, t, group * d), lambda bi, h, qi: (bi, qi, h)),
        out_shape=jax.ShapeDtypeStruct((b, s, n_heads * d), BF16),
        scratch_shapes=[pltpu.VMEM((group, d, s), BF16), pltpu.VMEM((group, 1, t), F32),
                        pltpu.VMEM((group, d, t), F32), pltpu.VMEM((group, t, t), F32),
                        pltpu.VMEM((group, t, t), BF16)],
        compiler_params=_cparams(("parallel", "parallel", "arbitrary"), vmem),
        name="stick_breaking_attention",
    )(proj, proj, proj)


def _split3(x):
    hi = x.astype(BF16)
    rest = x - hi.astype(F32)
    mid = rest.astype(BF16)
    lo = (rest - mid.astype(F32)).astype(BF16)
    return hi, mid, lo


def _fox_gate_kernel(fl_ref, b_ref, pq_ref, pk_ref, oq_ref, ok_ref, qa_ref, ka_ref, carry_ref):
    ts = fl_ref.shape[0]

    @pl.when(pl.program_id(1) == 0)
    def _():
        carry_ref[...] = jnp.zeros(carry_ref.shape, F32)

    z = fl_ref[...] + b_ref[...]
    log_f = jnp.minimum(z, 0.0) - jnp.log(1.0 + jnp.exp(-jnp.abs(z)))
    row = lax.broadcasted_iota(jnp.int32, (ts, ts), 0)
    col = lax.broadcasted_iota(jnp.int32, (ts, ts), 1)
    prefix = jnp.where(col <= row, 1.0, 0.0).astype(BF16)
    hi, mid, lo = _split3(log_f)
    c = _dot(prefix, hi) + _dot(prefix, mid) + _dot(prefix, lo) + carry_ref[0:1, :]
    carry_ref[0:1, :] = c[ts - 1:ts, :]
    parts = jnp.concatenate(_split3(c * LOG2E), axis=1)
    qa = (_dot(parts, pq_ref[...]) + oq_ref[...]).astype(BF16)
    ka = (_dot(parts, pk_ref[...]) + ok_ref[...]).astype(BF16)
    for h in range(qa_ref.shape[0]):
        qa_ref[h] = qa[:, h * LANES:(h + 1) * LANES]
        ka_ref[h] = ka[:, h * LANES:(h + 1) * LANES]


def _fox_aug_tables(n_heads):
    part, src = jnp.arange(3 * LANES) // LANES, jnp.arange(3 * LANES) % LANES
    head, lane = jnp.arange(n_heads * LANES) // LANES, jnp.arange(n_heads * LANES) % LANES
    mine = src[:, None] == head[None, :]
    pq = jnp.where(mine & (lane[None, :] == part[:, None] + 3), 1.0, 0.0).astype(BF16)
    pk = jnp.where(mine & (lane[None, :] == part[:, None]), -1.0, 0.0).astype(BF16)
    oq = jnp.where(lane < 3, 1.0, 0.0).astype(F32)[None]
    ok = jnp.where((lane >= 3) & (lane < 6), 1.0, 0.0).astype(F32)[None]
    return pq, pk, oq, ok


def _fox_gates(f_logit, b_f, n_heads, ts=256):
    b, s, w = f_logit.shape
    hw = n_heads * LANES
    out = jax.ShapeDtypeStruct((b, n_heads, s, LANES), BF16)
    const = lambda bi, i: (0, 0)
    return pl.pallas_call(
        _fox_gate_kernel,
        grid=(b, s // ts),
        in_specs=[pl.BlockSpec((None, ts, w), lambda bi, i: (bi, i, 0)),
                  pl.BlockSpec((1, w), const),
                  pl.BlockSpec((3 * LANES, hw), const), pl.BlockSpec((3 * LANES, hw), const),
                  pl.BlockSpec((1, hw), const), pl.BlockSpec((1, hw), const)],
        out_specs=[pl.BlockSpec((None, n_heads, ts, LANES), lambda bi, i: (bi, 0, i, 0)),
                   pl.BlockSpec((None, n_heads, ts, LANES), lambda bi, i: (bi, 0, i, 0))],
        out_shape=[out, out],
        scratch_shapes=[pltpu.VMEM((8, w), F32)],
        compiler_params=_cparams(("parallel", "arbitrary"), 32 * 2**20),
        name="fox_gate_cumsum",
    )(f_logit, b_f.reshape(1, w), *_fox_aug_tables(n_heads))


def _mem_attn_kernel(q_ref, kv_ref, o_ref, *, scale):
    outs = []
    for h in range(MEM_HEADS):
        q = q_ref[:, h * HEAD_DIM:(h + 1) * HEAD_DIM].astype(BF16)
        k = kv_ref[:, h * HEAD_DIM:(h + 1) * HEAD_DIM]
        v = kv_ref[:, MEM_W + h * HEAD_DIM:MEM_W + (h + 1) * HEAD_DIM]
        s = _dot_nt(q, k) * scale
        p = jnp.exp(s - jnp.max(s, axis=1, keepdims=True))
        o = _dot(p.astype(BF16), v) / jnp.sum(p, axis=1, keepdims=True)
        outs.append(o.astype(o_ref.dtype))
    o_ref[...] = jnp.concatenate(outs, axis=1)


def _mem_attention(q_arr, q_block, mem_kv, tq=512):
    b, s, _ = q_arr.shape
    length = mem_kv.shape[1]
    vmem = (2 * (_nbytes((tq, MEM_W), q_arr.dtype) + _nbytes((length, 2 * MEM_W), BF16)
                 + _nbytes((tq, MEM_W), BF16)) + 8 * _nbytes((tq, length), F32))
    return pl.pallas_call(
        functools.partial(_mem_attn_kernel, scale=HEAD_DIM ** -0.5),
        grid=(b, s // tq),
        in_specs=[pl.BlockSpec((None, tq, MEM_W), lambda bi, i: (bi, i, q_block)),
                  pl.BlockSpec((None, length, 2 * MEM_W), lambda bi, i: (bi, 0, 0))],
        out_specs=pl.BlockSpec((None, tq, MEM_W), lambda bi, i: (bi, i, 0)),
        out_shape=jax.ShapeDtypeStruct((b, s, MEM_W), BF16),
        compiler_params=_cparams(("parallel", "parallel"), vmem),
        name="memory_attention",
    )(q_arr, mem_kv)


def _t5_bucket_table():
    max_exact = NUM_BUCKETS // 2
    kj = jnp.arange(2 * WINDOW)[:, None]
    qi = jnp.arange(WINDOW)[None, :]
    signed = WINDOW + qi - kj
    dist = jnp.maximum(signed, 0)
    d = jnp.maximum(dist, 1).astype(F32)
    large = max_exact + (jnp.log(d / max_exact) / math.log(MAX_DISTANCE / max_exact)
                         * (NUM_BUCKETS - max_exact)).astype(jnp.int32)
    bucket = jnp.where(dist < max_exact, dist, jnp.minimum(large, NUM_BUCKETS - 1))
    return jnp.where((signed >= 0) & (signed < WINDOW), bucket, -1).astype(jnp.int32)


def _t5_bias_kernel(rb_ref, bucket_ref, o_ref):
    kvh = pl.program_id(0)
    bucket = bucket_ref[...]
    for g in range(SWA_GROUP):
        bias = jnp.full(bucket.shape, -jnp.inf, F32)
        for b in range(NUM_BUCKETS):
            bias = jnp.where(bucket == b, rb_ref[b, kvh * SWA_GROUP + g] * LOG2E, bias)
        o_ref[:, g * WINDOW:(g + 1) * WINDOW] = bias


def _t5_bias(rel_bias):
    return pl.pallas_call(
        _t5_bias_kernel,
        grid=(N_SWA_KV_HEADS,),
        in_specs=[pl.BlockSpec(memory_space=pltpu.SMEM),
                  pl.BlockSpec((2 * WINDOW, WINDOW), lambda h: (0, 0))],
        out_specs=pl.BlockSpec((None, 2 * WINDOW, SWA_GROUP * WINDOW), lambda h: (h, 0, 0)),
        out_shape=jax.ShapeDtypeStruct((N_SWA_KV_HEADS, 2 * WINDOW, SWA_GROUP * WINDOW), F32),
        compiler_params=_cparams(("parallel",), 16 * 2**20),
        name="t5_bias",
    )(rel_bias, _t5_bucket_table())


def _swa_kernel(sink_ref, q_ref, kvc_ref, kvp_ref, bias_ref, o_ref):
    n = pl.program_id(1)
    d = SWA_HEAD_DIM
    kv_w = N_SWA_KV_HEADS * d
    k_win = jnp.concatenate([kvp_ref[:, 0:kv_w], kvc_ref[:, 0:kv_w]], axis=0)
    v_win = jnp.concatenate([kvp_ref[:, kv_w:2 * kv_w], kvc_ref[:, kv_w:2 * kv_w]], axis=0)
    v_t = v_win.astype(F32).T.astype(BF16)
    no_prev = jnp.where(n == 0, -jnp.inf, 0.0)
    outs = []
    for kvh in range(N_SWA_KV_HEADS):
        heads = range(kvh * SWA_GROUP, (kvh + 1) * SWA_GROUP)
        q = jnp.concatenate([q_ref[:, h * d:(h + 1) * d] for h in heads], axis=0)
        sink = jnp.concatenate([jnp.full((1, WINDOW), sink_ref[h] * LOG2E, F32) for h in heads], axis=1)
        s = _dot_nt(k_win[:, kvh * d:(kvh + 1) * d], q) + bias_ref[kvh]
        s = jnp.concatenate([s[0:WINDOW, :] + no_prev, s[WINDOW:, :]], axis=0)
        m = jnp.maximum(jnp.max(s, axis=0, keepdims=True), sink)
        p = jnp.exp2(s - m)
        denom = jnp.sum(p, axis=0, keepdims=True) + jnp.exp2(sink - m)
        o_t = _dot(v_t[kvh * d:(kvh + 1) * d, :], p.astype(BF16)) / denom
        outs += [o_t[:, g * WINDOW:(g + 1) * WINDOW] for g in range(SWA_GROUP)]
    o_ref[...] = jnp.concatenate(outs, axis=0).T.astype(o_ref.dtype)


def _swa_attention(proj, sinks, bias):
    b, s, _ = proj.shape
    q_w = N_SWA_HEADS * SWA_HEAD_DIM
    kv_w = 2 * N_SWA_KV_HEADS * SWA_HEAD_DIM
    kv_block = q_w // kv_w
    vmem = (2 * (2 * _nbytes((WINDOW, q_w), BF16) + 2 * _nbytes((WINDOW, kv_w), BF16)
                 + _nbytes(bias.shape, F32)) + 16 * 2**20)
    return pl.pallas_call(
        _swa_kernel,
        grid=(b, s // WINDOW),
        in_specs=[pl.BlockSpec(memory_space=pltpu.SMEM),
                  pl.BlockSpec((None, WINDOW, q_w), lambda bi, n: (bi, n, 0)),
                  pl.BlockSpec((None, WINDOW, kv_w), lambda bi, n: (bi, n, kv_block)),
                  pl.BlockSpec((None, WINDOW, kv_w), lambda bi, n: (bi, jnp.maximum(n - 1, 0), kv_block)),
                  pl.BlockSpec(bias.shape, lambda bi, n: (0, 0, 0))],
        out_specs=pl.BlockSpec((None, WINDOW, q_w), lambda bi, n: (bi, n, 0)),
        out_shape=jax.ShapeDtypeStruct((b, s, q_w), BF16),
        compiler_params=_cparams(("parallel", "arbitrary"), vmem),
        name="sliding_window_attention",
    )(sinks, proj, proj, proj, bias)


def _rope_table_kernel(pos_ref, invf_ref, o_ref):
    ang = pos_ref[...].astype(F32) * invf_ref[...]
    lane = lax.broadcasted_iota(jnp.int32, ang.shape, 1)
    half = QK_ROPE // 2
    sin_signed = jnp.where(lane < QK_ROPE + half, -jnp.sin(ang), jnp.sin(ang))
    o_ref[...] = jnp.where(lane < QK_ROPE, jnp.cos(ang), sin_signed)


def _rope_table(positions, tm=1024):
    m = positions.shape[0]
    half = QK_ROPE // 2
    inv_freq = ROPE_THETA ** (-jnp.arange(half, dtype=F32) / half)
    invf = jnp.tile(inv_freq, LANES // half).reshape(1, LANES)
    return pl.pallas_call(
        _rope_table_kernel,
        grid=(m // tm,),
        in_specs=[pl.BlockSpec((tm, 1), lambda i: (i, 0)), pl.BlockSpec((1, LANES), lambda i: (0, 0))],
        out_specs=pl.BlockSpec((tm, LANES), lambda i: (i, 0)),
        out_shape=jax.ShapeDtypeStruct((m, LANES), F32),
        compiler_params=_cparams(("parallel",), 32 * 2**20),
        name="rope_table",
    )(positions, invf)


def _apply_rope(x_and_partner, cs):
    z = x_and_partner * cs
    z = z + pltpu.roll(z, QK_ROPE, axis=1)
    lane = lax.broadcasted_iota(jnp.int32, z.shape, 1)
    return jnp.where(lane < QK_ROPE, z, 0.0)


def _mla_q_kernel(cq_ref, g_ref, w_ref, cs_ref, o_ref, cn_ref):
    @pl.when(pl.program_id(1) == 0)
    def _():
        cn_ref[...] = _rms(cq_ref[...], g_ref[...]).astype(BF16)

    y = _dot(cn_ref[...], w_ref[...]) * ((QK_NOPE + QK_ROPE) ** -0.5 * LOG2E)
    cs = cs_ref[...]
    parts = []
    for h in range(y.shape[1] // (2 * LANES)):
        yh = y[:, h * 2 * LANES:(h + 1) * 2 * LANES]
        parts += [yh[:, :QK_NOPE], _apply_rope(yh[:, QK_NOPE:], cs)]
    o_ref[...] = jnp.concatenate(parts, axis=1).astype(o_ref.dtype)


def _mla_q(proj, q_norm, w_uq_aug, cs, tm=512, heads_per_step=4):
    m = proj.shape[0]
    hw = 2 * LANES * heads_per_step
    n_heads = w_uq_aug.shape[1] // hw
    return pl.pallas_call(
        _mla_q_kernel,
        grid=(m // tm, n_heads),
        in_specs=[pl.BlockSpec((tm, Q_LORA), lambda i, h: (i, 1)),
                  pl.BlockSpec((1, Q_LORA), lambda i, h: (0, 0)),
                  pl.BlockSpec((Q_LORA, hw), lambda i, h: (0, h)),
                  pl.BlockSpec((tm, LANES), lambda i, h: (i, 0))],
        out_specs=pl.BlockSpec((tm, hw), lambda i, h: (i, h)),
        out_shape=jax.ShapeDtypeStruct((m, n_heads * hw), BF16),
        scratch_shapes=[pltpu.VMEM((tm, Q_LORA), BF16)],
        compiler_params=_cparams(("parallel", "arbitrary"), 32 * 2**20),
        name="mla_q_expand",
    )(proj, q_norm.reshape(1, Q_LORA), w_uq_aug, cs)


def _mla_kv_kernel(ckv_ref, g_ref, w_ref, kr_ref, cs_ref, k_ref, v_ref, cn_ref, kpe_ref):
    @pl.when(pl.program_id(1) == 0)
    def _():
        cn_ref[...] = _rms(ckv_ref[...], g_ref[...]).astype(BF16)
        kpe_ref[...] = _apply_rope(kr_ref[...], cs_ref[...])

    y = _dot(cn_ref[...], w_ref[...])
    kpe = kpe_ref[...]
    k_parts, v_parts = [], []
    for h in range(y.shape[1] // (2 * LANES)):
        yh = y[:, h * 2 * LANES:(h + 1) * 2 * LANES]
        k_parts += [yh[:, :QK_NOPE], kpe]
        v_parts.append(yh[:, QK_NOPE:])
    k_ref[...] = jnp.concatenate(k_parts, axis=1).astype(k_ref.dtype)
    v_ref[...] = jnp.concatenate(v_parts, axis=1).astype(v_ref.dtype)


def _mla_kv(proj, kv_norm, w_ukv, cs, tm=512, heads_per_step=4):
    m = proj.shape[0]
    hw = 2 * LANES * heads_per_step
    n_heads = w_ukv.shape[1] // hw
    return pl.pallas_call(
        _mla_kv_kernel,
        grid=(m // tm, n_heads),
        in_specs=[pl.BlockSpec((tm, KV_LORA), lambda i, h: (i, 4)),
                  pl.BlockSpec((1, KV_LORA), lambda i, h: (0, 0)),
                  pl.BlockSpec((KV_LORA, hw), lambda i, h: (0, h)),
                  pl.BlockSpec((tm, LANES), lambda i, h: (i, 10)),
                  pl.BlockSpec((tm, LANES), lambda i, h: (i, 0))],
        out_specs=[pl.BlockSpec((tm, hw), lambda i, h: (i, h)),
                   pl.BlockSpec((tm, hw // 2), lambda i, h: (i, h))],
        out_shape=[jax.ShapeDtypeStruct((m, n_heads * hw), BF16),
                   jax.ShapeDtypeStruct((m, n_heads * hw // 2), BF16)],
        scratch_shapes=[pltpu.VMEM((tm, KV_LORA), BF16), pltpu.VMEM((tm, LANES), F32)],
        compiler_params=_cparams(("parallel", "arbitrary"), 32 * 2**20),
        name="mla_kv_expand",
    )(proj, kv_norm.reshape(1, KV_LORA), w_ukv, proj, cs)


def _swap_halves(w):
    half = w.shape[-1] // 2
    return jnp.concatenate([w[..., half:], w[..., :half]], axis=-1)


def _mixer_out(x2, o, o_mem, w_out, b, s):
    return _matmul_res([o.reshape(b * s, o.shape[-1]), o_mem.reshape(b * s, MEM_W)],
                       w_out[0], w_out[1], x2, tm=1024, tn=512)


def _sb_layer(x2, mem_kv, g, w_in, w_out, b, s):
    proj = _norm_matmul(x2, g, w_in.astype(BF16), BF16, tm=IN_PROJ_ROWS, tn=512, scaled_cols=N_HEADS * HEAD_DIM,
                        col_scale=HEAD_DIM ** -0.5 * LOG2E).reshape(b, s, -1)
    o = _sb_attention(proj, N_HEADS)
    o_mem = _mem_attention(proj, 3 * N_HEADS * HEAD_DIM // MEM_W, mem_kv)
    return _mixer_out(x2, o, o_mem, w_out, b, s)


def _fox_layer(x2, mem_kv, g, w_in, b_f, w_out, b, s):
    qkv_w = 3 * N_HEADS * HEAD_DIM
    w_main = jnp.concatenate([w_in[:, :qkv_w], w_in[:, qkv_w + N_HEADS:]], axis=1).astype(BF16)
    w_gate = jnp.pad(w_in[:, qkv_w:qkv_w + N_HEADS], ((0, 0), (0, LANES - N_HEADS))).astype(BF16)
    proj = _norm_matmul(x2, g, w_main, BF16, tm=IN_PROJ_ROWS, tn=512, scaled_cols=N_HEADS * HEAD_DIM,
                        col_scale=HEAD_DIM ** -0.5 * LOG2E).reshape(b, s, -1)
    f_logit = _norm_matmul(x2, g, w_gate, F32, tm=512, tn=LANES).reshape(b, s, LANES)
    aug = _fox_gates(f_logit, jnp.pad(b_f, (0, LANES - N_HEADS)), N_HEADS)
    o = _flash_attention(proj, proj, proj, 0, N_HEADS, 2 * N_HEADS, HEAD_DIM, HEAD_DIM, N_HEADS, aug=aug)
    o_mem = _mem_attention(proj, qkv_w // MEM_W, mem_kv)
    return _mixer_out(x2, o, o_mem, w_out, b, s)


def _swa_layer(x2, mem_kv, g, rel_bias, w_in, sinks, w_out, b, s):
    proj = _norm_matmul(x2, g, w_in.astype(BF16), BF16, tm=IN_PROJ_ROWS, tn=512,
                        scaled_cols=N_SWA_HEADS * SWA_HEAD_DIM,
                        col_scale=SWA_HEAD_DIM ** -0.5 * LOG2E).reshape(b, s, -1)
    o = _swa_attention(proj, sinks, _t5_bias(rel_bias))
    q_w = N_SWA_HEADS * SWA_HEAD_DIM
    kv_w = 2 * N_SWA_KV_HEADS * SWA_HEAD_DIM
    o_mem = _mem_attention(proj, (q_w + kv_w) // MEM_W, mem_kv)
    return _mixer_out(x2, o, o_mem, w_out, b, s)


def _mla_layer(x2, mem_kv, g, positions, w_in, q_norm, w_uq, kv_norm, w_ukv, w_out, b, s):
    d = x2.shape[1]
    o1, o2, o3 = Q_LORA, Q_LORA + KV_LORA, Q_LORA + KV_LORA + QK_ROPE
    w_main = jnp.concatenate([w_in[:, o3:], w_in[:, :o3], _swap_halves(w_in[:, o2:o3])], axis=1).astype(BF16)
    proj = _norm_matmul(x2, g, w_main, F32, tm=512, tn=w_main.shape[1])
    cs = _rope_table(positions.reshape(b * s, 1))
    wq = w_uq.reshape(Q_LORA, N_HEADS, QK_NOPE + QK_ROPE)
    wq = jnp.concatenate([wq, _swap_halves(wq[:, :, QK_NOPE:])], axis=-1).reshape(Q_LORA, -1).astype(BF16)
    q = _mla_q(proj, q_norm, wq, cs).reshape(b, s, -1)
    k, v = _mla_kv(proj, kv_norm, w_ukv.astype(BF16), cs)
    k, v = k.reshape(b, s, -1), v.reshape(b, s, -1)
    o = _flash_attention(q, k, v, 0, 0, 0, 2 * LANES, HEAD_DIM, N_HEADS)
    o_mem = _mem_attention(proj.reshape(b, s, -1), 0, mem_kv)
    return _mixer_out(x2, o, o_mem, w_out, b, s)


def kernel(x, mem, positions, rel_bias, attn_norm, mem_norm, w_mem_kv, ffn_norm, ffn_w_up, ffn_conv_w, ffn_conv_b, ffn_w_down, final_norm, sb_w_in, sb_w_out, fox_w_in, fox_b_f, fox_w_out, swa_w_in, swa_sinks, swa_w_out, mla_w_in, mla_q_norm, mla_w_uq, mla_kv_norm, mla_w_ukv, mla_w_out):
    b, s, d = x.shape
    depth = attn_norm.shape[0]
    mem_len = mem.shape[1]
    x2 = x.reshape(b * s, d)
    mem2 = mem.reshape(b * mem_len, d)
    w_mem_kv_bf16 = w_mem_kv.astype(BF16)
    for i in range(depth):
        kind, j = i % 4, i // 4
        mem_kv = _norm_matmul(mem2, mem_norm[i], w_mem_kv_bf16, BF16, tm=b * mem_len, tn=2 * MEM_W,
                              layer=i).reshape(b, mem_len, 2 * MEM_W)
        g = attn_norm[i]
        if kind == 0:
            x2 = _sb_layer(x2, mem_kv, g, sb_w_in[j], (sb_w_out, j), b, s)
        elif kind == 1:
            x2 = _fox_layer(x2, mem_kv, g, fox_w_in[j], fox_b_f[j], (fox_w_out, j), b, s)
        elif kind == 2:
            x2 = _swa_layer(x2, mem_kv, g, rel_bias, swa_w_in[j], swa_sinks[j], (swa_w_out, j), b, s)
        else:
            x2 = _mla_layer(x2, mem_kv, g, positions, mla_w_in[j], mla_q_norm[j], mla_w_uq[j],
                            mla_kv_norm[j], mla_w_ukv[j], (mla_w_out, j), b, s)
        gated = _ffn_up(_rmsnorm(x2, ffn_norm[i], BF16, tm=512), ffn_w_up, i, ffn_conv_w[i], ffn_conv_b[i],
                        seq=s, tm=1024, tn=512)
        x2 = _matmul_res([gated], ffn_w_down, i, x2, tm=512, tn=512)
    return _rmsnorm(x2, final_norm, F32, tm=512).reshape(b, s, d)
```

```python
import functools
import math

import jax
import jax.numpy as jnp
from jax import lax
from jax.experimental import pallas as pl
from jax.experimental.pallas import tpu as pltpu

F32 = jnp.float32
BF16 = jnp.bfloat16

EPS = 1e-6
LOG2E = math.log2(math.e)
HEAD_DIM = 128
N_HEADS = 16
SWA_HEAD_DIM = 64
N_SWA_HEADS = 32
N_SWA_KV_HEADS = 4
SWA_GROUP = N_SWA_HEADS // N_SWA_KV_HEADS
WINDOW = 128
Q_LORA = 512
KV_LORA = 256
QK_NOPE = 128
QK_ROPE = 64
ROPE_THETA = 10000.0
MEM_HEADS = 4
MEM_W = MEM_HEADS * HEAD_DIM
NUM_BUCKETS = 32
MAX_DISTANCE = 128
CONV_WIDTH = 3

V7X_VMEM_BYTES = 64 * 1024 * 1024
VMEM_CAP = V7X_VMEM_BYTES - 8 * 1024 * 1024
LANES = 128
IN_PROJ_ROWS = 1024
TAIL = 8


def _cparams(sems, vmem_bytes):
    return pltpu.CompilerParams(dimension_semantics=sems,
                                vmem_limit_bytes=int(min(max(vmem_bytes, 16 * 2**20), VMEM_CAP)))


def _nbytes(shape, dtype):
    return math.prod(shape) * jnp.dtype(dtype).itemsize


def _rms(x, g):
    return x * lax.rsqrt(jnp.mean(x * x, axis=-1, keepdims=True) + EPS) * g


def _dot(a, b):
    return jnp.dot(a, b, preferred_element_type=F32)


def _dot_nt(a, b):
    return lax.dot_general(a, b, (((1,), (1,)), ((), ())), preferred_element_type=F32)


def _norm_matmul_kernel(x_ref, g_ref, w_ref, cs_ref, o_ref, xn_ref):
    @pl.when(pl.program_id(1) == 0)
    def _():
        xn_ref[...] = _rms(x_ref[...], g_ref[...]).astype(BF16)

    o_ref[...] = (_dot(xn_ref[...], w_ref[...]) * cs_ref[...]).astype(o_ref.dtype)


def _w_tile_spec(w, layer, k, tn, col_block):
    if w.ndim == 3:
        return pl.BlockSpec((None, k, tn), lambda i, j: (layer, 0, col_block(j)))
    return pl.BlockSpec((k, tn), lambda i, j: (0, col_block(j)))


def _norm_matmul(x, g, w, out_dtype, tm, tn, scaled_cols=0, col_scale=1.0, layer=None):
    m, k = x.shape
    n = w.shape[-1]
    cs = jnp.where(jnp.arange(n) < scaled_cols, col_scale, 1.0).astype(F32).reshape(1, n)
    vmem = (2 * (_nbytes((tm, k), F32) + _nbytes((k, tn), BF16) + _nbytes((tm, tn), out_dtype))
            + _nbytes((tm, k), BF16) + 2 * _nbytes((tm, k), F32) + _nbytes((tm, tn), F32))
    return pl.pallas_call(
        _norm_matmul_kernel,
        grid=(m // tm, n // tn),
        in_specs=[pl.BlockSpec((tm, k), lambda i, j: (i, 0)),
                  pl.BlockSpec((1, k), lambda i, j: (0, 0)),
                  _w_tile_spec(w, layer, k, tn, lambda j: j),
                  pl.BlockSpec((1, tn), lambda i, j: (0, j))],
        out_specs=pl.BlockSpec((tm, tn), lambda i, j: (i, j)),
        out_shape=jax.ShapeDtypeStruct((m, n), out_dtype),
        scratch_shapes=[pltpu.VMEM((tm, k), BF16)],
        compiler_params=_cparams(("parallel", "arbitrary"), vmem),
        name="norm_matmul",
    )(x, g.reshape(1, k), w, cs)


def _matmul_res_kernel(*refs, n_a):
    a_refs, w_refs = refs[:n_a], refs[n_a:2 * n_a]
    x_ref, o_ref = refs[2 * n_a], refs[2 * n_a + 1]
    wb_refs = refs[2 * n_a + 2:]

    @pl.when(pl.program_id(1) == 0)
    def _():
        for w_ref, wb_ref in zip(w_refs, wb_refs):
            wb_ref[...] = w_ref[...].astype(BF16)

    acc = x_ref[...]
    for a_ref, wb_ref in zip(a_refs, wb_refs):
        acc = acc + _dot(a_ref[...], wb_ref[...])
    o_ref[...] = acc


def _matmul_res(a_list, w, layer, x, tm, tn):
    m, n = x.shape
    vmem = 6 * _nbytes((tm, tn), F32)
    a_specs, w_specs, scratch = [], [], []
    row = 0
    for a in a_list:
        kp = a.shape[1]
        row_block = row // kp
        assert row_block * kp == row
        a_specs.append(pl.BlockSpec((tm, kp), lambda j, i: (i, 0)))
        w_specs.append(pl.BlockSpec((None, kp, tn), lambda j, i, row_block=row_block: (layer, row_block, j)))
        scratch.append(pltpu.VMEM((kp, tn), BF16))
        vmem += 2 * _nbytes((tm, kp), a.dtype) + 2 * _nbytes((kp, tn), F32) + 2 * _nbytes((kp, tn), BF16)
        row += kp
    assert row == w.shape[1]
    return pl.pallas_call(
        functools.partial(_matmul_res_kernel, n_a=len(a_list)),
        grid=(n // tn, m // tm),
        in_specs=a_specs + w_specs + [pl.BlockSpec((tm, tn), lambda j, i: (i, j))],
        out_specs=pl.BlockSpec((tm, tn), lambda j, i: (i, j)),
        out_shape=jax.ShapeDtypeStruct((m, n), F32),
        scratch_shapes=scratch,
        compiler_params=_cparams(("parallel", "arbitrary"), vmem),
        name="matmul_res",
    )(*a_list, *([w] * len(a_list)), x)


def _rmsnorm_kernel(x_ref, g_ref, o_ref):
    o_ref[...] = _rms(x_ref[...], g_ref[...]).astype(o_ref.dtype)


def _rmsnorm(x, g, out_dtype, tm):
    m, k = x.shape
    return pl.pallas_call(
        _rmsnorm_kernel,
        grid=(m // tm,),
        in_specs=[pl.BlockSpec((tm, k), lambda i: (i, 0)), pl.BlockSpec((1, k), lambda i: (0, 0))],
        out_specs=pl.BlockSpec((tm, k), lambda i: (i, 0)),
        out_shape=jax.ShapeDtypeStruct((m, k), out_dtype),
        compiler_params=_cparams(("parallel",), 8 * _nbytes((tm, k), F32)),
        name="rmsnorm",
    )(x, g.reshape(1, k))


def _ffn_up_kernel(xn_ref, wg_ref, wv_ref, cwg_ref, cwv_ref, cbg_ref, cbv_ref, o_ref,
                   wgb_ref, wvb_ref, tail_g_ref, tail_v_ref, *u_refs, rows, tiles_per_seq):
    i = pl.program_id(1)

    @pl.when(i == 0)
    def _():
        wgb_ref[...] = wg_ref[...].astype(BF16)
        wvb_ref[...] = wv_ref[...].astype(BF16)

    @pl.when(i % tiles_per_seq == 0)
    def _():
        tail_g_ref[...] = jnp.zeros(tail_g_ref.shape, F32)
        tail_v_ref[...] = jnp.zeros(tail_v_ref.shape, F32)

    n_chunks = len(u_refs) // 2

    for r in range(n_chunks):
        xr = xn_ref[r * rows:(r + 1) * rows, :]

        def conv(wb_ref, cw_ref, cb_ref, u_ref, before):
            u_ref[TAIL:, :] = _dot(xr, wb_ref[...])
            u_ref[0:TAIL, :] = before
            c = cb_ref[...]
            for tap in range(CONV_WIDTH):
                start = TAIL - (CONV_WIDTH - 1) + tap
                c = c + cw_ref[tap:tap + 1, :] * u_ref[start:start + rows, :]
            return c

        if r == 0:
            before_g, before_v = tail_g_ref[...], tail_v_ref[...]
        else:
            before_g, before_v = u_refs[2 * r - 2][rows:, :], u_refs[2 * r - 1][rows:, :]
        gate = conv(wgb_ref, cwg_ref, cbg_ref, u_refs[2 * r], before_g)
        val = conv(wvb_ref, cwv_ref, cbv_ref, u_refs[2 * r + 1], before_v)
        o_ref[r * rows:(r + 1) * rows, :] = (gate * (1.0 / (1.0 + jnp.exp(-gate))) * val).astype(o_ref.dtype)

    tail_g_ref[...] = u_refs[2 * n_chunks - 2][rows:, :]
    tail_v_ref[...] = u_refs[2 * n_chunks - 1][rows:, :]


def _ffn_up(xn, w_up, layer, conv_w, conv_b, seq, tm, tn, rows=256):
    m, k = xn.shape
    d_ff = w_up.shape[-1] // 2
    nj = d_ff // tn
    n_chunks = tm // rows
    vmem = (2 * (_nbytes((tm, k), BF16) + 2 * _nbytes((k, tn), F32) + _nbytes((tm, tn), BF16))
            + 2 * _nbytes((k, tn), BF16) + 2 * n_chunks * _nbytes((rows + TAIL, tn), F32)
            + 2 * _nbytes((k, tn), F32) + 6 * n_chunks * _nbytes((rows, tn), F32))
    return pl.pallas_call(
        functools.partial(_ffn_up_kernel, rows=rows, tiles_per_seq=seq // tm),
        grid=(nj, m // tm),
        in_specs=[pl.BlockSpec((tm, k), lambda j, i: (i, 0)),
                  pl.BlockSpec((None, k, tn), lambda j, i: (layer, 0, j)),
                  pl.BlockSpec((None, k, tn), lambda j, i: (layer, 0, nj + j)),
                  pl.BlockSpec((CONV_WIDTH, tn), lambda j, i: (0, j)),
                  pl.BlockSpec((CONV_WIDTH, tn), lambda j, i: (0, nj + j)),
                  pl.BlockSpec((1, tn), lambda j, i: (0, j)),
                  pl.BlockSpec((1, tn), lambda j, i: (0, nj + j))],
        out_specs=pl.BlockSpec((tm, tn), lambda j, i: (i, j)),
        out_shape=jax.ShapeDtypeStruct((m, d_ff), BF16),
        scratch_shapes=[pltpu.VMEM((k, tn), BF16), pltpu.VMEM((k, tn), BF16),
                        pltpu.VMEM((TAIL, tn), F32), pltpu.VMEM((TAIL, tn), F32)]
        + [pltpu.VMEM((rows + TAIL, tn), F32)] * (2 * n_chunks),
        compiler_params=_cparams(("arbitrary", "arbitrary"), vmem),
        name="ffn_up_conv_gate",
    )(xn, w_up, w_up, conv_w, conv_w, conv_b.reshape(1, -1), conv_b.reshape(1, -1))


def _eye(n):
    return jnp.where(lax.broadcasted_iota(jnp.int32, (n, n), 0) == lax.broadcasted_iota(jnp.int32, (n, n), 1),
                     1.0, 0.0).astype(BF16)


def _transpose_bf16(x):
    return _dot_nt(_eye(x.shape[1]), x)


def _fill_v_transposed(v_ref, vt_ref, t):
    s, dv = v_ref.shape
    for c in range(s // t):
        vt_ref[0:dv, c * t:(c + 1) * t] = _transpose_bf16(v_ref[c * t:(c + 1) * t, :]).astype(BF16)
    if vt_ref.shape[0] > dv:
        vt_ref[dv:, :] = jnp.ones((vt_ref.shape[0] - dv, s), BF16)


def _flash_kernel(*refs, t, dk, dv, group, has_aug):
    if has_aug:
        q_ref, k_ref, v_ref, qa_ref, ka_ref, o_ref, vt_ref, m_ref, acc_ref, s_ref = refs
    else:
        q_ref, k_ref, v_ref, o_ref, vt_ref, m_ref, acc_ref, s_ref = refs
    qi = pl.program_id(2)
    heads = range(group)

    @pl.when(qi == 0)
    def _():
        for g in heads:
            _fill_v_transposed(v_ref.at[:, g * dv:(g + 1) * dv], vt_ref.at[g], t)

    qs = []
    for g in heads:
        q = q_ref[:, g * dk:(g + 1) * dk]
        qs.append(jnp.concatenate([q, qa_ref[g]], axis=1) if has_aug else q)
    m_ref[...] = jnp.full(m_ref.shape, -jnp.inf, F32)
    acc_ref[...] = jnp.zeros(acc_ref.shape, F32)

    def scores(g, kb):
        ks = pl.multiple_of(kb * t, t)
        k = k_ref[pl.ds(ks, t), g * dk:(g + 1) * dk]
        if has_aug:
            k = jnp.concatenate([k, ka_ref[g, pl.ds(ks, t), :]], axis=1)
        return _dot_nt(k, qs[g])

    def col_max(s):
        return jnp.max(s, axis=0, keepdims=True)

    def update(g, s, s_max, kb):
        m_prev = m_ref[g]
        m_new = jnp.maximum(m_prev, s_max)
        m_ref[g] = m_new
        p = jnp.exp2(s - m_new).astype(BF16)
        ks = pl.multiple_of(kb * t, t)
        acc_ref[g] = jnp.exp2(m_prev - m_new) * acc_ref[g] + _dot(vt_ref[g, :, pl.ds(ks, t)], p)

    def body(kb, s_max):
        nxt = []
        for g in heads:
            update(g, s_ref[g], s_max[g], kb)
            s_next = scores(g, kb + 1)
            nxt.append(col_max(s_next))
            s_ref[g] = s_next
        return tuple(nxt)

    first = []
    for g in heads:
        s0 = scores(g, 0)
        first.append(col_max(s0))
        s_ref[g] = s0
    lax.fori_loop(0, qi, body, tuple(first))
    key = lax.broadcasted_iota(jnp.int32, (t, t), 0)
    qry = lax.broadcasted_iota(jnp.int32, (t, t), 1)
    outs = []
    for g in heads:
        s = jnp.where(key <= qry, s_ref[g], -jnp.inf)
        update(g, s, col_max(s), qi)
        acc = acc_ref[g]
        outs.append((acc[0:dv, :] / acc[dv:dv + 1, :]).T)
    o_ref[...] = jnp.concatenate(outs, axis=1).astype(o_ref.dtype)


def _flash_attention(q_arr, k_arr, v_arr, q_off, k_off, v_off, dk, dv, n_heads, aug=None, t=512, group=4):
    b, s, _ = q_arr.shape
    assert q_off % group == 0 and k_off % group == 0 and v_off % group == 0 and n_heads % group == 0
    qo, ko, vo = q_off // group, k_off // group, v_off // group
    in_specs = [pl.BlockSpec((None, t, group * dk), lambda bi, h, qi: (bi, qi, qo + h)),
                pl.BlockSpec((None, s, group * dk), lambda bi, h, qi: (bi, 0, ko + h)),
                pl.BlockSpec((None, s, group * dv), lambda bi, h, qi: (bi, 0, vo + h))]
    args = [q_arr, k_arr, v_arr]
    vmem = 2 * group * (_nbytes((t, dk), BF16) + _nbytes((s, dk), BF16) + _nbytes((s, dv), BF16)
                        + _nbytes((t, dv), BF16))
    if aug is not None:
        in_specs += [pl.BlockSpec((None, group, t, LANES), lambda bi, h, qi: (bi, h, qi, 0)),
                     pl.BlockSpec((None, group, s, LANES), lambda bi, h, qi: (bi, h, 0, 0))]
        args += list(aug)
        vmem += 2 * group * (_nbytes((t, LANES), BF16) + _nbytes((s, LANES), BF16))
    acc_rows = dv + 16
    vmem += group * (_nbytes((acc_rows, s), BF16) + 2 * _nbytes((acc_rows, t), F32) + 8 * _nbytes((t, t), F32))
    return pl.pallas_call(
        functools.partial(_flash_kernel, t=t, dk=dk, dv=dv, group=group, has_aug=aug is not None),
        grid=(b, n_heads // group, s // t),
        in_specs=in_specs,
        out_specs=pl.BlockSpec((None, t, group * dv), lambda bi, h, qi: (bi, qi, h)),
        out_shape=jax.ShapeDtypeStruct((b, s, n_heads * dv), BF16),
        scratch_shapes=[pltpu.VMEM((group, acc_rows, s), BF16), pltpu.VMEM((group, 1, t), F32),
                        pltpu.VMEM((group, acc_rows, t), F32), pltpu.VMEM((group, t, t), F32)],
        compiler_params=_cparams(("parallel", "parallel", "arbitrary"), vmem),
        name="flash_attention",
    )(*args)


def _sb_kernel(q_ref, k_ref, v_ref, o_ref, vt_ref, carry_ref, acc_ref, s_ref, a_ref, *, t, sub, d, group):
    qi = pl.program_id(2)
    heads = range(group)

    @pl.when(qi == 0)
    def _():
        for g in heads:
            _fill_v_transposed(v_ref.at[:, g * d:(g + 1) * d], vt_ref.at[g], t)

    carry_ref[...] = jnp.zeros(carry_ref.shape, F32)
    acc_ref[...] = jnp.zeros(acc_ref.shape, F32)
    qs = [q_ref[:, g * d:(g + 1) * d] for g in heads]
    r = lax.broadcasted_iota(jnp.int32, (sub + 16, sub), 0)
    c = lax.broadcasted_iota(jnp.int32, (sub + 16, sub), 1)
    suffix = jnp.where(((c > r) & (r < sub)) | (r == sub), 1.0, 0.0).astype(BF16)
    suffix = jnp.concatenate([suffix, suffix], axis=1)

    def scores(g, kb):
        ks = pl.multiple_of(kb * t, t)
        return _dot_nt(k_ref[pl.ds(ks, t), g * d:(g + 1) * d], qs[g])

    def weights(g, s, on_diagonal):
        later = carry_ref[g]
        out = [None] * (t // sub)
        for i in reversed(range(t // sub)):
            q0 = i * sub if on_diagonal else 0
            z = s[i * sub:(i + 1) * sub, q0:]
            neg_abs = lax.bitcast_convert_type(
                lax.bitcast_convert_type(z, jnp.uint32) | jnp.uint32(0x80000000), F32)
            log_beta = jnp.minimum(z, 0.0) - jnp.log2(1.0 + jnp.exp2(neg_abs))
            log_keep = log_beta - z
            if on_diagonal:
                key = lax.broadcasted_iota(jnp.int32, z.shape, 0) + i * sub
                strict = key < lax.broadcasted_iota(jnp.int32, z.shape, 1) + q0
                log_keep = jnp.where(strict, log_keep, 0.0)
            hi = log_keep.astype(BF16)
            lo = (log_keep - hi.astype(F32)).astype(BF16)
            sums = _dot(suffix, jnp.concatenate([hi, lo], axis=0))
            seen = later[:, q0:]
            a = jnp.exp2(log_beta + (sums[0:sub, :] + seen[0:1, :]))
            if on_diagonal:
                a = jnp.where(strict, a, 0.0)
            a = a.astype(BF16)
            total = seen + sums[sub:sub + 8, :]
            if q0:
                a = jnp.concatenate([jnp.zeros((sub, q0), BF16), a], axis=1)
                total = jnp.concatenate([later[:, 0:q0], total], axis=1)
            out[i] = a
            later = total
        carry_ref[g] = later
        return jnp.concatenate(out, axis=0)

    def accumulate(g, a, kb):
        ks = pl.multiple_of(kb * t, t)
        acc_ref[g] += _dot(vt_ref[g, :, pl.ds(ks, t)], a)

    def finish():
        o_ref[...] = jnp.concatenate([acc_ref[g].T for g in heads], axis=1).astype(o_ref.dtype)

    @pl.when(qi == 0)
    def _():
        for g in heads:
            accumulate(g, weights(g, scores(g, 0), True), 0)
        finish()

    @pl.when(qi > 0)
    def _():
        for g in heads:
            a_ref[g] = weights(g, scores(g, qi), True)
            s_ref[g] = scores(g, qi - 1)

        def body(i, carry):
            for g in heads:
                accumulate(g, a_ref[g], qi - i)
                a_ref[g] = weights(g, s_ref[g], False)
                s_ref[g] = scores(g, qi - i - 2)
            return carry

        lax.fori_loop(0, qi - 1, body, 0)
        for g in heads:
            accumulate(g, a_ref[g], 1)
            accumulate(g, weights(g, s_ref[g], False), 0)
        finish()


def _sb_attention(proj, n_heads, t=512, group=4):
    b, s, _ = proj.shape
    d = HEAD_DIM
    assert n_heads % group == 0
    blocks = n_heads // group
    vmem = group * (2 * (2 * _nbytes((t, d), BF16) + 2 * _nbytes((s, d), BF16))
                    + _nbytes((d, s), BF16) + 2 * _nbytes((d, t), F32) + 12 * _nbytes((t, t), F32))
    return pl.pallas_call(
        functools.partial(_sb_kernel, t=t, sub=LANES, d=d, group=group),
        grid=(b, blocks, s // t),
        in_specs=[pl.BlockSpec((None, t, group * d), lambda bi, h, qi: (bi, qi, h)),
                  pl.BlockSpec((None, s, group * d), lambda bi, h, qi: (bi, 0, blocks + h)),
                  pl.BlockSpec((None, s, group * d), lambda bi, h, qi: (bi, 0, 2 * blocks + h))],
        out_specs=pl.BlockSpec((None, t, group * d), lambda bi, h, qi: (bi, qi, h)),
        out_shape=jax.ShapeDtypeStruct((b, s, n_heads * d), BF16),
        scratch_shapes=[pltpu.VMEM((group, d, s), BF16), pltpu.VMEM((group, 8, t), F32),
                        pltpu.VMEM((group, d, t), F32), pltpu.VMEM((group, t, t), F32),
                        pltpu.VMEM((group, t, t), BF16)],
        compiler_params=_cparams(("parallel", "parallel", "arbitrary"), vmem),
        name="stick_breaking_attention",
    )(proj, proj, proj)


def _split3(x):
    hi = x.astype(BF16)
    rest = x - hi.astype(F32)
    mid = rest.astype(BF16)
    lo = (rest - mid.astype(F32)).astype(BF16)
    return hi, mid, lo


def _fox_gate_kernel(fl_ref, b_ref, pq_ref, pk_ref, oq_ref, ok_ref, qa_ref, ka_ref, carry_ref):
    ts = fl_ref.shape[0]

    @pl.when(pl.program_id(1) == 0)
    def _():
        carry_ref[...] = jnp.zeros(carry_ref.shape, F32)

    z = fl_ref[...] + b_ref[...]
    log_f = jnp.minimum(z, 0.0) - jnp.log(1.0 + jnp.exp(-jnp.abs(z)))
    row = lax.broadcasted_iota(jnp.int32, (ts, ts), 0)
    col = lax.broadcasted_iota(jnp.int32, (ts, ts), 1)
    prefix = jnp.where(col <= row, 1.0, 0.0).astype(BF16)
    hi, mid, lo = _split3(log_f)
    c = _dot(prefix, hi) + _dot(prefix, mid) + _dot(prefix, lo) + carry_ref[0:1, :]
    carry_ref[0:1, :] = c[ts - 1:ts, :]
    parts = jnp.concatenate(_split3(c * LOG2E), axis=1)
    qa = (_dot(parts, pq_ref[...]) + oq_ref[...]).astype(BF16)
    ka = (_dot(parts, pk_ref[...]) + ok_ref[...]).astype(BF16)
    for h in range(qa_ref.shape[0]):
        qa_ref[h] = qa[:, h * LANES:(h + 1) * LANES]
        ka_ref[h] = ka[:, h * LANES:(h + 1) * LANES]


def _fox_aug_tables(n_heads):
    part, src = jnp.arange(3 * LANES) // LANES, jnp.arange(3 * LANES) % LANES
    head, lane = jnp.arange(n_heads * LANES) // LANES, jnp.arange(n_heads * LANES) % LANES
    mine = src[:, None] == head[None, :]
    pq = jnp.where(mine & (lane[None, :] == part[:, None] + 3), 1.0, 0.0).astype(BF16)
    pk = jnp.where(mine & (lane[None, :] == part[:, None]), -1.0, 0.0).astype(BF16)
    oq = jnp.where(lane < 3, 1.0, 0.0).astype(F32)[None]
    ok = jnp.where((lane >= 3) & (lane < 6), 1.0, 0.0).astype(F32)[None]
    return pq, pk, oq, ok


def _fox_gates(f_logit, b_f, n_heads, ts=256):
    b, s, w = f_logit.shape
    hw = n_heads * LANES
    out = jax.ShapeDtypeStruct((b, n_heads, s, LANES), BF16)
    const = lambda bi, i: (0, 0)
    return pl.pallas_call(
        _fox_gate_kernel,
        grid=(b, s // ts),
        in_specs=[pl.BlockSpec((None, ts, w), lambda bi, i: (bi, i, 0)),
                  pl.BlockSpec((1, w), const),
                  pl.BlockSpec((3 * LANES, hw), const), pl.BlockSpec((3 * LANES, hw), const),
                  pl.BlockSpec((1, hw), const), pl.BlockSpec((1, hw), const)],
        out_specs=[pl.BlockSpec((None, n_heads, ts, LANES), lambda bi, i: (bi, 0, i, 0)),
                   pl.BlockSpec((None, n_heads, ts, LANES), lambda bi, i: (bi, 0, i, 0))],
        out_shape=[out, out],
        scratch_shapes=[pltpu.VMEM((8, w), F32)],
        compiler_params=_cparams(("parallel", "arbitrary"), 32 * 2**20),
        name="fox_gate_cumsum",
    )(f_logit, b_f.reshape(1, w), *_fox_aug_tables(n_heads))


def _mem_attn_kernel(q_ref, kv_ref, o_ref, *, scale):
    outs = []
    for h in range(MEM_HEADS):
        q = q_ref[:, h * HEAD_DIM:(h + 1) * HEAD_DIM].astype(BF16)
        k = kv_ref[:, h * HEAD_DIM:(h + 1) * HEAD_DIM]
        v = kv_ref[:, MEM_W + h * HEAD_DIM:MEM_W + (h + 1) * HEAD_DIM]
        s = _dot_nt(q, k) * scale
        p = jnp.exp(s - jnp.max(s, axis=1, keepdims=True))
        o = _dot(p.astype(BF16), v) / jnp.sum(p, axis=1, keepdims=True)
        outs.append(o.astype(o_ref.dtype))
    o_ref[...] = jnp.concatenate(outs, axis=1)


def _mem_attention(q_arr, q_block, mem_kv, tq=512):
    b, s, _ = q_arr.shape
    length = mem_kv.shape[1]
    vmem = (2 * (_nbytes((tq, MEM_W), q_arr.dtype) + _nbytes((length, 2 * MEM_W), BF16)
                 + _nbytes((tq, MEM_W), BF16)) + 8 * _nbytes((tq, length), F32))
    return pl.pallas_call(
        functools.partial(_mem_attn_kernel, scale=HEAD_DIM ** -0.5),
        grid=(b, s // tq),
        in_specs=[pl.BlockSpec((None, tq, MEM_W), lambda bi, i: (bi, i, q_block)),
                  pl.BlockSpec((None, length, 2 * MEM_W), lambda bi, i: (bi, 0, 0))],
        out_specs=pl.BlockSpec((None, tq, MEM_W), lambda bi, i: (bi, i, 0)),
        out_shape=jax.ShapeDtypeStruct((b, s, MEM_W), BF16),
        compiler_params=_cparams(("parallel", "parallel"), vmem),
        name="memory_attention",
    )(q_arr, mem_kv)


def _t5_bucket_table():
    max_exact = NUM_BUCKETS // 2
    kj = jnp.arange(2 * WINDOW)[:, None]
    qi = jnp.arange(WINDOW)[None, :]
    signed = WINDOW + qi - kj
    dist = jnp.maximum(signed, 0)
    d = jnp.maximum(dist, 1).astype(F32)
    large = max_exact + (jnp.log(d / max_exact) / math.log(MAX_DISTANCE / max_exact)
                         * (NUM_BUCKETS - max_exact)).astype(jnp.int32)
    bucket = jnp.where(dist < max_exact, dist, jnp.minimum(large, NUM_BUCKETS - 1))
    return jnp.where((signed >= 0) & (signed < WINDOW), bucket, -1).astype(jnp.int32)


def _t5_bias_kernel(rb_ref, bucket_ref, o_ref):
    kvh = pl.program_id(0)
    bucket = bucket_ref[...]
    for g in range(SWA_GROUP):
        bias = jnp.full(bucket.shape, -jnp.inf, F32)
        for b in range(NUM_BUCKETS):
            bias = jnp.where(bucket == b, rb_ref[b, kvh * SWA_GROUP + g] * LOG2E, bias)
        o_ref[:, g * WINDOW:(g + 1) * WINDOW] = bias


def _t5_bias(rel_bias):
    return pl.pallas_call(
        _t5_bias_kernel,
        grid=(N_SWA_KV_HEADS,),
        in_specs=[pl.BlockSpec(memory_space=pltpu.SMEM),
                  pl.BlockSpec((2 * WINDOW, WINDOW), lambda h: (0, 0))],
        out_specs=pl.BlockSpec((None, 2 * WINDOW, SWA_GROUP * WINDOW), lambda h: (h, 0, 0)),
        out_shape=jax.ShapeDtypeStruct((N_SWA_KV_HEADS, 2 * WINDOW, SWA_GROUP * WINDOW), F32),
        compiler_params=_cparams(("parallel",), 16 * 2**20),
        name="t5_bias",
    )(rel_bias, _t5_bucket_table())


def _swa_kernel(sink_ref, q_ref, kvc_ref, kvp_ref, bias_ref, o_ref):
    n = pl.program_id(1)
    d = SWA_HEAD_DIM
    kv_w = N_SWA_KV_HEADS * d
    k_win = jnp.concatenate([kvp_ref[:, 0:kv_w], kvc_ref[:, 0:kv_w]], axis=0)
    v_win = jnp.concatenate([kvp_ref[:, kv_w:2 * kv_w], kvc_ref[:, kv_w:2 * kv_w]], axis=0)
    v_t = v_win.astype(F32).T.astype(BF16)
    no_prev = jnp.where(n == 0, -jnp.inf, 0.0)
    outs = []
    for kvh in range(N_SWA_KV_HEADS):
        heads = range(kvh * SWA_GROUP, (kvh + 1) * SWA_GROUP)
        q = jnp.concatenate([q_ref[:, h * d:(h + 1) * d] for h in heads], axis=0)
        sink = jnp.concatenate([jnp.full((1, WINDOW), sink_ref[h] * LOG2E, F32) for h in heads], axis=1)
        s = _dot_nt(k_win[:, kvh * d:(kvh + 1) * d], q) + bias_ref[kvh]
        s = jnp.concatenate([s[0:WINDOW, :] + no_prev, s[WINDOW:, :]], axis=0)
        m = jnp.maximum(jnp.max(s, axis=0, keepdims=True), sink)
        p = jnp.exp2(s - m)
        denom = jnp.sum(p, axis=0, keepdims=True) + jnp.exp2(sink - m)
        o_t = _dot(v_t[kvh * d:(kvh + 1) * d, :], p.astype(BF16)) / denom
        outs += [o_t[:, g * WINDOW:(g + 1) * WINDOW] for g in range(SWA_GROUP)]
    o_ref[...] = jnp.concatenate(outs, axis=0).T.astype(o_ref.dtype)


def _swa_attention(proj, sinks, bias):
    b, s, _ = proj.shape
    q_w = N_SWA_HEADS * SWA_HEAD_DIM
    kv_w = 2 * N_SWA_KV_HEADS * SWA_HEAD_DIM
    kv_block = q_w // kv_w
    vmem = (2 * (2 * _nbytes((WINDOW, q_w), BF16) + 2 * _nbytes((WINDOW, kv_w), BF16)
                 + _nbytes(bias.shape, F32)) + 16 * 2**20)
    return pl.pallas_call(
        _swa_kernel,
        grid=(b, s // WINDOW),
        in_specs=[pl.BlockSpec(memory_space=pltpu.SMEM),
                  pl.BlockSpec((None, WINDOW, q_w), lambda bi, n: (bi, n, 0)),
                  pl.BlockSpec((None, WINDOW, kv_w), lambda bi, n: (bi, n, kv_block)),
                  pl.BlockSpec((None, WINDOW, kv_w), lambda bi, n: (bi, jnp.maximum(n - 1, 0), kv_block)),
                  pl.BlockSpec(bias.shape, lambda bi, n: (0, 0, 0))],
        out_specs=pl.BlockSpec((None, WINDOW, q_w), lambda bi, n: (bi, n, 0)),
        out_shape=jax.ShapeDtypeStruct((b, s, q_w), BF16),
        compiler_params=_cparams(("parallel", "arbitrary"), vmem),
        name="sliding_window_attention",
    )(sinks, proj, proj, proj, bias)


def _rope_table_kernel(pos_ref, invf_ref, o_ref):
    ang = pos_ref[...].astype(F32) * invf_ref[...]
    lane = lax.broadcasted_iota(jnp.int32, ang.shape, 1)
    half = QK_ROPE // 2
    sin_signed = jnp.where(lane < QK_ROPE + half, -jnp.sin(ang), jnp.sin(ang))
    o_ref[...] = jnp.where(lane < QK_ROPE, jnp.cos(ang), sin_signed)


def _rope_table(positions, tm=1024):
    m = positions.shape[0]
    half = QK_ROPE // 2
    inv_freq = ROPE_THETA ** (-jnp.arange(half, dtype=F32) / half)
    invf = jnp.tile(inv_freq, LANES // half).reshape(1, LANES)
    return pl.pallas_call(
        _rope_table_kernel,
        grid=(m // tm,),
        in_specs=[pl.BlockSpec((tm, 1), lambda i: (i, 0)), pl.BlockSpec((1, LANES), lambda i: (0, 0))],
        out_specs=pl.BlockSpec((tm, LANES), lambda i: (i, 0)),
        out_shape=jax.ShapeDtypeStruct((m, LANES), F32),
        compiler_params=_cparams(("parallel",), 32 * 2**20),
        name="rope_table",
    )(positions, invf)


def _apply_rope(x_and_partner, cs):
    z = x_and_partner * cs
    z = z + pltpu.roll(z, QK_ROPE, axis=1)
    lane = lax.broadcasted_iota(jnp.int32, z.shape, 1)
    return jnp.where(lane < QK_ROPE, z, 0.0)


def _mla_q_kernel(cq_ref, g_ref, w_ref, cs_ref, o_ref, cn_ref):
    @pl.when(pl.program_id(1) == 0)
    def _():
        cn_ref[...] = _rms(cq_ref[...], g_ref[...]).astype(BF16)

    y = _dot(cn_ref[...], w_ref[...]) * ((QK_NOPE + QK_ROPE) ** -0.5 * LOG2E)
    cs = cs_ref[...]
    parts = []
    for h in range(y.shape[1] // (2 * LANES)):
        yh = y[:, h * 2 * LANES:(h + 1) * 2 * LANES]
        parts += [yh[:, :QK_NOPE], _apply_rope(yh[:, QK_NOPE:], cs)]
    o_ref[...] = jnp.concatenate(parts, axis=1).astype(o_ref.dtype)


def _mla_q(proj, q_norm, w_uq_aug, cs, tm=512, heads_per_step=4):
    m = proj.shape[0]
    hw = 2 * LANES * heads_per_step
    n_heads = w_uq_aug.shape[1] // hw
    return pl.pallas_call(
        _mla_q_kernel,
        grid=(m // tm, n_heads),
        in_specs=[pl.BlockSpec((tm, Q_LORA), lambda i, h: (i, 1)),
                  pl.BlockSpec((1, Q_LORA), lambda i, h: (0, 0)),
                  pl.BlockSpec((Q_LORA, hw), lambda i, h: (0, h)),
                  pl.BlockSpec((tm, LANES), lambda i, h: (i, 0))],
        out_specs=pl.BlockSpec((tm, hw), lambda i, h: (i, h)),
        out_shape=jax.ShapeDtypeStruct((m, n_heads * hw), BF16),
        scratch_shapes=[pltpu.VMEM((tm, Q_LORA), BF16)],
        compiler_params=_cparams(("parallel", "arbitrary"), 32 * 2**20),
        name="mla_q_expand",
    )(proj, q_norm.reshape(1, Q_LORA), w_uq_aug, cs)


def _mla_kv_kernel(ckv_ref, g_ref, w_ref, kr_ref, cs_ref, k_ref, v_ref, cn_ref, kpe_ref):
    @pl.when(pl.program_id(1) == 0)
    def _():
        cn_ref[...] = _rms(ckv_ref[...], g_ref[...]).astype(BF16)
        kpe_ref[...] = _apply_rope(kr_ref[...], cs_ref[...])

    y = _dot(cn_ref[...], w_ref[...])
    kpe = kpe_ref[...]
    k_parts, v_parts = [], []
    for h in range(y.shape[1] // (2 * LANES)):
        yh = y[:, h * 2 * LANES:(h + 1) * 2 * LANES]
        k_parts += [yh[:, :QK_NOPE], kpe]
        v_parts.append(yh[:, QK_NOPE:])
    k_ref[...] = jnp.concatenate(k_parts, axis=1).astype(k_ref.dtype)
    v_ref[...] = jnp.concatenate(v_parts, axis=1).astype(v_ref.dtype)


def _mla_kv(proj, kv_norm, w_ukv, cs, tm=512, heads_per_step=4):
    m = proj.shape[0]
    hw = 2 * LANES * heads_per_step
    n_heads = w_ukv.shape[1] // hw
    return pl.pallas_call(
        _mla_kv_kernel,
        grid=(m // tm, n_heads),
        in_specs=[pl.BlockSpec((tm, KV_LORA), lambda i, h: (i, 4)),
                  pl.BlockSpec((1, KV_LORA), lambda i, h: (0, 0)),
                  pl.BlockSpec((KV_LORA, hw), lambda i, h: (0, h)),
                  pl.BlockSpec((tm, LANES), lambda i, h: (i, 10)),
                  pl.BlockSpec((tm, LANES), lambda i, h: (i, 0))],
        out_specs=[pl.BlockSpec((tm, hw), lambda i, h: (i, h)),
                   pl.BlockSpec((tm, hw // 2), lambda i, h: (i, h))],
        out_shape=[jax.ShapeDtypeStruct((m, n_heads * hw), BF16),
                   jax.ShapeDtypeStruct((m, n_heads * hw // 2), BF16)],
        scratch_shapes=[pltpu.VMEM((tm, KV_LORA), BF16), pltpu.VMEM((tm, LANES), F32)],
        compiler_params=_cparams(("parallel", "arbitrary"), 32 * 2**20),
        name="mla_kv_expand",
    )(proj, kv_norm.reshape(1, KV_LORA), w_ukv, proj, cs)


def _swap_halves(w):
    half = w.shape[-1] // 2
    return jnp.concatenate([w[..., half:], w[..., :half]], axis=-1)


def _mixer_out(x2, o, o_mem, w_out, b, s):
    return _matmul_res([o.reshape(b * s, o.shape[-1]), o_mem.reshape(b * s, MEM_W)],
                       w_out[0], w_out[1], x2, tm=1024, tn=512)


def _sb_layer(x2, mem_kv, g, w_in, w_out, b, s):
    proj = _norm_matmul(x2, g, w_in.astype(BF16), BF16, tm=IN_PROJ_ROWS, tn=512, scaled_cols=N_HEADS * HEAD_DIM,
                        col_scale=HEAD_DIM ** -0.5 * LOG2E).reshape(b, s, -1)
    o = _sb_attention(proj, N_HEADS)
    o_mem = _mem_attention(proj, 3 * N_HEADS * HEAD_DIM // MEM_W, mem_kv)
    return _mixer_out(x2, o, o_mem, w_out, b, s)


def _fox_layer(x2, mem_kv, g, w_in, b_f, w_out, b, s):
    qkv_w = 3 * N_HEADS * HEAD_DIM
    w_main = jnp.concatenate([w_in[:, :qkv_w], w_in[:, qkv_w + N_HEADS:]], axis=1).astype(BF16)
    w_gate = jnp.pad(w_in[:, qkv_w:qkv_w + N_HEADS], ((0, 0), (0, LANES - N_HEADS))).astype(BF16)
    proj = _norm_matmul(x2, g, w_main, BF16, tm=IN_PROJ_ROWS, tn=512, scaled_cols=N_HEADS * HEAD_DIM,
                        col_scale=HEAD_DIM ** -0.5 * LOG2E).reshape(b, s, -1)
    f_logit = _norm_matmul(x2, g, w_gate, F32, tm=512, tn=LANES).reshape(b, s, LANES)
    aug = _fox_gates(f_logit, jnp.pad(b_f, (0, LANES - N_HEADS)), N_HEADS)
    o = _flash_attention(proj, proj, proj, 0, N_HEADS, 2 * N_HEADS, HEAD_DIM, HEAD_DIM, N_HEADS, aug=aug)
    o_mem = _mem_attention(proj, qkv_w // MEM_W, mem_kv)
    return _mixer_out(x2, o, o_mem, w_out, b, s)


def _swa_layer(x2, mem_kv, g, rel_bias, w_in, sinks, w_out, b, s):
    proj = _norm_matmul(x2, g, w_in.astype(BF16), BF16, tm=IN_PROJ_ROWS, tn=512,
                        scaled_cols=N_SWA_HEADS * SWA_HEAD_DIM,
                        col_scale=SWA_HEAD_DIM ** -0.5 * LOG2E).reshape(b, s, -1)
    o = _swa_attention(proj, sinks, _t5_bias(rel_bias))
    q_w = N_SWA_HEADS * SWA_HEAD_DIM
    kv_w = 2 * N_SWA_KV_HEADS * SWA_HEAD_DIM
    o_mem = _mem_attention(proj, (q_w + kv_w) // MEM_W, mem_kv)
    return _mixer_out(x2, o, o_mem, w_out, b, s)


def _mla_layer(x2, mem_kv, g, positions, w_in, q_norm, w_uq, kv_norm, w_ukv, w_out, b, s):
    d = x2.shape[1]
    o1, o2, o3 = Q_LORA, Q_LORA + KV_LORA, Q_LORA + KV_LORA + QK_ROPE
    w_main = jnp.concatenate([w_in[:, o3:], w_in[:, :o3], _swap_halves(w_in[:, o2:o3])], axis=1).astype(BF16)
    proj = _norm_matmul(x2, g, w_main, F32, tm=512, tn=w_main.shape[1])
    cs = _rope_table(positions.reshape(b * s, 1))
    wq = w_uq.reshape(Q_LORA, N_HEADS, QK_NOPE + QK_ROPE)
    wq = jnp.concatenate([wq, _swap_halves(wq[:, :, QK_NOPE:])], axis=-1).reshape(Q_LORA, -1).astype(BF16)
    q = _mla_q(proj, q_norm, wq, cs).reshape(b, s, -1)
    k, v = _mla_kv(proj, kv_norm, w_ukv.astype(BF16), cs)
    k, v = k.reshape(b, s, -1), v.reshape(b, s, -1)
    o = _flash_attention(q, k, v, 0, 0, 0, 2 * LANES, HEAD_DIM, N_HEADS)
    o_mem = _mem_attention(proj.reshape(b, s, -1), 0, mem_kv)
    return _mixer_out(x2, o, o_mem, w_out, b, s)


def kernel(x, mem, positions, rel_bias, attn_norm, mem_norm, w_mem_kv, ffn_norm, ffn_w_up, ffn_conv_w, ffn_conv_b, ffn_w_down, final_norm, sb_w_in, sb_w_out, fox_w_in, fox_b_f, fox_w_out, swa_w_in, swa_sinks, swa_w_out, mla_w_in, mla_q_norm, mla_w_uq, mla_kv_norm, mla_w_ukv, mla_w_out):
    b, s, d = x.shape
    depth = attn_norm.shape[0]
    mem_len = mem.shape[1]
    x2 = x.reshape(b * s, d)
    mem2 = mem.reshape(b * mem_len, d)
    w_mem_kv_bf16 = w_mem_kv.astype(BF16)
    for i in range(depth):
        kind, j = i % 4, i // 4
        mem_kv = _norm_matmul(mem2, mem_norm[i], w_mem_kv_bf16, BF16, tm=b * mem_len, tn=2 * MEM_W,
                              layer=i).reshape(b, mem_len, 2 * MEM_W)
        g = attn_norm[i]
        if kind == 0:
            x2 = _sb_layer(x2, mem_kv, g, sb_w_in[j], (sb_w_out, j), b, s)
        elif kind == 1:
            x2 = _fox_layer(x2, mem_kv, g, fox_w_in[j], fox_b_f[j], (fox_w_out, j), b, s)
        elif kind == 2:
            x2 = _swa_layer(x2, mem_kv, g, rel_bias, swa_w_in[j], swa_sinks[j], (swa_w_out, j), b, s)
        else:
            x2 = _mla_layer(x2, mem_kv, g, positions, mla_w_in[j], mla_q_norm[j], mla_w_uq[j],
                            mla_kv_norm[j], mla_w_ukv[j], (mla_w_out, j), b, s)
        gated = _ffn_up(_rmsnorm(x2, ffn_norm[i], BF16, tm=512), ffn_w_up, i, ffn_conv_w[i], ffn_conv_b[i],
                        seq=s, tm=1024, tn=512)
        x2 = _matmul_res([gated], ffn_w_down, i, x2, tm=512, tn=512)
    return _rmsnorm(x2, final_norm, F32, tm=512).reshape(b, s, d)
```

```python
import functools
import math

import jax
import jax.numpy as jnp
from jax import lax
from jax.experimental import pallas as pl
from jax.experimental.pallas import tpu as pltpu

F32 = jnp.float32
BF16 = jnp.bfloat16

EPS = 1e-6
LOG2E = math.log2(math.e)
HEAD_DIM = 128
N_HEADS = 16
SWA_HEAD_DIM = 64
N_SWA_HEADS = 32
N_SWA_KV_HEADS = 4
SWA_GROUP = N_SWA_HEADS // N_SWA_KV_HEADS
WINDOW = 128
Q_LORA = 512
KV_LORA = 256
QK_NOPE = 128
QK_ROPE = 64
ROPE_THETA = 10000.0
MEM_HEADS = 4
MEM_W = MEM_HEADS * HEAD_DIM
NUM_BUCKETS = 32
MAX_DISTANCE = 128
CONV_WIDTH = 3

V7X_VMEM_BYTES = 64 * 1024 * 1024
VMEM_CAP = V7X_VMEM_BYTES - 8 * 1024 * 1024
LANES = 128
IN_PROJ_ROWS = 1024
TAIL = 8


def _cparams(sems, vmem_bytes):
    return pltpu.CompilerParams(dimension_semantics=sems,
                                vmem_limit_bytes=int(min(max(vmem_bytes, 16 * 2**20), VMEM_CAP)))


def _nbytes(shape, dtype):
    return math.prod(shape) * jnp.dtype(dtype).itemsize


def _rms(x, g):
    return x * lax.rsqrt(jnp.mean(x * x, axis=-1, keepdims=True) + EPS) * g


def _dot(a, b):
    return jnp.dot(a, b, preferred_element_type=F32)


def _dot_nt(a, b):
    return lax.dot_general(a, b, (((1,), (1,)), ((), ())), preferred_element_type=F32)


def _norm_matmul_kernel(x_ref, g_ref, w_ref, cs_ref, o_ref, xn_ref):
    @pl.when(pl.program_id(1) == 0)
    def _():
        xn_ref[...] = _rms(x_ref[...], g_ref[...]).astype(BF16)

    o_ref[...] = (_dot(xn_ref[...], w_ref[...]) * cs_ref[...]).astype(o_ref.dtype)


def _w_tile_spec(w, layer, k, tn, col_block):
    if w.ndim == 3:
        return pl.BlockSpec((None, k, tn), lambda i, j: (layer, 0, col_block(j)))
    return pl.BlockSpec((k, tn), lambda i, j: (0, col_block(j)))


def _norm_matmul(x, g, w, out_dtype, tm, tn, scaled_cols=0, col_scale=1.0, layer=None):
    m, k = x.shape
    n = w.shape[-1]
    cs = jnp.where(jnp.arange(n) < scaled_cols, col_scale, 1.0).astype(F32).reshape(1, n)
    vmem = (2 * (_nbytes((tm, k), F32) + _nbytes((k, tn), BF16) + _nbytes((tm, tn), out_dtype))
            + _nbytes((tm, k), BF16) + 2 * _nbytes((tm, k), F32) + _nbytes((tm, tn), F32))
    return pl.pallas_call(
        _norm_matmul_kernel,
        grid=(m // tm, n // tn),
        in_specs=[pl.BlockSpec((tm, k), lambda i, j: (i, 0)),
                  pl.BlockSpec((1, k), lambda i, j: (0, 0)),
                  _w_tile_spec(w, layer, k, tn, lambda j: j),
                  pl.BlockSpec((1, tn), lambda i, j: (0, j))],
        out_specs=pl.BlockSpec((tm, tn), lambda i, j: (i, j)),
        out_shape=jax.ShapeDtypeStruct((m, n), out_dtype),
        scratch_shapes=[pltpu.VMEM((tm, k), BF16)],
        compiler_params=_cparams(("parallel", "arbitrary"), vmem),
        name="norm_matmul",
    )(x, g.reshape(1, k), w, cs)


def _matmul_res_kernel(*refs, n_a):
    a_refs, w_refs = refs[:n_a], refs[n_a:2 * n_a]
    x_ref, o_ref = refs[2 * n_a], refs[2 * n_a + 1]
    wb_refs = refs[2 * n_a + 2:]

    @pl.when(pl.program_id(1) == 0)
    def _():
        for w_ref, wb_ref in zip(w_refs, wb_refs):
            wb_ref[...] = w_ref[...].astype(BF16)

    acc = x_ref[...]
    for a_ref, wb_ref in zip(a_refs, wb_refs):
        acc = acc + _dot(a_ref[...], wb_ref[...])
    o_ref[...] = acc


def _matmul_res(a_list, w, layer, x, tm, tn):
    m, n = x.shape
    vmem = 6 * _nbytes((tm, tn), F32)
    a_specs, w_specs, scratch = [], [], []
    row = 0
    for a in a_list:
        kp = a.shape[1]
        row_block = row // kp
        assert row_block * kp == row
        a_specs.append(pl.BlockSpec((tm, kp), lambda j, i: (i, 0)))
        w_specs.append(pl.BlockSpec((None, kp, tn), lambda j, i, row_block=row_block: (layer, row_block, j)))
        scratch.append(pltpu.VMEM((kp, tn), BF16))
        vmem += 2 * _nbytes((tm, kp), a.dtype) + 2 * _nbytes((kp, tn), F32) + 2 * _nbytes((kp, tn), BF16)
        row += kp
    assert row == w.shape[1]
    return pl.pallas_call(
        functools.partial(_matmul_res_kernel, n_a=len(a_list)),
        grid=(n // tn, m // tm),
        in_specs=a_specs + w_specs + [pl.BlockSpec((tm, tn), lambda j, i: (i, j))],
        out_specs=pl.BlockSpec((tm, tn), lambda j, i: (i, j)),
        out_shape=jax.ShapeDtypeStruct((m, n), F32),
        scratch_shapes=scratch,
        compiler_params=_cparams(("parallel", "arbitrary"), vmem),
        name="matmul_res",
    )(*a_list, *([w] * len(a_list)), x)


def _rmsnorm_kernel(x_ref, g_ref, o_ref):
    o_ref[...] = _rms(x_ref[...], g_ref[...]).astype(o_ref.dtype)


def _rmsnorm(x, g, out_dtype, tm):
    m, k = x.shape
    return pl.pallas_call(
        _rmsnorm_kernel,
        grid=(m // tm,),
        in_specs=[pl.BlockSpec((tm, k), lambda i: (i, 0)), pl.BlockSpec((1, k), lambda i: (0, 0))],
        out_specs=pl.BlockSpec((tm, k), lambda i: (i, 0)),
        out_shape=jax.ShapeDtypeStruct((m, k), out_dtype),
        compiler_params=_cparams(("parallel",), 8 * _nbytes((tm, k), F32)),
        name="rmsnorm",
    )(x, g.reshape(1, k))


def _ffn_up_kernel(xn_ref, wg_ref, wv_ref, cwg_ref, cwv_ref, cbg_ref, cbv_ref, o_ref,
                   wgb_ref, wvb_ref, tail_g_ref, tail_v_ref, *u_refs, rows, tiles_per_seq):
    i = pl.program_id(1)

    @pl.when(i == 0)
    def _():
        wgb_ref[...] = wg_ref[...].astype(BF16)
        wvb_ref[...] = wv_ref[...].astype(BF16)

    @pl.when(i % tiles_per_seq == 0)
    def _():
        tail_g_ref[...] = jnp.zeros(tail_g_ref.shape, F32)
        tail_v_ref[...] = jnp.zeros(tail_v_ref.shape, F32)

    n_chunks = len(u_refs) // 2

    for r in range(n_chunks):
        xr = xn_ref[r * rows:(r + 1) * rows, :]

        def conv(wb_ref, cw_ref, cb_ref, u_ref, before):
            u_ref[TAIL:, :] = _dot(xr, wb_ref[...])
            u_ref[0:TAIL, :] = before
            c = cb_ref[...]
            for tap in range(CONV_WIDTH):
                start = TAIL - (CONV_WIDTH - 1) + tap
                c = c + cw_ref[tap:tap + 1, :] * u_ref[start:start + rows, :]
            return c

        if r == 0:
            before_g, before_v = tail_g_ref[...], tail_v_ref[...]
        else:
            before_g, before_v = u_refs[2 * r - 2][rows:, :], u_refs[2 * r - 1][rows:, :]
        gate = conv(wgb_ref, cwg_ref, cbg_ref, u_refs[2 * r], before_g)
        val = conv(wvb_ref, cwv_ref, cbv_ref, u_refs[2 * r + 1], before_v)
        o_ref[r * rows:(r + 1) * rows, :] = (gate * (1.0 / (1.0 + jnp.exp(-gate))) * val).astype(o_ref.dtype)

    tail_g_ref[...] = u_refs[2 * n_chunks - 2][rows:, :]
    tail_v_ref[...] = u_refs[2 * n_chunks - 1][rows:, :]


def _ffn_up(xn, w_up, layer, conv_w, conv_b, seq, tm, tn, rows=256):
    m, k = xn.shape
    d_ff = w_up.shape[-1] // 2
    nj = d_ff // tn
    n_chunks = tm // rows
    vmem = (2 * (_nbytes((tm, k), BF16) + 2 * _nbytes((k, tn), F32) + _nbytes((tm, tn), BF16))
            + 2 * _nbytes((k, tn), BF16) + 2 * n_chunks * _nbytes((rows + TAIL, tn), F32)
            + 2 * _nbytes((k, tn), F32) + 6 * n_chunks * _nbytes((rows, tn), F32))
    return pl.pallas_call(
        functools.partial(_ffn_up_kernel, rows=rows, tiles_per_seq=seq // tm),
        grid=(nj, m // tm),
        in_specs=[pl.BlockSpec((tm, k), lambda j, i: (i, 0)),
                  pl.BlockSpec((None, k, tn), lambda j, i: (layer, 0, j)),
                  pl.BlockSpec((None, k, tn), lambda j, i: (layer, 0, nj + j)),
                  pl.BlockSpec((CONV_WIDTH, tn), lambda j, i: (0, j)),
                  pl.BlockSpec((CONV_WIDTH, tn), lambda j, i: (0, nj + j)),
                  pl.BlockSpec((1, tn), lambda j, i: (0, j)),
                  pl.BlockSpec((1, tn), lambda j, i: (0, nj + j))],
        out_specs=pl.BlockSpec((tm, tn), lambda j, i: (i, j)),
        out_shape=jax.ShapeDtypeStruct((m, d_ff), BF16),
        scratch_shapes=[pltpu.VMEM((k, tn), BF16), pltpu.VMEM((k, tn), BF16),
                        pltpu.VMEM((TAIL, tn), F32), pltpu.VMEM((TAIL, tn), F32)]
        + [pltpu.VMEM((rows + TAIL, tn), F32)] * (2 * n_chunks),
        compiler_params=_cparams(("arbitrary", "arbitrary"), vmem),
        name="ffn_up_conv_gate",
    )(xn, w_up, w_up, conv_w, conv_w, conv_b.reshape(1, -1), conv_b.reshape(1, -1))


def _eye(n):
    return jnp.where(lax.broadcasted_iota(jnp.int32, (n, n), 0) == lax.broadcasted_iota(jnp.int32, (n, n), 1),
                     1.0, 0.0).astype(BF16)


def _transpose_bf16(x):
    return _dot_nt(_eye(x.shape[1]), x)


def _fill_v_transposed(v_ref, vt_ref, t):
    s, dv = v_ref.shape
    for c in range(s // t):
        vt_ref[0:dv, c * t:(c + 1) * t] = _transpose_bf16(v_ref[c * t:(c + 1) * t, :]).astype(BF16)
    if vt_ref.shape[0] > dv:
        vt_ref[dv:, :] = jnp.ones((vt_ref.shape[0] - dv, s), BF16)


def _flash_kernel(*refs, t, dk, dv, group, has_aug):
    if has_aug:
        q_ref, k_ref, v_ref, qa_ref, ka_ref, o_ref, vt_ref, m_ref, acc_ref, s_ref = refs
    else:
        q_ref, k_ref, v_ref, o_ref, vt_ref, m_ref, acc_ref, s_ref = refs
    qi = pl.program_id(2)
    heads = range(group)

    @pl.when(qi == 0)
    def _():
        for g in heads:
            _fill_v_transposed(v_ref.at[:, g * dv:(g + 1) * dv], vt_ref.at[g], t)

    qs = []
    for g in heads:
        q = q_ref[:, g * dk:(g + 1) * dk]
        qs.append(jnp.concatenate([q, qa_ref[g]], axis=1) if has_aug else q)
    m_ref[...] = jnp.full(m_ref.shape, -jnp.inf, F32)
    acc_ref[...] = jnp.zeros(acc_ref.shape, F32)

    def scores(g, kb):
        ks = pl.multiple_of(kb * t, t)
        k = k_ref[pl.ds(ks, t), g * dk:(g + 1) * dk]
        if has_aug:
            k = jnp.concatenate([k, ka_ref[g, pl.ds(ks, t), :]], axis=1)
        return _dot_nt(k, qs[g])

    def col_max(s):
        return jnp.max(s, axis=0, keepdims=True)

    def update(g, s, s_max, kb):
        m_prev = m_ref[g]
        m_new = jnp.maximum(m_prev, s_max)
        m_ref[g] = m_new
        p = jnp.exp2(s - m_new).astype(BF16)
        ks = pl.multiple_of(kb * t, t)
        acc_ref[g] = jnp.exp2(m_prev - m_new) * acc_ref[g] + _dot(vt_ref[g, :, pl.ds(ks, t)], p)

    def body(kb, s_max):
        nxt = []
        for g in heads:
            update(g, s_ref[g], s_max[g], kb)
            s_next = scores(g, kb + 1)
            nxt.append(col_max(s_next))
            s_ref[g] = s_next
        return tuple(nxt)

    first = []
    for g in heads:
        s0 = scores(g, 0)
        first.append(col_max(s0))
        s_ref[g] = s0
    lax.fori_loop(0, qi, body, tuple(first))
    key = lax.broadcasted_iota(jnp.int32, (t, t), 0)
    qry = lax.broadcasted_iota(jnp.int32, (t, t), 1)
    outs = []
    for g in heads:
        s = jnp.where(key <= qry, s_ref[g], -jnp.inf)
        update(g, s, col_max(s), qi)
        acc = acc_ref[g]
        outs.append((acc[0:dv, :] / acc[dv:dv + 1, :]).T)
    o_ref[...] = jnp.concatenate(outs, axis=1).astype(o_ref.dtype)


def _flash_attention(q_arr, k_arr, v_arr, q_off, k_off, v_off, dk, dv, n_heads, aug=None, t=512, group=4):
    b, s, _ = q_arr.shape
    assert q_off % group == 0 and k_off % group == 0 and v_off % group == 0 and n_heads % group == 0
    qo, ko, vo = q_off // group, k_off // group, v_off // group
    in_specs = [pl.BlockSpec((None, t, group * dk), lambda bi, h, qi: (bi, qi, qo + h)),
                pl.BlockSpec((None, s, group * dk), lambda bi, h, qi: (bi, 0, ko + h)),
                pl.BlockSpec((None, s, group * dv), lambda bi, h, qi: (bi, 0, vo + h))]
    args = [q_arr, k_arr, v_arr]
    vmem = 2 * group * (_nbytes((t, dk), BF16) + _nbytes((s, dk), BF16) + _nbytes((s, dv), BF16)
                        + _nbytes((t, dv), BF16))
    if aug is not None:
        in_specs += [pl.BlockSpec((None, group, t, LANES), lambda bi, h, qi: (bi, h, qi, 0)),
                     pl.BlockSpec((None, group, s, LANES), lambda bi, h, qi: (bi, h, 0, 0))]
        args += list(aug)
        vmem += 2 * group * (_nbytes((t, LANES), BF16) + _nbytes((s, LANES), BF16))
    acc_rows = dv + 16
    vmem += group * (_nbytes((acc_rows, s), BF16) + 2 * _nbytes((acc_rows, t), F32) + 8 * _nbytes((t, t), F32))
    return pl.pallas_call(
        functools.partial(_flash_kernel, t=t, dk=dk, dv=dv, group=group, has_aug=aug is not None),
        grid=(b, n_heads // group, s // t),
        in_specs=in_specs,
        out_specs=pl.BlockSpec((None, t, group * dv), lambda bi, h, qi: (bi, qi, h)),
        out_shape=jax.ShapeDtypeStruct((b, s, n_heads * dv), BF16),
        scratch_shapes=[pltpu.VMEM((group, acc_rows, s), BF16), pltpu.VMEM((group, 1, t), F32),
                        pltpu.VMEM((group, acc_rows, t), F32), pltpu.VMEM((group, t, t), F32)],
        compiler_params=_cparams(("parallel", "parallel", "arbitrary"), vmem),
        name="flash_attention",
    )(*args)


def _sb_kernel(q_ref, k_ref, v_ref, o_ref, vt_ref, carry_ref, acc_ref, s_ref, a_ref, *, t, sub, d, group):
    qi = pl.program_id(2)
    heads = range(group)

    @pl.when(qi == 0)
    def _():
        for g in heads:
            _fill_v_transposed(v_ref.at[:, g * d:(g + 1) * d], vt_ref.at[g], t)

    carry_ref[...] = jnp.zeros(carry_ref.shape, F32)
    acc_ref[...] = jnp.zeros(acc_ref.shape, F32)
    qs = [q_ref[:, g * d:(g + 1) * d] for g in heads]
    r = lax.broadcasted_iota(jnp.int32, (sub + 16, sub), 0)
    c = lax.broadcasted_iota(jnp.int32, (sub + 16, sub), 1)
    suffix = jnp.where(((c > r) & (r < sub)) | (r == sub), 1.0, 0.0).astype(BF16)
    suffix = jnp.concatenate([suffix, suffix], axis=1)

    def scores(g, kb):
        ks = pl.multiple_of(kb * t, t)
        return _dot_nt(k_ref[pl.ds(ks, t), g * d:(g + 1) * d], qs[g])

    def weights(g, on_diagonal):
        later = carry_ref[g]
        for i in reversed(range(t // sub)):
            q0 = i * sub if on_diagonal else 0
            rows = slice(i * sub, (i + 1) * sub)
            z = s_ref[g, rows, q0:]
            neg_abs = lax.bitcast_convert_type(
                lax.bitcast_convert_type(z, jnp.uint32) | jnp.uint32(0x80000000), F32)
            log_beta = jnp.minimum(z, 0.0) - jnp.log2(1.0 + jnp.exp2(neg_abs))
            log_keep = log_beta - z
            if on_diagonal:
                key = lax.broadcasted_iota(jnp.int32, z.shape, 0) + i * sub
                strict = key < lax.broadcasted_iota(jnp.int32, z.shape, 1) + q0
                log_keep = jnp.where(strict, log_keep, 0.0)
            hi = log_keep.astype(BF16)
            lo = (log_keep - hi.astype(F32)).astype(BF16)
            sums = _dot(suffix, jnp.concatenate([hi, lo], axis=0))
            seen = later[:, q0:]
            a = jnp.exp2(log_beta + (sums[0:sub, :] + seen[0:1, :]))
            if on_diagonal:
                a = jnp.where(strict, a, 0.0)
            a_ref[g, rows, q0:] = a.astype(BF16)
            total = seen + sums[sub:sub + 8, :]
            if q0:
                a_ref[g, rows, 0:q0] = jnp.zeros((sub, q0), BF16)
                total = jnp.concatenate([later[:, 0:q0], total], axis=1)
            later = total
        carry_ref[g] = later

    def accumulate(g, kb):
        ks = pl.multiple_of(kb * t, t)
        acc_ref[g] += _dot(vt_ref[g, :, pl.ds(ks, t)], a_ref[g])

    def finish():
        o_ref[...] = jnp.concatenate([acc_ref[g].T for g in heads], axis=1).astype(o_ref.dtype)

    @pl.when(qi == 0)
    def _():
        for g in heads:
            s_ref[g] = scores(g, 0)
            weights(g, True)
            accumulate(g, 0)
        finish()

    @pl.when(qi > 0)
    def _():
        for g in heads:
            s_ref[g] = scores(g, qi)
            weights(g, True)
            s_ref[g] = scores(g, qi - 1)

        def body(i, carry):
            for g in heads:
                accumulate(g, qi - i)
                weights(g, False)
                s_ref[g] = scores(g, qi - i - 2)
            return carry

        lax.fori_loop(0, qi - 1, body, 0)
        for g in heads:
            accumulate(g, 1)
            weights(g, False)
            accumulate(g, 0)
        finish()


def _sb_attention(proj, n_heads, t=512, group=4):
    b, s, _ = proj.shape
    d = HEAD_DIM
    assert n_heads % group == 0
    blocks = n_heads // group
    vmem = group * (2 * (2 * _nbytes((t, d), BF16) + 2 * _nbytes((s, d), BF16))
                    + _nbytes((d, s), BF16) + 2 * _nbytes((d, t), F32) + 12 * _nbytes((t, t), F32))
    return pl.pallas_call(
        functools.partial(_sb_kernel, t=t, sub=LANES, d=d, group=group),
        grid=(b, blocks, s // t),
        in_specs=[pl.BlockSpec((None, t, group * d), lambda bi, h, qi: (bi, qi, h)),
                  pl.BlockSpec((None, s, group * d), lambda bi, h, qi: (bi, 0, blocks + h)),
                  pl.BlockSpec((None, s, group * d), lambda bi, h, qi: (bi, 0, 2 * blocks + h))],
        out_specs=pl.BlockSpec((None, t, group * d), lambda bi, h, qi: (bi, qi, h)),
        out_shape=jax.ShapeDtypeStruct((b, s, n_heads * d), BF16),
        scratch_shapes=[pltpu.VMEM((group, d, s), BF16), pltpu.VMEM((group, 8, t), F32),
                        pltpu.VMEM((group, d, t), F32), pltpu.VMEM((group, t, t), F32),
                        pltpu.VMEM((group, t, t), BF16)],
        compiler_params=_cparams(("parallel", "parallel", "arbitrary"), vmem),
        name="stick_breaking_attention",
    )(proj, proj, proj)


def _split3(x):
    hi = x.astype(BF16)
    rest = x - hi.astype(F32)
    mid = rest.astype(BF16)
    lo = (rest - mid.astype(F32)).astype(BF16)
    return hi, mid, lo


def _fox_gate_kernel(fl_ref, b_ref, pq_ref, pk_ref, oq_ref, ok_ref, qa_ref, ka_ref, carry_ref):
    ts = fl_ref.shape[0]

    @pl.when(pl.program_id(1) == 0)
    def _():
        carry_ref[...] = jnp.zeros(carry_ref.shape, F32)

    z = fl_ref[...] + b_ref[...]
    log_f = jnp.minimum(z, 0.0) - jnp.log(1.0 + jnp.exp(-jnp.abs(z)))
    row = lax.broadcasted_iota(jnp.int32, (ts, ts), 0)
    col = lax.broadcasted_iota(jnp.int32, (ts, ts), 1)
    prefix = jnp.where(col <= row, 1.0, 0.0).astype(BF16)
    hi, mid, lo = _split3(log_f)
    c = _dot(prefix, hi) + _dot(prefix, mid) + _dot(prefix, lo) + carry_ref[0:1, :]
    carry_ref[0:1, :] = c[ts - 1:ts, :]
    parts = jnp.concatenate(_split3(c * LOG2E), axis=1)
    qa = (_dot(parts, pq_ref[...]) + oq_ref[...]).astype(BF16)
    ka = (_dot(parts, pk_ref[...]) + ok_ref[...]).astype(BF16)
    for h in range(qa_ref.shape[0]):
        qa_ref[h] = qa[:, h * LANES:(h + 1) * LANES]
        ka_ref[h] = ka[:, h * LANES:(h + 1) * LANES]


def _fox_aug_tables(n_heads):
    part, src = jnp.arange(3 * LANES) // LANES, jnp.arange(3 * LANES) % LANES
    head, lane = jnp.arange(n_heads * LANES) // LANES, jnp.arange(n_heads * LANES) % LANES
    mine = src[:, None] == head[None, :]
    pq = jnp.where(mine & (lane[None, :] == part[:, None] + 3), 1.0, 0.0).astype(BF16)
    pk = jnp.where(mine & (lane[None, :] == part[:, None]), -1.0, 0.0).astype(BF16)
    oq = jnp.where(lane < 3, 1.0, 0.0).astype(F32)[None]
    ok = jnp.where((lane >= 3) & (lane < 6), 1.0, 0.0).astype(F32)[None]
    return pq, pk, oq, ok


def _fox_gates(f_logit, b_f, n_heads, ts=256):
    b, s, w = f_logit.shape
    hw = n_heads * LANES
    out = jax.ShapeDtypeStruct((b, n_heads, s, LANES), BF16)
    const = lambda bi, i: (0, 0)
    return pl.pallas_call(
        _fox_gate_kernel,
        grid=(b, s // ts),
        in_specs=[pl.BlockSpec((None, ts, w), lambda bi, i: (bi, i, 0)),
                  pl.BlockSpec((1, w), const),
                  pl.BlockSpec((3 * LANES, hw), const), pl.BlockSpec((3 * LANES, hw), const),
                  pl.BlockSpec((1, hw), const), pl.BlockSpec((1, hw), const)],
        out_specs=[pl.BlockSpec((None, n_heads, ts, LANES), lambda bi, i: (bi, 0, i, 0)),
                   pl.BlockSpec((None, n_heads, ts, LANES), lambda bi, i: (bi, 0, i, 0))],
        out_shape=[out, out],
        scratch_shapes=[pltpu.VMEM((8, w), F32)],
        compiler_params=_cparams(("parallel", "arbitrary"), 32 * 2**20),
        name="fox_gate_cumsum",
    )(f_logit, b_f.reshape(1, w), *_fox_aug_tables(n_heads))


def _mem_attn_kernel(q_ref, kv_ref, o_ref, *, scale):
    outs = []
    for h in range(MEM_HEADS):
        q = q_ref[:, h * HEAD_DIM:(h + 1) * HEAD_DIM].astype(BF16)
        k = kv_ref[:, h * HEAD_DIM:(h + 1) * HEAD_DIM]
        v = kv_ref[:, MEM_W + h * HEAD_DIM:MEM_W + (h + 1) * HEAD_DIM]
        s = _dot_nt(q, k) * scale
        p = jnp.exp(s - jnp.max(s, axis=1, keepdims=True))
        o = _dot(p.astype(BF16), v) / jnp.sum(p, axis=1, keepdims=True)
        outs.append(o.astype(o_ref.dtype))
    o_ref[...] = jnp.concatenate(outs, axis=1)


def _mem_attention(q_arr, q_block, mem_kv, tq=512):
    b, s, _ = q_arr.shape
    length = mem_kv.shape[1]
    vmem = (2 * (_nbytes((tq, MEM_W), q_arr.dtype) + _nbytes((length, 2 * MEM_W), BF16)
                 + _nbytes((tq, MEM_W), BF16)) + 8 * _nbytes((tq, length), F32))
    return pl.pallas_call(
        functools.partial(_mem_attn_kernel, scale=HEAD_DIM ** -0.5),
        grid=(b, s // tq),
        in_specs=[pl.BlockSpec((None, tq, MEM_W), lambda bi, i: (bi, i, q_block)),
                  pl.BlockSpec((None, length, 2 * MEM_W), lambda bi, i: (bi, 0, 0))],
        out_specs=pl.BlockSpec((None, tq, MEM_W), lambda bi, i: (bi, i, 0)),
        out_shape=jax.ShapeDtypeStruct((b, s, MEM_W), BF16),
        compiler_params=_cparams(("parallel", "parallel"), vmem),
        name="memory_attention",
    )(q_arr, mem_kv)


def _t5_bucket_table():
    max_exact = NUM_BUCKETS // 2
    kj = jnp.arange(2 * WINDOW)[:, None]
    qi = jnp.arange(WINDOW)[None, :]
    signed = WINDOW + qi - kj
    dist = jnp.maximum(signed, 0)
    d = jnp.maximum(dist, 1).astype(F32)
    large = max_exact + (jnp.log(d / max_exact) / math.log(MAX_DISTANCE / max_exact)
                         * (NUM_BUCKETS - max_exact)).astype(jnp.int32)
    bucket = jnp.where(dist < max_exact, dist, jnp.minimum(large, NUM_BUCKETS - 1))
    return jnp.where((signed >= 0) & (signed < WINDOW), bucket, -1).astype(jnp.int32)


def _t5_bias_kernel(rb_ref, bucket_ref, o_ref):
    kvh = pl.program_id(0)
    bucket = bucket_ref[...]
    for g in range(SWA_GROUP):
        bias = jnp.full(bucket.shape, -jnp.inf, F32)
        for b in range(NUM_BUCKETS):
            bias = jnp.where(bucket == b, rb_ref[b, kvh * SWA_GROUP + g] * LOG2E, bias)
        o_ref[:, g * WINDOW:(g + 1) * WINDOW] = bias


def _t5_bias(rel_bias):
    return pl.pallas_call(
        _t5_bias_kernel,
        grid=(N_SWA_KV_HEADS,),
        in_specs=[pl.BlockSpec(memory_space=pltpu.SMEM),
                  pl.BlockSpec((2 * WINDOW, WINDOW), lambda h: (0, 0))],
        out_specs=pl.BlockSpec((None, 2 * WINDOW, SWA_GROUP * WINDOW), lambda h: (h, 0, 0)),
        out_shape=jax.ShapeDtypeStruct((N_SWA_KV_HEADS, 2 * WINDOW, SWA_GROUP * WINDOW), F32),
        compiler_params=_cparams(("parallel",), 16 * 2**20),
        name="t5_bias",
    )(rel_bias, _t5_bucket_table())


def _swa_kernel(sink_ref, q_ref, kvc_ref, kvp_ref, bias_ref, o_ref):
    n = pl.program_id(1)
    d = SWA_HEAD_DIM
    kv_w = N_SWA_KV_HEADS * d
    k_win = jnp.concatenate([kvp_ref[:, 0:kv_w], kvc_ref[:, 0:kv_w]], axis=0)
    v_win = jnp.concatenate([kvp_ref[:, kv_w:2 * kv_w], kvc_ref[:, kv_w:2 * kv_w]], axis=0)
    v_t = v_win.astype(F32).T.astype(BF16)
    no_prev = jnp.where(n == 0, -jnp.inf, 0.0)
    outs = []
    for kvh in range(N_SWA_KV_HEADS):
        heads = range(kvh * SWA_GROUP, (kvh + 1) * SWA_GROUP)
        q = jnp.concatenate([q_ref[:, h * d:(h + 1) * d] for h in heads], axis=0)
        sink = jnp.concatenate([jnp.full((1, WINDOW), sink_ref[h] * LOG2E, F32) for h in heads], axis=1)
        s = _dot_nt(k_win[:, kvh * d:(kvh + 1) * d], q) + bias_ref[kvh]
        s = jnp.concatenate([s[0:WINDOW, :] + no_prev, s[WINDOW:, :]], axis=0)
        m = jnp.maximum(jnp.max(s, axis=0, keepdims=True), sink)
        p = jnp.exp2(s - m)
        denom = jnp.sum(p, axis=0, keepdims=True) + jnp.exp2(sink - m)
        o_t = _dot(v_t[kvh * d:(kvh + 1) * d, :], p.astype(BF16)) / denom
        outs += [o_t[:, g * WINDOW:(g + 1) * WINDOW] for g in range(SWA_GROUP)]
    o_ref[...] = jnp.concatenate(outs, axis=0).T.astype(o_ref.dtype)


def _swa_attention(proj, sinks, bias):
    b, s, _ = proj.shape
    q_w = N_SWA_HEADS * SWA_HEAD_DIM
    kv_w = 2 * N_SWA_KV_HEADS * SWA_HEAD_DIM
    kv_block = q_w // kv_w
    vmem = (2 * (2 * _nbytes((WINDOW, q_w), BF16) + 2 * _nbytes((WINDOW, kv_w), BF16)
                 + _nbytes(bias.shape, F32)) + 16 * 2**20)
    return pl.pallas_call(
        _swa_kernel,
        grid=(b, s // WINDOW),
        in_specs=[pl.BlockSpec(memory_space=pltpu.SMEM),
                  pl.BlockSpec((None, WINDOW, q_w), lambda bi, n: (bi, n, 0)),
                  pl.BlockSpec((None, WINDOW, kv_w), lambda bi, n: (bi, n, kv_block)),
                  pl.BlockSpec((None, WINDOW, kv_w), lambda bi, n: (bi, jnp.maximum(n - 1, 0), kv_block)),
                  pl.BlockSpec(bias.shape, lambda bi, n: (0, 0, 0))],
        out_specs=pl.BlockSpec((None, WINDOW, q_w), lambda bi, n: (bi, n, 0)),
        out_shape=jax.ShapeDtypeStruct((b, s, q_w), BF16),
        compiler_params=_cparams(("parallel", "arbitrary"), vmem),
        name="sliding_window_attention",
    )(sinks, proj, proj, proj, bias)


def _rope_table_kernel(pos_ref, invf_ref, o_ref):
    ang = pos_ref[...].astype(F32) * invf_ref[...]
    lane = lax.broadcasted_iota(jnp.int32, ang.shape, 1)
    half = QK_ROPE // 2
    sin_signed = jnp.where(lane < QK_ROPE + half, -jnp.sin(ang), jnp.sin(ang))
    o_ref[...] = jnp.where(lane < QK_ROPE, jnp.cos(ang), sin_signed)


def _rope_table(positions, tm=1024):
    m = positions.shape[0]
    half = QK_ROPE // 2
    inv_freq = ROPE_THETA ** (-jnp.arange(half, dtype=F32) / half)
    invf = jnp.tile(inv_freq, LANES // half).reshape(1, LANES)
    return pl.pallas_call(
        _rope_table_kernel,
        grid=(m // tm,),
        in_specs=[pl.BlockSpec((tm, 1), lambda i: (i, 0)), pl.BlockSpec((1, LANES), lambda i: (0, 0))],
        out_specs=pl.BlockSpec((tm, LANES), lambda i: (i, 0)),
        out_shape=jax.ShapeDtypeStruct((m, LANES), F32),
        compiler_params=_cparams(("parallel",), 32 * 2**20),
        name="rope_table",
    )(positions, invf)


def _apply_rope(x_and_partner, cs):
    z = x_and_partner * cs
    z = z + pltpu.roll(z, QK_ROPE, axis=1)
    lane = lax.broadcasted_iota(jnp.int32, z.shape, 1)
    return jnp.where(lane < QK_ROPE, z, 0.0)


def _mla_q_kernel(cq_ref, g_ref, w_ref, cs_ref, o_ref, cn_ref):
    @pl.when(pl.program_id(1) == 0)
    def _():
        cn_ref[...] = _rms(cq_ref[...], g_ref[...]).astype(BF16)

    y = _dot(cn_ref[...], w_ref[...]) * ((QK_NOPE + QK_ROPE) ** -0.5 * LOG2E)
    cs = cs_ref[...]
    parts = []
    for h in range(y.shape[1] // (2 * LANES)):
        yh = y[:, h * 2 * LANES:(h + 1) * 2 * LANES]
        parts += [yh[:, :QK_NOPE], _apply_rope(yh[:, QK_NOPE:], cs)]
    o_ref[...] = jnp.concatenate(parts, axis=1).astype(o_ref.dtype)


def _mla_q(proj, q_norm, w_uq_aug, cs, tm=1024, heads_per_step=4):
    m = proj.shape[0]
    hw = 2 * LANES * heads_per_step
    n_heads = w_uq_aug.shape[1] // hw
    return pl.pallas_call(
        _mla_q_kernel,
        grid=(m // tm, n_heads),
        in_specs=[pl.BlockSpec((tm, Q_LORA), lambda i, h: (i, 1)),
                  pl.BlockSpec((1, Q_LORA), lambda i, h: (0, 0)),
                  pl.BlockSpec((Q_LORA, hw), lambda i, h: (0, h)),
                  pl.BlockSpec((tm, LANES), lambda i, h: (i, 0))],
        out_specs=pl.BlockSpec((tm, hw), lambda i, h: (i, h)),
        out_shape=jax.ShapeDtypeStruct((m, n_heads * hw), BF16),
        scratch_shapes=[pltpu.VMEM((tm, Q_LORA), BF16)],
        compiler_params=_cparams(("parallel", "arbitrary"), 32 * 2**20),
        name="mla_q_expand",
    )(proj, q_norm.reshape(1, Q_LORA), w_uq_aug, cs)


def _mla_kv_kernel(ckv_ref, g_ref, w_ref, kr_ref, cs_ref, k_ref, v_ref, cn_ref, kpe_ref):
    @pl.when(pl.program_id(1) == 0)
    def _():
        cn_ref[...] = _rms(ckv_ref[...], g_ref[...]).astype(BF16)
        kpe_ref[...] = _apply_rope(kr_ref[...], cs_ref[...])

    y = _dot(cn_ref[...], w_ref[...])
    kpe = kpe_ref[...]
    k_parts, v_parts = [], []
    for h in range(y.shape[1] // (2 * LANES)):
        yh = y[:, h * 2 * LANES:(h + 1) * 2 * LANES]
        k_parts += [yh[:, :QK_NOPE], kpe]
        v_parts.append(yh[:, QK_NOPE:])
    k_ref[...] = jnp.concatenate(k_parts, axis=1).astype(k_ref.dtype)
    v_ref[...] = jnp.concatenate(v_parts, axis=1).astype(v_ref.dtype)


def _mla_kv(proj, kv_norm, w_ukv, cs, tm=1024, heads_per_step=4):
    m = proj.shape[0]
    hw = 2 * LANES * heads_per_step
    n_heads = w_ukv.shape[1] // hw
    return pl.pallas_call(
        _mla_kv_kernel,
        grid=(m // tm, n_heads),
        in_specs=[pl.BlockSpec((tm, KV_LORA), lambda i, h: (i, 4)),
                  pl.BlockSpec((1, KV_LORA), lambda i, h: (0, 0)),
                  pl.BlockSpec((KV_LORA, hw), lambda i, h: (0, h)),
                  pl.BlockSpec((tm, LANES), lambda i, h: (i, 10)),
                  pl.BlockSpec((tm, LANES), lambda i, h: (i, 0))],
        out_specs=[pl.BlockSpec((tm, hw), lambda i, h: (i, h)),
                   pl.BlockSpec((tm, hw // 2), lambda i, h: (i, h))],
        out_shape=[jax.ShapeDtypeStruct((m, n_heads * hw), BF16),
                   jax.ShapeDtypeStruct((m, n_heads * hw // 2), BF16)],
        scratch_shapes=[pltpu.VMEM((tm, KV_LORA), BF16), pltpu.VMEM((tm, LANES), F32)],
        compiler_params=_cparams(("parallel", "arbitrary"), 32 * 2**20),
        name="mla_kv_expand",
    )(proj, kv_norm.reshape(1, KV_LORA), w_ukv, proj, cs)


def _swap_halves(w):
    half = w.shape[-1] // 2
    return jnp.concatenate([w[..., half:], w[..., :half]], axis=-1)


def _mixer_out(x2, o, o_mem, w_out, b, s):
    return _matmul_res([o.reshape(b * s, o.shape[-1]), o_mem.reshape(b * s, MEM_W)],
                       w_out[0], w_out[1], x2, tm=1024, tn=512)


def _sb_layer(x2, mem_kv, g, w_in, w_out, b, s):
    proj = _norm_matmul(x2, g, w_in.astype(BF16), BF16, tm=IN_PROJ_ROWS, tn=512, scaled_cols=N_HEADS * HEAD_DIM,
                        col_scale=HEAD_DIM ** -0.5 * LOG2E).reshape(b, s, -1)
    o = _sb_attention(proj, N_HEADS)
    o_mem = _mem_attention(proj, 3 * N_HEADS * HEAD_DIM // MEM_W, mem_kv)
    return _mixer_out(x2, o, o_mem, w_out, b, s)


def _fox_layer(x2, mem_kv, g, w_in, b_f, w_out, b, s):
    qkv_w = 3 * N_HEADS * HEAD_DIM
    w_main = jnp.concatenate([w_in[:, :qkv_w], w_in[:, qkv_w + N_HEADS:]], axis=1).astype(BF16)
    w_gate = jnp.pad(w_in[:, qkv_w:qkv_w + N_HEADS], ((0, 0), (0, LANES - N_HEADS))).astype(BF16)
    proj = _norm_matmul(x2, g, w_main, BF16, tm=IN_PROJ_ROWS, tn=512, scaled_cols=N_HEADS * HEAD_DIM,
                        col_scale=HEAD_DIM ** -0.5 * LOG2E).reshape(b, s, -1)
    f_logit = _norm_matmul(x2, g, w_gate, F32, tm=512, tn=LANES).reshape(b, s, LANES)
    aug = _fox_gates(f_logit, jnp.pad(b_f, (0, LANES - N_HEADS)), N_HEADS)
    o = _flash_attention(proj, proj, proj, 0, N_HEADS, 2 * N_HEADS, HEAD_DIM, HEAD_DIM, N_HEADS, aug=aug)
    o_mem = _mem_attention(proj, qkv_w // MEM_W, mem_kv)
    return _mixer_out(x2, o, o_mem, w_out, b, s)


def _swa_layer(x2, mem_kv, g, rel_bias, w_in, sinks, w_out, b, s):
    proj = _norm_matmul(x2, g, w_in.astype(BF16), BF16, tm=IN_PROJ_ROWS, tn=512,
                        scaled_cols=N_SWA_HEADS * SWA_HEAD_DIM,
                        col_scale=SWA_HEAD_DIM ** -0.5 * LOG2E).reshape(b, s, -1)
    o = _swa_attention(proj, sinks, _t5_bias(rel_bias))
    q_w = N_SWA_HEADS * SWA_HEAD_DIM
    kv_w = 2 * N_SWA_KV_HEADS * SWA_HEAD_DIM
    o_mem = _mem_attention(proj, (q_w + kv_w) // MEM_W, mem_kv)
    return _mixer_out(x2, o, o_mem, w_out, b, s)


def _mla_layer(x2, mem_kv, g, positions, w_in, q_norm, w_uq, kv_norm, w_ukv, w_out, b, s):
    o2, o3 = Q_LORA + KV_LORA, Q_LORA + KV_LORA + QK_ROPE
    w_main = jnp.concatenate([w_in[:, o3:], w_in[:, :o3], _swap_halves(w_in[:, o2:o3])], axis=1).astype(BF16)
    proj = _norm_matmul(x2, g, w_main, F32, tm=512, tn=w_main.shape[1])
    cs = _rope_table(positions.reshape(b * s, 1))
    wq = w_uq.reshape(Q_LORA, N_HEADS, QK_NOPE + QK_ROPE)
    wq = jnp.concatenate([wq, _swap_halves(wq[:, :, QK_NOPE:])], axis=-1).reshape(Q_LORA, -1).astype(BF16)
    q = _mla_q(proj, q_norm, wq, cs).reshape(b, s, -1)
    k, v = _mla_kv(proj, kv_norm, w_ukv.astype(BF16), cs)
    k, v = k.reshape(b, s, -1), v.reshape(b, s, -1)
    o = _flash_attention(q, k, v, 0, 0, 0, 2 * LANES, HEAD_DIM, N_HEADS)
    o_mem = _mem_attention(proj.reshape(b, s, -1), 0, mem_kv)
    return _mixer_out(x2, o, o_mem, w_out, b, s)


def kernel(x, mem, positions, rel_bias, attn_norm, mem_norm, w_mem_kv, ffn_norm, ffn_w_up, ffn_conv_w, ffn_conv_b, ffn_w_down, final_norm, sb_w_in, sb_w_out, fox_w_in, fox_b_f, fox_w_out, swa_w_in, swa_sinks, swa_w_out, mla_w_in, mla_q_norm, mla_w_uq, mla_kv_norm, mla_w_ukv, mla_w_out):
    b, s, d = x.shape
    depth = attn_norm.shape[0]
    mem_len = mem.shape[1]
    x2 = x.reshape(b * s, d)
    mem2 = mem.reshape(b * mem_len, d)
    w_mem_kv_bf16 = w_mem_kv.astype(BF16)
    for i in range(depth):
        kind, j = i % 4, i // 4
        mem_kv = _norm_matmul(mem2, mem_norm[i], w_mem_kv_bf16, BF16, tm=b * mem_len, tn=2 * MEM_W,
                              layer=i).reshape(b, mem_len, 2 * MEM_W)
        g = attn_norm[i]
        if kind == 0:
            x2 = _sb_layer(x2, mem_kv, g, sb_w_in[j], (sb_w_out, j), b, s)
        elif kind == 1:
            x2 = _fox_layer(x2, mem_kv, g, fox_w_in[j], fox_b_f[j], (fox_w_out, j), b, s)
        elif kind == 2:
            x2 = _swa_layer(x2, mem_kv, g, rel_bias, swa_w_in[j], swa_sinks[j], (swa_w_out, j), b, s)
        else:
            x2 = _mla_layer(x2, mem_kv, g, positions, mla_w_in[j], mla_q_norm[j], mla_w_uq[j],
                            mla_kv_norm[j], mla_w_ukv[j], (mla_w_out, j), b, s)
        gated = _ffn_up(_rmsnorm(x2, ffn_norm[i], BF16, tm=512), ffn_w_up, i, ffn_conv_w[i], ffn_conv_b[i],
                        seq=s, tm=1024, tn=512)
        x2 = _matmul_res([gated], ffn_w_down, i, x2, tm=512, tn=512)
    return _rmsnorm(x2, final_norm, F32, tm=512).reshape(b, s, d)
```

```python
import functools
import math

import jax
import jax.numpy as jnp
from jax import lax
from jax.experimental import pallas as pl
from jax.experimental.pallas import tpu as pltpu

F32 = jnp.float32
BF16 = jnp.bfloat16

EPS = 1e-6
LOG2E = math.log2(math.e)
HEAD_DIM = 128
N_HEADS = 16
SWA_HEAD_DIM = 64
N_SWA_HEADS = 32
N_SWA_KV_HEADS = 4
SWA_GROUP = N_SWA_HEADS // N_SWA_KV_HEADS
WINDOW = 128
Q_LORA = 512
KV_LORA = 256
QK_NOPE = 128
QK_ROPE = 64
ROPE_THETA = 10000.0
MEM_HEADS = 4
MEM_W = MEM_HEADS * HEAD_DIM
NUM_BUCKETS = 32
MAX_DISTANCE = 128
CONV_WIDTH = 3

V7X_VMEM_BYTES = 64 * 1024 * 1024
VMEM_CAP = V7X_VMEM_BYTES - 8 * 1024 * 1024
LANES = 128
IN_PROJ_ROWS = 1024
TAIL = 8


def _cparams(sems, vmem_bytes):
    return pltpu.CompilerParams(dimension_semantics=sems,
                                vmem_limit_bytes=int(min(max(vmem_bytes, 16 * 2**20), VMEM_CAP)))


def _nbytes(shape, dtype):
    return math.prod(shape) * jnp.dtype(dtype).itemsize


def _rms(x, g):
    return x * lax.rsqrt(jnp.mean(x * x, axis=-1, keepdims=True) + EPS) * g


def _dot(a, b):
    return jnp.dot(a, b, preferred_element_type=F32)


def _dot_nt(a, b):
    return lax.dot_general(a, b, (((1,), (1,)), ((), ())), preferred_element_type=F32)


def _norm_matmul_kernel(x_ref, g_ref, w_ref, cs_ref, o_ref, xn_ref):
    @pl.when(pl.program_id(1) == 0)
    def _():
        xn_ref[...] = _rms(x_ref[...], g_ref[...]).astype(BF16)

    o_ref[...] = (_dot(xn_ref[...], w_ref[...]) * cs_ref[...]).astype(o_ref.dtype)


def _w_tile_spec(w, layer, k, tn, col_block):
    if w.ndim == 3:
        return pl.BlockSpec((None, k, tn), lambda i, j: (layer, 0, col_block(j)))
    return pl.BlockSpec((k, tn), lambda i, j: (0, col_block(j)))


def _norm_matmul(x, g, w, out_dtype, tm, tn, scaled_cols=0, col_scale=1.0, layer=None):
    m, k = x.shape
    n = w.shape[-1]
    cs = jnp.where(jnp.arange(n) < scaled_cols, col_scale, 1.0).astype(F32).reshape(1, n)
    vmem = (2 * (_nbytes((tm, k), F32) + _nbytes((k, tn), BF16) + _nbytes((tm, tn), out_dtype))
            + _nbytes((tm, k), BF16) + 2 * _nbytes((tm, k), F32) + _nbytes((tm, tn), F32))
    return pl.pallas_call(
        _norm_matmul_kernel,
        grid=(m // tm, n // tn),
        in_specs=[pl.BlockSpec((tm, k), lambda i, j: (i, 0)),
                  pl.BlockSpec((1, k), lambda i, j: (0, 0)),
                  _w_tile_spec(w, layer, k, tn, lambda j: j),
                  pl.BlockSpec((1, tn), lambda i, j: (0, j))],
        out_specs=pl.BlockSpec((tm, tn), lambda i, j: (i, j)),
        out_shape=jax.ShapeDtypeStruct((m, n), out_dtype),
        scratch_shapes=[pltpu.VMEM((tm, k), BF16)],
        compiler_params=_cparams(("parallel", "arbitrary"), vmem),
        name="norm_matmul",
    )(x, g.reshape(1, k), w, cs)


def _matmul_res_kernel(*refs, n_a):
    a_refs, w_refs = refs[:n_a], refs[n_a:2 * n_a]
    x_ref, o_ref = refs[2 * n_a], refs[2 * n_a + 1]
    wb_refs = refs[2 * n_a + 2:]

    @pl.when(pl.program_id(1) == 0)
    def _():
        for w_ref, wb_ref in zip(w_refs, wb_refs):
            wb_ref[...] = w_ref[...].astype(BF16)

    acc = x_ref[...]
    for a_ref, wb_ref in zip(a_refs, wb_refs):
        acc = acc + _dot(a_ref[...], wb_ref[...])
    o_ref[...] = acc


def _matmul_res(a_list, w, layer, x, tm, tn):
    m, n = x.shape
    vmem = 6 * _nbytes((tm, tn), F32)
    a_specs, w_specs, scratch = [], [], []
    row = 0
    for a in a_list:
        kp = a.shape[1]
        row_block = row // kp
        assert row_block * kp == row
        a_specs.append(pl.BlockSpec((tm, kp), lambda j, i: (i, 0)))
        w_specs.append(pl.BlockSpec((None, kp, tn), lambda j, i, row_block=row_block: (layer, row_block, j)))
        scratch.append(pltpu.VMEM((kp, tn), BF16))
        vmem += 2 * _nbytes((tm, kp), a.dtype) + 2 * _nbytes((kp, tn), F32) + 2 * _nbytes((kp, tn), BF16)
        row += kp
    assert row == w.shape[1]
    return pl.pallas_call(
        functools.partial(_matmul_res_kernel, n_a=len(a_list)),
        grid=(n // tn, m // tm),
        in_specs=a_specs + w_specs + [pl.BlockSpec((tm, tn), lambda j, i: (i, j))],
        out_specs=pl.BlockSpec((tm, tn), lambda j, i: (i, j)),
        out_shape=jax.ShapeDtypeStruct((m, n), F32),
        scratch_shapes=scratch,
        compiler_params=_cparams(("parallel", "arbitrary"), vmem),
        name="matmul_res",
    )(*a_list, *([w] * len(a_list)), x)


def _rmsnorm_kernel(x_ref, g_ref, o_ref):
    o_ref[...] = _rms(x_ref[...], g_ref[...]).astype(o_ref.dtype)


def _rmsnorm(x, g, out_dtype, tm):
    m, k = x.shape
    return pl.pallas_call(
        _rmsnorm_kernel,
        grid=(m // tm,),
        in_specs=[pl.BlockSpec((tm, k), lambda i: (i, 0)), pl.BlockSpec((1, k), lambda i: (0, 0))],
        out_specs=pl.BlockSpec((tm, k), lambda i: (i, 0)),
        out_shape=jax.ShapeDtypeStruct((m, k), out_dtype),
        compiler_params=_cparams(("parallel",), 8 * _nbytes((tm, k), F32)),
        name="rmsnorm",
    )(x, g.reshape(1, k))


def _ffn_up_kernel(xn_ref, wg_ref, wv_ref, cwg_ref, cwv_ref, cbg_ref, cbv_ref, o_ref,
                   wgb_ref, wvb_ref, tail_g_ref, tail_v_ref, *u_refs, rows, tiles_per_seq):
    i = pl.program_id(1)

    @pl.when(i == 0)
    def _():
        wgb_ref[...] = wg_ref[...].astype(BF16)
        wvb_ref[...] = wv_ref[...].astype(BF16)

    @pl.when(i % tiles_per_seq == 0)
    def _():
        tail_g_ref[...] = jnp.zeros(tail_g_ref.shape, F32)
        tail_v_ref[...] = jnp.zeros(tail_v_ref.shape, F32)

    n_chunks = len(u_refs) // 2

    for r in range(n_chunks):
        xr = xn_ref[r * rows:(r + 1) * rows, :]

        def conv(wb_ref, cw_ref, cb_ref, u_ref, before):
            u_ref[TAIL:, :] = _dot(xr, wb_ref[...])
            u_ref[0:TAIL, :] = before
            c = cb_ref[...]
            for tap in range(CONV_WIDTH):
                start = TAIL - (CONV_WIDTH - 1) + tap
                c = c + cw_ref[tap:tap + 1, :] * u_ref[start:start + rows, :]
            return c

        if r == 0:
            before_g, before_v = tail_g_ref[...], tail_v_ref[...]
        else:
            before_g, before_v = u_refs[2 * r - 2][rows:, :], u_refs[2 * r - 1][rows:, :]
        gate = conv(wgb_ref, cwg_ref, cbg_ref, u_refs[2 * r], before_g)
        val = conv(wvb_ref, cwv_ref, cbv_ref, u_refs[2 * r + 1], before_v)
        o_ref[r * rows:(r + 1) * rows, :] = (gate * (1.0 / (1.0 + jnp.exp(-gate))) * val).astype(o_ref.dtype)

    tail_g_ref[...] = u_refs[2 * n_chunks - 2][rows:, :]
    tail_v_ref[...] = u_refs[2 * n_chunks - 1][rows:, :]


def _ffn_up(xn, w_up, layer, conv_w, conv_b, seq, tm, tn, rows=256):
    m, k = xn.shape
    d_ff = w_up.shape[-1] // 2
    nj = d_ff // tn
    n_chunks = tm // rows
    vmem = (2 * (_nbytes((tm, k), BF16) + 2 * _nbytes((k, tn), F32) + _nbytes((tm, tn), BF16))
            + 2 * _nbytes((k, tn), BF16) + 2 * n_chunks * _nbytes((rows + TAIL, tn), F32)
            + 2 * _nbytes((k, tn), F32) + 6 * n_chunks * _nbytes((rows, tn), F32))
    return pl.pallas_call(
        functools.partial(_ffn_up_kernel, rows=rows, tiles_per_seq=seq // tm),
        grid=(nj, m // tm),
        in_specs=[pl.BlockSpec((tm, k), lambda j, i: (i, 0)),
                  pl.BlockSpec((None, k, tn), lambda j, i: (layer, 0, j)),
                  pl.BlockSpec((None, k, tn), lambda j, i: (layer, 0, nj + j)),
                  pl.BlockSpec((CONV_WIDTH, tn), lambda j, i: (0, j)),
                  pl.BlockSpec((CONV_WIDTH, tn), lambda j, i: (0, nj + j)),
                  pl.BlockSpec((1, tn), lambda j, i: (0, j)),
                  pl.BlockSpec((1, tn), lambda j, i: (0, nj + j))],
        out_specs=pl.BlockSpec((tm, tn), lambda j, i: (i, j)),
        out_shape=jax.ShapeDtypeStruct((m, d_ff), BF16),
        scratch_shapes=[pltpu.VMEM((k, tn), BF16), pltpu.VMEM((k, tn), BF16),
                        pltpu.VMEM((TAIL, tn), F32), pltpu.VMEM((TAIL, tn), F32)]
        + [pltpu.VMEM((rows + TAIL, tn), F32)] * (2 * n_chunks),
        compiler_params=_cparams(("arbitrary", "arbitrary"), vmem),
        name="ffn_up_conv_gate",
    )(xn, w_up, w_up, conv_w, conv_w, conv_b.reshape(1, -1), conv_b.reshape(1, -1))


def _eye(n):
    return jnp.where(lax.broadcasted_iota(jnp.int32, (n, n), 0) == lax.broadcasted_iota(jnp.int32, (n, n), 1),
                     1.0, 0.0).astype(BF16)


def _transpose_bf16(x):
    return _dot_nt(_eye(x.shape[1]), x)


def _fill_v_transposed(v_ref, vt_ref, t):
    s, dv = v_ref.shape
    for c in range(s // t):
        vt_ref[0:dv, c * t:(c + 1) * t] = _transpose_bf16(v_ref[c * t:(c + 1) * t, :]).astype(BF16)
    if vt_ref.shape[0] > dv:
        vt_ref[dv:, :] = jnp.ones((vt_ref.shape[0] - dv, s), BF16)


def _flash_kernel(*refs, t, dk, dv, group, has_aug):
    if has_aug:
        q_ref, k_ref, v_ref, qa_ref, ka_ref, o_ref, vt_ref, m_ref, acc_ref, s_ref = refs
    else:
        q_ref, k_ref, v_ref, o_ref, vt_ref, m_ref, acc_ref, s_ref = refs
    qi = pl.program_id(2)
    heads = range(group)

    @pl.when(qi == 0)
    def _():
        for g in heads:
            _fill_v_transposed(v_ref.at[:, g * dv:(g + 1) * dv], vt_ref.at[g], t)

    qs = []
    for g in heads:
        q = q_ref[:, g * dk:(g + 1) * dk]
        qs.append(jnp.concatenate([q, qa_ref[g]], axis=1) if has_aug else q)
    m_ref[...] = jnp.full(m_ref.shape, -jnp.inf, F32)
    acc_ref[...] = jnp.zeros(acc_ref.shape, F32)

    def scores(g, kb):
        ks = pl.multiple_of(kb * t, t)
        k = k_ref[pl.ds(ks, t), g * dk:(g + 1) * dk]
        if has_aug:
            k = jnp.concatenate([k, ka_ref[g, pl.ds(ks, t), :]], axis=1)
        return _dot_nt(k, qs[g])

    def col_max(s):
        return jnp.max(s, axis=0, keepdims=True)

    def update(g, s, s_max, kb):
        m_prev = m_ref[g]
        m_new = jnp.maximum(m_prev, s_max)
        m_ref[g] = m_new
        p = jnp.exp2(s - m_new).astype(BF16)
        ks = pl.multiple_of(kb * t, t)
        acc_ref[g] = jnp.exp2(m_prev - m_new) * acc_ref[g] + _dot(vt_ref[g, :, pl.ds(ks, t)], p)

    def body(kb, s_max):
        nxt = []
        for g in heads:
            update(g, s_ref[g], s_max[g], kb)
            s_next = scores(g, kb + 1)
            nxt.append(col_max(s_next))
            s_ref[g] = s_next
        return tuple(nxt)

    first = []
    for g in heads:
        s0 = scores(g, 0)
        first.append(col_max(s0))
        s_ref[g] = s0
    lax.fori_loop(0, qi, body, tuple(first))
    key = lax.broadcasted_iota(jnp.int32, (t, t), 0)
    qry = lax.broadcasted_iota(jnp.int32, (t, t), 1)
    outs = []
    for g in heads:
        s = jnp.where(key <= qry, s_ref[g], -jnp.inf)
        update(g, s, col_max(s), qi)
        acc = acc_ref[g]
        outs.append((acc[0:dv, :] / acc[dv:dv + 1, :]).T)
    o_ref[...] = jnp.concatenate(outs, axis=1).astype(o_ref.dtype)


def _flash_attention(q_arr, k_arr, v_arr, q_off, k_off, v_off, dk, dv, n_heads, aug=None, t=512, group=4):
    b, s, _ = q_arr.shape
    assert q_off % group == 0 and k_off % group == 0 and v_off % group == 0 and n_heads % group == 0
    qo, ko, vo = q_off // group, k_off // group, v_off // group
    in_specs = [pl.BlockSpec((None, t, group * dk), lambda bi, h, qi: (bi, qi, qo + h)),
                pl.BlockSpec((None, s, group * dk), lambda bi, h, qi: (bi, 0, ko + h)),
                pl.BlockSpec((None, s, group * dv), lambda bi, h, qi: (bi, 0, vo + h))]
    args = [q_arr, k_arr, v_arr]
    vmem = 2 * group * (_nbytes((t, dk), BF16) + _nbytes((s, dk), BF16) + _nbytes((s, dv), BF16)
                        + _nbytes((t, dv), BF16))
    if aug is not None:
        in_specs += [pl.BlockSpec((None, group, t, LANES), lambda bi, h, qi: (bi, h, qi, 0)),
                     pl.BlockSpec((None, group, s, LANES), lambda bi, h, qi: (bi, h, 0, 0))]
        args += list(aug)
        vmem += 2 * group * (_nbytes((t, LANES), BF16) + _nbytes((s, LANES), BF16))
    acc_rows = dv + 16
    vmem += group * (_nbytes((acc_rows, s), BF16) + 2 * _nbytes((acc_rows, t), F32) + 8 * _nbytes((t, t), F32))
    return pl.pallas_call(
        functools.partial(_flash_kernel, t=t, dk=dk, dv=dv, group=group, has_aug=aug is not None),
        grid=(b, n_heads // group, s // t),
        in_specs=in_specs,
        out_specs=pl.BlockSpec((None, t, group * dv), lambda bi, h, qi: (bi, qi, h)),
        out_shape=jax.ShapeDtypeStruct((b, s, n_heads * dv), BF16),
        scratch_shapes=[pltpu.VMEM((group, acc_rows, s), BF16), pltpu.VMEM((group, 1, t), F32),
                        pltpu.VMEM((group, acc_rows, t), F32), pltpu.VMEM((group, t, t), F32)],
        compiler_params=_cparams(("parallel", "parallel", "arbitrary"), vmem),
        name="flash_attention",
    )(*args)


def _sb_kernel(q_ref, k_ref, v_ref, o_ref, vt_ref, carry_ref, acc_ref, s_ref, a_ref, *, t, sub, d, group):
    qi = pl.program_id(2)
    heads = range(group)

    @pl.when(qi == 0)
    def _():
        for g in heads:
            _fill_v_transposed(v_ref.at[:, g * d:(g + 1) * d], vt_ref.at[g], t)

    carry_ref[...] = jnp.zeros(carry_ref.shape, F32)
    acc_ref[...] = jnp.zeros(acc_ref.shape, F32)
    qs = [q_ref[:, g * d:(g + 1) * d] for g in heads]
    r = lax.broadcasted_iota(jnp.int32, (sub + 16, sub), 0)
    c = lax.broadcasted_iota(jnp.int32, (sub + 16, sub), 1)
    suffix = jnp.where(((c > r) & (r < sub)) | (r == sub), 1.0, 0.0).astype(BF16)
    suffix = jnp.concatenate([suffix, suffix], axis=1)

    def scores(g, kb):
        ks = pl.multiple_of(kb * t, t)
        return _dot_nt(k_ref[pl.ds(ks, t), g * d:(g + 1) * d], qs[g])

    def weights(g, on_diagonal):
        later = carry_ref[g]
        for i in reversed(range(t // sub)):
            q0 = i * sub if on_diagonal else 0
            rows = slice(i * sub, (i + 1) * sub)
            z = s_ref[g, rows, q0:]
            neg_abs = lax.bitcast_convert_type(
                lax.bitcast_convert_type(z, jnp.uint32) | jnp.uint32(0x80000000), F32)
            log_beta = jnp.minimum(z, 0.0) - jnp.log2(1.0 + jnp.exp2(neg_abs))
            log_keep = log_beta - z
            if on_diagonal:
                key = lax.broadcasted_iota(jnp.int32, z.shape, 0) + i * sub
                strict = key < lax.broadcasted_iota(jnp.int32, z.shape, 1) + q0
                log_keep = jnp.where(strict, log_keep, 0.0)
            hi = log_keep.astype(BF16)
            lo = (log_keep - hi.astype(F32)).astype(BF16)
            sums = _dot(suffix, jnp.concatenate([hi, lo], axis=0))
            seen = later[:, q0:]
            a = jnp.exp2(log_beta + (sums[0:sub, :] + seen[0:1, :]))
            if on_diagonal:
                a = jnp.where(strict, a, 0.0)
            a_ref[g, rows, q0:] = a.astype(BF16)
            total = seen + sums[sub:sub + 8, :]
            if q0:
                a_ref[g, rows, 0:q0] = jnp.zeros((sub, q0), BF16)
                total = jnp.concatenate([later[:, 0:q0], total], axis=1)
            later = total
        carry_ref[g] = later

    def accumulate(g, kb):
        ks = pl.multiple_of(kb * t, t)
        acc_ref[g] += _dot(vt_ref[g, :, pl.ds(ks, t)], a_ref[g])

    def finish():
        o_ref[...] = jnp.concatenate([acc_ref[g].T for g in heads], axis=1).astype(o_ref.dtype)

    @pl.when(qi == 0)
    def _():
        for g in heads:
            s_ref[g] = scores(g, 0)
            weights(g, True)
            accumulate(g, 0)
        finish()

    @pl.when(qi > 0)
    def _():
        for g in heads:
            s_ref[g] = scores(g, qi)
            weights(g, True)
            s_ref[g] = scores(g, qi - 1)

        def body(i, carry):
            for g in heads:
                accumulate(g, qi - i)
                weights(g, False)
                s_ref[g] = scores(g, qi - i - 2)
            return carry

        lax.fori_loop(0, qi - 1, body, 0)
        for g in heads:
            accumulate(g, 1)
            weights(g, False)
            accumulate(g, 0)
        finish()


def _sb_attention(proj, n_heads, t=512, group=4):
    b, s, _ = proj.shape
    d = HEAD_DIM
    assert n_heads % group == 0
    blocks = n_heads // group
    vmem = group * (2 * (2 * _nbytes((t, d), BF16) + 2 * _nbytes((s, d), BF16))
                    + _nbytes((d, s), BF16) + 2 * _nbytes((d, t), F32) + 12 * _nbytes((t, t), F32))
    return pl.pallas_call(
        functools.partial(_sb_kernel, t=t, sub=LANES, d=d, group=group),
        grid=(b, blocks, s // t),
        in_specs=[pl.BlockSpec((None, t, group * d), lambda bi, h, qi: (bi, qi, h)),
                  pl.BlockSpec((None, s, group * d), lambda bi, h, qi: (bi, 0, blocks + h)),
                  pl.BlockSpec((None, s, group * d), lambda bi, h, qi: (bi, 0, 2 * blocks + h))],
        out_specs=pl.BlockSpec((None, t, group * d), lambda bi, h, qi: (bi, qi, h)),
        out_shape=jax.ShapeDtypeStruct((b, s, n_heads * d), BF16),
        scratch_shapes=[pltpu.VMEM((group, d, s), BF16), pltpu.VMEM((group, 8, t), F32),
                        pltpu.VMEM((group, d, t), F32), pltpu.VMEM((group, t, t), F32),
                        pltpu.VMEM((group, t, t), BF16)],
        compiler_params=_cparams(("parallel", "parallel", "arbitrary"), vmem),
        name="stick_breaking_attention",
    )(proj, proj, proj)


def _split3(x):
    hi = x.astype(BF16)
    rest = x - hi.astype(F32)
    mid = rest.astype(BF16)
    lo = (rest - mid.astype(F32)).astype(BF16)
    return hi, mid, lo


def _fox_gate_kernel(fl_ref, b_ref, pq_ref, pk_ref, oq_ref, ok_ref, qa_ref, ka_ref, carry_ref):
    ts = fl_ref.shape[0]

    @pl.when(pl.program_id(1) == 0)
    def _():
        carry_ref[...] = jnp.zeros(carry_ref.shape, F32)

    z = fl_ref[...] + b_ref[...]
    log_f = jnp.minimum(z, 0.0) - jnp.log(1.0 + jnp.exp(-jnp.abs(z)))
    row = lax.broadcasted_iota(jnp.int32, (ts, ts), 0)
    col = lax.broadcasted_iota(jnp.int32, (ts, ts), 1)
    prefix = jnp.where(col <= row, 1.0, 0.0).astype(BF16)
    hi, mid, lo = _split3(log_f)
    c = _dot(prefix, hi) + _dot(prefix, mid) + _dot(prefix, lo) + carry_ref[0:1, :]
    carry_ref[0:1, :] = c[ts - 1:ts, :]
    parts = jnp.concatenate(_split3(c * LOG2E), axis=1)
    qa = (_dot(parts, pq_ref[...]) + oq_ref[...]).astype(BF16)
    ka = (_dot(parts, pk_ref[...]) + ok_ref[...]).astype(BF16)
    for h in range(qa_ref.shape[0]):
        qa_ref[h] = qa[:, h * LANES:(h + 1) * LANES]
        ka_ref[h] = ka[:, h * LANES:(h + 1) * LANES]


def _fox_aug_tables(n_heads):
    part, src = jnp.arange(3 * LANES) // LANES, jnp.arange(3 * LANES) % LANES
    head, lane = jnp.arange(n_heads * LANES) // LANES, jnp.arange(n_heads * LANES) % LANES
    mine = src[:, None] == head[None, :]
    pq = jnp.where(mine & (lane[None, :] == part[:, None] + 3), 1.0, 0.0).astype(BF16)
    pk = jnp.where(mine & (lane[None, :] == part[:, None]), -1.0, 0.0).astype(BF16)
    oq = jnp.where(lane < 3, 1.0, 0.0).astype(F32)[None]
    ok = jnp.where((lane >= 3) & (lane < 6), 1.0, 0.0).astype(F32)[None]
    return pq, pk, oq, ok


def _fox_gates(f_logit, b_f, n_heads, ts=256):
    b, s, w = f_logit.shape
    hw = n_heads * LANES
    out = jax.ShapeDtypeStruct((b, n_heads, s, LANES), BF16)
    const = lambda bi, i: (0, 0)
    return pl.pallas_call(
        _fox_gate_kernel,
        grid=(b, s // ts),
        in_specs=[pl.BlockSpec((None, ts, w), lambda bi, i: (bi, i, 0)),
                  pl.BlockSpec((1, w), const),
                  pl.BlockSpec((3 * LANES, hw), const), pl.BlockSpec((3 * LANES, hw), const),
                  pl.BlockSpec((1, hw), const), pl.BlockSpec((1, hw), const)],
        out_specs=[pl.BlockSpec((None, n_heads, ts, LANES), lambda bi, i: (bi, 0, i, 0)),
                   pl.BlockSpec((None, n_heads, ts, LANES), lambda bi, i: (bi, 0, i, 0))],
        out_shape=[out, out],
        scratch_shapes=[pltpu.VMEM((8, w), F32)],
        compiler_params=_cparams(("parallel", "arbitrary"), 32 * 2**20),
        name="fox_gate_cumsum",
    )(f_logit, b_f.reshape(1, w), *_fox_aug_tables(n_heads))


def _mem_attn_kernel(q_ref, kv_ref, o_ref, *, scale):
    outs = []
    for h in range(MEM_HEADS):
        q = q_ref[:, h * HEAD_DIM:(h + 1) * HEAD_DIM].astype(BF16)
        k = kv_ref[:, h * HEAD_DIM:(h + 1) * HEAD_DIM]
        v = kv_ref[:, MEM_W + h * HEAD_DIM:MEM_W + (h + 1) * HEAD_DIM]
        s = _dot_nt(q, k) * scale
        p = jnp.exp(s - jnp.max(s, axis=1, keepdims=True))
        o = _dot(p.astype(BF16), v) / jnp.sum(p, axis=1, keepdims=True)
        outs.append(o.astype(o_ref.dtype))
    o_ref[...] = jnp.concatenate(outs, axis=1)


def _mem_attention(q_arr, q_block, mem_kv, tq=512):
    b, s, _ = q_arr.shape
    length = mem_kv.shape[1]
    vmem = (2 * (_nbytes((tq, MEM_W), q_arr.dtype) + _nbytes((length, 2 * MEM_W), BF16)
                 + _nbytes((tq, MEM_W), BF16)) + 8 * _nbytes((tq, length), F32))
    return pl.pallas_call(
        functools.partial(_mem_attn_kernel, scale=HEAD_DIM ** -0.5),
        grid=(b, s // tq),
        in_specs=[pl.BlockSpec((None, tq, MEM_W), lambda bi, i: (bi, i, q_block)),
                  pl.BlockSpec((None, length, 2 * MEM_W), lambda bi, i: (bi, 0, 0))],
        out_specs=pl.BlockSpec((None, tq, MEM_W), lambda bi, i: (bi, i, 0)),
        out_shape=jax.ShapeDtypeStruct((b, s, MEM_W), BF16),
        compiler_params=_cparams(("parallel", "parallel"), vmem),
        name="memory_attention",
    )(q_arr, mem_kv)


def _t5_bucket_table():
    max_exact = NUM_BUCKETS // 2
    kj = jnp.arange(2 * WINDOW)[:, None]
    qi = jnp.arange(WINDOW)[None, :]
    signed = WINDOW + qi - kj
    dist = jnp.maximum(signed, 0)
    d = jnp.maximum(dist, 1).astype(F32)
    large = max_exact + (jnp.log(d / max_exact) / math.log(MAX_DISTANCE / max_exact)
                         * (NUM_BUCKETS - max_exact)).astype(jnp.int32)
    bucket = jnp.where(dist < max_exact, dist, jnp.minimum(large, NUM_BUCKETS - 1))
    return jnp.where((signed >= 0) & (signed < WINDOW), bucket, -1).astype(jnp.int32)


def _t5_bias_kernel(rb_ref, bucket_ref, o_ref):
    kvh = pl.program_id(0)
    bucket = bucket_ref[...]
    for g in range(SWA_GROUP):
        bias = jnp.full(bucket.shape, -jnp.inf, F32)
        for b in range(NUM_BUCKETS):
            bias = jnp.where(bucket == b, rb_ref[b, kvh * SWA_GROUP + g] * LOG2E, bias)
        o_ref[:, g * WINDOW:(g + 1) * WINDOW] = bias


def _t5_bias(rel_bias):
    return pl.pallas_call(
        _t5_bias_kernel,
        grid=(N_SWA_KV_HEADS,),
        in_specs=[pl.BlockSpec(memory_space=pltpu.SMEM),
                  pl.BlockSpec((2 * WINDOW, WINDOW), lambda h: (0, 0))],
        out_specs=pl.BlockSpec((None, 2 * WINDOW, SWA_GROUP * WINDOW), lambda h: (h, 0, 0)),
        out_shape=jax.ShapeDtypeStruct((N_SWA_KV_HEADS, 2 * WINDOW, SWA_GROUP * WINDOW), F32),
        compiler_params=_cparams(("parallel",), 16 * 2**20),
        name="t5_bias",
    )(rel_bias, _t5_bucket_table())


def _swa_kernel(sink_ref, q_ref, kvc_ref, kvp_ref, bias_ref, o_ref):
    n = pl.program_id(1)
    d = SWA_HEAD_DIM
    kv_w = N_SWA_KV_HEADS * d
    k_win = jnp.concatenate([kvp_ref[:, 0:kv_w], kvc_ref[:, 0:kv_w]], axis=0)
    v_win = jnp.concatenate([kvp_ref[:, kv_w:2 * kv_w], kvc_ref[:, kv_w:2 * kv_w]], axis=0)
    v_t = v_win.astype(F32).T.astype(BF16)
    no_prev = jnp.where(n == 0, -jnp.inf, 0.0)
    outs = []
    for kvh in range(N_SWA_KV_HEADS):
        heads = range(kvh * SWA_GROUP, (kvh + 1) * SWA_GROUP)
        q = jnp.concatenate([q_ref[:, h * d:(h + 1) * d] for h in heads], axis=0)
        sink = jnp.concatenate([jnp.full((1, WINDOW), sink_ref[h] * LOG2E, F32) for h in heads], axis=1)
        s = _dot_nt(k_win[:, kvh * d:(kvh + 1) * d], q) + bias_ref[kvh]
        s = jnp.concatenate([s[0:WINDOW, :] + no_prev, s[WINDOW:, :]], axis=0)
        m = jnp.maximum(jnp.max(s, axis=0, keepdims=True), sink)
        p = jnp.exp2(s - m)
        denom = jnp.sum(p, axis=0, keepdims=True) + jnp.exp2(sink - m)
        o_t = _dot(v_t[kvh * d:(kvh + 1) * d, :], p.astype(BF16)) / denom
        outs += [o_t[:, g * WINDOW:(g + 1) * WINDOW] for g in range(SWA_GROUP)]
    o_ref[...] = jnp.concatenate(outs, axis=0).T.astype(o_ref.dtype)


def _swa_attention(proj, sinks, bias):
    b, s, _ = proj.shape
    q_w = N_SWA_HEADS * SWA_HEAD_DIM
    kv_w = 2 * N_SWA_KV_HEADS * SWA_HEAD_DIM
    kv_block = q_w // kv_w
    vmem = (2 * (2 * _nbytes((WINDOW, q_w), BF16) + 2 * _nbytes((WINDOW, kv_w), BF16)
                 + _nbytes(bias.shape, F32)) + 16 * 2**20)
    return pl.pallas_call(
        _swa_kernel,
        grid=(b, s // WINDOW),
        in_specs=[pl.BlockSpec(memory_space=pltpu.SMEM),
                  pl.BlockSpec((None, WINDOW, q_w), lambda bi, n: (bi, n, 0)),
                  pl.BlockSpec((None, WINDOW, kv_w), lambda bi, n: (bi, n, kv_block)),
                  pl.BlockSpec((None, WINDOW, kv_w), lambda bi, n: (bi, jnp.maximum(n - 1, 0), kv_block)),
                  pl.BlockSpec(bias.shape, lambda bi, n: (0, 0, 0))],
        out_specs=pl.BlockSpec((None, WINDOW, q_w), lambda bi, n: (bi, n, 0)),
        out_shape=jax.ShapeDtypeStruct((b, s, q_w), BF16),
        compiler_params=_cparams(("parallel", "arbitrary"), vmem),
        name="sliding_window_attention",
    )(sinks, proj, proj, proj, bias)


def _rope_table_kernel(pos_ref, invf_ref, o_ref):
    ang = pos_ref[...].astype(F32) * invf_ref[...]
    lane = lax.broadcasted_iota(jnp.int32, ang.shape, 1)
    half = QK_ROPE // 2
    sin_signed = jnp.where(lane < QK_ROPE + half, -jnp.sin(ang), jnp.sin(ang))
    o_ref[...] = jnp.where(lane < QK_ROPE, jnp.cos(ang), sin_signed)


def _rope_table(positions, tm=1024):
    m = positions.shape[0]
    half = QK_ROPE // 2
    inv_freq = ROPE_THETA ** (-jnp.arange(half, dtype=F32) / half)
    invf = jnp.tile(inv_freq, LANES // half).reshape(1, LANES)
    return pl.pallas_call(
        _rope_table_kernel,
        grid=(m // tm,),
        in_specs=[pl.BlockSpec((tm, 1), lambda i: (i, 0)), pl.BlockSpec((1, LANES), lambda i: (0, 0))],
        out_specs=pl.BlockSpec((tm, LANES), lambda i: (i, 0)),
        out_shape=jax.ShapeDtypeStruct((m, LANES), F32),
        compiler_params=_cparams(("parallel",), 32 * 2**20),
        name="rope_table",
    )(positions, invf)


def _apply_rope(x_and_partner, cs):
    z = x_and_partner * cs
    z = z + pltpu.roll(z, QK_ROPE, axis=1)
    lane = lax.broadcasted_iota(jnp.int32, z.shape, 1)
    return jnp.where(lane < QK_ROPE, z, 0.0)


def _mla_q_kernel(cq_ref, g_ref, w_ref, cs_ref, o_ref, cn_ref):
    @pl.when(pl.program_id(1) == 0)
    def _():
        cn_ref[...] = _rms(cq_ref[...], g_ref[...]).astype(BF16)

    y = _dot(cn_ref[...], w_ref[...]) * ((QK_NOPE + QK_ROPE) ** -0.5 * LOG2E)
    cs = cs_ref[...]
    parts = []
    for h in range(y.shape[1] // (2 * LANES)):
        yh = y[:, h * 2 * LANES:(h + 1) * 2 * LANES]
        parts += [yh[:, :QK_NOPE], _apply_rope(yh[:, QK_NOPE:], cs)]
    o_ref[...] = jnp.concatenate(parts, axis=1).astype(o_ref.dtype)


def _mla_q(proj, q_norm, w_uq_aug, cs, tm=1024, heads_per_step=4):
    m = proj.shape[0]
    hw = 2 * LANES * heads_per_step
    n_heads = w_uq_aug.shape[1] // hw
    return pl.pallas_call(
        _mla_q_kernel,
        grid=(m // tm, n_heads),
        in_specs=[pl.BlockSpec((tm, Q_LORA), lambda i, h: (i, 1)),
                  pl.BlockSpec((1, Q_LORA), lambda i, h: (0, 0)),
                  pl.BlockSpec((Q_LORA, hw), lambda i, h: (0, h)),
                  pl.BlockSpec((tm, LANES), lambda i, h: (i, 0))],
        out_specs=pl.BlockSpec((tm, hw), lambda i, h: (i, h)),
        out_shape=jax.ShapeDtypeStruct((m, n_heads * hw), BF16),
        scratch_shapes=[pltpu.VMEM((tm, Q_LORA), BF16)],
        compiler_params=_cparams(("parallel", "arbitrary"), 32 * 2**20),
        name="mla_q_expand",
    )(proj, q_norm.reshape(1, Q_LORA), w_uq_aug, cs)


def _mla_kv_kernel(ckv_ref, g_ref, w_ref, kr_ref, cs_ref, k_ref, v_ref, cn_ref, kpe_ref):
    @pl.when(pl.program_id(1) == 0)
    def _():
        cn_ref[...] = _rms(ckv_ref[...], g_ref[...]).astype(BF16)
        kpe_ref[...] = _apply_rope(kr_ref[...], cs_ref[...])

    y = _dot(cn_ref[...], w_ref[...])
    kpe = kpe_ref[...]
    k_parts, v_parts = [], []
    for h in range(y.shape[1] // (2 * LANES)):
        yh = y[:, h * 2 * LANES:(h + 1) * 2 * LANES]
        k_parts += [yh[:, :QK_NOPE], kpe]
        v_parts.append(yh[:, QK_NOPE:])
    k_ref[...] = jnp.concatenate(k_parts, axis=1).astype(k_ref.dtype)
    v_ref[...] = jnp.concatenate(v_parts, axis=1).astype(v_ref.dtype)


def _mla_kv(proj, kv_norm, w_ukv, cs, tm=1024, heads_per_step=4):
    m = proj.shape[0]
    hw = 2 * LANES * heads_per_step
    n_heads = w_ukv.shape[1] // hw
    return pl.pallas_call(
        _mla_kv_kernel,
        grid=(m // tm, n_heads),
        in_specs=[pl.BlockSpec((tm, KV_LORA), lambda i, h: (i, 4)),
                  pl.BlockSpec((1, KV_LORA), lambda i, h: (0, 0)),
                  pl.BlockSpec((KV_LORA, hw), lambda i, h: (0, h)),
                  pl.BlockSpec((tm, LANES), lambda i, h: (i, 10)),
                  pl.BlockSpec((tm, LANES), lambda i, h: (i, 0))],
        out_specs=[pl.BlockSpec((tm, hw), lambda i, h: (i, h)),
                   pl.BlockSpec((tm, hw // 2), lambda i, h: (i, h))],
        out_shape=[jax.ShapeDtypeStruct((m, n_heads * hw), BF16),
                   jax.ShapeDtypeStruct((m, n_heads * hw // 2), BF16)],
        scratch_shapes=[pltpu.VMEM((tm, KV_LORA), BF16), pltpu.VMEM((tm, LANES), F32)],
        compiler_params=_cparams(("parallel", "arbitrary"), 32 * 2**20),
        name="mla_kv_expand",
    )(proj, kv_norm.reshape(1, KV_LORA), w_ukv, proj, cs)


def _swap_halves(w):
    half = w.shape[-1] // 2
    return jnp.concatenate([w[..., half:], w[..., :half]], axis=-1)


def _mixer_out(x2, o, o_mem, w_out, b, s):
    return _matmul_res([o.reshape(b * s, o.shape[-1]), o_mem.reshape(b * s, MEM_W)],
                       w_out[0], w_out[1], x2, tm=1024, tn=512)


def _sb_layer(x2, mem_kv, g, w_in, w_out, b, s):
    proj = _norm_matmul(x2, g, w_in.astype(BF16), BF16, tm=IN_PROJ_ROWS, tn=512, scaled_cols=N_HEADS * HEAD_DIM,
                        col_scale=HEAD_DIM ** -0.5 * LOG2E).reshape(b, s, -1)
    o = _sb_attention(proj, N_HEADS)
    o_mem = _mem_attention(proj, 3 * N_HEADS * HEAD_DIM // MEM_W, mem_kv)
    return _mixer_out(x2, o, o_mem, w_out, b, s)


def _fox_layer(x2, mem_kv, g, w_in, b_f, w_out, b, s):
    qkv_w = 3 * N_HEADS * HEAD_DIM
    w_main = jnp.concatenate([w_in[:, :qkv_w], w_in[:, qkv_w + N_HEADS:]], axis=1).astype(BF16)
    w_gate = jnp.pad(w_in[:, qkv_w:qkv_w + N_HEADS], ((0, 0), (0, LANES - N_HEADS))).astype(BF16)
    proj = _norm_matmul(x2, g, w_main, BF16, tm=IN_PROJ_ROWS, tn=512, scaled_cols=N_HEADS * HEAD_DIM,
                        col_scale=HEAD_DIM ** -0.5 * LOG2E).reshape(b, s, -1)
    f_logit = _norm_matmul(x2, g, w_gate, F32, tm=IN_PROJ_ROWS, tn=LANES).reshape(b, s, LANES)
    aug = _fox_gates(f_logit, jnp.pad(b_f, (0, LANES - N_HEADS)), N_HEADS)
    o = _flash_attention(proj, proj, proj, 0, N_HEADS, 2 * N_HEADS, HEAD_DIM, HEAD_DIM, N_HEADS, aug=aug)
    o_mem = _mem_attention(proj, qkv_w // MEM_W, mem_kv)
    return _mixer_out(x2, o, o_mem, w_out, b, s)


def _swa_layer(x2, mem_kv, g, rel_bias, w_in, sinks, w_out, b, s):
    proj = _norm_matmul(x2, g, w_in.astype(BF16), BF16, tm=IN_PROJ_ROWS, tn=512,
                        scaled_cols=N_SWA_HEADS * SWA_HEAD_DIM,
                        col_scale=SWA_HEAD_DIM ** -0.5 * LOG2E).reshape(b, s, -1)
    o = _swa_attention(proj, sinks, _t5_bias(rel_bias))
    q_w = N_SWA_HEADS * SWA_HEAD_DIM
    kv_w = 2 * N_SWA_KV_HEADS * SWA_HEAD_DIM
    o_mem = _mem_attention(proj, (q_w + kv_w) // MEM_W, mem_kv)
    return _mixer_out(x2, o, o_mem, w_out, b, s)


def _mla_layer(x2, mem_kv, g, positions, w_in, q_norm, w_uq, kv_norm, w_ukv, w_out, b, s):
    o2, o3 = Q_LORA + KV_LORA, Q_LORA + KV_LORA + QK_ROPE
    w_main = jnp.concatenate([w_in[:, o3:], w_in[:, :o3], _swap_halves(w_in[:, o2:o3])], axis=1).astype(BF16)
    proj = _norm_matmul(x2, g, w_main, F32, tm=512, tn=w_main.shape[1])
    cs = _rope_table(positions.reshape(b * s, 1))
    wq = w_uq.reshape(Q_LORA, N_HEADS, QK_NOPE + QK_ROPE)
    wq = jnp.concatenate([wq, _swap_halves(wq[:, :, QK_NOPE:])], axis=-1).reshape(Q_LORA, -1).astype(BF16)
    q = _mla_q(proj, q_norm, wq, cs).reshape(b, s, -1)
    k, v = _mla_kv(proj, kv_norm, w_ukv.astype(BF16), cs)
    k, v = k.reshape(b, s, -1), v.reshape(b, s, -1)
    o = _flash_attention(q, k, v, 0, 0, 0, 2 * LANES, HEAD_DIM, N_HEADS)
    o_mem = _mem_attention(proj.reshape(b, s, -1), 0, mem_kv)
    return _mixer_out(x2, o, o_mem, w_out, b, s)


def kernel(x, mem, positions, rel_bias, attn_norm, mem_norm, w_mem_kv, ffn_norm, ffn_w_up, ffn_conv_w, ffn_conv_b, ffn_w_down, final_norm, sb_w_in, sb_w_out, fox_w_in, fox_b_f, fox_w_out, swa_w_in, swa_sinks, swa_w_out, mla_w_in, mla_q_norm, mla_w_uq, mla_kv_norm, mla_w_ukv, mla_w_out):
    b, s, d = x.shape
    depth = attn_norm.shape[0]
    mem_len = mem.shape[1]
    x2 = x.reshape(b * s, d)
    mem2 = mem.reshape(b * mem_len, d)
    w_mem_kv_bf16 = w_mem_kv.astype(BF16)
    for i in range(depth):
        kind, j = i % 4, i // 4
        mem_kv = _norm_matmul(mem2, mem_norm[i], w_mem_kv_bf16, BF16, tm=b * mem_len, tn=2 * MEM_W,
                              layer=i).reshape(b, mem_len, 2 * MEM_W)
        g = attn_norm[i]
        if kind == 0:
            x2 = _sb_layer(x2, mem_kv, g, sb_w_in[j], (sb_w_out, j), b, s)
        elif kind == 1:
            x2 = _fox_layer(x2, mem_kv, g, fox_w_in[j], fox_b_f[j], (fox_w_out, j), b, s)
        elif kind == 2:
            x2 = _swa_layer(x2, mem_kv, g, rel_bias, swa_w_in[j], swa_sinks[j], (swa_w_out, j), b, s)
        else:
            x2 = _mla_layer(x2, mem_kv, g, positions, mla_w_in[j], mla_q_norm[j], mla_w_uq[j],
                            mla_kv_norm[j], mla_w_ukv[j], (mla_w_out, j), b, s)
        gated = _ffn_up(_rmsnorm(x2, ffn_norm[i], BF16, tm=512), ffn_w_up, i, ffn_conv_w[i], ffn_conv_b[i],
                        seq=s, tm=1024, tn=512)
        x2 = _matmul_res([gated], ffn_w_down, i, x2, tm=512, tn=512)
    return _rmsnorm(x2, final_norm, F32, tm=512).reshape(b, s, d)
```

```python
import functools
import math

import jax
import jax.numpy as jnp
from jax import lax
from jax.experimental import pallas as pl
from jax.experimental.pallas import tpu as pltpu

F32 = jnp.float32
BF16 = jnp.bfloat16

EPS = 1e-6
LOG2E = math.log2(math.e)
HEAD_DIM = 128
N_HEADS = 16
SWA_HEAD_DIM = 64
N_SWA_HEADS = 32
N_SWA_KV_HEADS = 4
SWA_GROUP = N_SWA_HEADS // N_SWA_KV_HEADS
WINDOW = 128
Q_LORA = 512
KV_LORA = 256
QK_NOPE = 128
QK_ROPE = 64
ROPE_THETA = 10000.0
MEM_HEADS = 4
MEM_W = MEM_HEADS * HEAD_DIM
NUM_BUCKETS = 32
MAX_DISTANCE = 128
CONV_WIDTH = 3

V7X_VMEM_BYTES = 64 * 1024 * 1024
VMEM_CAP = V7X_VMEM_BYTES - 8 * 1024 * 1024
LANES = 128
IN_PROJ_ROWS = 1024
TAIL = 8


def _cparams(sems, vmem_bytes):
    return pltpu.CompilerParams(dimension_semantics=sems,
                                vmem_limit_bytes=int(min(max(vmem_bytes, 16 * 2**20), VMEM_CAP)))


def _nbytes(shape, dtype):
    return math.prod(shape) * jnp.dtype(dtype).itemsize


def _rms(x, g):
    return x * lax.rsqrt(jnp.mean(x * x, axis=-1, keepdims=True) + EPS) * g


def _dot(a, b):
    return jnp.dot(a, b, preferred_element_type=F32)


def _dot_nt(a, b):
    return lax.dot_general(a, b, (((1,), (1,)), ((), ())), preferred_element_type=F32)


def _norm_matmul_kernel(x_ref, g_ref, w_ref, cs_ref, o_ref, xn_ref):
    @pl.when(pl.program_id(1) == 0)
    def _():
        xn_ref[...] = _rms(x_ref[...], g_ref[...]).astype(BF16)

    o_ref[...] = (_dot(xn_ref[...], w_ref[...]) * cs_ref[...]).astype(o_ref.dtype)


def _w_tile_spec(w, layer, k, tn, col_block):
    if w.ndim == 3:
        return pl.BlockSpec((None, k, tn), lambda i, j: (layer, 0, col_block(j)))
    return pl.BlockSpec((k, tn), lambda i, j: (0, col_block(j)))


def _norm_matmul(x, g, w, out_dtype, tm, tn, scaled_cols=0, col_scale=1.0, layer=None):
    m, k = x.shape
    n = w.shape[-1]
    cs = jnp.where(jnp.arange(n) < scaled_cols, col_scale, 1.0).astype(F32).reshape(1, n)
    vmem = (2 * (_nbytes((tm, k), F32) + _nbytes((k, tn), BF16) + _nbytes((tm, tn), out_dtype))
            + _nbytes((tm, k), BF16) + 2 * _nbytes((tm, k), F32) + _nbytes((tm, tn), F32))
    return pl.pallas_call(
        _norm_matmul_kernel,
        grid=(m // tm, n // tn),
        in_specs=[pl.BlockSpec((tm, k), lambda i, j: (i, 0)),
                  pl.BlockSpec((1, k), lambda i, j: (0, 0)),
                  _w_tile_spec(w, layer, k, tn, lambda j: j),
                  pl.BlockSpec((1, tn), lambda i, j: (0, j))],
        out_specs=pl.BlockSpec((tm, tn), lambda i, j: (i, j)),
        out_shape=jax.ShapeDtypeStruct((m, n), out_dtype),
        scratch_shapes=[pltpu.VMEM((tm, k), BF16)],
        compiler_params=_cparams(("parallel", "arbitrary"), vmem),
        name="norm_matmul",
    )(x, g.reshape(1, k), w, cs)


def _matmul_res_kernel(*refs, n_a):
    a_refs, w_refs = refs[:n_a], refs[n_a:2 * n_a]
    x_ref, o_ref = refs[2 * n_a], refs[2 * n_a + 1]
    wb_refs = refs[2 * n_a + 2:]

    @pl.when(pl.program_id(1) == 0)
    def _():
        for w_ref, wb_ref in zip(w_refs, wb_refs):
            wb_ref[...] = w_ref[...].astype(BF16)

    acc = x_ref[...]
    for a_ref, wb_ref in zip(a_refs, wb_refs):
        acc = acc + _dot(a_ref[...], wb_ref[...])
    o_ref[...] = acc


def _matmul_res(a_list, w, layer, x, tm, tn):
    m, n = x.shape
    vmem = 6 * _nbytes((tm, tn), F32)
    a_specs, w_specs, scratch = [], [], []
    row = 0
    for a in a_list:
        kp = a.shape[1]
        row_block = row // kp
        assert row_block * kp == row
        a_specs.append(pl.BlockSpec((tm, kp), lambda j, i: (i, 0)))
        w_specs.append(pl.BlockSpec((None, kp, tn), lambda j, i, row_block=row_block: (layer, row_block, j)))
        scratch.append(pltpu.VMEM((kp, tn), BF16))
        vmem += 2 * _nbytes((tm, kp), a.dtype) + 2 * _nbytes((kp, tn), F32) + 2 * _nbytes((kp, tn), BF16)
        row += kp
    assert row == w.shape[1]
    return pl.pallas_call(
        functools.partial(_matmul_res_kernel, n_a=len(a_list)),
        grid=(n // tn, m // tm),
        in_specs=a_specs + w_specs + [pl.BlockSpec((tm, tn), lambda j, i: (i, j))],
        out_specs=pl.BlockSpec((tm, tn), lambda j, i: (i, j)),
        out_shape=jax.ShapeDtypeStruct((m, n), F32),
        scratch_shapes=scratch,
        compiler_params=_cparams(("parallel", "arbitrary"), vmem),
        name="matmul_res",
    )(*a_list, *([w] * len(a_list)), x)


def _rmsnorm_kernel(x_ref, g_ref, o_ref):
    o_ref[...] = _rms(x_ref[...], g_ref[...]).astype(o_ref.dtype)


def _rmsnorm(x, g, out_dtype, tm):
    m, k = x.shape
    return pl.pallas_call(
        _rmsnorm_kernel,
        grid=(m // tm,),
        in_specs=[pl.BlockSpec((tm, k), lambda i: (i, 0)), pl.BlockSpec((1, k), lambda i: (0, 0))],
        out_specs=pl.BlockSpec((tm, k), lambda i: (i, 0)),
        out_shape=jax.ShapeDtypeStruct((m, k), out_dtype),
        compiler_params=_cparams(("parallel",), 8 * _nbytes((tm, k), F32)),
        name="rmsnorm",
    )(x, g.reshape(1, k))


def _ffn_up_kernel(xn_ref, wg_ref, wv_ref, cwg_ref, cwv_ref, cbg_ref, cbv_ref, o_ref,
                   wgb_ref, wvb_ref, tail_g_ref, tail_v_ref, *u_refs, rows, tiles_per_seq):
    i = pl.program_id(1)

    @pl.when(i == 0)
    def _():
        wgb_ref[...] = wg_ref[...].astype(BF16)
        wvb_ref[...] = wv_ref[...].astype(BF16)

    @pl.when(i % tiles_per_seq == 0)
    def _():
        tail_g_ref[...] = jnp.zeros(tail_g_ref.shape, F32)
        tail_v_ref[...] = jnp.zeros(tail_v_ref.shape, F32)

    n_chunks = len(u_refs) // 2

    for r in range(n_chunks):
        xr = xn_ref[r * rows:(r + 1) * rows, :]

        def conv(wb_ref, cw_ref, cb_ref, u_ref, before):
            u_ref[TAIL:, :] = _dot(xr, wb_ref[...])
            u_ref[0:TAIL, :] = before
            c = cb_ref[...]
            for tap in range(CONV_WIDTH):
                start = TAIL - (CONV_WIDTH - 1) + tap
                c = c + cw_ref[tap:tap + 1, :] * u_ref[start:start + rows, :]
            return c

        if r == 0:
            before_g, before_v = tail_g_ref[...], tail_v_ref[...]
        else:
            before_g, before_v = u_refs[2 * r - 2][rows:, :], u_refs[2 * r - 1][rows:, :]
        gate = conv(wgb_ref, cwg_ref, cbg_ref, u_refs[2 * r], before_g)
        val = conv(wvb_ref, cwv_ref, cbv_ref, u_refs[2 * r + 1], before_v)
        o_ref[r * rows:(r + 1) * rows, :] = (gate * (1.0 / (1.0 + jnp.exp(-gate))) * val).astype(o_ref.dtype)

    tail_g_ref[...] = u_refs[2 * n_chunks - 2][rows:, :]
    tail_v_ref[...] = u_refs[2 * n_chunks - 1][rows:, :]


def _ffn_up(xn, w_up, layer, conv_w, conv_b, seq, tm, tn, rows=256):
    m, k = xn.shape
    d_ff = w_up.shape[-1] // 2
    nj = d_ff // tn
    n_chunks = tm // rows
    vmem = (2 * (_nbytes((tm, k), BF16) + 2 * _nbytes((k, tn), F32) + _nbytes((tm, tn), BF16))
            + 2 * _nbytes((k, tn), BF16) + 2 * n_chunks * _nbytes((rows + TAIL, tn), F32)
            + 2 * _nbytes((k, tn), F32) + 6 * n_chunks * _nbytes((rows, tn), F32))
    return pl.pallas_call(
        functools.partial(_ffn_up_kernel, rows=rows, tiles_per_seq=seq // tm),
        grid=(nj, m // tm),
        in_specs=[pl.BlockSpec((tm, k), lambda j, i: (i, 0)),
                  pl.BlockSpec((None, k, tn), lambda j, i: (layer, 0, j)),
                  pl.BlockSpec((None, k, tn), lambda j, i: (layer, 0, nj + j)),
                  pl.BlockSpec((CONV_WIDTH, tn), lambda j, i: (0, j)),
                  pl.BlockSpec((CONV_WIDTH, tn), lambda j, i: (0, nj + j)),
                  pl.BlockSpec((1, tn), lambda j, i: (0, j)),
                  pl.BlockSpec((1, tn), lambda j, i: (0, nj + j))],
        out_specs=pl.BlockSpec((tm, tn), lambda j, i: (i, j)),
        out_shape=jax.ShapeDtypeStruct((m, d_ff), BF16),
        scratch_shapes=[pltpu.VMEM((k, tn), BF16), pltpu.VMEM((k, tn), BF16),
                        pltpu.VMEM((TAIL, tn), F32), pltpu.VMEM((TAIL, tn), F32)]
        + [pltpu.VMEM((rows + TAIL, tn), F32)] * (2 * n_chunks),
        compiler_params=_cparams(("arbitrary", "arbitrary"), vmem),
        name="ffn_up_conv_gate",
    )(xn, w_up, w_up, conv_w, conv_w, conv_b.reshape(1, -1), conv_b.reshape(1, -1))


def _fill_v_transposed(v_ref, vt_ref, t):
    s, dv = v_ref.shape
    for c in range(s // t):
        vt_ref[0:dv, c * t:(c + 1) * t] = v_ref[c * t:(c + 1) * t, :].astype(F32).T.astype(BF16)
    if vt_ref.shape[0] > dv:
        vt_ref[dv:, :] = jnp.ones((vt_ref.shape[0] - dv, s), BF16)


def _flash_kernel(*refs, t, dk, dv, group, has_aug):
    if has_aug:
        q_ref, k_ref, v_ref, qa_ref, ka_ref, o_ref, vt_ref, m_ref, acc_ref, s_ref = refs
    else:
        q_ref, k_ref, v_ref, o_ref, vt_ref, m_ref, acc_ref, s_ref = refs
    qi = pl.program_id(2)
    heads = range(group)

    @pl.when(qi == 0)
    def _():
        for g in heads:
            _fill_v_transposed(v_ref.at[:, g * dv:(g + 1) * dv], vt_ref.at[g], t)

    qs = []
    for g in heads:
        q = q_ref[:, g * dk:(g + 1) * dk]
        qs.append(jnp.concatenate([q, qa_ref[g]], axis=1) if has_aug else q)
    m_ref[...] = jnp.full(m_ref.shape, -jnp.inf, F32)
    acc_ref[...] = jnp.zeros(acc_ref.shape, F32)

    def scores(g, kb):
        ks = pl.multiple_of(kb * t, t)
        k = k_ref[pl.ds(ks, t), g * dk:(g + 1) * dk]
        if has_aug:
            k = jnp.concatenate([k, ka_ref[g, pl.ds(ks, t), :]], axis=1)
        return _dot_nt(k, qs[g])

    def col_max(s):
        return jnp.max(s, axis=0, keepdims=True)

    def update(g, s, s_max, kb):
        m_prev = m_ref[g]
        m_new = jnp.maximum(m_prev, s_max)
        m_ref[g] = m_new
        p = jnp.exp2(s - m_new).astype(BF16)
        ks = pl.multiple_of(kb * t, t)
        acc_ref[g] = jnp.exp2(m_prev - m_new) * acc_ref[g] + _dot(vt_ref[g, :, pl.ds(ks, t)], p)

    def body(kb, s_max):
        nxt = []
        for g in heads:
            update(g, s_ref[g], s_max[g], kb)
            s_next = scores(g, kb + 1)
            nxt.append(col_max(s_next))
            s_ref[g] = s_next
        return tuple(nxt)

    first = []
    for g in heads:
        s0 = scores(g, 0)
        first.append(col_max(s0))
        s_ref[g] = s0
    lax.fori_loop(0, qi, body, tuple(first))
    key = lax.broadcasted_iota(jnp.int32, (t, t), 0)
    qry = lax.broadcasted_iota(jnp.int32, (t, t), 1)
    outs = []
    for g in heads:
        s = jnp.where(key <= qry, s_ref[g], -jnp.inf)
        update(g, s, col_max(s), qi)
        acc = acc_ref[g]
        outs.append((acc[0:dv, :] / acc[dv:dv + 1, :]).T)
    o_ref[...] = jnp.concatenate(outs, axis=1).astype(o_ref.dtype)


def _flash_attention(q_arr, k_arr, v_arr, q_off, k_off, v_off, dk, dv, n_heads, aug=None, t=512, group=4):
    b, s, _ = q_arr.shape
    assert q_off % group == 0 and k_off % group == 0 and v_off % group == 0 and n_heads % group == 0
    qo, ko, vo = q_off // group, k_off // group, v_off // group
    in_specs = [pl.BlockSpec((None, t, group * dk), lambda bi, h, qi: (bi, qi, qo + h)),
                pl.BlockSpec((None, s, group * dk), lambda bi, h, qi: (bi, 0, ko + h)),
                pl.BlockSpec((None, s, group * dv), lambda bi, h, qi: (bi, 0, vo + h))]
    args = [q_arr, k_arr, v_arr]
    vmem = 2 * group * (_nbytes((t, dk), BF16) + _nbytes((s, dk), BF16) + _nbytes((s, dv), BF16)
                        + _nbytes((t, dv), BF16))
    if aug is not None:
        in_specs += [pl.BlockSpec((None, group, t, LANES), lambda bi, h, qi: (bi, h, qi, 0)),
                     pl.BlockSpec((None, group, s, LANES), lambda bi, h, qi: (bi, h, 0, 0))]
        args += list(aug)
        vmem += 2 * group * (_nbytes((t, LANES), BF16) + _nbytes((s, LANES), BF16))
    acc_rows = dv + 16
    vmem += group * (_nbytes((acc_rows, s), BF16) + 2 * _nbytes((acc_rows, t), F32) + 8 * _nbytes((t, t), F32))
    return pl.pallas_call(
        functools.partial(_flash_kernel, t=t, dk=dk, dv=dv, group=group, has_aug=aug is not None),
        grid=(b, n_heads // group, s // t),
        in_specs=in_specs,
        out_specs=pl.BlockSpec((None, t, group * dv), lambda bi, h, qi: (bi, qi, h)),
        out_shape=jax.ShapeDtypeStruct((b, s, n_heads * dv), BF16),
        scratch_shapes=[pltpu.VMEM((group, acc_rows, s), BF16), pltpu.VMEM((group, 1, t), F32),
                        pltpu.VMEM((group, acc_rows, t), F32), pltpu.VMEM((group, t, t), F32)],
        compiler_params=_cparams(("parallel", "parallel", "arbitrary"), vmem),
        name="flash_attention",
    )(*args)


def _sb_kernel(q_ref, k_ref, v_ref, o_ref, vt_ref, carry_ref, acc_ref, s_ref, a_ref, *, t, sub, d, group):
    qi = pl.program_id(2)
    heads = range(group)

    @pl.when(qi == 0)
    def _():
        for g in heads:
            _fill_v_transposed(v_ref.at[:, g * d:(g + 1) * d], vt_ref.at[g], t)

    carry_ref[...] = jnp.zeros(carry_ref.shape, F32)
    acc_ref[...] = jnp.zeros(acc_ref.shape, F32)
    qs = [q_ref[:, g * d:(g + 1) * d] for g in heads]
    r = lax.broadcasted_iota(jnp.int32, (sub + 16, sub), 0)
    c = lax.broadcasted_iota(jnp.int32, (sub + 16, sub), 1)
    suffix = jnp.where(((c > r) & (r < sub)) | (r == sub), 1.0, 0.0).astype(BF16)
    suffix = jnp.concatenate([suffix, suffix], axis=1)

    def scores(g, kb):
        ks = pl.multiple_of(kb * t, t)
        return _dot_nt(k_ref[pl.ds(ks, t), g * d:(g + 1) * d], qs[g])

    def weights(g, on_diagonal):
        later = carry_ref[g]
        for i in reversed(range(t // sub)):
            q0 = i * sub if on_diagonal else 0
            rows = slice(i * sub, (i + 1) * sub)
            z = s_ref[g, rows, q0:]
            neg_abs = lax.bitcast_convert_type(
                lax.bitcast_convert_type(z, jnp.uint32) | jnp.uint32(0x80000000), F32)
            log_beta = jnp.minimum(z, 0.0) - jnp.log2(1.0 + jnp.exp2(neg_abs))
            log_keep = log_beta - z
            if on_diagonal:
                key = lax.broadcasted_iota(jnp.int32, z.shape, 0) + i * sub
                strict = key < lax.broadcasted_iota(jnp.int32, z.shape, 1) + q0
                log_keep = jnp.where(strict, log_keep, 0.0)
            hi = log_keep.astype(BF16)
            lo = (log_keep - hi.astype(F32)).astype(BF16)
            sums = _dot(suffix, jnp.concatenate([hi, lo], axis=0))
            seen = later[:, q0:]
            a = jnp.exp2(log_beta + (sums[0:sub, :] + seen[0:1, :]))
            if on_diagonal:
                a = jnp.where(strict, a, 0.0)
            a_ref[g, rows, q0:] = a.astype(BF16)
            total = seen + sums[sub:sub + 8, :]
            if q0:
                a_ref[g, rows, 0:q0] = jnp.zeros((sub, q0), BF16)
                total = jnp.concatenate([later[:, 0:q0], total], axis=1)
            later = total
        carry_ref[g] = later

    def accumulate(g, kb):
        ks = pl.multiple_of(kb * t, t)
        acc_ref[g] += _dot(vt_ref[g, :, pl.ds(ks, t)], a_ref[g])

    def finish():
        o_ref[...] = jnp.concatenate([acc_ref[g].T for g in heads], axis=1).astype(o_ref.dtype)

    @pl.when(qi == 0)
    def _():
        for g in heads:
            s_ref[g] = scores(g, 0)
            weights(g, True)
            accumulate(g, 0)
        finish()

    @pl.when(qi > 0)
    def _():
        for g in heads:
            s_ref[g] = scores(g, qi)
            weights(g, True)
            s_ref[g] = scores(g, qi - 1)

        def body(i, carry):
            for g in heads:
                accumulate(g, qi - i)
                weights(g, False)
                s_ref[g] = scores(g, qi - i - 2)
            return carry

        lax.fori_loop(0, qi - 1, body, 0)
        for g in heads:
            accumulate(g, 1)
            weights(g, False)
            accumulate(g, 0)
        finish()


def _sb_attention(proj, n_heads, t=512, group=4):
    b, s, _ = proj.shape
    d = HEAD_DIM
    assert n_heads % group == 0
    blocks = n_heads // group
    vmem = group * (2 * (2 * _nbytes((t, d), BF16) + 2 * _nbytes((s, d), BF16))
                    + _nbytes((d, s), BF16) + 2 * _nbytes((d, t), F32) + 12 * _nbytes((t, t), F32))
    return pl.pallas_call(
        functools.partial(_sb_kernel, t=t, sub=LANES, d=d, group=group),
        grid=(b, blocks, s // t),
        in_specs=[pl.BlockSpec((None, t, group * d), lambda bi, h, qi: (bi, qi, h)),
                  pl.BlockSpec((None, s, group * d), lambda bi, h, qi: (bi, 0, blocks + h)),
                  pl.BlockSpec((None, s, group * d), lambda bi, h, qi: (bi, 0, 2 * blocks + h))],
        out_specs=pl.BlockSpec((None, t, group * d), lambda bi, h, qi: (bi, qi, h)),
        out_shape=jax.ShapeDtypeStruct((b, s, n_heads * d), BF16),
        scratch_shapes=[pltpu.VMEM((group, d, s), BF16), pltpu.VMEM((group, 8, t), F32),
                        pltpu.VMEM((group, d, t), F32), pltpu.VMEM((group, t, t), F32),
                        pltpu.VMEM((group, t, t), BF16)],
        compiler_params=_cparams(("parallel", "parallel", "arbitrary"), vmem),
        name="stick_breaking_attention",
    )(proj, proj, proj)


def _split3(x):
    hi = x.astype(BF16)
    rest = x - hi.astype(F32)
    mid = rest.astype(BF16)
    lo = (rest - mid.astype(F32)).astype(BF16)
    return hi, mid, lo


def _fox_gate_kernel(fl_ref, b_ref, pq_ref, pk_ref, oq_ref, ok_ref, qa_ref, ka_ref, carry_ref):
    ts = fl_ref.shape[0]

    @pl.when(pl.program_id(1) == 0)
    def _():
        carry_ref[...] = jnp.zeros(carry_ref.shape, F32)

    z = fl_ref[...] + b_ref[...]
    log_f = jnp.minimum(z, 0.0) - jnp.log(1.0 + jnp.exp(-jnp.abs(z)))
    row = lax.broadcasted_iota(jnp.int32, (ts, ts), 0)
    col = lax.broadcasted_iota(jnp.int32, (ts, ts), 1)
    prefix = jnp.where(col <= row, 1.0, 0.0).astype(BF16)
    hi, mid, lo = _split3(log_f)
    c = _dot(prefix, hi) + _dot(prefix, mid) + _dot(prefix, lo) + carry_ref[0:1, :]
    carry_ref[0:1, :] = c[ts - 1:ts, :]
    parts = jnp.concatenate(_split3(c * LOG2E), axis=1)
    qa = (_dot(parts, pq_ref[...]) + oq_ref[...]).astype(BF16)
    ka = (_dot(parts, pk_ref[...]) + ok_ref[...]).astype(BF16)
    for h in range(qa_ref.shape[0]):
        qa_ref[h] = qa[:, h * LANES:(h + 1) * LANES]
        ka_ref[h] = ka[:, h * LANES:(h + 1) * LANES]


def _fox_aug_tables(n_heads):
    part, src = jnp.arange(3 * LANES) // LANES, jnp.arange(3 * LANES) % LANES
    head, lane = jnp.arange(n_heads * LANES) // LANES, jnp.arange(n_heads * LANES) % LANES
    mine = src[:, None] == head[None, :]
    pq = jnp.where(mine & (lane[None, :] == part[:, None] + 3), 1.0, 0.0).astype(BF16)
    pk = jnp.where(mine & (lane[None, :] == part[:, None]), -1.0, 0.0).astype(BF16)
    oq = jnp.where(lane < 3, 1.0, 0.0).astype(F32)[None]
    ok = jnp.where((lane >= 3) & (lane < 6), 1.0, 0.0).astype(F32)[None]
    return pq, pk, oq, ok


def _fox_gates(f_logit, b_f, n_heads, ts=256):
    b, s, w = f_logit.shape
    hw = n_heads * LANES
    out = jax.ShapeDtypeStruct((b, n_heads, s, LANES), BF16)
    const = lambda bi, i: (0, 0)
    return pl.pallas_call(
        _fox_gate_kernel,
        grid=(b, s // ts),
        in_specs=[pl.BlockSpec((None, ts, w), lambda bi, i: (bi, i, 0)),
                  pl.BlockSpec((1, w), const),
                  pl.BlockSpec((3 * LANES, hw), const), pl.BlockSpec((3 * LANES, hw), const),
                  pl.BlockSpec((1, hw), const), pl.BlockSpec((1, hw), const)],
        out_specs=[pl.BlockSpec((None, n_heads, ts, LANES), lambda bi, i: (bi, 0, i, 0)),
                   pl.BlockSpec((None, n_heads, ts, LANES), lambda bi, i: (bi, 0, i, 0))],
        out_shape=[out, out],
        scratch_shapes=[pltpu.VMEM((8, w), F32)],
        compiler_params=_cparams(("parallel", "arbitrary"), 32 * 2**20),
        name="fox_gate_cumsum",
    )(f_logit, b_f.reshape(1, w), *_fox_aug_tables(n_heads))


def _mem_attn_kernel(q_ref, kv_ref, o_ref, *, scale):
    outs = []
    for h in range(MEM_HEADS):
        q = q_ref[:, h * HEAD_DIM:(h + 1) * HEAD_DIM].astype(BF16)
        k = kv_ref[:, h * HEAD_DIM:(h + 1) * HEAD_DIM]
        v = kv_ref[:, MEM_W + h * HEAD_DIM:MEM_W + (h + 1) * HEAD_DIM]
        s = _dot_nt(q, k) * scale
        p = jnp.exp(s - jnp.max(s, axis=1, keepdims=True))
        o = _dot(p.astype(BF16), v) / jnp.sum(p, axis=1, keepdims=True)
        outs.append(o.astype(o_ref.dtype))
    o_ref[...] = jnp.concatenate(outs, axis=1)


def _mem_attention(q_arr, q_block, mem_kv, tq=512):
    b, s, _ = q_arr.shape
    length = mem_kv.shape[1]
    vmem = (2 * (_nbytes((tq, MEM_W), q_arr.dtype) + _nbytes((length, 2 * MEM_W), BF16)
                 + _nbytes((tq, MEM_W), BF16)) + 8 * _nbytes((tq, length), F32))
    return pl.pallas_call(
        functools.partial(_mem_attn_kernel, scale=HEAD_DIM ** -0.5),
        grid=(b, s // tq),
        in_specs=[pl.BlockSpec((None, tq, MEM_W), lambda bi, i: (bi, i, q_block)),
                  pl.BlockSpec((None, length, 2 * MEM_W), lambda bi, i: (bi, 0, 0))],
        out_specs=pl.BlockSpec((None, tq, MEM_W), lambda bi, i: (bi, i, 0)),
        out_shape=jax.ShapeDtypeStruct((b, s, MEM_W), BF16),
        compiler_params=_cparams(("parallel", "parallel"), vmem),
        name="memory_attention",
    )(q_arr, mem_kv)


def _t5_bucket_table():
    max_exact = NUM_BUCKETS // 2
    kj = jnp.arange(2 * WINDOW)[:, None]
    qi = jnp.arange(WINDOW)[None, :]
    signed = WINDOW + qi - kj
    dist = jnp.maximum(signed, 0)
    d = jnp.maximum(dist, 1).astype(F32)
    large = max_exact + (jnp.log(d / max_exact) / math.log(MAX_DISTANCE / max_exact)
                         * (NUM_BUCKETS - max_exact)).astype(jnp.int32)
    bucket = jnp.where(dist < max_exact, dist, jnp.minimum(large, NUM_BUCKETS - 1))
    return jnp.where((signed >= 0) & (signed < WINDOW), bucket, -1).astype(jnp.int32)


def _t5_bias_kernel(rb_ref, bucket_ref, o_ref):
    kvh = pl.program_id(0)
    bucket = bucket_ref[...]
    for g in range(SWA_GROUP):
        bias = jnp.full(bucket.shape, -jnp.inf, F32)
        for b in range(NUM_BUCKETS):
            bias = jnp.where(bucket == b, rb_ref[b, kvh * SWA_GROUP + g] * LOG2E, bias)
        o_ref[:, g * WINDOW:(g + 1) * WINDOW] = bias


def _t5_bias(rel_bias):
    return pl.pallas_call(
        _t5_bias_kernel,
        grid=(N_SWA_KV_HEADS,),
        in_specs=[pl.BlockSpec(memory_space=pltpu.SMEM),
                  pl.BlockSpec((2 * WINDOW, WINDOW), lambda h: (0, 0))],
        out_specs=pl.BlockSpec((None, 2 * WINDOW, SWA_GROUP * WINDOW), lambda h: (h, 0, 0)),
        out_shape=jax.ShapeDtypeStruct((N_SWA_KV_HEADS, 2 * WINDOW, SWA_GROUP * WINDOW), F32),
        compiler_params=_cparams(("parallel",), 16 * 2**20),
        name="t5_bias",
    )(rel_bias, _t5_bucket_table())


def _swa_kernel(sink_ref, q_ref, kvc_ref, kvp_ref, bias_ref, o_ref):
    n = pl.program_id(1)
    d = SWA_HEAD_DIM
    kv_w = N_SWA_KV_HEADS * d
    k_win = jnp.concatenate([kvp_ref[:, 0:kv_w], kvc_ref[:, 0:kv_w]], axis=0)
    v_win = jnp.concatenate([kvp_ref[:, kv_w:2 * kv_w], kvc_ref[:, kv_w:2 * kv_w]], axis=0)
    v_t = v_win.astype(F32).T.astype(BF16)
    no_prev = jnp.where(n == 0, -jnp.inf, 0.0)
    outs = []
    for kvh in range(N_SWA_KV_HEADS):
        heads = range(kvh * SWA_GROUP, (kvh + 1) * SWA_GROUP)
        q = jnp.concatenate([q_ref[:, h * d:(h + 1) * d] for h in heads], axis=0)
        sink = jnp.concatenate([jnp.full((1, WINDOW), sink_ref[h] * LOG2E, F32) for h in heads], axis=1)
        s = _dot_nt(k_win[:, kvh * d:(kvh + 1) * d], q) + bias_ref[kvh]
        s = jnp.concatenate([s[0:WINDOW, :] + no_prev, s[WINDOW:, :]], axis=0)
        m = jnp.maximum(jnp.max(s, axis=0, keepdims=True), sink)
        p = jnp.exp2(s - m)
        denom = jnp.sum(p, axis=0, keepdims=True) + jnp.exp2(sink - m)
        o_t = _dot(v_t[kvh * d:(kvh + 1) * d, :], p.astype(BF16)) / denom
        outs += [o_t[:, g * WINDOW:(g + 1) * WINDOW] for g in range(SWA_GROUP)]
    o_ref[...] = jnp.concatenate(outs, axis=0).T.astype(o_ref.dtype)


def _swa_attention(proj, sinks, bias):
    b, s, _ = proj.shape
    q_w = N_SWA_HEADS * SWA_HEAD_DIM
    kv_w = 2 * N_SWA_KV_HEADS * SWA_HEAD_DIM
    kv_block = q_w // kv_w
    vmem = (2 * (2 * _nbytes((WINDOW, q_w), BF16) + 2 * _nbytes((WINDOW, kv_w), BF16)
                 + _nbytes(bias.shape, F32)) + 16 * 2**20)
    return pl.pallas_call(
        _swa_kernel,
        grid=(b, s // WINDOW),
        in_specs=[pl.BlockSpec(memory_space=pltpu.SMEM),
                  pl.BlockSpec((None, WINDOW, q_w), lambda bi, n: (bi, n, 0)),
                  pl.BlockSpec((None, WINDOW, kv_w), lambda bi, n: (bi, n, kv_block)),
                  pl.BlockSpec((None, WINDOW, kv_w), lambda bi, n: (bi, jnp.maximum(n - 1, 0), kv_block)),
                  pl.BlockSpec(bias.shape, lambda bi, n: (0, 0, 0))],
        out_specs=pl.BlockSpec((None, WINDOW, q_w), lambda bi, n: (bi, n, 0)),
        out_shape=jax.ShapeDtypeStruct((b, s, q_w), BF16),
        compiler_params=_cparams(("parallel", "arbitrary"), vmem),
        name="sliding_window_attention",
    )(sinks, proj, proj, proj, bias)


def _rope_table_kernel(pos_ref, invf_ref, o_ref):
    ang = pos_ref[...].astype(F32) * invf_ref[...]
    lane = lax.broadcasted_iota(jnp.int32, ang.shape, 1)
    half = QK_ROPE // 2
    sin_signed = jnp.where(lane < QK_ROPE + half, -jnp.sin(ang), jnp.sin(ang))
    o_ref[...] = jnp.where(lane < QK_ROPE, jnp.cos(ang), sin_signed)


def _rope_table(positions, tm=1024):
    m = positions.shape[0]
    half = QK_ROPE // 2
    inv_freq = ROPE_THETA ** (-jnp.arange(half, dtype=F32) / half)
    invf = jnp.tile(inv_freq, LANES // half).reshape(1, LANES)
    return pl.pallas_call(
        _rope_table_kernel,
        grid=(m // tm,),
        in_specs=[pl.BlockSpec((tm, 1), lambda i: (i, 0)), pl.BlockSpec((1, LANES), lambda i: (0, 0))],
        out_specs=pl.BlockSpec((tm, LANES), lambda i: (i, 0)),
        out_shape=jax.ShapeDtypeStruct((m, LANES), F32),
        compiler_params=_cparams(("parallel",), 32 * 2**20),
        name="rope_table",
    )(positions, invf)


def _apply_rope(x_and_partner, cs):
    z = x_and_partner * cs
    z = z + pltpu.roll(z, QK_ROPE, axis=1)
    lane = lax.broadcasted_iota(jnp.int32, z.shape, 1)
    return jnp.where(lane < QK_ROPE, z, 0.0)


def _mla_q_kernel(cq_ref, g_ref, w_ref, cs_ref, o_ref, cn_ref):
    @pl.when(pl.program_id(1) == 0)
    def _():
        cn_ref[...] = _rms(cq_ref[...], g_ref[...]).astype(BF16)

    y = _dot(cn_ref[...], w_ref[...]) * ((QK_NOPE + QK_ROPE) ** -0.5 * LOG2E)
    cs = cs_ref[...]
    parts = []
    for h in range(y.shape[1] // (2 * LANES)):
        yh = y[:, h * 2 * LANES:(h + 1) * 2 * LANES]
        parts += [yh[:, :QK_NOPE], _apply_rope(yh[:, QK_NOPE:], cs)]
    o_ref[...] = jnp.concatenate(parts, axis=1).astype(o_ref.dtype)


def _mla_q(proj, q_norm, w_uq_aug, cs, tm=1024, heads_per_step=4):
    m = proj.shape[0]
    hw = 2 * LANES * heads_per_step
    n_heads = w_uq_aug.shape[1] // hw
    return pl.pallas_call(
        _mla_q_kernel,
        grid=(m // tm, n_heads),
        in_specs=[pl.BlockSpec((tm, Q_LORA), lambda i, h: (i, 1)),
                  pl.BlockSpec((1, Q_LORA), lambda i, h: (0, 0)),
                  pl.BlockSpec((Q_LORA, hw), lambda i, h: (0, h)),
                  pl.BlockSpec((tm, LANES), lambda i, h: (i, 0))],
        out_specs=pl.BlockSpec((tm, hw), lambda i, h: (i, h)),
        out_shape=jax.ShapeDtypeStruct((m, n_heads * hw), BF16),
        scratch_shapes=[pltpu.VMEM((tm, Q_LORA), BF16)],
        compiler_params=_cparams(("parallel", "arbitrary"), 32 * 2**20),
        name="mla_q_expand",
    )(proj, q_norm.reshape(1, Q_LORA), w_uq_aug, cs)


def _mla_kv_kernel(ckv_ref, g_ref, w_ref, kr_ref, cs_ref, k_ref, v_ref, cn_ref, kpe_ref):
    @pl.when(pl.program_id(1) == 0)
    def _():
        cn_ref[...] = _rms(ckv_ref[...], g_ref[...]).astype(BF16)
        kpe_ref[...] = _apply_rope(kr_ref[...], cs_ref[...])

    y = _dot(cn_ref[...], w_ref[...])
    kpe = kpe_ref[...]
    k_parts, v_parts = [], []
    for h in range(y.shape[1] // (2 * LANES)):
        yh = y[:, h * 2 * LANES:(h + 1) * 2 * LANES]
        k_parts += [yh[:, :QK_NOPE], kpe]
        v_parts.append(yh[:, QK_NOPE:])
    k_ref[...] = jnp.concatenate(k_parts, axis=1).astype(k_ref.dtype)
    v_ref[...] = jnp.concatenate(v_parts, axis=1).astype(v_ref.dtype)


def _mla_kv(proj, kv_norm, w_ukv, cs, tm=1024, heads_per_step=4):
    m = proj.shape[0]
    hw = 2 * LANES * heads_per_step
    n_heads = w_ukv.shape[1] // hw
    return pl.pallas_call(
        _mla_kv_kernel,
        grid=(m // tm, n_heads),
        in_specs=[pl.BlockSpec((tm, KV_LORA), lambda i, h: (i, 4)),
                  pl.BlockSpec((1, KV_LORA), lambda i, h: (0, 0)),
                  pl.BlockSpec((KV_LORA, hw), lambda i, h: (0, h)),
                  pl.BlockSpec((tm, LANES), lambda i, h: (i, 10)),
                  pl.BlockSpec((tm, LANES), lambda i, h: (i, 0))],
        out_specs=[pl.BlockSpec((tm, hw), lambda i, h: (i, h)),
                   pl.BlockSpec((tm, hw // 2), lambda i, h: (i, h))],
        out_shape=[jax.ShapeDtypeStruct((m, n_heads * hw), BF16),
                   jax.ShapeDtypeStruct((m, n_heads * hw // 2), BF16)],
        scratch_shapes=[pltpu.VMEM((tm, KV_LORA), BF16), pltpu.VMEM((tm, LANES), F32)],
        compiler_params=_cparams(("parallel", "arbitrary"), 32 * 2**20),
        name="mla_kv_expand",
    )(proj, kv_norm.reshape(1, KV_LORA), w_ukv, proj, cs)


def _swap_halves(w):
    half = w.shape[-1] // 2
    return jnp.concatenate([w[..., half:], w[..., :half]], axis=-1)


def _mixer_out(x2, o, o_mem, w_out, b, s):
    return _matmul_res([o.reshape(b * s, o.shape[-1]), o_mem.reshape(b * s, MEM_W)],
                       w_out[0], w_out[1], x2, tm=1024, tn=512)


def _sb_layer(x2, mem_kv, g, w_in, w_out, b, s):
    proj = _norm_matmul(x2, g, w_in.astype(BF16), BF16, tm=IN_PROJ_ROWS, tn=512, scaled_cols=N_HEADS * HEAD_DIM,
                        col_scale=HEAD_DIM ** -0.5 * LOG2E).reshape(b, s, -1)
    o = _sb_attention(proj, N_HEADS)
    o_mem = _mem_attention(proj, 3 * N_HEADS * HEAD_DIM // MEM_W, mem_kv)
    return _mixer_out(x2, o, o_mem, w_out, b, s)


def _fox_layer(x2, mem_kv, g, w_in, b_f, w_out, b, s):
    qkv_w = 3 * N_HEADS * HEAD_DIM
    w_main = jnp.concatenate([w_in[:, :qkv_w], w_in[:, qkv_w + N_HEADS:]], axis=1).astype(BF16)
    w_gate = jnp.pad(w_in[:, qkv_w:qkv_w + N_HEADS], ((0, 0), (0, LANES - N_HEADS))).astype(BF16)
    proj = _norm_matmul(x2, g, w_main, BF16, tm=IN_PROJ_ROWS, tn=512, scaled_cols=N_HEADS * HEAD_DIM,
                        col_scale=HEAD_DIM ** -0.5 * LOG2E).reshape(b, s, -1)
    f_logit = _norm_matmul(x2, g, w_gate, F32, tm=IN_PROJ_ROWS, tn=LANES).reshape(b, s, LANES)
    aug = _fox_gates(f_logit, jnp.pad(b_f, (0, LANES - N_HEADS)), N_HEADS)
    o = _flash_attention(proj, proj, proj, 0, N_HEADS, 2 * N_HEADS, HEAD_DIM, HEAD_DIM, N_HEADS, aug=aug)
    o_mem = _mem_attention(proj, qkv_w // MEM_W, mem_kv)
    return _mixer_out(x2, o, o_mem, w_out, b, s)


def _swa_layer(x2, mem_kv, g, rel_bias, w_in, sinks, w_out, b, s):
    proj = _norm_matmul(x2, g, w_in.astype(BF16), BF16, tm=IN_PROJ_ROWS, tn=512,
                        scaled_cols=N_SWA_HEADS * SWA_HEAD_DIM,
                        col_scale=SWA_HEAD_DIM ** -0.5 * LOG2E).reshape(b, s, -1)
    o = _swa_attention(proj, sinks, _t5_bias(rel_bias))
    q_w = N_SWA_HEADS * SWA_HEAD_DIM
    kv_w = 2 * N_SWA_KV_HEADS * SWA_HEAD_DIM
    o_mem = _mem_attention(proj, (q_w + kv_w) // MEM_W, mem_kv)
    return _mixer_out(x2, o, o_mem, w_out, b, s)


def _mla_layer(x2, mem_kv, g, positions, w_in, q_norm, w_uq, kv_norm, w_ukv, w_out, b, s):
    o2, o3 = Q_LORA + KV_LORA, Q_LORA + KV_LORA + QK_ROPE
    w_main = jnp.concatenate([w_in[:, o3:], w_in[:, :o3], _swap_halves(w_in[:, o2:o3])], axis=1).astype(BF16)
    proj = _norm_matmul(x2, g, w_main, F32, tm=512, tn=w_main.shape[1])
    cs = _rope_table(positions.reshape(b * s, 1))
    wq = w_uq.reshape(Q_LORA, N_HEADS, QK_NOPE + QK_ROPE)
    wq = jnp.concatenate([wq, _swap_halves(wq[:, :, QK_NOPE:])], axis=-1).reshape(Q_LORA, -1).astype(BF16)
    q = _mla_q(proj, q_norm, wq, cs).reshape(b, s, -1)
    k, v = _mla_kv(proj, kv_norm, w_ukv.astype(BF16), cs)
    k, v = k.reshape(b, s, -1), v.reshape(b, s, -1)
    o = _flash_attention(q, k, v, 0, 0, 0, 2 * LANES, HEAD_DIM, N_HEADS)
    o_mem = _mem_attention(proj.reshape(b, s, -1), 0, mem_kv)
    return _mixer_out(x2, o, o_mem, w_out, b, s)


def kernel(x, mem, positions, rel_bias, attn_norm, mem_norm, w_mem_kv, ffn_norm, ffn_w_up, ffn_conv_w, ffn_conv_b, ffn_w_down, final_norm, sb_w_in, sb_w_out, fox_w_in, fox_b_f, fox_w_out, swa_w_in, swa_sinks, swa_w_out, mla_w_in, mla_q_norm, mla_w_uq, mla_kv_norm, mla_w_ukv, mla_w_out):
    b, s, d = x.shape
    depth = attn_norm.shape[0]
    mem_len = mem.shape[1]
    x2 = x.reshape(b * s, d)
    mem2 = mem.reshape(b * mem_len, d)
    w_mem_kv_bf16 = w_mem_kv.astype(BF16)
    for i in range(depth):
        kind, j = i % 4, i // 4
        mem_kv = _norm_matmul(mem2, mem_norm[i], w_mem_kv_bf16, BF16, tm=b * mem_len, tn=2 * MEM_W,
                              layer=i).reshape(b, mem_len, 2 * MEM_W)
        g = attn_norm[i]
        if kind == 0:
            x2 = _sb_layer(x2, mem_kv, g, sb_w_in[j], (sb_w_out, j), b, s)
        elif kind == 1:
            x2 = _fox_layer(x2, mem_kv, g, fox_w_in[j], fox_b_f[j], (fox_w_out, j), b, s)
        elif kind == 2:
            x2 = _swa_layer(x2, mem_kv, g, rel_bias, swa_w_in[j], swa_sinks[j], (swa_w_out, j), b, s)
        else:
            x2 = _mla_layer(x2, mem_kv, g, positions, mla_w_in[j], mla_q_norm[j], mla_w_uq[j],
                            mla_kv_norm[j], mla_w_ukv[j], (mla_w_out, j), b, s)
        gated = _ffn_up(_rmsnorm(x2, ffn_norm[i], BF16, tm=512), ffn_w_up, i, ffn_conv_w[i], ffn_conv_b[i],
                        seq=s, tm=1024, tn=512)
        x2 = _matmul_res([gated], ffn_w_down, i, x2, tm=512, tn=512)
    return _rmsnorm(x2, final_norm, F32, tm=512).reshape(b, s, d)
```

```python
import functools
import math

import jax
import jax.numpy as jnp
from jax import lax
from jax.experimental import pallas as pl
from jax.experimental.pallas import tpu as pltpu

F32 = jnp.float32
BF16 = jnp.bfloat16

EPS = 1e-6
LOG2E = math.log2(math.e)
HEAD_DIM = 128
N_HEADS = 16
SWA_HEAD_DIM = 64
N_SWA_HEADS = 32
N_SWA_KV_HEADS = 4
SWA_GROUP = N_SWA_HEADS // N_SWA_KV_HEADS
WINDOW = 128
Q_LORA = 512
KV_LORA = 256
QK_NOPE = 128
QK_ROPE = 64
ROPE_THETA = 10000.0
MEM_HEADS = 4
MEM_W = MEM_HEADS * HEAD_DIM
NUM_BUCKETS = 32
MAX_DISTANCE = 128
CONV_WIDTH = 3

V7X_VMEM_BYTES = 64 * 1024 * 1024
VMEM_CAP = V7X_VMEM_BYTES - 8 * 1024 * 1024
LANES = 128
IN_PROJ_ROWS = 1024
TAIL = 8


def _cparams(sems, vmem_bytes):
    return pltpu.CompilerParams(dimension_semantics=sems,
                                vmem_limit_bytes=int(min(max(vmem_bytes, 16 * 2**20), VMEM_CAP)))


def _nbytes(shape, dtype):
    return math.prod(shape) * jnp.dtype(dtype).itemsize


def _rms(x, g):
    return x * lax.rsqrt(jnp.mean(x * x, axis=-1, keepdims=True) + EPS) * g


def _dot(a, b):
    return jnp.dot(a, b, preferred_element_type=F32)


def _dot_nt(a, b):
    return lax.dot_general(a, b, (((1,), (1,)), ((), ())), preferred_element_type=F32)


def _norm_matmul_kernel(x_ref, g_ref, w_ref, cs_ref, o_ref, xn_ref):
    @pl.when(pl.program_id(1) == 0)
    def _():
        xn_ref[...] = _rms(x_ref[...], g_ref[...]).astype(BF16)

    o_ref[...] = (_dot(xn_ref[...], w_ref[...]) * cs_ref[...]).astype(o_ref.dtype)


def _w_tile_spec(w, layer, k, tn, col_block):
    if w.ndim == 3:
        return pl.BlockSpec((None, k, tn), lambda i, j: (layer, 0, col_block(j)))
    return pl.BlockSpec((k, tn), lambda i, j: (0, col_block(j)))


def _norm_matmul(x, g, w, out_dtype, tm, tn, scaled_cols=0, col_scale=1.0, layer=None):
    m, k = x.shape
    n = w.shape[-1]
    cs = jnp.where(jnp.arange(n) < scaled_cols, col_scale, 1.0).astype(F32).reshape(1, n)
    vmem = (2 * (_nbytes((tm, k), F32) + _nbytes((k, tn), BF16) + _nbytes((tm, tn), out_dtype))
            + _nbytes((tm, k), BF16) + 2 * _nbytes((tm, k), F32) + _nbytes((tm, tn), F32))
    return pl.pallas_call(
        _norm_matmul_kernel,
        grid=(m // tm, n // tn),
        in_specs=[pl.BlockSpec((tm, k), lambda i, j: (i, 0)),
                  pl.BlockSpec((1, k), lambda i, j: (0, 0)),
                  _w_tile_spec(w, layer, k, tn, lambda j: j),
                  pl.BlockSpec((1, tn), lambda i, j: (0, j))],
        out_specs=pl.BlockSpec((tm, tn), lambda i, j: (i, j)),
        out_shape=jax.ShapeDtypeStruct((m, n), out_dtype),
        scratch_shapes=[pltpu.VMEM((tm, k), BF16)],
        compiler_params=_cparams(("parallel", "arbitrary"), vmem),
        name="norm_matmul",
    )(x, g.reshape(1, k), w, cs)


def _matmul_res_kernel(*refs, n_a):
    a_refs, w_refs = refs[:n_a], refs[n_a:2 * n_a]
    x_ref, o_ref = refs[2 * n_a], refs[2 * n_a + 1]
    wb_refs = refs[2 * n_a + 2:]

    @pl.when(pl.program_id(1) == 0)
    def _():
        for w_ref, wb_ref in zip(w_refs, wb_refs):
            wb_ref[...] = w_ref[...].astype(BF16)

    acc = x_ref[...]
    for a_ref, wb_ref in zip(a_refs, wb_refs):
        acc = acc + _dot(a_ref[...], wb_ref[...])
    o_ref[...] = acc


def _matmul_res(a_list, w, layer, x, tm, tn):
    m, n = x.shape
    vmem = 6 * _nbytes((tm, tn), F32)
    a_specs, w_specs, scratch = [], [], []
    row = 0
    for a in a_list:
        kp = a.shape[1]
        row_block = row // kp
        assert row_block * kp == row
        a_specs.append(pl.BlockSpec((tm, kp), lambda j, i: (i, 0)))
        w_specs.append(pl.BlockSpec((None, kp, tn), lambda j, i, row_block=row_block: (layer, row_block, j),
                                    pipeline_mode=pl.Buffered(1)))
        scratch.append(pltpu.VMEM((kp, tn), BF16))
        vmem += 2 * _nbytes((tm, kp), a.dtype) + 2 * _nbytes((kp, tn), F32) + 2 * _nbytes((kp, tn), BF16)
        row += kp
    assert row == w.shape[1]
    return pl.pallas_call(
        functools.partial(_matmul_res_kernel, n_a=len(a_list)),
        grid=(n // tn, m // tm),
        in_specs=a_specs + w_specs + [pl.BlockSpec((tm, tn), lambda j, i: (i, j))],
        out_specs=pl.BlockSpec((tm, tn), lambda j, i: (i, j)),
        out_shape=jax.ShapeDtypeStruct((m, n), F32),
        scratch_shapes=scratch,
        compiler_params=_cparams(("parallel", "arbitrary"), vmem),
        name="matmul_res",
    )(*a_list, *([w] * len(a_list)), x)


def _rmsnorm_kernel(x_ref, g_ref, o_ref):
    o_ref[...] = _rms(x_ref[...], g_ref[...]).astype(o_ref.dtype)


def _rmsnorm(x, g, out_dtype, tm):
    m, k = x.shape
    return pl.pallas_call(
        _rmsnorm_kernel,
        grid=(m // tm,),
        in_specs=[pl.BlockSpec((tm, k), lambda i: (i, 0)), pl.BlockSpec((1, k), lambda i: (0, 0))],
        out_specs=pl.BlockSpec((tm, k), lambda i: (i, 0)),
        out_shape=jax.ShapeDtypeStruct((m, k), out_dtype),
        compiler_params=_cparams(("parallel",), 8 * _nbytes((tm, k), F32)),
        name="rmsnorm",
    )(x, g.reshape(1, k))


def _ffn_up_kernel(xn_ref, wg_ref, wv_ref, cwg_ref, cwv_ref, cbg_ref, cbv_ref, o_ref,
                   wgb_ref, wvb_ref, tail_g_ref, tail_v_ref, *u_refs, rows, tiles_per_seq):
    i = pl.program_id(1)

    @pl.when(i == 0)
    def _():
        wgb_ref[...] = wg_ref[...].astype(BF16)
        wvb_ref[...] = wv_ref[...].astype(BF16)

    @pl.when(i % tiles_per_seq == 0)
    def _():
        tail_g_ref[...] = jnp.zeros(tail_g_ref.shape, F32)
        tail_v_ref[...] = jnp.zeros(tail_v_ref.shape, F32)

    n_chunks = len(u_refs) // 2

    for r in range(n_chunks):
        xr = xn_ref[r * rows:(r + 1) * rows, :]

        def conv(wb_ref, cw_ref, cb_ref, u_ref, before):
            u_ref[TAIL:, :] = _dot(xr, wb_ref[...])
            u_ref[0:TAIL, :] = before
            c = cb_ref[...]
            for tap in range(CONV_WIDTH):
                start = TAIL - (CONV_WIDTH - 1) + tap
                c = c + cw_ref[tap:tap + 1, :] * u_ref[start:start + rows, :]
            return c

        if r == 0:
            before_g, before_v = tail_g_ref[...], tail_v_ref[...]
        else:
            before_g, before_v = u_refs[2 * r - 2][rows:, :], u_refs[2 * r - 1][rows:, :]
        gate = conv(wgb_ref, cwg_ref, cbg_ref, u_refs[2 * r], before_g)
        val = conv(wvb_ref, cwv_ref, cbv_ref, u_refs[2 * r + 1], before_v)
        o_ref[r * rows:(r + 1) * rows, :] = (gate * (1.0 / (1.0 + jnp.exp(-gate))) * val).astype(o_ref.dtype)

    tail_g_ref[...] = u_refs[2 * n_chunks - 2][rows:, :]
    tail_v_ref[...] = u_refs[2 * n_chunks - 1][rows:, :]


def _ffn_up(xn, w_up, layer, conv_w, conv_b, seq, tm, tn, rows=256):
    m, k = xn.shape
    d_ff = w_up.shape[-1] // 2
    nj = d_ff // tn
    n_chunks = tm // rows
    vmem = (2 * (_nbytes((tm, k), BF16) + 2 * _nbytes((k, tn), F32) + _nbytes((tm, tn), BF16))
            + 2 * _nbytes((k, tn), BF16) + 2 * n_chunks * _nbytes((rows + TAIL, tn), F32)
            + 2 * _nbytes((k, tn), F32) + 6 * n_chunks * _nbytes((rows, tn), F32))
    return pl.pallas_call(
        functools.partial(_ffn_up_kernel, rows=rows, tiles_per_seq=seq // tm),
        grid=(nj, m // tm),
        in_specs=[pl.BlockSpec((tm, k), lambda j, i: (i, 0)),
                  pl.BlockSpec((None, k, tn), lambda j, i: (layer, 0, j)),
                  pl.BlockSpec((None, k, tn), lambda j, i: (layer, 0, nj + j)),
                  pl.BlockSpec((CONV_WIDTH, tn), lambda j, i: (0, j)),
                  pl.BlockSpec((CONV_WIDTH, tn), lambda j, i: (0, nj + j)),
                  pl.BlockSpec((1, tn), lambda j, i: (0, j)),
                  pl.BlockSpec((1, tn), lambda j, i: (0, nj + j))],
        out_specs=pl.BlockSpec((tm, tn), lambda j, i: (i, j)),
        out_shape=jax.ShapeDtypeStruct((m, d_ff), BF16),
        scratch_shapes=[pltpu.VMEM((k, tn), BF16), pltpu.VMEM((k, tn), BF16),
                        pltpu.VMEM((TAIL, tn), F32), pltpu.VMEM((TAIL, tn), F32)]
        + [pltpu.VMEM((rows + TAIL, tn), F32)] * (2 * n_chunks),
        compiler_params=_cparams(("arbitrary", "arbitrary"), vmem),
        name="ffn_up_conv_gate",
    )(xn, w_up, w_up, conv_w, conv_w, conv_b.reshape(1, -1), conv_b.reshape(1, -1))


def _eye(n):
    return jnp.where(lax.broadcasted_iota(jnp.int32, (n, n), 0) == lax.broadcasted_iota(jnp.int32, (n, n), 1),
                     1.0, 0.0).astype(BF16)


def _transpose_bf16(x):
    return _dot_nt(_eye(x.shape[1]), x)


def _fill_v_transposed(v_ref, vt_ref, t):
    s, dv = v_ref.shape
    for c in range(s // t):
        vt_ref[0:dv, c * t:(c + 1) * t] = _transpose_bf16(v_ref[c * t:(c + 1) * t, :]).astype(BF16)
    if vt_ref.shape[0] > dv:
        vt_ref[dv:, :] = jnp.ones((vt_ref.shape[0] - dv, s), BF16)


def _flash_kernel(*refs, t, dk, dv, group, has_aug):
    if has_aug:
        q_ref, k_ref, v_ref, qa_ref, ka_ref, o_ref, vt_ref, m_ref, acc_ref, s_ref = refs
    else:
        q_ref, k_ref, v_ref, o_ref, vt_ref, m_ref, acc_ref, s_ref = refs
    qi = pl.program_id(2)
    heads = range(group)

    @pl.when(qi == 0)
    def _():
        for g in heads:
            _fill_v_transposed(v_ref.at[:, g * dv:(g + 1) * dv], vt_ref.at[g], t)

    qs = []
    for g in heads:
        q = q_ref[:, g * dk:(g + 1) * dk]
        qs.append(jnp.concatenate([q, qa_ref[g]], axis=1) if has_aug else q)
    m_ref[...] = jnp.full(m_ref.shape, -jnp.inf, F32)
    acc_ref[...] = jnp.zeros(acc_ref.shape, F32)

    def scores(g, kb):
        ks = pl.multiple_of(kb * t, t)
        k = k_ref[pl.ds(ks, t), g * dk:(g + 1) * dk]
        if has_aug:
            k = jnp.concatenate([k, ka_ref[g, pl.ds(ks, t), :]], axis=1)
        return _dot_nt(k, qs[g])

    def col_max(s):
        return jnp.max(s, axis=0, keepdims=True)

    def update(g, s, s_max, kb):
        m_prev = m_ref[g]
        m_new = jnp.maximum(m_prev, s_max)
        m_ref[g] = m_new
        p = jnp.exp2(s - m_new).astype(BF16)
        ks = pl.multiple_of(kb * t, t)
        acc_ref[g] = jnp.exp2(m_prev - m_new) * acc_ref[g] + _dot(vt_ref[g, :, pl.ds(ks, t)], p)

    def body(kb, s_max):
        nxt = []
        for g in heads:
            update(g, s_ref[g], s_max[g], kb)
            s_next = scores(g, kb + 1)
            nxt.append(col_max(s_next))
            s_ref[g] = s_next
        return tuple(nxt)

    first = []
    for g in heads:
        s0 = scores(g, 0)
        first.append(col_max(s0))
        s_ref[g] = s0
    lax.fori_loop(0, qi, body, tuple(first))
    key = lax.broadcasted_iota(jnp.int32, (t, t), 0)
    qry = lax.broadcasted_iota(jnp.int32, (t, t), 1)
    outs = []
    for g in heads:
        s = jnp.where(key <= qry, s_ref[g], -jnp.inf)
        update(g, s, col_max(s), qi)
        acc = acc_ref[g]
        outs.append((acc[0:dv, :] / acc[dv:dv + 1, :]).T)
    o_ref[...] = jnp.concatenate(outs, axis=1).astype(o_ref.dtype)


def _flash_attention(q_arr, k_arr, v_arr, q_off, k_off, v_off, dk, dv, n_heads, aug=None, t=512, group=4):
    b, s, _ = q_arr.shape
    assert q_off % group == 0 and k_off % group == 0 and v_off % group == 0 and n_heads % group == 0
    qo, ko, vo = q_off // group, k_off // group, v_off // group
    in_specs = [pl.BlockSpec((None, t, group * dk), lambda bi, h, qi: (bi, qi, qo + h)),
                pl.BlockSpec((None, s, group * dk), lambda bi, h, qi: (bi, 0, ko + h)),
                pl.BlockSpec((None, s, group * dv), lambda bi, h, qi: (bi, 0, vo + h))]
    args = [q_arr, k_arr, v_arr]
    vmem = 2 * group * (_nbytes((t, dk), BF16) + _nbytes((s, dk), BF16) + _nbytes((s, dv), BF16)
                        + _nbytes((t, dv), BF16))
    if aug is not None:
        in_specs += [pl.BlockSpec((None, group, t, LANES), lambda bi, h, qi: (bi, h, qi, 0)),
                     pl.BlockSpec((None, group, s, LANES), lambda bi, h, qi: (bi, h, 0, 0))]
        args += list(aug)
        vmem += 2 * group * (_nbytes((t, LANES), BF16) + _nbytes((s, LANES), BF16))
    acc_rows = dv + 16
    vmem += group * (_nbytes((acc_rows, s), BF16) + 2 * _nbytes((acc_rows, t), F32) + 8 * _nbytes((t, t), F32))
    return pl.pallas_call(
        functools.partial(_flash_kernel, t=t, dk=dk, dv=dv, group=group, has_aug=aug is not None),
        grid=(b, n_heads // group, s // t),
        in_specs=in_specs,
        out_specs=pl.BlockSpec((None, t, group * dv), lambda bi, h, qi: (bi, qi, h)),
        out_shape=jax.ShapeDtypeStruct((b, s, n_heads * dv), BF16),
        scratch_shapes=[pltpu.VMEM((group, acc_rows, s), BF16), pltpu.VMEM((group, 1, t), F32),
                        pltpu.VMEM((group, acc_rows, t), F32), pltpu.VMEM((group, t, t), F32)],
        compiler_params=_cparams(("parallel", "parallel", "arbitrary"), vmem),
        name="flash_attention",
    )(*args)


def _sb_kernel(q_ref, k_ref, v_ref, o_ref, vt_ref, carry_ref, acc_ref, s_ref, a_ref, *, t, sub, d, group):
    qi = pl.program_id(2)
    heads = range(group)

    @pl.when(qi == 0)
    def _():
        for g in heads:
            _fill_v_transposed(v_ref.at[:, g * d:(g + 1) * d], vt_ref.at[g], t)

    carry_ref[...] = jnp.zeros(carry_ref.shape, F32)
    acc_ref[...] = jnp.zeros(acc_ref.shape, F32)
    qs = [q_ref[:, g * d:(g + 1) * d] for g in heads]
    r = lax.broadcasted_iota(jnp.int32, (sub + 16, sub), 0)
    c = lax.broadcasted_iota(jnp.int32, (sub + 16, sub), 1)
    suffix = jnp.where(((c > r) & (r < sub)) | (r == sub), 1.0, 0.0).astype(BF16)
    suffix = jnp.concatenate([suffix, suffix], axis=1)

    def scores(g, kb):
        ks = pl.multiple_of(kb * t, t)
        return _dot_nt(k_ref[pl.ds(ks, t), g * d:(g + 1) * d], qs[g])

    def weights(g, on_diagonal):
        later = carry_ref[g]
        for i in reversed(range(t // sub)):
            q0 = i * sub if on_diagonal else 0
            rows = slice(i * sub, (i + 1) * sub)
            z = s_ref[g, rows, q0:]
            neg_abs = lax.bitcast_convert_type(
                lax.bitcast_convert_type(z, jnp.uint32) | jnp.uint32(0x80000000), F32)
            log_beta = jnp.minimum(z, 0.0) - jnp.log2(1.0 + jnp.exp2(neg_abs))
            log_keep = log_beta - z
            if on_diagonal:
                key = lax.broadcasted_iota(jnp.int32, z.shape, 0) + i * sub
                strict = key < lax.broadcasted_iota(jnp.int32, z.shape, 1) + q0
                log_keep = jnp.where(strict, log_keep, 0.0)
            hi = log_keep.astype(BF16)
            lo = (log_keep - hi.astype(F32)).astype(BF16)
            sums = _dot(suffix, jnp.concatenate([hi, lo], axis=0))
            seen = later[:, q0:]
            a = jnp.exp2(log_beta + (sums[0:sub, :] + seen[0:1, :]))
            if on_diagonal:
                a = jnp.where(strict, a, 0.0)
            a_ref[g, rows, q0:] = a.astype(BF16)
            total = seen + sums[sub:sub + 8, :]
            if q0:
                a_ref[g, rows, 0:q0] = jnp.zeros((sub, q0), BF16)
                total = jnp.concatenate([later[:, 0:q0], total], axis=1)
            later = total
        carry_ref[g] = later

    def accumulate(g, kb):
        ks = pl.multiple_of(kb * t, t)
        acc_ref[g] += _dot(vt_ref[g, :, pl.ds(ks, t)], a_ref[g])

    def finish():
        o_ref[...] = jnp.concatenate([acc_ref[g].T for g in heads], axis=1).astype(o_ref.dtype)

    @pl.when(qi == 0)
    def _():
        for g in heads:
            s_ref[g] = scores(g, 0)
            weights(g, True)
            accumulate(g, 0)
        finish()

    @pl.when(qi > 0)
    def _():
        for g in heads:
            s_ref[g] = scores(g, qi)
            weights(g, True)
            s_ref[g] = scores(g, qi - 1)

        def body(i, carry):
            for g in heads:
                accumulate(g, qi - i)
                weights(g, False)
                s_ref[g] = scores(g, qi - i - 2)
            return carry

        lax.fori_loop(0, qi - 1, body, 0)
        for g in heads:
            accumulate(g, 1)
            weights(g, False)
            accumulate(g, 0)
        finish()


def _sb_attention(proj, n_heads, t=512, group=4):
    b, s, _ = proj.shape
    d = HEAD_DIM
    assert n_heads % group == 0
    blocks = n_heads // group
    vmem = group * (2 * (2 * _nbytes((t, d), BF16) + 2 * _nbytes((s, d), BF16))
                    + _nbytes((d, s), BF16) + 2 * _nbytes((d, t), F32) + 12 * _nbytes((t, t), F32))
    return pl.pallas_call(
        functools.partial(_sb_kernel, t=t, sub=LANES, d=d, group=group),
        grid=(b, blocks, s // t),
        in_specs=[pl.BlockSpec((None, t, group * d), lambda bi, h, qi: (bi, qi, h)),
                  pl.BlockSpec((None, s, group * d), lambda bi, h, qi: (bi, 0, blocks + h)),
                  pl.BlockSpec((None, s, group * d), lambda bi, h, qi: (bi, 0, 2 * blocks + h))],
        out_specs=pl.BlockSpec((None, t, group * d), lambda bi, h, qi: (bi, qi, h)),
        out_shape=jax.ShapeDtypeStruct((b, s, n_heads * d), BF16),
        scratch_shapes=[pltpu.VMEM((group, d, s), BF16), pltpu.VMEM((group, 8, t), F32),
                        pltpu.VMEM((group, d, t), F32), pltpu.VMEM((group, t, t), F32),
                        pltpu.VMEM((group, t, t), BF16)],
        compiler_params=_cparams(("parallel", "parallel", "arbitrary"), vmem),
        name="stick_breaking_attention",
    )(proj, proj, proj)


def _split3(x):
    hi = x.astype(BF16)
    rest = x - hi.astype(F32)
    mid = rest.astype(BF16)
    lo = (rest - mid.astype(F32)).astype(BF16)
    return hi, mid, lo


def _fox_gate_kernel(fl_ref, b_ref, pq_ref, pk_ref, oq_ref, ok_ref, qa_ref, ka_ref, carry_ref):
    ts = fl_ref.shape[0]

    @pl.when(pl.program_id(1) == 0)
    def _():
        carry_ref[...] = jnp.zeros(carry_ref.shape, F32)

    z = fl_ref[...] + b_ref[...]
    log_f = jnp.minimum(z, 0.0) - jnp.log(1.0 + jnp.exp(-jnp.abs(z)))
    row = lax.broadcasted_iota(jnp.int32, (ts, ts), 0)
    col = lax.broadcasted_iota(jnp.int32, (ts, ts), 1)
    prefix = jnp.where(col <= row, 1.0, 0.0).astype(BF16)
    hi, mid, lo = _split3(log_f)
    c = _dot(prefix, hi) + _dot(prefix, mid) + _dot(prefix, lo) + carry_ref[0:1, :]
    carry_ref[0:1, :] = c[ts - 1:ts, :]
    parts = jnp.concatenate(_split3(c * LOG2E), axis=1)
    qa = (_dot(parts, pq_ref[...]) + oq_ref[...]).astype(BF16)
    ka = (_dot(parts, pk_ref[...]) + ok_ref[...]).astype(BF16)
    for h in range(qa_ref.shape[0]):
        qa_ref[h] = qa[:, h * LANES:(h + 1) * LANES]
        ka_ref[h] = ka[:, h * LANES:(h + 1) * LANES]


def _fox_aug_tables(n_heads):
    part, src = jnp.arange(3 * LANES) // LANES, jnp.arange(3 * LANES) % LANES
    head, lane = jnp.arange(n_heads * LANES) // LANES, jnp.arange(n_heads * LANES) % LANES
    mine = src[:, None] == head[None, :]
    pq = jnp.where(mine & (lane[None, :] == part[:, None] + 3), 1.0, 0.0).astype(BF16)
    pk = jnp.where(mine & (lane[None, :] == part[:, None]), -1.0, 0.0).astype(BF16)
    oq = jnp.where(lane < 3, 1.0, 0.0).astype(F32)[None]
    ok = jnp.where((lane >= 3) & (lane < 6), 1.0, 0.0).astype(F32)[None]
    return pq, pk, oq, ok


def _fox_gates(f_logit, b_f, n_heads, ts=256):
    b, s, w = f_logit.shape
    hw = n_heads * LANES
    out = jax.ShapeDtypeStruct((b, n_heads, s, LANES), BF16)
    const = lambda bi, i: (0, 0)
    return pl.pallas_call(
        _fox_gate_kernel,
        grid=(b, s // ts),
        in_specs=[pl.BlockSpec((None, ts, w), lambda bi, i: (bi, i, 0)),
                  pl.BlockSpec((1, w), const),
                  pl.BlockSpec((3 * LANES, hw), const), pl.BlockSpec((3 * LANES, hw), const),
                  pl.BlockSpec((1, hw), const), pl.BlockSpec((1, hw), const)],
        out_specs=[pl.BlockSpec((None, n_heads, ts, LANES), lambda bi, i: (bi, 0, i, 0)),
                   pl.BlockSpec((None, n_heads, ts, LANES), lambda bi, i: (bi, 0, i, 0))],
        out_shape=[out, out],
        scratch_shapes=[pltpu.VMEM((8, w), F32)],
        compiler_params=_cparams(("parallel", "arbitrary"), 32 * 2**20),
        name="fox_gate_cumsum",
    )(f_logit, b_f.reshape(1, w), *_fox_aug_tables(n_heads))


def _mem_attn_kernel(q_ref, kv_ref, o_ref, *, scale):
    outs = []
    for h in range(MEM_HEADS):
        q = q_ref[:, h * HEAD_DIM:(h + 1) * HEAD_DIM].astype(BF16)
        k = kv_ref[:, h * HEAD_DIM:(h + 1) * HEAD_DIM]
        v = kv_ref[:, MEM_W + h * HEAD_DIM:MEM_W + (h + 1) * HEAD_DIM]
        s = _dot_nt(q, k) * scale
        p = jnp.exp(s - jnp.max(s, axis=1, keepdims=True))
        o = _dot(p.astype(BF16), v) / jnp.sum(p, axis=1, keepdims=True)
        outs.append(o.astype(o_ref.dtype))
    o_ref[...] = jnp.concatenate(outs, axis=1)


def _mem_attention(q_arr, q_block, mem_kv, tq=512):
    b, s, _ = q_arr.shape
    length = mem_kv.shape[1]
    vmem = (2 * (_nbytes((tq, MEM_W), q_arr.dtype) + _nbytes((length, 2 * MEM_W), BF16)
                 + _nbytes((tq, MEM_W), BF16)) + 8 * _nbytes((tq, length), F32))
    return pl.pallas_call(
        functools.partial(_mem_attn_kernel, scale=HEAD_DIM ** -0.5),
        grid=(b, s // tq),
        in_specs=[pl.BlockSpec((None, tq, MEM_W), lambda bi, i: (bi, i, q_block)),
                  pl.BlockSpec((None, length, 2 * MEM_W), lambda bi, i: (bi, 0, 0))],
        out_specs=pl.BlockSpec((None, tq, MEM_W), lambda bi, i: (bi, i, 0)),
        out_shape=jax.ShapeDtypeStruct((b, s, MEM_W), BF16),
        compiler_params=_cparams(("parallel", "parallel"), vmem),
        name="memory_attention",
    )(q_arr, mem_kv)


def _t5_bucket_table():
    max_exact = NUM_BUCKETS // 2
    kj = jnp.arange(2 * WINDOW)[:, None]
    qi = jnp.arange(WINDOW)[None, :]
    signed = WINDOW + qi - kj
    dist = jnp.maximum(signed, 0)
    d = jnp.maximum(dist, 1).astype(F32)
    large = max_exact + (jnp.log(d / max_exact) / math.log(MAX_DISTANCE / max_exact)
                         * (NUM_BUCKETS - max_exact)).astype(jnp.int32)
    bucket = jnp.where(dist < max_exact, dist, jnp.minimum(large, NUM_BUCKETS - 1))
    return jnp.where((signed >= 0) & (signed < WINDOW), bucket, -1).astype(jnp.int32)


def _t5_bias_kernel(rb_ref, bucket_ref, o_ref):
    kvh = pl.program_id(0)
    bucket = bucket_ref[...]
    for g in range(SWA_GROUP):
        bias = jnp.full(bucket.shape, -jnp.inf, F32)
        for b in range(NUM_BUCKETS):
            bias = jnp.where(bucket == b, rb_ref[b, kvh * SWA_GROUP + g] * LOG2E, bias)
        o_ref[:, g * WINDOW:(g + 1) * WINDOW] = bias


def _t5_bias(rel_bias):
    return pl.pallas_call(
        _t5_bias_kernel,
        grid=(N_SWA_KV_HEADS,),
        in_specs=[pl.BlockSpec(memory_space=pltpu.SMEM),
                  pl.BlockSpec((2 * WINDOW, WINDOW), lambda h: (0, 0))],
        out_specs=pl.BlockSpec((None, 2 * WINDOW, SWA_GROUP * WINDOW), lambda h: (h, 0, 0)),
        out_shape=jax.ShapeDtypeStruct((N_SWA_KV_HEADS, 2 * WINDOW, SWA_GROUP * WINDOW), F32),
        compiler_params=_cparams(("parallel",), 16 * 2**20),
        name="t5_bias",
    )(rel_bias, _t5_bucket_table())


def _swa_kernel(sink_ref, q_ref, kvc_ref, kvp_ref, bias_ref, o_ref):
    n = pl.program_id(1)
    d = SWA_HEAD_DIM
    kv_w = N_SWA_KV_HEADS * d
    k_win = jnp.concatenate([kvp_ref[:, 0:kv_w], kvc_ref[:, 0:kv_w]], axis=0)
    v_win = jnp.concatenate([kvp_ref[:, kv_w:2 * kv_w], kvc_ref[:, kv_w:2 * kv_w]], axis=0)
    v_t = v_win.astype(F32).T.astype(BF16)
    no_prev = jnp.where(n == 0, -jnp.inf, 0.0)
    outs = []
    for kvh in range(N_SWA_KV_HEADS):
        heads = range(kvh * SWA_GROUP, (kvh + 1) * SWA_GROUP)
        q = jnp.concatenate([q_ref[:, h * d:(h + 1) * d] for h in heads], axis=0)
        sink = jnp.concatenate([jnp.full((1, WINDOW), sink_ref[h] * LOG2E, F32) for h in heads], axis=1)
        s = _dot_nt(k_win[:, kvh * d:(kvh + 1) * d], q) + bias_ref[kvh]
        s = jnp.concatenate([s[0:WINDOW, :] + no_prev, s[WINDOW:, :]], axis=0)
        m = jnp.maximum(jnp.max(s, axis=0, keepdims=True), sink)
        p = jnp.exp2(s - m)
        denom = jnp.sum(p, axis=0, keepdims=True) + jnp.exp2(sink - m)
        o_t = _dot(v_t[kvh * d:(kvh + 1) * d, :], p.astype(BF16)) / denom
        outs += [o_t[:, g * WINDOW:(g + 1) * WINDOW] for g in range(SWA_GROUP)]
    o_ref[...] = jnp.concatenate(outs, axis=0).T.astype(o_ref.dtype)


def _swa_attention(proj, sinks, bias):
    b, s, _ = proj.shape
    q_w = N_SWA_HEADS * SWA_HEAD_DIM
    kv_w = 2 * N_SWA_KV_HEADS * SWA_HEAD_DIM
    kv_block = q_w // kv_w
    vmem = (2 * (2 * _nbytes((WINDOW, q_w), BF16) + 2 * _nbytes((WINDOW, kv_w), BF16)
                 + _nbytes(bias.shape, F32)) + 16 * 2**20)
    return pl.pallas_call(
        _swa_kernel,
        grid=(b, s // WINDOW),
        in_specs=[pl.BlockSpec(memory_space=pltpu.SMEM),
                  pl.BlockSpec((None, WINDOW, q_w), lambda bi, n: (bi, n, 0)),
                  pl.BlockSpec((None, WINDOW, kv_w), lambda bi, n: (bi, n, kv_block)),
                  pl.BlockSpec((None, WINDOW, kv_w), lambda bi, n: (bi, jnp.maximum(n - 1, 0), kv_block)),
                  pl.BlockSpec(bias.shape, lambda bi, n: (0, 0, 0))],
        out_specs=pl.BlockSpec((None, WINDOW, q_w), lambda bi, n: (bi, n, 0)),
        out_shape=jax.ShapeDtypeStruct((b, s, q_w), BF16),
        compiler_params=_cparams(("parallel", "arbitrary"), vmem),
        name="sliding_window_attention",
    )(sinks, proj, proj, proj, bias)


def _rope_table_kernel(pos_ref, invf_ref, o_ref):
    ang = pos_ref[...].astype(F32) * invf_ref[...]
    lane = lax.broadcasted_iota(jnp.int32, ang.shape, 1)
    half = QK_ROPE // 2
    sin_signed = jnp.where(lane < QK_ROPE + half, -jnp.sin(ang), jnp.sin(ang))
    o_ref[...] = jnp.where(lane < QK_ROPE, jnp.cos(ang), sin_signed)


def _rope_table(positions, tm=1024):
    m = positions.shape[0]
    half = QK_ROPE // 2
    inv_freq = ROPE_THETA ** (-jnp.arange(half, dtype=F32) / half)
    invf = jnp.tile(inv_freq, LANES // half).reshape(1, LANES)
    return pl.pallas_call(
        _rope_table_kernel,
        grid=(m // tm,),
        in_specs=[pl.BlockSpec((tm, 1), lambda i: (i, 0)), pl.BlockSpec((1, LANES), lambda i: (0, 0))],
        out_specs=pl.BlockSpec((tm, LANES), lambda i: (i, 0)),
        out_shape=jax.ShapeDtypeStruct((m, LANES), F32),
        compiler_params=_cparams(("parallel",), 32 * 2**20),
        name="rope_table",
    )(positions, invf)


def _apply_rope(x_and_partner, cs):
    z = x_and_partner * cs
    z = z + pltpu.roll(z, QK_ROPE, axis=1)
    lane = lax.broadcasted_iota(jnp.int32, z.shape, 1)
    return jnp.where(lane < QK_ROPE, z, 0.0)


def _mla_q_kernel(cq_ref, g_ref, w_ref, cs_ref, o_ref, cn_ref):
    @pl.when(pl.program_id(1) == 0)
    def _():
        cn_ref[...] = _rms(cq_ref[...], g_ref[...]).astype(BF16)

    y = _dot(cn_ref[...], w_ref[...]) * ((QK_NOPE + QK_ROPE) ** -0.5 * LOG2E)
    cs = cs_ref[...]
    parts = []
    for h in range(y.shape[1] // (2 * LANES)):
        yh = y[:, h * 2 * LANES:(h + 1) * 2 * LANES]
        parts += [yh[:, :QK_NOPE], _apply_rope(yh[:, QK_NOPE:], cs)]
    o_ref[...] = jnp.concatenate(parts, axis=1).astype(o_ref.dtype)


def _mla_q(proj, q_norm, w_uq_aug, cs, tm=1024, heads_per_step=4):
    m = proj.shape[0]
    hw = 2 * LANES * heads_per_step
    n_heads = w_uq_aug.shape[1] // hw
    return pl.pallas_call(
        _mla_q_kernel,
        grid=(m // tm, n_heads),
        in_specs=[pl.BlockSpec((tm, Q_LORA), lambda i, h: (i, 1)),
                  pl.BlockSpec((1, Q_LORA), lambda i, h: (0, 0)),
                  pl.BlockSpec((Q_LORA, hw), lambda i, h: (0, h)),
                  pl.BlockSpec((tm, LANES), lambda i, h: (i, 0))],
        out_specs=pl.BlockSpec((tm, hw), lambda i, h: (i, h)),
        out_shape=jax.ShapeDtypeStruct((m, n_heads * hw), BF16),
        scratch_shapes=[pltpu.VMEM((tm, Q_LORA), BF16)],
        compiler_params=_cparams(("parallel", "arbitrary"), 32 * 2**20),
        name="mla_q_expand",
    )(proj, q_norm.reshape(1, Q_LORA), w_uq_aug, cs)


def _mla_kv_kernel(ckv_ref, g_ref, w_ref, kr_ref, cs_ref, k_ref, v_ref, cn_ref, kpe_ref):
    @pl.when(pl.program_id(1) == 0)
    def _():
        cn_ref[...] = _rms(ckv_ref[...], g_ref[...]).astype(BF16)
        kpe_ref[...] = _apply_rope(kr_ref[...], cs_ref[...])

    y = _dot(cn_ref[...], w_ref[...])
    kpe = kpe_ref[...]
    k_parts, v_parts = [], []
    for h in range(y.shape[1] // (2 * LANES)):
        yh = y[:, h * 2 * LANES:(h + 1) * 2 * LANES]
        k_parts += [yh[:, :QK_NOPE], kpe]
        v_parts.append(yh[:, QK_NOPE:])
    k_ref[...] = jnp.concatenate(k_parts, axis=1).astype(k_ref.dtype)
    v_ref[...] = jnp.concatenate(v_parts, axis=1).astype(v_ref.dtype)


def _mla_kv(proj, kv_norm, w_ukv, cs, tm=1024, heads_per_step=4):
    m = proj.shape[0]
    hw = 2 * LANES * heads_per_step
    n_heads = w_ukv.shape[1] // hw
    return pl.pallas_call(
        _mla_kv_kernel,
        grid=(m // tm, n_heads),
        in_specs=[pl.BlockSpec((tm, KV_LORA), lambda i, h: (i, 4)),
                  pl.BlockSpec((1, KV_LORA), lambda i, h: (0, 0)),
                  pl.BlockSpec((KV_LORA, hw), lambda i, h: (0, h)),
                  pl.BlockSpec((tm, LANES), lambda i, h: (i, 10)),
                  pl.BlockSpec((tm, LANES), lambda i, h: (i, 0))],
        out_specs=[pl.BlockSpec((tm, hw), lambda i, h: (i, h)),
                   pl.BlockSpec((tm, hw // 2), lambda i, h: (i, h))],
        out_shape=[jax.ShapeDtypeStruct((m, n_heads * hw), BF16),
                   jax.ShapeDtypeStruct((m, n_heads * hw // 2), BF16)],
        scratch_shapes=[pltpu.VMEM((tm, KV_LORA), BF16), pltpu.VMEM((tm, LANES), F32)],
        compiler_params=_cparams(("parallel", "arbitrary"), 32 * 2**20),
        name="mla_kv_expand",
    )(proj, kv_norm.reshape(1, KV_LORA), w_ukv, proj, cs)


def _swap_halves(w):
    half = w.shape[-1] // 2
    return jnp.concatenate([w[..., half:], w[..., :half]], axis=-1)


def _mixer_out(x2, o, o_mem, w_out, b, s):
    return _matmul_res([o.reshape(b * s, o.shape[-1]), o_mem.reshape(b * s, MEM_W)],
                       w_out[0], w_out[1], x2, tm=1024, tn=512)


def _sb_layer(x2, mem_kv, g, w_in, w_out, b, s):
    proj = _norm_matmul(x2, g, w_in.astype(BF16), BF16, tm=IN_PROJ_ROWS, tn=512, scaled_cols=N_HEADS * HEAD_DIM,
                        col_scale=HEAD_DIM ** -0.5 * LOG2E).reshape(b, s, -1)
    o = _sb_attention(proj, N_HEADS)
    o_mem = _mem_attention(proj, 3 * N_HEADS * HEAD_DIM // MEM_W, mem_kv)
    return _mixer_out(x2, o, o_mem, w_out, b, s)


def _fox_layer(x2, mem_kv, g, w_in, b_f, w_out, b, s):
    qkv_w = 3 * N_HEADS * HEAD_DIM
    w_main = jnp.concatenate([w_in[:, :qkv_w], w_in[:, qkv_w + N_HEADS:]], axis=1).astype(BF16)
    w_gate = jnp.pad(w_in[:, qkv_w:qkv_w + N_HEADS], ((0, 0), (0, LANES - N_HEADS))).astype(BF16)
    proj = _norm_matmul(x2, g, w_main, BF16, tm=IN_PROJ_ROWS, tn=512, scaled_cols=N_HEADS * HEAD_DIM,
                        col_scale=HEAD_DIM ** -0.5 * LOG2E).reshape(b, s, -1)
    f_logit = _norm_matmul(x2, g, w_gate, F32, tm=512, tn=LANES).reshape(b, s, LANES)
    aug = _fox_gates(f_logit, jnp.pad(b_f, (0, LANES - N_HEADS)), N_HEADS)
    o = _flash_attention(proj, proj, proj, 0, N_HEADS, 2 * N_HEADS, HEAD_DIM, HEAD_DIM, N_HEADS, aug=aug)
    o_mem = _mem_attention(proj, qkv_w // MEM_W, mem_kv)
    return _mixer_out(x2, o, o_mem, w_out, b, s)


def _swa_layer(x2, mem_kv, g, rel_bias, w_in, sinks, w_out, b, s):
    proj = _norm_matmul(x2, g, w_in.astype(BF16), BF16, tm=IN_PROJ_ROWS, tn=512,
                        scaled_cols=N_SWA_HEADS * SWA_HEAD_DIM,
                        col_scale=SWA_HEAD_DIM ** -0.5 * LOG2E).reshape(b, s, -1)
    o = _swa_attention(proj, sinks, _t5_bias(rel_bias))
    q_w = N_SWA_HEADS * SWA_HEAD_DIM
    kv_w = 2 * N_SWA_KV_HEADS * SWA_HEAD_DIM
    o_mem = _mem_attention(proj, (q_w + kv_w) // MEM_W, mem_kv)
    return _mixer_out(x2, o, o_mem, w_out, b, s)


def _mla_layer(x2, mem_kv, g, positions, w_in, q_norm, w_uq, kv_norm, w_ukv, w_out, b, s):
    o2, o3 = Q_LORA + KV_LORA, Q_LORA + KV_LORA + QK_ROPE
    w_main = jnp.concatenate([w_in[:, o3:], w_in[:, :o3], _swap_halves(w_in[:, o2:o3])], axis=1).astype(BF16)
    proj = _norm_matmul(x2, g, w_main, F32, tm=512, tn=w_main.shape[1])
    cs = _rope_table(positions.reshape(b * s, 1))
    wq = w_uq.reshape(Q_LORA, N_HEADS, QK_NOPE + QK_ROPE)
    wq = jnp.concatenate([wq, _swap_halves(wq[:, :, QK_NOPE:])], axis=-1).reshape(Q_LORA, -1).astype(BF16)
    q = _mla_q(proj, q_norm, wq, cs).reshape(b, s, -1)
    k, v = _mla_kv(proj, kv_norm, w_ukv.astype(BF16), cs)
    k, v = k.reshape(b, s, -1), v.reshape(b, s, -1)
    o = _flash_attention(q, k, v, 0, 0, 0, 2 * LANES, HEAD_DIM, N_HEADS)
    o_mem = _mem_attention(proj.reshape(b, s, -1), 0, mem_kv)
    return _mixer_out(x2, o, o_mem, w_out, b, s)


def kernel(x, mem, positions, rel_bias, attn_norm, mem_norm, w_mem_kv, ffn_norm, ffn_w_up, ffn_conv_w, ffn_conv_b, ffn_w_down, final_norm, sb_w_in, sb_w_out, fox_w_in, fox_b_f, fox_w_out, swa_w_in, swa_sinks, swa_w_out, mla_w_in, mla_q_norm, mla_w_uq, mla_kv_norm, mla_w_ukv, mla_w_out):
    b, s, d = x.shape
    depth = attn_norm.shape[0]
    mem_len = mem.shape[1]
    x2 = x.reshape(b * s, d)
    mem2 = mem.reshape(b * mem_len, d)
    w_mem_kv_bf16 = w_mem_kv.astype(BF16)
    for i in range(depth):
        kind, j = i % 4, i // 4
        mem_kv = _norm_matmul(mem2, mem_norm[i], w_mem_kv_bf16, BF16, tm=b * mem_len, tn=2 * MEM_W,
                              layer=i).reshape(b, mem_len, 2 * MEM_W)
        g = attn_norm[i]
        if kind == 0:
            x2 = _sb_layer(x2, mem_kv, g, sb_w_in[j], (sb_w_out, j), b, s)
        elif kind == 1:
            x2 = _fox_layer(x2, mem_kv, g, fox_w_in[j], fox_b_f[j], (fox_w_out, j), b, s)
        elif kind == 2:
            x2 = _swa_layer(x2, mem_kv, g, rel_bias, swa_w_in[j], swa_sinks[j], (swa_w_out, j), b, s)
        else:
            x2 = _mla_layer(x2, mem_kv, g, positions, mla_w_in[j], mla_q_norm[j], mla_w_uq[j],
                            mla_kv_norm[j], mla_w_ukv[j], (mla_w_out, j), b, s)
        gated = _ffn_up(_rmsnorm(x2, ffn_norm[i], BF16, tm=512), ffn_w_up, i, ffn_conv_w[i], ffn_conv_b[i],
                        seq=s, tm=1024, tn=512)
        x2 = _matmul_res([gated], ffn_w_down, i, x2, tm=1024, tn=512)
    return _rmsnorm(x2, final_norm, F32, tm=512).reshape(b, s, d)
```
